```python
import math
import jax, jax.numpy as jnp
from jax import lax
import numpy as np

D_MODEL = 1024
BATCH = 8
SEQ = 2048
DEPTH = 1

RET_HEADS = 4
RET_HEAD_DIM = 128
RET_WIDTH = RET_HEADS * RET_HEAD_DIM
RET_CHUNK = 128
SWA_HEADS = 8
SWA_KV_HEADS = 2
SWA_GROUP = SWA_HEADS // SWA_KV_HEADS
SWA_HEAD_DIM = 64
SWA_WIDTH = SWA_HEADS * SWA_HEAD_DIM
SWA_KV_WIDTH = SWA_KV_HEADS * SWA_HEAD_DIM
WINDOW = 128
MIX_WIDTH = RET_WIDTH + SWA_WIDTH
IN_WIDTH = 4 * RET_WIDTH + SWA_WIDTH + 2 * SWA_KV_WIDTH
N_GROUPS = 4
EXPERTS_PER_GROUP = 8
N_EXPERTS = N_GROUPS * EXPERTS_PER_GROUP
TOP_K = 2
D_EXPERT = 512
MOE_BLOCK = 256
LN_EPS = 1e-5
GN_EPS = 1e-6
DEEPNORM_ALPHA = (2 * DEPTH) ** 0.25
DEEPNORM_BETA = (8 * DEPTH) ** -0.25

kernel_name = 'hymba_retention_swa_alibi_hmoe_deepnorm'


def layer_norm(x, g, b):
    xf = x.astype(jnp.float32)
    mu = xf.mean(-1, keepdims=True)
    var = jnp.square(xf - mu).mean(-1, keepdims=True)
    return ((xf - mu) * lax.rsqrt(var + LN_EPS) * g + b).astype(x.dtype)


def retention_chunkwise(q, k, v):
    bsz, nh, s_len, d = q.shape
    c = RET_CHUNK
    n = s_len // c
    log_g = jnp.log1p(-jnp.exp2(-5.0 - jnp.arange(nh, dtype=jnp.float32)))
    idx = jnp.arange(c, dtype=jnp.float32)
    diff = idx[:, None] - idx[None, :]
    decay_in = jnp.where(diff >= 0, jnp.exp(log_g[:, None, None] * jnp.maximum(diff, 0.0)), 0.0)
    q_decay = jnp.exp(log_g[:, None] * (idx + 1.0))
    k_decay = jnp.exp(log_g[:, None] * (c - 1.0 - idx))
    chunk_decay = jnp.exp(log_g * c)
    qc = q.reshape(bsz, nh, n, c, d)
    kc = k.reshape(bsz, nh, n, c, d)
    vc = v.reshape(bsz, nh, n, c, d)
    scores = jnp.einsum('bhncd,bhnmd->bhncm', qc, kc) * decay_in[None, :, None]
    inner = jnp.einsum('bhncm,bhnme->bhnce', scores, vc)
    kv = jnp.einsum('bhnmd,bhnme->nbhde', kc * k_decay[None, :, None, :, None], vc)

    def step(state, kv_n):
        return state * chunk_decay[None, :, None, None] + kv_n, state

    _, states = lax.scan(step, jnp.zeros_like(kv[0]), kv)
    cross = jnp.einsum('bhncd,nbhde->bhnce', qc * q_decay[None, :, None, :, None], states)
    return (inner + cross).reshape(bsz, nh, s_len, d)


def sliding_window_gqa(q, k, v, sinks):
    bsz, s_len = q.shape[:2]
    w = WINDOW
    n = s_len // w
    qb = q.reshape(bsz, n, w, SWA_KV_HEADS, SWA_GROUP, SWA_HEAD_DIM)
    pad = ((0, 0), (w, 0), (0, 0), (0, 0))
    kp = jnp.pad(k, pad).reshape(bsz, n + 1, w, SWA_KV_HEADS, SWA_HEAD_DIM)
    vp = jnp.pad(v, pad).reshape(bsz, n + 1, w, SWA_KV_HEADS, SWA_HEAD_DIM)
    kb = jnp.concatenate([kp[:, :-1], kp[:, 1:]], axis=2)
    vb = jnp.concatenate([vp[:, :-1], vp[:, 1:]], axis=2)
    s = jnp.einsum('bnqkgd,bnskd->bnkgqs', qb, kb,
                   preferred_element_type=jnp.float32) * (SWA_HEAD_DIM ** -0.5)
    qpos = jnp.arange(w)[:, None]
    kpos = jnp.arange(2 * w)[None, :] - w
    dist = qpos - kpos
    abs_k = jnp.arange(n)[:, None, None] * w + kpos[None]
    valid = (dist >= 0) & (dist < w) & (abs_k >= 0)
    slopes = jnp.exp2(-8.0 * (jnp.arange(SWA_HEADS, dtype=jnp.float32) + 1.0) / SWA_HEADS)
    slopes = slopes.reshape(SWA_KV_HEADS, SWA_GROUP)[:, :, None, None]
    s = s - slopes * dist.astype(jnp.float32)
    s = jnp.where(valid[None, :, None, None], s, -jnp.inf)
    sink = sinks.astype(jnp.float32).reshape(SWA_KV_HEADS, SWA_GROUP)[:, :, None, None]
    m = jnp.maximum(s.max(-1, keepdims=True), sink)
    p = jnp.exp(s - m)
    p = p / (p.sum(-1, keepdims=True) + jnp.exp(sink - m))
    o = jnp.einsum('bnkgqs,bnskd->bnqkgd', p, vb.astype(jnp.float32))
    return o.reshape(bsz, s_len, SWA_WIDTH).astype(q.dtype)


def hybrid_mixer(x, w_in, ret_gn_g, attn_sinks, w_out):
    bsz, s_len, _ = x.shape
    proj = jnp.einsum('bsd,de->bse', x, w_in)
    splits = [RET_WIDTH, 2 * RET_WIDTH, 3 * RET_WIDTH, 4 * RET_WIDTH,
              4 * RET_WIDTH + SWA_WIDTH, 4 * RET_WIDTH + SWA_WIDTH + SWA_KV_WIDTH]
    q_r, k_r, v_r, g_r, q_a, k_a, v_a = jnp.split(proj, splits, axis=-1)

    def heads(t):
        return t.reshape(bsz, s_len, RET_HEADS, RET_HEAD_DIM).transpose(0, 2, 1, 3).astype(jnp.float32)
    o_r = retention_chunkwise(heads(q_r), heads(k_r) * (RET_HEAD_DIM ** -0.5), heads(v_r))
    mu = o_r.mean(-1, keepdims=True)
    var = jnp.square(o_r - mu).mean(-1, keepdims=True)
    o_r = ((o_r - mu) * lax.rsqrt(var + GN_EPS)).transpose(0, 2, 1, 3).reshape(bsz, s_len, RET_WIDTH)
    o_r = (o_r * ret_gn_g).astype(x.dtype) * jax.nn.silu(g_r)

    o_a = sliding_window_gqa(
        q_a.reshape(bsz, s_len, SWA_KV_HEADS, SWA_GROUP, SWA_HEAD_DIM),
        k_a.reshape(bsz, s_len, SWA_KV_HEADS, SWA_HEAD_DIM),
        v_a.reshape(bsz, s_len, SWA_KV_HEADS, SWA_HEAD_DIM),
        attn_sinks)

    o = jnp.concatenate([o_r, o_a.astype(x.dtype)], axis=-1)
    return jnp.einsum('bse,ed->bsd', o, w_out)


def hierarchical_moe(h, w_group_router, b_group_router, w_expert_router, b_expert_router,
                     w_gate, w_up, w_down):
    t_len, d = h.shape
    gl = (h @ w_group_router + b_group_router).astype(jnp.float32)
    gp = jax.nn.softmax(gl, axis=-1)
    g_idx = jnp.argmax(gl, axis=-1)
    g_w = jnp.take_along_axis(gp, g_idx[:, None], axis=-1)
    el = jnp.einsum('td,gde->tge', h, w_expert_router) + b_expert_router
    el = jnp.take_along_axis(el, g_idx[:, None, None], axis=1)[:, 0].astype(jnp.float32)
    top_l, top_i = lax.top_k(el, TOP_K)
    e_w = jax.nn.softmax(top_l, axis=-1) * g_w
    e_id = g_idx[:, None] * EXPERTS_PER_GROUP + top_i

    flat_e = e_id.reshape(-1)
    flat_w = e_w.reshape(-1)
    flat_tok = jnp.repeat(jnp.arange(t_len), TOP_K)
    order = jnp.argsort(flat_e)
    e_s = flat_e[order]
    tok_s = flat_tok[order]
    w_s = flat_w[order]
    counts = jnp.bincount(flat_e, length=N_EXPERTS)
    padded = (counts + MOE_BLOCK - 1) // MOE_BLOCK * MOE_BLOCK
    start = jnp.cumsum(counts) - counts
    pend = jnp.cumsum(padded)
    pstart = pend - padded
    dest = pstart[e_s] + (jnp.arange(t_len * TOP_K) - start[e_s])
    n_blocks = -(-(t_len * TOP_K) // MOE_BLOCK) + N_EXPERTS
    n_rows = n_blocks * MOE_BLOCK
    rows = jnp.zeros((n_rows, d), h.dtype).at[dest].set(h[tok_s])
    block_expert = jnp.searchsorted(pend, jnp.arange(n_blocks) * MOE_BLOCK, side='right')
    block_expert = jnp.minimum(block_expert, N_EXPERTS - 1)

    def expert_block(args):
        xb, e = args
        return (jax.nn.silu(xb @ w_gate[e]) * (xb @ w_up[e])) @ w_down[e]

    y_rows = lax.map(expert_block, (rows.reshape(n_blocks, MOE_BLOCK, d), block_expert))
    y_rows = y_rows.reshape(n_rows, d)
    return jnp.zeros((t_len, d), h.dtype).at[tok_s].add(y_rows[dest] * w_s[:, None].astype(h.dtype))


def setup_inputs(seed: int = 0) -> dict:
    key = jax.random.key(seed)
    ks = jax.random.split(key, 16)
    f32 = jnp.float32

    def normal(k, shape, scale):
        return jax.random.normal(k, shape, f32) * scale

    col_scale = jnp.concatenate([
        jnp.ones((2 * RET_WIDTH,), f32),
        jnp.full((RET_WIDTH,), DEEPNORM_BETA, f32),
        jnp.ones((RET_WIDTH + SWA_WIDTH + SWA_KV_WIDTH,), f32),
        jnp.full((SWA_KV_WIDTH,), DEEPNORM_BETA, f32)])
    x = normal(ks[0], (BATCH, SEQ, D_MODEL), 1.0)
    w_in = normal(ks[1], (DEPTH, D_MODEL, IN_WIDTH), D_MODEL ** -0.5) * col_scale
    ret_gn_g = 1.0 + normal(ks[2], (DEPTH, RET_WIDTH), 0.02)
    attn_sinks = normal(ks[3], (DEPTH, SWA_HEADS), 0.5)
    w_out = normal(ks[4], (DEPTH, MIX_WIDTH, D_MODEL), MIX_WIDTH ** -0.5 * DEEPNORM_BETA)
    ln1_g = 1.0 + normal(ks[5], (DEPTH, D_MODEL), 0.02)
    ln1_b = normal(ks[6], (DEPTH, D_MODEL), 0.02)
    w_group_router = normal(ks[7], (DEPTH, D_MODEL, N_GROUPS), D_MODEL ** -0.5)
    b_group_router = normal(ks[8], (DEPTH, N_GROUPS), 0.01)
    w_expert_router = normal(ks[9], (DEPTH, N_GROUPS, D_MODEL, EXPERTS_PER_GROUP), D_MODEL ** -0.5)
    b_expert_router = normal(ks[10], (DEPTH, N_GROUPS, EXPERTS_PER_GROUP), 0.01)
    w_gate = normal(ks[11], (DEPTH, N_EXPERTS, D_MODEL, D_EXPERT), D_MODEL ** -0.5 * DEEPNORM_BETA)
    w_up = normal(ks[12], (DEPTH, N_EXPERTS, D_MODEL, D_EXPERT), D_MODEL ** -0.5 * DEEPNORM_BETA)
    w_down = normal(ks[13], (DEPTH, N_EXPERTS, D_EXPERT, D_MODEL), D_EXPERT ** -0.5 * DEEPNORM_BETA)
    ln2_g = 1.0 + normal(ks[14], (DEPTH, D_MODEL), 0.02)
    ln2_b = normal(ks[15], (DEPTH, D_MODEL), 0.02)
    return {'x': x, 'w_in': w_in, 'ret_gn_g': ret_gn_g, 'attn_sinks': attn_sinks,
            'w_out': w_out, 'ln1_g': ln1_g, 'ln1_b': ln1_b,
            'w_group_router': w_group_router, 'b_group_router': b_group_router,
            'w_expert_router': w_expert_router, 'b_expert_router': b_expert_router,
            'w_gate': w_gate, 'w_up': w_up, 'w_down': w_down,
            'ln2_g': ln2_g, 'ln2_b': ln2_b}


def reference(x, w_in, ret_gn_g, attn_sinks, w_out, ln1_g, ln1_b,
              w_group_router, b_group_router, w_expert_router, b_expert_router,
              w_gate, w_up, w_down, ln2_g, ln2_b):
    bsz, s_len, d = x.shape
    for i in range(DEPTH):
        mix = hybrid_mixer(x, w_in[i], ret_gn_g[i], attn_sinks[i], w_out[i])
        x = layer_norm(DEEPNORM_ALPHA * x + mix, ln1_g[i], ln1_b[i])
        ffn = hierarchical_moe(x.reshape(bsz * s_len, d), w_group_router[i], b_group_router[i],
                               w_expert_router[i], b_expert_router[i],
                               w_gate[i], w_up[i], w_down[i]).reshape(bsz, s_len, d)
        x = layer_norm(DEEPNORM_ALPHA * x + ffn, ln2_g[i], ln2_b[i])
    return x
```

```python
import functools

import jax
import jax.numpy as jnp
from jax import lax
from jax.experimental import pallas as pl
from jax.experimental.pallas import tpu as pltpu

F32 = jnp.float32
BF16 = jnp.bfloat16

D_MODEL = 1024
RET_HEADS = 4
RET_HEAD_DIM = 128
RET_WIDTH = RET_HEADS * RET_HEAD_DIM
CHUNK = 128
SWA_HEADS = 8
SWA_KV_HEADS = 2
SWA_GROUP = SWA_HEADS // SWA_KV_HEADS
SWA_HEAD_DIM = 64
SWA_WIDTH = SWA_HEADS * SWA_HEAD_DIM
SWA_KV_WIDTH = SWA_KV_HEADS * SWA_HEAD_DIM
IN_WIDTH = 4 * RET_WIDTH + SWA_WIDTH + 2 * SWA_KV_WIDTH
N_GROUPS = 4
EXPERTS_PER_GROUP = 8
N_EXPERTS = N_GROUPS * EXPERTS_PER_GROUP
TOP_K = 2
D_EXPERT = 512
LN_EPS = 1e-5
GN_EPS = 1e-6
DEPTH = 1
DEEPNORM_ALPHA = (2 * DEPTH) ** 0.25
NEG = -1e30

LANES = 128
SEQ_TILE = 256
MOE_ROWS = 256
DISPATCH_TILE = 256
COMBINE_TILE = 256
VMEM_LIMIT = 56 * 1024 * 1024

_QR, _KR, _VR, _GR = 0, RET_WIDTH, 2 * RET_WIDTH, 3 * RET_WIDTH
_QA = 4 * RET_WIDTH
_KA = _QA + SWA_WIDTH
_VA = _KA + SWA_KV_WIDTH
GROUP_LANE0 = N_EXPERTS


def _dot(a, b):
    return jnp.dot(a, b, preferred_element_type=F32)


def _dot_nt(a, b):
    return lax.dot_general(a, b, (((1,), (1,)), ((), ())), preferred_element_type=F32)


def _dot_tn(a, b):
    return lax.dot_general(a, b, (((0,), (0,)), ((), ())), preferred_element_type=F32)


def _layer_norm(z, g, b):
    mu = jnp.mean(z, axis=-1, keepdims=True)
    zc = z - mu
    var = jnp.mean(zc * zc, axis=-1, keepdims=True)
    return zc * lax.rsqrt(var + LN_EPS) * g + b


def _silu(g):
    return g / (1.0 + jnp.exp(-g))


def _mixer_kernel(cdec_ref, x_ref, w_in_ref, w_out_ref, gn_ref, ln_g_ref, ln_b_ref,
                  wr_hi_ref, wr_lo_ref, br_ref, din_ref, qdec_ref, kdec_ref,
                  bprev_ref, bcur_ref, sink_ref, tri_ref,
                  h_ref, ids_ref, wts_ref, cnt_ref,
                  state_scr, kprev_scr, vprev_scr, o_scr, carry_scr):
    b = pl.program_id(0)
    n = pl.program_id(1)
    ts = x_ref.shape[0]

    @pl.when(n == 0)
    def _():
        state_scr[...] = jnp.zeros_like(state_scr)
        kprev_scr[...] = jnp.zeros_like(kprev_scr)
        vprev_scr[...] = jnp.zeros_like(vprev_scr)

    @pl.when((b == 0) & (n == 0))
    def _():
        carry_scr[...] = jnp.zeros_like(carry_scr)

    x = x_ref[...]
    xb = x.astype(BF16)

    def proj(lo, hi):
        return _dot(xb, w_in_ref[:, lo:hi])

    q_r, k_r, v_r, g_r = proj(_QR, _KR), proj(_KR, _VR), proj(_VR, _GR), proj(_GR, _QA)
    q_a, k_a, v_a = proj(_QA, _KA), proj(_KA, _VA), proj(_VA, IN_WIDTH)
    k_ab = k_a.astype(BF16)
    v_ab = v_a.astype(BF16)
    first_pen = jnp.where(n == 0, NEG, 0.0).astype(F32)

    for c in range(ts // CHUNK):
        rs = slice(c * CHUNK, (c + 1) * CHUNK)
        for hd in range(RET_HEADS):
            cs = slice(hd * RET_HEAD_DIM, (hd + 1) * RET_HEAD_DIM)
            q = q_r[rs, cs]
            k = k_r[rs, cs]
            v = v_r[rs, cs].astype(BF16)
            scores = _dot_nt(q.astype(BF16), k.astype(BF16)) * din_ref[hd]
            inner = _dot(scores.astype(BF16), v)
            st = state_scr[hd]
            cross = _dot((q * qdec_ref[hd]).astype(BF16), st.astype(BF16))
            kv = _dot_tn((k * kdec_ref[hd]).astype(BF16), v)
            state_scr[hd] = st * cdec_ref[hd] + kv
            o = inner + cross
            mu = jnp.mean(o, axis=-1, keepdims=True)
            oc = o - mu
            var = jnp.mean(oc * oc, axis=-1, keepdims=True)
            on = oc * lax.rsqrt(var + GN_EPS) * gn_ref[:, cs] * _silu(g_r[rs, cs])
            o_scr[rs, cs] = on.astype(BF16)
        for j in range(SWA_KV_HEADS):
            ks = slice(j * SWA_HEAD_DIM, (j + 1) * SWA_HEAD_DIM)
            kc = k_ab[rs, ks]
            vc = v_ab[rs, ks]
            if c == 0:
                kp = kprev_scr[:, ks]
                vp = vprev_scr[:, ks]
            else:
                ps = slice((c - 1) * CHUNK, c * CHUNK)
                kp = k_ab[ps, ks]
                vp = v_ab[ps, ks]
            q0 = j * SWA_GROUP * SWA_HEAD_DIM
            qs = jnp.concatenate(
                [q_a[rs, q0 + g * SWA_HEAD_DIM:q0 + (g + 1) * SWA_HEAD_DIM] for g in range(SWA_GROUP)],
                axis=0)
            qs = (qs * (SWA_HEAD_DIM ** -0.5)).astype(BF16)
            s_p = _dot_nt(qs, kp) + bprev_ref[j]
            if c == 0:
                s_p = s_p + first_pen
            s_c = _dot_nt(qs, kc) + bcur_ref[j]
            sink = sink_ref[j]
            m = jnp.maximum(jnp.maximum(jnp.max(s_p, axis=-1, keepdims=True),
                                        jnp.max(s_c, axis=-1, keepdims=True)), sink)
            p_p = jnp.exp(s_p - m)
            p_c = jnp.exp(s_c - m)
            den = (jnp.sum(p_p, axis=-1, keepdims=True) + jnp.sum(p_c, axis=-1, keepdims=True)
                   + jnp.exp(sink - m))
            o = (_dot(p_p.astype(BF16), vp) + _dot(p_c.astype(BF16), vc)) / den
            o = jnp.concatenate([o[g * CHUNK:(g + 1) * CHUNK] for g in range(SWA_GROUP)], axis=1)
            c0 = RET_WIDTH + j * SWA_GROUP * SWA_HEAD_DIM
            o_scr[rs, c0:c0 + SWA_GROUP * SWA_HEAD_DIM] = o.astype(BF16)

    kprev_scr[...] = k_ab[ts - CHUNK:, :]
    vprev_scr[...] = v_ab[ts - CHUNK:, :]

    mix = _dot(o_scr[...], w_out_ref[...])
    h = _layer_norm(DEEPNORM_ALPHA * x + mix, ln_g_ref[...], ln_b_ref[...])
    h_ref[...] = h

    h_hi = h.astype(BF16)
    h_lo = (h - h_hi.astype(F32)).astype(BF16)
    logits = (_dot(h_hi, wr_hi_ref[...]) + _dot(h_lo, wr_hi_ref[...]) + _dot(h_hi, wr_lo_ref[...])
              + br_ref[...])
    lane = lax.broadcasted_iota(jnp.int32, (ts, LANES), 1).astype(F32)
    big = 1e9
    ninf = -jnp.inf
    gmask = (lane >= GROUP_LANE0) & (lane < GROUP_LANE0 + N_GROUPS)
    gl = jnp.where(gmask, logits, ninf)
    gmax = jnp.max(gl, axis=-1, keepdims=True)
    gidx = jnp.min(jnp.where(gl == gmax, lane, big), axis=-1, keepdims=True) - GROUP_LANE0
    g_w = 1.0 / jnp.sum(jnp.exp(gl - gmax), axis=-1, keepdims=True)
    lo = gidx * EXPERTS_PER_GROUP
    el = jnp.where((lane >= lo) & (lane < lo + EXPERTS_PER_GROUP), logits, ninf)
    m1 = jnp.max(el, axis=-1, keepdims=True)
    i1 = jnp.min(jnp.where(el == m1, lane, big), axis=-1, keepdims=True)
    el2 = jnp.where(lane == i1, ninf, el)
    m2 = jnp.max(el2, axis=-1, keepdims=True)
    i2 = jnp.min(jnp.where(el2 == m2, lane, big), axis=-1, keepdims=True)
    t = jnp.exp(m2 - m1)
    w1 = g_w / (1.0 + t)
    w2 = g_w * t / (1.0 + t)
    hit1 = lane == i1
    hit2 = lane == i2
    onehot = (hit1 | hit2).astype(BF16)
    prefix = _dot(tri_ref[...], onehot) + carry_scr[0:1, :]
    r1 = jnp.sum(jnp.where(hit1, prefix, 0.0), axis=-1, keepdims=True)
    r2 = jnp.sum(jnp.where(hit2, prefix, 0.0), axis=-1, keepdims=True)
    carry = carry_scr[0:1, :] + jnp.sum(onehot.astype(F32), axis=0, keepdims=True)
    carry_scr[0:1, :] = carry
    cnt_ref[...] = jnp.broadcast_to(carry, cnt_ref.shape)
    idsf = jnp.where(lane == 0, i1, jnp.where(lane == 1, i2,
                     jnp.where(lane == 2, r1, jnp.where(lane == 3, r2, 0.0))))
    ids_ref[...] = idsf.astype(jnp.int32)
    wts_ref[...] = jnp.where(lane == 0, w1, jnp.where(lane == 1, w2, 0.0))


def _mixer_call(x2, w_in, w_out, gn, ln_g, ln_b, wr_hi, wr_lo, br, consts, bsz, s_len):
    cdec, din, qdec, kdec, bprev, bcur, sink, tri = consts
    t_len = bsz * s_len
    ns = s_len // SEQ_TILE
    tok = lambda b, n, *_: (b * ns + n, 0)
    full2 = lambda b, n, *_: (0, 0)
    full3 = lambda b, n, *_: (0, 0, 0)
    grid_spec = pltpu.PrefetchScalarGridSpec(
        num_scalar_prefetch=1,
        grid=(bsz, ns),
        in_specs=[
            pl.BlockSpec((SEQ_TILE, D_MODEL), tok),
            pl.BlockSpec((D_MODEL, IN_WIDTH), full2),
            pl.BlockSpec((D_MODEL, D_MODEL), full2),
            pl.BlockSpec((1, RET_WIDTH), full2),
            pl.BlockSpec((1, D_MODEL), full2),
            pl.BlockSpec((1, D_MODEL), full2),
            pl.BlockSpec((D_MODEL, LANES), full2),
            pl.BlockSpec((D_MODEL, LANES), full2),
            pl.BlockSpec((1, LANES), full2),
            pl.BlockSpec((RET_HEADS, CHUNK, CHUNK), full3),
            pl.BlockSpec((RET_HEADS, CHUNK, RET_HEAD_DIM), full3),
            pl.BlockSpec((RET_HEADS, CHUNK, RET_HEAD_DIM), full3),
            pl.BlockSpec((SWA_KV_HEADS, SWA_GROUP * CHUNK, CHUNK), full3),
            pl.BlockSpec((SWA_KV_HEADS, SWA_GROUP * CHUNK, CHUNK), full3),
            pl.BlockSpec((SWA_KV_HEADS, SWA_GROUP * CHUNK, 1), full3),
            pl.BlockSpec((SEQ_TILE, SEQ_TILE), full2),
        ],
        out_specs=[
            pl.BlockSpec((SEQ_TILE, D_MODEL), tok),
            pl.BlockSpec((SEQ_TILE, LANES), tok),
            pl.BlockSpec((SEQ_TILE, LANES), tok),
            pl.BlockSpec((8, LANES), full2),
        ],
        scratch_shapes=[
            pltpu.VMEM((RET_HEADS, RET_HEAD_DIM, RET_HEAD_DIM), F32),
            pltpu.VMEM((CHUNK, SWA_KV_WIDTH), BF16),
            pltpu.VMEM((CHUNK, SWA_KV_WIDTH), BF16),
            pltpu.VMEM((SEQ_TILE, D_MODEL), BF16),
            pltpu.VMEM((8, LANES), F32),
        ],
    )
    return pl.pallas_call(
        _mixer_kernel,
        grid_spec=grid_spec,
        out_shape=[
            jax.ShapeDtypeStruct((t_len, D_MODEL), F32),
            jax.ShapeDtypeStruct((t_len, LANES), jnp.int32),
            jax.ShapeDtypeStruct((t_len, LANES), F32),
            jax.ShapeDtypeStruct((8, LANES), F32),
        ],
        compiler_params=pltpu.CompilerParams(
            dimension_semantics=("arbitrary", "arbitrary"), vmem_limit_bytes=VMEM_LIMIT),
        name="mixer_router",
    )(cdec, x2, w_in, w_out, gn, ln_g, ln_b, wr_hi, wr_lo, br, din, qdec, kdec, bprev, bcur, sink, tri)


def _dispatch_kernel(dest_ref, pend_ref, padded_ref, nused_ref, h_ref, rows_ref, zbuf, sem, zsem):
    i = pl.program_id(0)
    td = h_ref.shape[0]
    n_blocks = rows_ref.shape[0] // MOE_ROWS

    @pl.when(i == 0)
    def _():
        zbuf[...] = jnp.zeros_like(zbuf)

        def zero_block(start):
            cp = pltpu.make_async_copy(
                zbuf, rows_ref.at[pl.ds(pl.multiple_of(start, MOE_ROWS), MOE_ROWS)], zsem)
            cp.start()
            cp.wait()

        def zero_tail(e, carry):
            @pl.when(padded_ref[e] > 0)
            def _():
                zero_block(pend_ref[e] - MOE_ROWS)
            return carry

        lax.fori_loop(0, N_EXPERTS, zero_tail, 0)

        def zero_unused(p, carry):
            zero_block(p * MOE_ROWS)
            return carry

        lax.fori_loop(nused_ref[0], n_blocks, zero_unused, 0)

    def row_copy(t, slot):
        return pltpu.make_async_copy(h_ref.at[pl.ds(t, 1)], rows_ref.at[pl.ds(slot, 1)], sem)

    def issue(t, carry):
        base = (i * td + t) * TOP_K
        for k in range(TOP_K):
            row_copy(t, dest_ref[base + k]).start()
        return carry

    lax.fori_loop(0, td, issue, 0)

    def drain(t, carry):
        for k in range(TOP_K):
            row_copy(0, 0).wait()
        return carry

    lax.fori_loop(0, td, drain, 0)


def _dispatch_call(dest_flat, pend, padded, nused, h, n_rows):
    t_len = h.shape[0]
    grid_spec = pltpu.PrefetchScalarGridSpec(
        num_scalar_prefetch=4,
        grid=(t_len // DISPATCH_TILE,),
        in_specs=[pl.BlockSpec((DISPATCH_TILE, D_MODEL), lambda i, *_: (i, 0))],
        out_specs=pl.BlockSpec(memory_space=pl.ANY),
        scratch_shapes=[
            pltpu.VMEM((MOE_ROWS, D_MODEL), F32),
            pltpu.SemaphoreType.DMA(()),
            pltpu.SemaphoreType.DMA(()),
        ],
    )
    return pl.pallas_call(
        _dispatch_kernel,
        grid_spec=grid_spec,
        out_shape=jax.ShapeDtypeStruct((n_rows, D_MODEL), F32),
        compiler_params=pltpu.CompilerParams(
            dimension_semantics=("arbitrary",), vmem_limit_bytes=VMEM_LIMIT),
        name="moe_dispatch",
    )(dest_flat, pend, padded, nused, h)


def _expert_kernel(bexp_ref, nused_ref, rows_ref, wg_ref, wu_ref, wd_ref, y_ref, wg_s, wu_s, wd_s):
    p = pl.program_id(0)
    prev = bexp_ref[jnp.maximum(p - 1, 0)]

    @pl.when((p == 0) | (bexp_ref[p] != prev))
    def _():
        wg_s[...] = wg_ref[0].astype(BF16)
        wu_s[...] = wu_ref[0].astype(BF16)
        wd_s[...] = wd_ref[0].astype(BF16)

    @pl.when(p < nused_ref[0])
    def _():
        xb = rows_ref[...].astype(BF16)
        g = _dot(xb, wg_s[...])
        u = _dot(xb, wu_s[...])
        a = (_silu(g) * u).astype(BF16)
        y_ref[...] = _dot(a, wd_s[...])

    @pl.when(p >= nused_ref[0])
    def _():
        y_ref[...] = jnp.zeros_like(y_ref)


def _expert_call(bexp, nused, rows, w_gate, w_up, w_down):
    n_rows = rows.shape[0]
    n_blocks = n_rows // MOE_ROWS
    blk = lambda p, be, nu: (jnp.minimum(p, nu[0] - 1), 0)
    wsel = lambda p, be, nu: (be[p], 0, 0)
    grid_spec = pltpu.PrefetchScalarGridSpec(
        num_scalar_prefetch=2,
        grid=(n_blocks,),
        in_specs=[
            pl.BlockSpec((MOE_ROWS, D_MODEL), blk),
            pl.BlockSpec((1, D_MODEL, D_EXPERT), wsel),
            pl.BlockSpec((1, D_MODEL, D_EXPERT), wsel),
            pl.BlockSpec((1, D_EXPERT, D_MODEL), wsel),
        ],
        out_specs=pl.BlockSpec((MOE_ROWS, D_MODEL), lambda p, be, nu: (p, 0)),
        scratch_shapes=[
            pltpu.VMEM((D_MODEL, D_EXPERT), BF16),
            pltpu.VMEM((D_MODEL, D_EXPERT), BF16),
            pltpu.VMEM((D_EXPERT, D_MODEL), BF16),
        ],
    )
    return pl.pallas_call(
        _expert_kernel,
        grid_spec=grid_spec,
        out_shape=jax.ShapeDtypeStruct((n_rows, D_MODEL), F32),
        compiler_params=pltpu.CompilerParams(
            dimension_semantics=("arbitrary",), vmem_limit_bytes=VMEM_LIMIT),
        name="moe_experts",
    )(bexp, nused, rows, w_gate, w_up, w_down)


def _combine_kernel(dest_ref, h_ref, wts_ref, ln_g_ref, ln_b_ref, y_ref, out_ref, ybuf, sem):
    i = pl.program_id(0)
    tc = h_ref.shape[0]

    def row_copy(slot, k, t):
        return pltpu.make_async_copy(y_ref.at[pl.ds(slot, 1)], ybuf.at[k, pl.ds(t, 1)], sem)

    def issue(t, carry):
        base = (i * tc + t) * TOP_K
        for k in range(TOP_K):
            row_copy(dest_ref[base + k], k, t).start()
        return carry

    lax.fori_loop(0, tc, issue, 0)

    def drain(t, carry):
        for k in range(TOP_K):
            row_copy(0, k, 0).wait()
        return carry

    lax.fori_loop(0, tc, drain, 0)

    wts = wts_ref[...]
    ffn = ybuf[0] * wts[:, 0:1] + ybuf[1] * wts[:, 1:2]
    out_ref[...] = _layer_norm(DEEPNORM_ALPHA * h_ref[...] + ffn, ln_g_ref[...], ln_b_ref[...])


def _combine_call(dest_flat, h, wts, ln_g, ln_b, y):
    t_len = h.shape[0]
    tok = lambda i, *_: (i, 0)
    full2 = lambda i, *_: (0, 0)
    grid_spec = pltpu.PrefetchScalarGridSpec(
        num_scalar_prefetch=1,
        grid=(t_len // COMBINE_TILE,),
        in_specs=[
            pl.BlockSpec((COMBINE_TILE, D_MODEL), tok),
            pl.BlockSpec((COMBINE_TILE, LANES), tok),
            pl.BlockSpec((1, D_MODEL), full2),
            pl.BlockSpec((1, D_MODEL), full2),
            pl.BlockSpec(memory_space=pl.ANY),
        ],
        out_specs=pl.BlockSpec((COMBINE_TILE, D_MODEL), tok),
        scratch_shapes=[
            pltpu.VMEM((TOP_K, COMBINE_TILE, D_MODEL), F32),
            pltpu.SemaphoreType.DMA(()),
        ],
    )
    return pl.pallas_call(
        _combine_kernel,
        grid_spec=grid_spec,
        out_shape=jax.ShapeDtypeStruct((t_len, D_MODEL), F32),
        compiler_params=pltpu.CompilerParams(
            dimension_semantics=("arbitrary",), vmem_limit_bytes=VMEM_LIMIT),
        name="moe_combine_ln",
    )(dest_flat, h, wts, ln_g, ln_b, y)


def _mixer_constants(attn_sinks):
    c = CHUNK
    log_g = jnp.log1p(-jnp.exp2(-5.0 - jnp.arange(RET_HEADS, dtype=F32)))
    idx = jnp.arange(c, dtype=F32)
    diff = idx[:, None] - idx[None, :]
    scale = RET_HEAD_DIM ** -0.5
    din = jnp.where(diff >= 0, jnp.exp(log_g[:, None, None] * jnp.maximum(diff, 0.0)), 0.0) * scale
    qdec = jnp.broadcast_to(jnp.exp(log_g[:, None] * (idx + 1.0))[:, :, None], (RET_HEADS, c, RET_HEAD_DIM))
    kdec = jnp.broadcast_to((jnp.exp(log_g[:, None] * (c - 1.0 - idx)) * scale)[:, :, None],
                            (RET_HEADS, c, RET_HEAD_DIM))
    cdec = jnp.exp(log_g * c)
    slopes = jnp.exp2(-8.0 * (jnp.arange(SWA_HEADS, dtype=F32) + 1.0) / SWA_HEADS)
    r = jnp.arange(c)[:, None]
    col = jnp.arange(c)[None, :]
    dist_prev = (r - col + c).astype(F32)
    dist_cur = (r - col).astype(F32)
    bprev = jnp.where((r < col)[None], -slopes[:, None, None] * dist_prev[None], NEG)
    bcur = jnp.where((r >= col)[None], -slopes[:, None, None] * dist_cur[None], NEG)
    bprev = bprev.reshape(SWA_KV_HEADS, SWA_GROUP * c, c)
    bcur = bcur.reshape(SWA_KV_HEADS, SWA_GROUP * c, c)
    sink = jnp.broadcast_to(attn_sinks.astype(F32)[:, None], (SWA_HEADS, c)).reshape(
        SWA_KV_HEADS, SWA_GROUP * c, 1)
    tr = jnp.arange(SEQ_TILE)
    tri = (tr[None, :] < tr[:, None]).astype(BF16)
    return cdec.astype(F32), din.astype(F32), qdec.astype(F32), kdec.astype(F32), \
        bprev.astype(F32), bcur.astype(F32), sink, tri


def _router_tables(w_group_router, b_group_router, w_expert_router, b_expert_router):
    w_e = jnp.transpose(w_expert_router, (1, 0, 2)).reshape(D_MODEL, N_EXPERTS)
    w = jnp.concatenate([w_e, w_group_router,
                         jnp.zeros((D_MODEL, LANES - N_EXPERTS - N_GROUPS), F32)], axis=1)
    bias = jnp.concatenate([b_expert_router.reshape(N_EXPERTS), b_group_router,
                            jnp.zeros((LANES - N_EXPERTS - N_GROUPS,), F32)])[None, :]
    w_hi = w.astype(BF16)
    w_lo = (w - w_hi.astype(F32)).astype(BF16)
    return w_hi, w_lo, bias


def kernel(x, w_in, ret_gn_g, attn_sinks, w_out, ln1_g, ln1_b, w_group_router, b_group_router,
           w_expert_router, b_expert_router, w_gate, w_up, w_down, ln2_g, ln2_b):
    bsz, s_len, d = x.shape
    assert d == D_MODEL and s_len % SEQ_TILE == 0 and w_in.shape[0] == DEPTH == 1
    t_len = bsz * s_len
    n_blocks = t_len * TOP_K // MOE_ROWS + N_EXPERTS
    n_rows = n_blocks * MOE_ROWS

    consts = _mixer_constants(attn_sinks[0])
    wr_hi, wr_lo, br = _router_tables(w_group_router[0], b_group_router[0],
                                      w_expert_router[0], b_expert_router[0])
    h, ids, wts, cnt = _mixer_call(
        x.reshape(t_len, d), w_in[0].astype(BF16), w_out[0].astype(BF16), ret_gn_g[0][None, :],
        ln1_g[0][None, :], ln1_b[0][None, :], wr_hi, wr_lo, br, consts, bsz, s_len)

    counts = cnt[0, :N_EXPERTS].astype(jnp.int32)
    padded = (counts + MOE_ROWS - 1) // MOE_ROWS * MOE_ROWS
    pend = jnp.cumsum(padded)
    pstart = pend - padded
    dest = (jnp.take(pstart, ids[:, 0:TOP_K], axis=0) + ids[:, TOP_K:2 * TOP_K]).reshape(-1)
    nused = (pend[-1:] // MOE_ROWS).astype(jnp.int32)
    blk_start = jnp.minimum(jnp.arange(n_blocks, dtype=jnp.int32), nused[0] - 1) * MOE_ROWS
    bexp = jnp.minimum(jnp.searchsorted(pend, blk_start, side='right'), N_EXPERTS - 1).astype(jnp.int32)

    rows = _dispatch_call(dest, pend.astype(jnp.int32), padded.astype(jnp.int32), nused, h, n_rows)
    y = _expert_call(bexp, nused, rows, w_gate[0], w_up[0], w_down[0])
    out = _combine_call(dest, h, wts, ln2_g[0][None, :], ln2_b[0][None, :], y)
    return out.reshape(bsz, s_len, d)
```

```python
import functools

import jax
import jax.numpy as jnp
from jax import lax
from jax.experimental import pallas as pl
from jax.experimental.pallas import tpu as pltpu
from jax.experimental.pallas import tpu_sc as plsc

F32 = jnp.float32
BF16 = jnp.bfloat16

D_MODEL = 1024
RET_HEADS = 4
RET_HEAD_DIM = 128
RET_WIDTH = RET_HEADS * RET_HEAD_DIM
CHUNK = 128
SWA_HEADS = 8
SWA_KV_HEADS = 2
SWA_GROUP = SWA_HEADS // SWA_KV_HEADS
SWA_HEAD_DIM = 64
SWA_WIDTH = SWA_HEADS * SWA_HEAD_DIM
SWA_KV_WIDTH = SWA_KV_HEADS * SWA_HEAD_DIM
IN_WIDTH = 4 * RET_WIDTH + SWA_WIDTH + 2 * SWA_KV_WIDTH
N_GROUPS = 4
EXPERTS_PER_GROUP = 8
N_EXPERTS = N_GROUPS * EXPERTS_PER_GROUP
TOP_K = 2
D_EXPERT = 512
LN_EPS = 1e-5
GN_EPS = 1e-6
DEPTH = 1
DEEPNORM_ALPHA = (2 * DEPTH) ** 0.25
NEG = -1e30

LANES = 128
ROW_SUBLANES = D_MODEL // LANES
SEQ_TILE = 256
MOE_ROWS = 256
COMBINE_TILE = 256
SC_CORES = 2
SC_SUBCORES = 16
SC_WORKERS = SC_CORES * SC_SUBCORES
SC_CHUNK_ROWS = 32
VMEM_LIMIT = 56 * 1024 * 1024

_QR, _KR, _VR, _GR = 0, RET_WIDTH, 2 * RET_WIDTH, 3 * RET_WIDTH
_QA = 4 * RET_WIDTH
_KA = _QA + SWA_WIDTH
_VA = _KA + SWA_KV_WIDTH
GROUP_LANE0 = N_EXPERTS


def _dot(a, b):
    return jnp.dot(a, b, preferred_element_type=F32)


def _dot_nt(a, b):
    return lax.dot_general(a, b, (((1,), (1,)), ((), ())), preferred_element_type=F32)


def _dot_tn(a, b):
    return lax.dot_general(a, b, (((0,), (0,)), ((), ())), preferred_element_type=F32)


def _layer_norm(z, g, b):
    mu = jnp.mean(z, axis=-1, keepdims=True)
    zc = z - mu
    var = jnp.mean(zc * zc, axis=-1, keepdims=True)
    return zc * lax.rsqrt(var + LN_EPS) * g + b


def _silu(g):
    return g / (1.0 + jnp.exp(-g))


def _store_row_tiles(ref, val):
    n = val.shape[0]
    for j in range(ROW_SUBLANES):
        ref[pl.ds(j, n, stride=ROW_SUBLANES), :] = val[:, j * LANES:(j + 1) * LANES]


def _load_row_tiles(ref):
    n = ref.shape[0] // ROW_SUBLANES
    return jnp.concatenate([ref[pl.ds(j, n, stride=ROW_SUBLANES), :] for j in range(ROW_SUBLANES)], axis=1)


def _mixer_kernel(cdec_ref, x_ref, w_in_ref, w_out_ref, gn_ref, ln_g_ref, ln_b_ref,
                  wr_hi_ref, wr_lo_ref, br_ref, din_ref, qdec_ref, kdec_ref,
                  bprev_ref, bcur_ref, sink_ref, tri_ref,
                  h3_ref, ids_ref, wts_ref, cnt_ref,
                  state_scr, kprev_scr, vprev_scr, o_scr, carry_scr):
    b = pl.program_id(0)
    n = pl.program_id(1)
    ts = x_ref.shape[0]

    @pl.when(n == 0)
    def _():
        state_scr[...] = jnp.zeros_like(state_scr)
        kprev_scr[...] = jnp.zeros_like(kprev_scr)
        vprev_scr[...] = jnp.zeros_like(vprev_scr)

    @pl.when((b == 0) & (n == 0))
    def _():
        carry_scr[...] = jnp.zeros_like(carry_scr)

    x = x_ref[...]
    xb = x.astype(BF16)

    def proj(lo, hi):
        return _dot(xb, w_in_ref[:, lo:hi])

    q_r, k_r, v_r, g_r = proj(_QR, _KR), proj(_KR, _VR), proj(_VR, _GR), proj(_GR, _QA)
    q_a, k_a, v_a = proj(_QA, _KA), proj(_KA, _VA), proj(_VA, IN_WIDTH)
    k_ab = k_a.astype(BF16)
    v_ab = v_a.astype(BF16)
    first_pen = jnp.where(n == 0, NEG, 0.0).astype(F32)

    for c in range(ts // CHUNK):
        rs = slice(c * CHUNK, (c + 1) * CHUNK)
        for hd in range(RET_HEADS):
            cs = slice(hd * RET_HEAD_DIM, (hd + 1) * RET_HEAD_DIM)
            q = q_r[rs, cs]
            k = k_r[rs, cs]
            v = v_r[rs, cs].astype(BF16)
            scores = _dot_nt(q.astype(BF16), k.astype(BF16)) * din_ref[hd]
            inner = _dot(scores.astype(BF16), v)
            st = state_scr[hd]
            cross = _dot((q * qdec_ref[hd]).astype(BF16), st.astype(BF16))
            kv = _dot_tn((k * kdec_ref[hd]).astype(BF16), v)
            state_scr[hd] = st * cdec_ref[hd] + kv
            o = inner + cross
            mu = jnp.mean(o, axis=-1, keepdims=True)
            oc = o - mu
            var = jnp.mean(oc * oc, axis=-1, keepdims=True)
            on = oc * lax.rsqrt(var + GN_EPS) * gn_ref[:, cs] * _silu(g_r[rs, cs])
            o_scr[rs, cs] = on.astype(BF16)
        for j in range(SWA_KV_HEADS):
            ks = slice(j * SWA_HEAD_DIM, (j + 1) * SWA_HEAD_DIM)
            kc = k_ab[rs, ks]
            vc = v_ab[rs, ks]
            if c == 0:
                kp = kprev_scr[:, ks]
                vp = vprev_scr[:, ks]
            else:
                ps = slice((c - 1) * CHUNK, c * CHUNK)
                kp = k_ab[ps, ks]
                vp = v_ab[ps, ks]
            q0 = j * SWA_GROUP * SWA_HEAD_DIM
            qs = jnp.concatenate(
                [q_a[rs, q0 + g * SWA_HEAD_DIM:q0 + (g + 1) * SWA_HEAD_DIM] for g in range(SWA_GROUP)],
                axis=0)
            qs = (qs * (SWA_HEAD_DIM ** -0.5)).astype(BF16)
            s_p = _dot_nt(qs, kp) + bprev_ref[j]
            if c == 0:
                s_p = s_p + first_pen
            s_c = _dot_nt(qs, kc) + bcur_ref[j]
            sink = sink_ref[j]
            m = jnp.maximum(jnp.maximum(jnp.max(s_p, axis=-1, keepdims=True),
                                        jnp.max(s_c, axis=-1, keepdims=True)), sink)
            p_p = jnp.exp(s_p - m)
            p_c = jnp.exp(s_c - m)
            den = (jnp.sum(p_p, axis=-1, keepdims=True) + jnp.sum(p_c, axis=-1, keepdims=True)
                   + jnp.exp(sink - m))
            o = (_dot(p_p.astype(BF16), vp) + _dot(p_c.astype(BF16), vc)) / den
            o = jnp.concatenate([o[g * CHUNK:(g + 1) * CHUNK] for g in range(SWA_GROUP)], axis=1)
            c0 = RET_WIDTH + j * SWA_GROUP * SWA_HEAD_DIM
            o_scr[rs, c0:c0 + SWA_GROUP * SWA_HEAD_DIM] = o.astype(BF16)

    kprev_scr[...] = k_ab[ts - CHUNK:, :]
    vprev_scr[...] = v_ab[ts - CHUNK:, :]

    mix = _dot(o_scr[...], w_out_ref[...])
    h = _layer_norm(DEEPNORM_ALPHA * x + mix, ln_g_ref[...], ln_b_ref[...])
    _store_row_tiles(h3_ref, h)

    h_hi = h.astype(BF16)
    h_lo = (h - h_hi.astype(F32)).astype(BF16)
    logits = (_dot(h_hi, wr_hi_ref[...]) + _dot(h_lo, wr_hi_ref[...]) + _dot(h_hi, wr_lo_ref[...])
              + br_ref[...])
    lane = lax.broadcasted_iota(jnp.int32, (ts, LANES), 1).astype(F32)
    big = 1e9
    ninf = -jnp.inf
    gmask = (lane >= GROUP_LANE0) & (lane < GROUP_LANE0 + N_GROUPS)
    gl = jnp.where(gmask, logits, ninf)
    gmax = jnp.max(gl, axis=-1, keepdims=True)
    gidx = jnp.min(jnp.where(gl == gmax, lane, big), axis=-1, keepdims=True) - GROUP_LANE0
    g_w = 1.0 / jnp.sum(jnp.exp(gl - gmax), axis=-1, keepdims=True)
    lo = gidx * EXPERTS_PER_GROUP
    el = jnp.where((lane >= lo) & (lane < lo + EXPERTS_PER_GROUP), logits, ninf)
    m1 = jnp.max(el, axis=-1, keepdims=True)
    i1 = jnp.min(jnp.where(el == m1, lane, big), axis=-1, keepdims=True)
    el2 = jnp.where(lane == i1, ninf, el)
    m2 = jnp.max(el2, axis=-1, keepdims=True)
    i2 = jnp.min(jnp.where(el2 == m2, lane, big), axis=-1, keepdims=True)
    t = jnp.exp(m2 - m1)
    w1 = g_w / (1.0 + t)
    w2 = g_w * t / (1.0 + t)
    hit1 = lane == i1
    hit2 = lane == i2
    onehot = (hit1 | hit2).astype(BF16)
    prefix = _dot(tri_ref[...], onehot) + carry_scr[0:1, :]
    r1 = jnp.sum(jnp.where(hit1, prefix, 0.0), axis=-1, keepdims=True)
    r2 = jnp.sum(jnp.where(hit2, prefix, 0.0), axis=-1, keepdims=True)
    carry = carry_scr[0:1, :] + jnp.sum(onehot.astype(F32), axis=0, keepdims=True)
    carry_scr[0:1, :] = carry
    cnt_ref[...] = jnp.broadcast_to(carry, cnt_ref.shape)
    idsf = jnp.where(lane == 0, i1, jnp.where(lane == 1, i2,
                     jnp.where(lane == 2, r1, jnp.where(lane == 3, r2, 0.0))))
    ids_ref[...] = idsf.astype(jnp.int32)
    wts_ref[...] = jnp.where(lane == 0, w1, jnp.where(lane == 1, w2, 0.0))


def _mixer_call(x2, w_in, w_out, gn, ln_g, ln_b, wr_hi, wr_lo, br, consts, bsz, s_len):
    cdec, din, qdec, kdec, bprev, bcur, sink, tri = consts
    t_len = bsz * s_len
    ns = s_len // SEQ_TILE
    tok = lambda b, n, *_: (b * ns + n, 0)
    full2 = lambda b, n, *_: (0, 0)
    full3 = lambda b, n, *_: (0, 0, 0)
    grid_spec = pltpu.PrefetchScalarGridSpec(
        num_scalar_prefetch=1,
        grid=(bsz, ns),
        in_specs=[
            pl.BlockSpec((SEQ_TILE, D_MODEL), tok),
            pl.BlockSpec((D_MODEL, IN_WIDTH), full2),
            pl.BlockSpec((D_MODEL, D_MODEL), full2),
            pl.BlockSpec((1, RET_WIDTH), full2),
            pl.BlockSpec((1, D_MODEL), full2),
            pl.BlockSpec((1, D_MODEL), full2),
            pl.BlockSpec((D_MODEL, LANES), full2),
            pl.BlockSpec((D_MODEL, LANES), full2),
            pl.BlockSpec((1, LANES), full2),
            pl.BlockSpec((RET_HEADS, CHUNK, CHUNK), full3),
            pl.BlockSpec((RET_HEADS, CHUNK, RET_HEAD_DIM), full3),
            pl.BlockSpec((RET_HEADS, CHUNK, RET_HEAD_DIM), full3),
            pl.BlockSpec((SWA_KV_HEADS, SWA_GROUP * CHUNK, CHUNK), full3),
            pl.BlockSpec((SWA_KV_HEADS, SWA_GROUP * CHUNK, CHUNK), full3),
            pl.BlockSpec((SWA_KV_HEADS, SWA_GROUP * CHUNK, 1), full3),
            pl.BlockSpec((SEQ_TILE, SEQ_TILE), full2),
        ],
        out_specs=[
            pl.BlockSpec((SEQ_TILE * ROW_SUBLANES, LANES), tok),
            pl.BlockSpec((SEQ_TILE, LANES), tok),
            pl.BlockSpec((SEQ_TILE, LANES), tok),
            pl.BlockSpec((8, LANES), full2),
        ],
        scratch_shapes=[
            pltpu.VMEM((RET_HEADS, RET_HEAD_DIM, RET_HEAD_DIM), F32),
            pltpu.VMEM((CHUNK, SWA_KV_WIDTH), BF16),
            pltpu.VMEM((CHUNK, SWA_KV_WIDTH), BF16),
            pltpu.VMEM((SEQ_TILE, D_MODEL), BF16),
            pltpu.VMEM((8, LANES), F32),
        ],
    )
    return pl.pallas_call(
        _mixer_kernel,
        grid_spec=grid_spec,
        out_shape=[
            jax.ShapeDtypeStruct((t_len * ROW_SUBLANES, LANES), F32),
            jax.ShapeDtypeStruct((t_len, LANES), jnp.int32),
            jax.ShapeDtypeStruct((t_len, LANES), F32),
            jax.ShapeDtypeStruct((8, LANES), F32),
        ],
        compiler_params=pltpu.CompilerParams(
            dimension_semantics=("arbitrary", "arbitrary"), vmem_limit_bytes=VMEM_LIMIT),
        name="mixer_router",
    )(cdec, x2, w_in, w_out, gn, ln_g, ln_b, wr_hi, wr_lo, br, din, qdec, kdec, bprev, bcur, sink, tri)


def _sc_mesh():
    return plsc.VectorSubcoreMesh(core_axis_name="core", subcore_axis_name="subcore")


def _sc_gather_rows(table, idx, name):
    m = idx.shape[0]
    per_worker = m // (SC_CHUNK_ROWS * SC_WORKERS)
    assert per_worker * SC_CHUNK_ROWS * SC_WORKERS == m

    @functools.partial(
        pl.kernel, mesh=_sc_mesh(), name=name,
        out_type=jax.ShapeDtypeStruct((m, ROW_SUBLANES, LANES), F32),
        scratch_types=[pltpu.VMEM((m,), jnp.int32),
                       pltpu.VMEM((SC_CHUNK_ROWS, ROW_SUBLANES, LANES), F32),
                       pltpu.SemaphoreType.DMA])
    def gather(table_hbm, idx_hbm, out_hbm, idx_v, buf, sem):
        wid = lax.axis_index("subcore") * SC_CORES + lax.axis_index("core")
        pltpu.sync_copy(idx_hbm, idx_v)

        @pl.loop(0, per_worker)
        def _(j):
            off = pl.multiple_of((j * SC_WORKERS + wid) * SC_CHUNK_ROWS, SC_CHUNK_ROWS)
            pltpu.async_copy(table_hbm.at[idx_v.at[pl.ds(off, SC_CHUNK_ROWS)]], buf, sem).wait()
            pltpu.sync_copy(buf, out_hbm.at[pl.ds(off, SC_CHUNK_ROWS)])

    return gather(table, idx)


def _expert_kernel(bexp_ref, nused_ref, rows_ref, wg_ref, wu_ref, wd_ref, y_ref, wg_s, wu_s, wd_s):
    p = pl.program_id(0)
    prev = bexp_ref[jnp.maximum(p - 1, 0)]

    @pl.when((p == 0) | (bexp_ref[p] != prev))
    def _():
        wg_s[...] = wg_ref[0].astype(BF16)
        wu_s[...] = wu_ref[0].astype(BF16)
        wd_s[...] = wd_ref[0].astype(BF16)

    @pl.when(p < nused_ref[0])
    def _():
        xb = _load_row_tiles(rows_ref).astype(BF16)
        g = _dot(xb, wg_s[...])
        u = _dot(xb, wu_s[...])
        a = (_silu(g) * u).astype(BF16)
        _store_row_tiles(y_ref, _dot(a, wd_s[...]))

    @pl.when(p >= nused_ref[0])
    def _():
        y_ref[...] = jnp.zeros_like(y_ref)


def _expert_call(bexp, nused, rows, w_gate, w_up, w_down):
    n_rows = rows.shape[0] // ROW_SUBLANES
    n_blocks = n_rows // MOE_ROWS
    blk = lambda p, be, nu: (jnp.minimum(p, nu[0] - 1), 0)
    wsel = lambda p, be, nu: (be[p], 0, 0)
    grid_spec = pltpu.PrefetchScalarGridSpec(
        num_scalar_prefetch=2,
        grid=(n_blocks,),
        in_specs=[
            pl.BlockSpec((MOE_ROWS * ROW_SUBLANES, LANES), blk),
            pl.BlockSpec((1, D_MODEL, D_EXPERT), wsel),
            pl.BlockSpec((1, D_MODEL, D_EXPERT), wsel),
            pl.BlockSpec((1, D_EXPERT, D_MODEL), wsel),
        ],
        out_specs=pl.BlockSpec((MOE_ROWS * ROW_SUBLANES, LANES), lambda p, be, nu: (p, 0)),
        scratch_shapes=[
            pltpu.VMEM((D_MODEL, D_EXPERT), BF16),
            pltpu.VMEM((D_MODEL, D_EXPERT), BF16),
            pltpu.VMEM((D_EXPERT, D_MODEL), BF16),
        ],
    )
    return pl.pallas_call(
        _expert_kernel,
        grid_spec=grid_spec,
        out_shape=jax.ShapeDtypeStruct((n_rows * ROW_SUBLANES, LANES), F32),
        compiler_params=pltpu.CompilerParams(
            dimension_semantics=("arbitrary",), vmem_limit_bytes=VMEM_LIMIT),
        name="moe_experts",
    )(bexp, nused, rows, w_gate, w_up, w_down)


def _combine_kernel(h_ref, y0_ref, y1_ref, wts_ref, ln_g_ref, ln_b_ref, out_ref):
    wts = wts_ref[...]
    ffn = _load_row_tiles(y0_ref) * wts[:, 0:1] + _load_row_tiles(y1_ref) * wts[:, 1:2]
    out_ref[...] = _layer_norm(DEEPNORM_ALPHA * _load_row_tiles(h_ref) + ffn, ln_g_ref[...], ln_b_ref[...])


def _combine_call(h_rows, yk, wts, ln_g, ln_b):
    t_len = wts.shape[0]
    n_tiles = t_len // COMBINE_TILE
    tok = lambda i: (i, 0)
    full2 = lambda i: (0, 0)
    tiles = pl.BlockSpec((COMBINE_TILE * ROW_SUBLANES, LANES), tok)
    tiles_k1 = pl.BlockSpec((COMBINE_TILE * ROW_SUBLANES, LANES), lambda i: (i + n_tiles, 0))
    return pl.pallas_call(
        _combine_kernel,
        grid=(n_tiles,),
        in_specs=[tiles, tiles, tiles_k1,
                  pl.BlockSpec((COMBINE_TILE, LANES), tok),
                  pl.BlockSpec((1, D_MODEL), full2),
                  pl.BlockSpec((1, D_MODEL), full2)],
        out_specs=pl.BlockSpec((COMBINE_TILE, D_MODEL), tok),
        out_shape=jax.ShapeDtypeStruct((t_len, D_MODEL), F32),
        compiler_params=pltpu.CompilerParams(
            dimension_semantics=("arbitrary",), vmem_limit_bytes=VMEM_LIMIT),
        name="moe_combine_ln",
    )(h_rows, yk, yk, wts, ln_g, ln_b)


def _mixer_constants(attn_sinks):
    c = CHUNK
    log_g = jnp.log1p(-jnp.exp2(-5.0 - jnp.arange(RET_HEADS, dtype=F32)))
    idx = jnp.arange(c, dtype=F32)
    diff = idx[:, None] - idx[None, :]
    scale = RET_HEAD_DIM ** -0.5
    din = jnp.where(diff >= 0, jnp.exp(log_g[:, None, None] * jnp.maximum(diff, 0.0)), 0.0) * scale
    qdec = jnp.broadcast_to(jnp.exp(log_g[:, None] * (idx + 1.0))[:, :, None], (RET_HEADS, c, RET_HEAD_DIM))
    kdec = jnp.broadcast_to((jnp.exp(log_g[:, None] * (c - 1.0 - idx)) * scale)[:, :, None],
                            (RET_HEADS, c, RET_HEAD_DIM))
    cdec = jnp.exp(log_g * c)
    slopes = jnp.exp2(-8.0 * (jnp.arange(SWA_HEADS, dtype=F32) + 1.0) / SWA_HEADS)
    r = jnp.arange(c)[:, None]
    col = jnp.arange(c)[None, :]
    dist_prev = (r - col + c).astype(F32)
    dist_cur = (r - col).astype(F32)
    bprev = jnp.where((r < col)[None], -slopes[:, None, None] * dist_prev[None], NEG)
    bcur = jnp.where((r >= col)[None], -slopes[:, None, None] * dist_cur[None], NEG)
    bprev = bprev.reshape(SWA_KV_HEADS, SWA_GROUP * c, c)
    bcur = bcur.reshape(SWA_KV_HEADS, SWA_GROUP * c, c)
    sink = jnp.broadcast_to(attn_sinks.astype(F32)[:, None], (SWA_HEADS, c)).reshape(
        SWA_KV_HEADS, SWA_GROUP * c, 1)
    tr = jnp.arange(SEQ_TILE)
    tri = (tr[None, :] < tr[:, None]).astype(BF16)
    return cdec.astype(F32), din.astype(F32), qdec.astype(F32), kdec.astype(F32), \
        bprev.astype(F32), bcur.astype(F32), sink, tri


def _router_tables(w_group_router, b_group_router, w_expert_router, b_expert_router):
    w_e = jnp.transpose(w_expert_router, (1, 0, 2)).reshape(D_MODEL, N_EXPERTS)
    w = jnp.concatenate([w_e, w_group_router,
                         jnp.zeros((D_MODEL, LANES - N_EXPERTS - N_GROUPS), F32)], axis=1)
    bias = jnp.concatenate([b_expert_router.reshape(N_EXPERTS), b_group_router,
                            jnp.zeros((LANES - N_EXPERTS - N_GROUPS,), F32)])[None, :]
    w_hi = w.astype(BF16)
    w_lo = (w - w_hi.astype(F32)).astype(BF16)
    return w_hi, w_lo, bias


def kernel(x, w_in, ret_gn_g, attn_sinks, w_out, ln1_g, ln1_b, w_group_router, b_group_router,
           w_expert_router, b_expert_router, w_gate, w_up, w_down, ln2_g, ln2_b):
    bsz, s_len, d = x.shape
    assert d == D_MODEL and s_len % SEQ_TILE == 0 and w_in.shape[0] == DEPTH == 1
    t_len = bsz * s_len
    n_blocks = t_len * TOP_K // MOE_ROWS + N_EXPERTS
    n_rows = n_blocks * MOE_ROWS

    consts = _mixer_constants(attn_sinks[0])
    wr_hi, wr_lo, br = _router_tables(w_group_router[0], b_group_router[0],
                                      w_expert_router[0], b_expert_router[0])
    h_rows, ids, wts, cnt = _mixer_call(
        x.reshape(t_len, d), w_in[0].astype(BF16), w_out[0].astype(BF16), ret_gn_g[0][None, :],
        ln1_g[0][None, :], ln1_b[0][None, :], wr_hi, wr_lo, br, consts, bsz, s_len)

    counts = cnt[0, :N_EXPERTS].astype(jnp.int32)
    padded = (counts + MOE_ROWS - 1) // MOE_ROWS * MOE_ROWS
    pend = jnp.cumsum(padded)
    pstart = pend - padded
    onehot = ids[:, 0:TOP_K, None] == jnp.arange(N_EXPERTS, dtype=jnp.int32)
    dest = jnp.sum(jnp.where(onehot, pstart, 0), axis=-1) + ids[:, TOP_K:2 * TOP_K]
    src = jnp.zeros((n_rows,), jnp.int32).at[dest.reshape(-1)].set(
        jnp.arange(t_len * TOP_K, dtype=jnp.int32) // TOP_K)
    nused = (pend[-1:] // MOE_ROWS).astype(jnp.int32)
    blk_start = jnp.minimum(jnp.arange(n_blocks, dtype=jnp.int32), nused[0] - 1) * MOE_ROWS
    bexp = jnp.minimum(jnp.sum(pend[None, :] <= blk_start[:, None], axis=-1), N_EXPERTS - 1).astype(jnp.int32)

    rows = _sc_gather_rows(h_rows.reshape(t_len, ROW_SUBLANES, LANES), src, "moe_dispatch_sc")
    y = _expert_call(bexp, nused, rows.reshape(n_rows * ROW_SUBLANES, LANES), w_gate[0], w_up[0], w_down[0])
    yk = _sc_gather_rows(y.reshape(n_rows, ROW_SUBLANES, LANES), dest.T.reshape(-1), "moe_combine_sc")
    out = _combine_call(h_rows, yk.reshape(TOP_K * t_len * ROW_SUBLANES, LANES), wts,
                        ln2_g[0][None, :], ln2_b[0][None, :])
    return out.reshape(bsz, s_len, d)
```

```python
import functools

import jax
import jax.numpy as jnp
from jax import lax
from jax.experimental import pallas as pl
from jax.experimental.pallas import tpu as pltpu
from jax.experimental.pallas import tpu_sc as plsc

F32 = jnp.float32
BF16 = jnp.bfloat16

D_MODEL = 1024
RET_HEADS = 4
RET_HEAD_DIM = 128
RET_WIDTH = RET_HEADS * RET_HEAD_DIM
CHUNK = 128
SWA_HEADS = 8
SWA_KV_HEADS = 2
SWA_GROUP = SWA_HEADS // SWA_KV_HEADS
SWA_HEAD_DIM = 64
SWA_WIDTH = SWA_HEADS * SWA_HEAD_DIM
SWA_KV_WIDTH = SWA_KV_HEADS * SWA_HEAD_DIM
IN_WIDTH = 4 * RET_WIDTH + SWA_WIDTH + 2 * SWA_KV_WIDTH
N_GROUPS = 4
EXPERTS_PER_GROUP = 8
N_EXPERTS = N_GROUPS * EXPERTS_PER_GROUP
TOP_K = 2
D_EXPERT = 512
LN_EPS = 1e-5
GN_EPS = 1e-6
DEPTH = 1
DEEPNORM_ALPHA = (2 * DEPTH) ** 0.25
NEG = -1e30

LANES = 128
ROW_SUBLANES = D_MODEL // LANES
SEQ_TILE = 256
MOE_ROWS = 256
COMBINE_TILE = 256
SC_CORES = 2
SC_SUBCORES = 16
SC_WORKERS = SC_CORES * SC_SUBCORES
SC_LANES = 16
SC_CHUNK_ROWS = 32
VMEM_LIMIT = 56 * 1024 * 1024

_QR, _KR, _VR, _GR = 0, RET_WIDTH, 2 * RET_WIDTH, 3 * RET_WIDTH
_QA = 4 * RET_WIDTH
_KA = _QA + SWA_WIDTH
_VA = _KA + SWA_KV_WIDTH
GROUP_LANE0 = N_EXPERTS


def _dot(a, b):
    return jnp.dot(a, b, preferred_element_type=F32)


def _dot_nt(a, b):
    return lax.dot_general(a, b, (((1,), (1,)), ((), ())), preferred_element_type=F32)


def _dot_tn(a, b):
    return lax.dot_general(a, b, (((0,), (0,)), ((), ())), preferred_element_type=F32)


def _layer_norm(z, g, b):
    mu = jnp.mean(z, axis=-1, keepdims=True)
    zc = z - mu
    var = jnp.mean(zc * zc, axis=-1, keepdims=True)
    return zc * lax.rsqrt(var + LN_EPS) * g + b


def _silu(g):
    return g / (1.0 + jnp.exp(-g))


def _store_row_tiles(ref, val):
    n = val.shape[0]
    for j in range(ROW_SUBLANES):
        ref[pl.ds(j, n, stride=ROW_SUBLANES), :] = val[:, j * LANES:(j + 1) * LANES]


def _load_row_tiles(ref):
    n = ref.shape[0] // ROW_SUBLANES
    return jnp.concatenate([ref[pl.ds(j, n, stride=ROW_SUBLANES), :] for j in range(ROW_SUBLANES)], axis=1)


def _mixer_kernel(cdec_ref, x_ref, w_in_ref, w_out_ref, gn_ref, ln_g_ref, ln_b_ref,
                  wr_hi_ref, wr_lo_ref, br_ref, din_ref, qdec_ref, kdec_ref,
                  bprev_ref, bcur_ref, sink_ref, tri_ref,
                  h3_ref, ids_ref, wts_ref, cnt_ref,
                  state_scr, kprev_scr, vprev_scr, o_scr, carry_scr):
    b = pl.program_id(0)
    n = pl.program_id(1)
    ts = x_ref.shape[0]

    @pl.when(n == 0)
    def _():
        state_scr[...] = jnp.zeros_like(state_scr)
        kprev_scr[...] = jnp.zeros_like(kprev_scr)
        vprev_scr[...] = jnp.zeros_like(vprev_scr)

    @pl.when((b == 0) & (n == 0))
    def _():
        carry_scr[...] = jnp.zeros_like(carry_scr)

    x = x_ref[...]
    xb = x.astype(BF16)

    def proj(lo, hi):
        return _dot(xb, w_in_ref[:, lo:hi])

    q_r, k_r, v_r, g_r = proj(_QR, _KR), proj(_KR, _VR), proj(_VR, _GR), proj(_GR, _QA)
    q_a, k_a, v_a = proj(_QA, _KA), proj(_KA, _VA), proj(_VA, IN_WIDTH)
    k_ab = k_a.astype(BF16)
    v_ab = v_a.astype(BF16)
    first_pen = jnp.where(n == 0, NEG, 0.0).astype(F32)

    for c in range(ts // CHUNK):
        rs = slice(c * CHUNK, (c + 1) * CHUNK)
        for hd in range(RET_HEADS):
            cs = slice(hd * RET_HEAD_DIM, (hd + 1) * RET_HEAD_DIM)
            q = q_r[rs, cs]
            k = k_r[rs, cs]
            v = v_r[rs, cs].astype(BF16)
            scores = _dot_nt(q.astype(BF16), k.astype(BF16)) * din_ref[hd]
            inner = _dot(scores.astype(BF16), v)
            st = state_scr[hd]
            cross = _dot((q * qdec_ref[hd]).astype(BF16), st.astype(BF16))
            kv = _dot_tn((k * kdec_ref[hd]).astype(BF16), v)
            state_scr[hd] = st * cdec_ref[hd] + kv
            o = inner + cross
            mu = jnp.mean(o, axis=-1, keepdims=True)
            oc = o - mu
            var = jnp.mean(oc * oc, axis=-1, keepdims=True)
            on = oc * lax.rsqrt(var + GN_EPS) * gn_ref[:, cs] * _silu(g_r[rs, cs])
            o_scr[rs, cs] = on.astype(BF16)
        for j in range(SWA_KV_HEADS):
            ks = slice(j * SWA_HEAD_DIM, (j + 1) * SWA_HEAD_DIM)
            kc = k_ab[rs, ks]
            vc = v_ab[rs, ks]
            if c == 0:
                kp = kprev_scr[:, ks]
                vp = vprev_scr[:, ks]
            else:
                ps = slice((c - 1) * CHUNK, c * CHUNK)
                kp = k_ab[ps, ks]
                vp = v_ab[ps, ks]
            q0 = j * SWA_GROUP * SWA_HEAD_DIM
            qs = jnp.concatenate(
                [q_a[rs, q0 + g * SWA_HEAD_DIM:q0 + (g + 1) * SWA_HEAD_DIM] for g in range(SWA_GROUP)],
                axis=0)
            qs = (qs * (SWA_HEAD_DIM ** -0.5)).astype(BF16)
            s_p = _dot_nt(qs, kp) + bprev_ref[j]
            if c == 0:
                s_p = s_p + first_pen
            s_c = _dot_nt(qs, kc) + bcur_ref[j]
            sink = sink_ref[j]
            m = jnp.maximum(jnp.maximum(jnp.max(s_p, axis=-1, keepdims=True),
                                        jnp.max(s_c, axis=-1, keepdims=True)), sink)
            p_p = jnp.exp(s_p - m)
            p_c = jnp.exp(s_c - m)
            den = (jnp.sum(p_p, axis=-1, keepdims=True) + jnp.sum(p_c, axis=-1, keepdims=True)
                   + jnp.exp(sink - m))
            o = (_dot(p_p.astype(BF16), vp) + _dot(p_c.astype(BF16), vc)) / den
            o = jnp.concatenate([o[g * CHUNK:(g + 1) * CHUNK] for g in range(SWA_GROUP)], axis=1)
            c0 = RET_WIDTH + j * SWA_GROUP * SWA_HEAD_DIM
            o_scr[rs, c0:c0 + SWA_GROUP * SWA_HEAD_DIM] = o.astype(BF16)

    kprev_scr[...] = k_ab[ts - CHUNK:, :]
    vprev_scr[...] = v_ab[ts - CHUNK:, :]

    mix = _dot(o_scr[...], w_out_ref[...])
    h = _layer_norm(DEEPNORM_ALPHA * x + mix, ln_g_ref[...], ln_b_ref[...])
    _store_row_tiles(h3_ref, h)

    h_hi = h.astype(BF16)
    h_lo = (h - h_hi.astype(F32)).astype(BF16)
    logits = (_dot(h_hi, wr_hi_ref[...]) + _dot(h_lo, wr_hi_ref[...]) + _dot(h_hi, wr_lo_ref[...])
              + br_ref[...])
    lane = lax.broadcasted_iota(jnp.int32, (ts, LANES), 1).astype(F32)
    big = 1e9
    ninf = -jnp.inf
    gmask = (lane >= GROUP_LANE0) & (lane < GROUP_LANE0 + N_GROUPS)
    gl = jnp.where(gmask, logits, ninf)
    gmax = jnp.max(gl, axis=-1, keepdims=True)
    gidx = jnp.min(jnp.where(gl == gmax, lane, big), axis=-1, keepdims=True) - GROUP_LANE0
    g_w = 1.0 / jnp.sum(jnp.exp(gl - gmax), axis=-1, keepdims=True)
    lo = gidx * EXPERTS_PER_GROUP
    el = jnp.where((lane >= lo) & (lane < lo + EXPERTS_PER_GROUP), logits, ninf)
    m1 = jnp.max(el, axis=-1, keepdims=True)
    i1 = jnp.min(jnp.where(el == m1, lane, big), axis=-1, keepdims=True)
    el2 = jnp.where(lane == i1, ninf, el)
    m2 = jnp.max(el2, axis=-1, keepdims=True)
    i2 = jnp.min(jnp.where(el2 == m2, lane, big), axis=-1, keepdims=True)
    t = jnp.exp(m2 - m1)
    w1 = g_w / (1.0 + t)
    w2 = g_w * t / (1.0 + t)
    hit1 = lane == i1
    hit2 = lane == i2
    onehot = (hit1 | hit2).astype(BF16)
    prefix = _dot(tri_ref[...], onehot) + carry_scr[0:1, :]
    r1 = jnp.sum(jnp.where(hit1, prefix, 0.0), axis=-1, keepdims=True)
    r2 = jnp.sum(jnp.where(hit2, prefix, 0.0), axis=-1, keepdims=True)
    carry = carry_scr[0:1, :] + jnp.sum(onehot.astype(F32), axis=0, keepdims=True)
    carry_scr[0:1, :] = carry
    cnt_ref[...] = jnp.broadcast_to(carry, cnt_ref.shape)
    idsf = jnp.where(lane == 0, i1, jnp.where(lane == 1, i2,
                     jnp.where(lane == 2, r1, jnp.where(lane == 3, r2, 0.0))))
    ids_ref[...] = idsf.astype(jnp.int32)
    wts_ref[...] = jnp.where(lane == 0, w1, jnp.where(lane == 1, w2, 0.0))


def _mixer_call(x2, w_in, w_out, gn, ln_g, ln_b, wr_hi, wr_lo, br, consts, bsz, s_len):
    cdec, din, qdec, kdec, bprev, bcur, sink, tri = consts
    t_len = bsz * s_len
    ns = s_len // SEQ_TILE
    tok = lambda b, n, *_: (b * ns + n, 0)
    full2 = lambda b, n, *_: (0, 0)
    full3 = lambda b, n, *_: (0, 0, 0)
    grid_spec = pltpu.PrefetchScalarGridSpec(
        num_scalar_prefetch=1,
        grid=(bsz, ns),
        in_specs=[
            pl.BlockSpec((SEQ_TILE, D_MODEL), tok),
            pl.BlockSpec((D_MODEL, IN_WIDTH), full2),
            pl.BlockSpec((D_MODEL, D_MODEL), full2),
            pl.BlockSpec((1, RET_WIDTH), full2),
            pl.BlockSpec((1, D_MODEL), full2),
            pl.BlockSpec((1, D_MODEL), full2),
            pl.BlockSpec((D_MODEL, LANES), full2),
            pl.BlockSpec((D_MODEL, LANES), full2),
            pl.BlockSpec((1, LANES), full2),
            pl.BlockSpec((RET_HEADS, CHUNK, CHUNK), full3),
            pl.BlockSpec((RET_HEADS, CHUNK, RET_HEAD_DIM), full3),
            pl.BlockSpec((RET_HEADS, CHUNK, RET_HEAD_DIM), full3),
            pl.BlockSpec((SWA_KV_HEADS, SWA_GROUP * CHUNK, CHUNK), full3),
            pl.BlockSpec((SWA_KV_HEADS, SWA_GROUP * CHUNK, CHUNK), full3),
            pl.BlockSpec((SWA_KV_HEADS, SWA_GROUP * CHUNK, 1), full3),
            pl.BlockSpec((SEQ_TILE, SEQ_TILE), full2),
        ],
        out_specs=[
            pl.BlockSpec((SEQ_TILE * ROW_SUBLANES, LANES), tok),
            pl.BlockSpec((SEQ_TILE, LANES), tok),
            pl.BlockSpec((SEQ_TILE, LANES), tok),
            pl.BlockSpec((8, LANES), full2),
        ],
        scratch_shapes=[
            pltpu.VMEM((RET_HEADS, RET_HEAD_DIM, RET_HEAD_DIM), F32),
            pltpu.VMEM((CHUNK, SWA_KV_WIDTH), BF16),
            pltpu.VMEM((CHUNK, SWA_KV_WIDTH), BF16),
            pltpu.VMEM((SEQ_TILE, D_MODEL), BF16),
            pltpu.VMEM((8, LANES), F32),
        ],
    )
    return pl.pallas_call(
        _mixer_kernel,
        grid_spec=grid_spec,
        out_shape=[
            jax.ShapeDtypeStruct((t_len * ROW_SUBLANES, LANES), F32),
            jax.ShapeDtypeStruct((t_len, LANES), jnp.int32),
            jax.ShapeDtypeStruct((t_len, LANES), F32),
            jax.ShapeDtypeStruct((8, LANES), F32),
        ],
        compiler_params=pltpu.CompilerParams(
            dimension_semantics=("arbitrary", "arbitrary"), vmem_limit_bytes=VMEM_LIMIT),
        name="mixer_router",
    )(cdec, x2, w_in, w_out, gn, ln_g, ln_b, wr_hi, wr_lo, br, din, qdec, kdec, bprev, bcur, sink, tri)


def _sc_mesh():
    return plsc.VectorSubcoreMesh(core_axis_name="core", subcore_axis_name="subcore")


def _sc_dispatch(h_rows, dest, n_rows):
    t_len = h_rows.shape[0]
    n_assign = dest.shape[0]
    per_worker = n_rows // SC_WORKERS
    n_chunks = per_worker // SC_CHUNK_ROWS
    assert n_chunks * SC_CHUNK_ROWS * SC_WORKERS == n_rows and n_rows < 3 * t_len
    assert per_worker % SC_LANES == 0 and n_assign % SC_LANES == 0 and n_assign == TOP_K * t_len

    @functools.partial(
        pl.kernel, mesh=_sc_mesh(), name="moe_dispatch_sc",
        compiler_params=pltpu.CompilerParams(needs_layout_passes=False),
        out_type=jax.ShapeDtypeStruct((n_rows, ROW_SUBLANES, LANES), F32),
        scratch_types=[pltpu.VMEM((n_assign,), jnp.int32),
                       pltpu.VMEM((per_worker,), jnp.int32),
                       pltpu.VMEM((SC_CHUNK_ROWS, ROW_SUBLANES, LANES), F32),
                       pltpu.SemaphoreType.DMA])
    def dispatch(h_hbm, dest_hbm, rows_hbm, dest_v, src_v, buf, sem):
        wid = lax.axis_index("subcore") * SC_CORES + lax.axis_index("core")
        base = wid * per_worker
        pltpu.sync_copy(dest_hbm, dest_v)
        lane = lax.iota(jnp.int32, SC_LANES)

        def wrap(a):
            a = jnp.where(a >= t_len, a - t_len, a)
            return jnp.where(a >= t_len, a - t_len, a)

        @pl.loop(0, per_worker // SC_LANES)
        def _(i):
            src_v[pl.ds(i * SC_LANES, SC_LANES)] = wrap(base + i * SC_LANES + lane)

        @pl.loop(0, n_assign // SC_LANES)
        def _(i):
            d = dest_v[pl.ds(i * SC_LANES, SC_LANES)] - base
            hit = (d >= 0) & (d < per_worker)
            plsc.store_scatter(src_v, [jnp.where(hit, d, 0)], wrap(i * SC_LANES + lane), mask=hit)

        @pl.loop(0, n_chunks)
        def _(c):
            off = pl.multiple_of(c * SC_CHUNK_ROWS, SC_CHUNK_ROWS)
            pltpu.async_copy(h_hbm.at[src_v.at[pl.ds(off, SC_CHUNK_ROWS)]], buf, sem).wait()
            pltpu.sync_copy(buf, rows_hbm.at[pl.ds(base + off, SC_CHUNK_ROWS)])

    return dispatch(h_rows, dest)


def _sc_gather_rows(table, idx, name):
    m = idx.shape[0]
    per_worker = m // (SC_CHUNK_ROWS * SC_WORKERS)
    assert per_worker * SC_CHUNK_ROWS * SC_WORKERS == m

    @functools.partial(
        pl.kernel, mesh=_sc_mesh(), name=name,
        out_type=jax.ShapeDtypeStruct((m, ROW_SUBLANES, LANES), F32),
        scratch_types=[pltpu.VMEM((m,), jnp.int32),
                       pltpu.VMEM((SC_CHUNK_ROWS, ROW_SUBLANES, LANES), F32),
                       pltpu.SemaphoreType.DMA])
    def gather(table_hbm, idx_hbm, out_hbm, idx_v, buf, sem):
        wid = lax.axis_index("subcore") * SC_CORES + lax.axis_index("core")
        pltpu.sync_copy(idx_hbm, idx_v)

        @pl.loop(0, per_worker)
        def _(j):
            off = pl.multiple_of((j * SC_WORKERS + wid) * SC_CHUNK_ROWS, SC_CHUNK_ROWS)
            pltpu.async_copy(table_hbm.at[idx_v.at[pl.ds(off, SC_CHUNK_ROWS)]], buf, sem).wait()
            pltpu.sync_copy(buf, out_hbm.at[pl.ds(off, SC_CHUNK_ROWS)])

    return gather(table, idx)


def _expert_kernel(bexp_ref, nused_ref, rows_ref, wg_ref, wu_ref, wd_ref, y_ref, wg_s, wu_s, wd_s):
    p = pl.program_id(0)
    prev = bexp_ref[jnp.maximum(p - 1, 0)]

    @pl.when((p == 0) | (bexp_ref[p] != prev))
    def _():
        wg_s[...] = wg_ref[0].astype(BF16)
        wu_s[...] = wu_ref[0].astype(BF16)
        wd_s[...] = wd_ref[0].astype(BF16)

    @pl.when(p < nused_ref[0])
    def _():
        xb = _load_row_tiles(rows_ref).astype(BF16)
        g = _dot(xb, wg_s[...])
        u = _dot(xb, wu_s[...])
        a = (_silu(g) * u).astype(BF16)
        _store_row_tiles(y_ref, _dot(a, wd_s[...]))

    @pl.when(p >= nused_ref[0])
    def _():
        y_ref[...] = jnp.zeros_like(y_ref)


def _expert_call(bexp, nused, rows, w_gate, w_up, w_down):
    n_rows = rows.shape[0] // ROW_SUBLANES
    n_blocks = n_rows // MOE_ROWS
    blk = lambda p, be, nu: (jnp.minimum(p, nu[0] - 1), 0)
    wsel = lambda p, be, nu: (be[p], 0, 0)
    grid_spec = pltpu.PrefetchScalarGridSpec(
        num_scalar_prefetch=2,
        grid=(n_blocks,),
        in_specs=[
            pl.BlockSpec((MOE_ROWS * ROW_SUBLANES, LANES), blk),
            pl.BlockSpec((1, D_MODEL, D_EXPERT), wsel),
            pl.BlockSpec((1, D_MODEL, D_EXPERT), wsel),
            pl.BlockSpec((1, D_EXPERT, D_MODEL), wsel),
        ],
        out_specs=pl.BlockSpec((MOE_ROWS * ROW_SUBLANES, LANES), lambda p, be, nu: (p, 0)),
        scratch_shapes=[
            pltpu.VMEM((D_MODEL, D_EXPERT), BF16),
            pltpu.VMEM((D_MODEL, D_EXPERT), BF16),
            pltpu.VMEM((D_EXPERT, D_MODEL), BF16),
        ],
    )
    return pl.pallas_call(
        _expert_kernel,
        grid_spec=grid_spec,
        out_shape=jax.ShapeDtypeStruct((n_rows * ROW_SUBLANES, LANES), F32),
        compiler_params=pltpu.CompilerParams(
            dimension_semantics=("arbitrary",), vmem_limit_bytes=VMEM_LIMIT),
        name="moe_experts",
    )(bexp, nused, rows, w_gate, w_up, w_down)


def _combine_kernel(h_ref, y0_ref, y1_ref, wts_ref, ln_g_ref, ln_b_ref, out_ref):
    wts = wts_ref[...]
    ffn = _load_row_tiles(y0_ref) * wts[:, 0:1] + _load_row_tiles(y1_ref) * wts[:, 1:2]
    out_ref[...] = _layer_norm(DEEPNORM_ALPHA * _load_row_tiles(h_ref) + ffn, ln_g_ref[...], ln_b_ref[...])


def _combine_call(h_rows, yk, wts, ln_g, ln_b):
    t_len = wts.shape[0]
    n_tiles = t_len // COMBINE_TILE
    tok = lambda i: (i, 0)
    full2 = lambda i: (0, 0)
    tiles = pl.BlockSpec((COMBINE_TILE * ROW_SUBLANES, LANES), tok)
    tiles_k1 = pl.BlockSpec((COMBINE_TILE * ROW_SUBLANES, LANES), lambda i: (i + n_tiles, 0))
    return pl.pallas_call(
        _combine_kernel,
        grid=(n_tiles,),
        in_specs=[tiles, tiles, tiles_k1,
                  pl.BlockSpec((COMBINE_TILE, LANES), tok),
                  pl.BlockSpec((1, D_MODEL), full2),
                  pl.BlockSpec((1, D_MODEL), full2)],
        out_specs=pl.BlockSpec((COMBINE_TILE, D_MODEL), tok),
        out_shape=jax.ShapeDtypeStruct((t_len, D_MODEL), F32),
        compiler_params=pltpu.CompilerParams(
            dimension_semantics=("arbitrary",), vmem_limit_bytes=VMEM_LIMIT),
        name="moe_combine_ln",
    )(h_rows, yk, yk, wts, ln_g, ln_b)


def _mixer_constants(attn_sinks):
    c = CHUNK
    log_g = jnp.log1p(-jnp.exp2(-5.0 - jnp.arange(RET_HEADS, dtype=F32)))
    idx = jnp.arange(c, dtype=F32)
    diff = idx[:, None] - idx[None, :]
    scale = RET_HEAD_DIM ** -0.5
    din = jnp.where(diff >= 0, jnp.exp(log_g[:, None, None] * jnp.maximum(diff, 0.0)), 0.0) * scale
    qdec = jnp.broadcast_to(jnp.exp(log_g[:, None] * (idx + 1.0))[:, :, None], (RET_HEADS, c, RET_HEAD_DIM))
    kdec = jnp.broadcast_to((jnp.exp(log_g[:, None] * (c - 1.0 - idx)) * scale)[:, :, None],
                            (RET_HEADS, c, RET_HEAD_DIM))
    cdec = jnp.exp(log_g * c)
    slopes = jnp.exp2(-8.0 * (jnp.arange(SWA_HEADS, dtype=F32) + 1.0) / SWA_HEADS)
    r = jnp.arange(c)[:, None]
    col = jnp.arange(c)[None, :]
    dist_prev = (r - col + c).astype(F32)
    dist_cur = (r - col).astype(F32)
    bprev = jnp.where((r < col)[None], -slopes[:, None, None] * dist_prev[None], NEG)
    bcur = jnp.where((r >= col)[None], -slopes[:, None, None] * dist_cur[None], NEG)
    bprev = bprev.reshape(SWA_KV_HEADS, SWA_GROUP * c, c)
    bcur = bcur.reshape(SWA_KV_HEADS, SWA_GROUP * c, c)
    sink = jnp.broadcast_to(attn_sinks.astype(F32)[:, None], (SWA_HEADS, c)).reshape(
        SWA_KV_HEADS, SWA_GROUP * c, 1)
    tr = jnp.arange(SEQ_TILE)
    tri = (tr[None, :] < tr[:, None]).astype(BF16)
    return cdec.astype(F32), din.astype(F32), qdec.astype(F32), kdec.astype(F32), \
        bprev.astype(F32), bcur.astype(F32), sink, tri


def _router_tables(w_group_router, b_group_router, w_expert_router, b_expert_router):
    w_e = jnp.transpose(w_expert_router, (1, 0, 2)).reshape(D_MODEL, N_EXPERTS)
    w = jnp.concatenate([w_e, w_group_router,
                         jnp.zeros((D_MODEL, LANES - N_EXPERTS - N_GROUPS), F32)], axis=1)
    bias = jnp.concatenate([b_expert_router.reshape(N_EXPERTS), b_group_router,
                            jnp.zeros((LANES - N_EXPERTS - N_GROUPS,), F32)])[None, :]
    w_hi = w.astype(BF16)
    w_lo = (w - w_hi.astype(F32)).astype(BF16)
    return w_hi, w_lo, bias


def kernel(x, w_in, ret_gn_g, attn_sinks, w_out, ln1_g, ln1_b, w_group_router, b_group_router,
           w_expert_router, b_expert_router, w_gate, w_up, w_down, ln2_g, ln2_b):
    bsz, s_len, d = x.shape
    assert d == D_MODEL and s_len % SEQ_TILE == 0 and w_in.shape[0] == DEPTH == 1
    t_len = bsz * s_len
    n_blocks = t_len * TOP_K // MOE_ROWS + N_EXPERTS
    n_rows = n_blocks * MOE_ROWS

    consts = _mixer_constants(attn_sinks[0])
    wr_hi, wr_lo, br = _router_tables(w_group_router[0], b_group_router[0],
                                      w_expert_router[0], b_expert_router[0])
    h_rows, ids, wts, cnt = _mixer_call(
        x.reshape(t_len, d), w_in[0].astype(BF16), w_out[0].astype(BF16), ret_gn_g[0][None, :],
        ln1_g[0][None, :], ln1_b[0][None, :], wr_hi, wr_lo, br, consts, bsz, s_len)

    counts = cnt[0, :N_EXPERTS].astype(jnp.int32)
    padded = (counts + MOE_ROWS - 1) // MOE_ROWS * MOE_ROWS
    pend = jnp.cumsum(padded)
    pstart = pend - padded
    onehot = ids[:, 0:TOP_K, None] == jnp.arange(N_EXPERTS, dtype=jnp.int32)
    dest = jnp.sum(jnp.where(onehot, pstart, 0), axis=-1) + ids[:, TOP_K:2 * TOP_K]
    dest = dest.T.reshape(-1)
    nused = (pend[-1:] // MOE_ROWS).astype(jnp.int32)
    blk_start = jnp.minimum(jnp.arange(n_blocks, dtype=jnp.int32), nused[0] - 1) * MOE_ROWS
    bexp = jnp.minimum(jnp.sum(pend[None, :] <= blk_start[:, None], axis=-1), N_EXPERTS - 1).astype(jnp.int32)

    rows = _sc_dispatch(h_rows.reshape(t_len, ROW_SUBLANES, LANES), dest, n_rows)
    y = _expert_call(bexp, nused, rows.reshape(n_rows * ROW_SUBLANES, LANES), w_gate[0], w_up[0], w_down[0])
    yk = _sc_gather_rows(y.reshape(n_rows, ROW_SUBLANES, LANES), dest, "moe_combine_sc")
    out = _combine_call(h_rows, yk.reshape(TOP_K * t_len * ROW_SUBLANES, LANES), wts,
                        ln2_g[0][None, :], ln2_b[0][None, :])
    return out.reshape(bsz, s_len, d)
```

```python
import functools

import jax
import jax.numpy as jnp
from jax import lax
from jax.experimental import pallas as pl
from jax.experimental.pallas import tpu as pltpu
from jax.experimental.pallas import tpu_sc as plsc

F32 = jnp.float32
BF16 = jnp.bfloat16

D_MODEL = 1024
RET_HEADS = 4
RET_HEAD_DIM = 128
RET_WIDTH = RET_HEADS * RET_HEAD_DIM
CHUNK = 128
SWA_HEADS = 8
SWA_KV_HEADS = 2
SWA_GROUP = SWA_HEADS // SWA_KV_HEADS
SWA_HEAD_DIM = 64
SWA_WIDTH = SWA_HEADS * SWA_HEAD_DIM
SWA_KV_WIDTH = SWA_KV_HEADS * SWA_HEAD_DIM
IN_WIDTH = 4 * RET_WIDTH + SWA_WIDTH + 2 * SWA_KV_WIDTH
N_GROUPS = 4
EXPERTS_PER_GROUP = 8
N_EXPERTS = N_GROUPS * EXPERTS_PER_GROUP
TOP_K = 2
D_EXPERT = 512
LN_EPS = 1e-5
GN_EPS = 1e-6
DEPTH = 1
DEEPNORM_ALPHA = (2 * DEPTH) ** 0.25
NEG = -1e30

LANES = 128
ROW_SUBLANES = D_MODEL // LANES
SEQ_TILE = 512
MOE_ROWS = 256
COMBINE_TILE = 256
SC_CORES = 2
SC_SUBCORES = 16
SC_WORKERS = SC_CORES * SC_SUBCORES
SC_LANES = 16
SC_CHUNK_ROWS = 32
VMEM_LIMIT = 56 * 1024 * 1024

_QR, _KR, _VR, _GR = 0, RET_WIDTH, 2 * RET_WIDTH, 3 * RET_WIDTH
_QA = 4 * RET_WIDTH
_KA = _QA + SWA_WIDTH
_VA = _KA + SWA_KV_WIDTH
SWA_VREP = SWA_GROUP * SWA_HEAD_DIM
IN_WIDTH_TILED = _VA + SWA_KV_HEADS * SWA_VREP
GROUP_LANE0 = N_EXPERTS


def _dot(a, b):
    return jnp.dot(a, b, preferred_element_type=F32)


def _dot_nt(a, b):
    return lax.dot_general(a, b, (((1,), (1,)), ((), ())), preferred_element_type=F32)


def _dot_tn(a, b):
    return lax.dot_general(a, b, (((0,), (0,)), ((), ())), preferred_element_type=F32)


def _layer_norm(z, g, b):
    mu = jnp.mean(z, axis=-1, keepdims=True)
    zc = z - mu
    var = jnp.mean(zc * zc, axis=-1, keepdims=True)
    return zc * lax.rsqrt(var + LN_EPS) * g + b


def _silu(g):
    return g / (1.0 + jnp.exp(-g))


def _store_row_tiles(ref, val):
    n = val.shape[0]
    for j in range(ROW_SUBLANES):
        ref[pl.ds(j, n, stride=ROW_SUBLANES), :] = val[:, j * LANES:(j + 1) * LANES]


def _load_row_tiles(ref):
    n = ref.shape[0] // ROW_SUBLANES
    return jnp.concatenate([ref[pl.ds(j, n, stride=ROW_SUBLANES), :] for j in range(ROW_SUBLANES)], axis=1)


def _mixer_kernel(cdec_ref, x_ref, w_in_ref, w_out_ref, gn_ref, ln_g_ref, ln_b_ref,
                  wr_hi_ref, wr_lo_ref, br_ref, din_ref, qdec_ref, kdec_ref,
                  bias_ref, pen_ref, kmask_ref, vmask_ref, omask_ref, tri_ref,
                  h3_ref, ids_ref, wts_ref, cnt_ref,
                  state_scr, kprev_scr, vprev_scr, o_scr, carry_scr):
    b = pl.program_id(0)
    n = pl.program_id(1)
    ts = x_ref.shape[0]

    @pl.when(n == 0)
    def _():
        state_scr[...] = jnp.zeros_like(state_scr)
        kprev_scr[...] = jnp.zeros_like(kprev_scr)
        vprev_scr[...] = jnp.zeros_like(vprev_scr)

    @pl.when((b == 0) & (n == 0))
    def _():
        carry_scr[...] = jnp.zeros_like(carry_scr)

    x = x_ref[...]
    xb = x.astype(BF16)

    def proj(lo, hi):
        return _dot(xb, w_in_ref[:, lo:hi])

    q_r, k_r, v_r, g_r = proj(_QR, _KR), proj(_KR, _VR), proj(_VR, _GR), proj(_GR, _QA)
    q_a, k_a = proj(_QA, _KA), proj(_KA, _VA)
    k_ab = k_a.astype(BF16)
    v_rep = proj(_VA, _VA + SWA_KV_HEADS * SWA_VREP).astype(BF16)
    first_pen = jnp.where(n == 0, pen_ref[...], 0.0)

    for c in range(ts // CHUNK):
        rs = slice(c * CHUNK, (c + 1) * CHUNK)
        for hd in range(RET_HEADS):
            cs = slice(hd * RET_HEAD_DIM, (hd + 1) * RET_HEAD_DIM)
            q = q_r[rs, cs]
            k = k_r[rs, cs]
            v = v_r[rs, cs].astype(BF16)
            scores = _dot_nt(q.astype(BF16), k.astype(BF16)) * din_ref[hd]
            inner = _dot(scores.astype(BF16), v)
            st = state_scr[hd]
            cross = _dot((q * qdec_ref[hd]).astype(BF16), st.astype(BF16))
            kv = _dot_tn((k * kdec_ref[hd]).astype(BF16), v)
            state_scr[hd] = st * cdec_ref[hd] + kv
            o = inner + cross
            mu = jnp.mean(o, axis=-1, keepdims=True)
            oc = o - mu
            var = jnp.mean(oc * oc, axis=-1, keepdims=True)
            on = oc * lax.rsqrt(var + GN_EPS) * gn_ref[:, cs] * _silu(g_r[rs, cs])
            o_scr[rs, cs] = on.astype(BF16)
        for j in range(SWA_KV_HEADS):
            ks = slice(j * SWA_HEAD_DIM, (j + 1) * SWA_HEAD_DIM)
            vs = slice(j * SWA_VREP, (j + 1) * SWA_VREP)
            if c == 0:
                kp = kprev_scr[:, ks].astype(BF16)
                vp = vprev_scr[:, vs].astype(BF16)
            else:
                ps = slice((c - 1) * CHUNK, c * CHUNK)
                kp = k_ab[ps, ks]
                vp = v_rep[ps, vs]
            kp = kp * kmask_ref[...]
            kcat = jnp.concatenate([kp, k_ab[rs, ks]], axis=0)
            vcat = jnp.concatenate([vp, v_rep[rs, vs]], axis=0)
            q0 = j * SWA_GROUP * SWA_HEAD_DIM
            qs = jnp.concatenate(
                [q_a[rs, q0 + g * SWA_HEAD_DIM:q0 + (g + 1) * SWA_HEAD_DIM] for g in range(SWA_GROUP)],
                axis=0)
            qs = (qs * (SWA_HEAD_DIM ** -0.5)).astype(BF16)
            s = _dot_nt(qs, kcat) + bias_ref[j]
            if c == 0:
                s = s + first_pen
            m = jnp.max(jnp.maximum(s[:, :CHUNK], s[:, CHUNK:]), axis=-1, keepdims=True)
            p = jnp.exp(s - m).astype(BF16)
            p_all = jnp.concatenate([p[g * CHUNK:(g + 1) * CHUNK] for g in range(SWA_GROUP)], axis=1)
            v_blk = jnp.concatenate([vcat * vmask_ref[g] for g in range(SWA_GROUP)], axis=0)
            num = _dot(p_all, v_blk)
            den = _dot(p_all, omask_ref[...])
            c0 = RET_WIDTH + j * SWA_GROUP * SWA_HEAD_DIM
            o_scr[rs, c0:c0 + SWA_GROUP * SWA_HEAD_DIM] = (num / den).astype(BF16)

    kprev_scr[...] = k_ab[ts - CHUNK:, :].astype(F32)
    vprev_scr[...] = v_rep[ts - CHUNK:, :].astype(F32)

    mix = _dot(o_scr[...], w_out_ref[...])
    h = _layer_norm(DEEPNORM_ALPHA * x + mix, ln_g_ref[...], ln_b_ref[...])
    _store_row_tiles(h3_ref, h)

    h_hi = h.astype(BF16)
    h_lo = (h - h_hi.astype(F32)).astype(BF16)
    logits = (_dot(h_hi, wr_hi_ref[...]) + _dot(h_lo, wr_hi_ref[...]) + _dot(h_hi, wr_lo_ref[...])
              + br_ref[...])
    lane = lax.broadcasted_iota(jnp.int32, (ts, LANES), 1).astype(F32)
    big = 1e9
    ninf = -jnp.inf
    gmask = (lane >= GROUP_LANE0) & (lane < GROUP_LANE0 + N_GROUPS)
    gl = jnp.where(gmask, logits, ninf)
    gmax = jnp.max(gl, axis=-1, keepdims=True)
    gidx = jnp.min(jnp.where(gl == gmax, lane, big), axis=-1, keepdims=True) - GROUP_LANE0
    g_w = 1.0 / jnp.sum(jnp.exp(gl - gmax), axis=-1, keepdims=True)
    lo = gidx * EXPERTS_PER_GROUP
    el = jnp.where((lane >= lo) & (lane < lo + EXPERTS_PER_GROUP), logits, ninf)
    m1 = jnp.max(el, axis=-1, keepdims=True)
    i1 = jnp.min(jnp.where(el == m1, lane, big), axis=-1, keepdims=True)
    el2 = jnp.where(lane == i1, ninf, el)
    m2 = jnp.max(el2, axis=-1, keepdims=True)
    i2 = jnp.min(jnp.where(el2 == m2, lane, big), axis=-1, keepdims=True)
    t = jnp.exp(m2 - m1)
    w1 = g_w / (1.0 + t)
    w2 = g_w * t / (1.0 + t)
    hit1 = lane == i1
    hit2 = lane == i2
    onehot = (hit1 | hit2).astype(BF16)
    prefix = _dot(tri_ref[...], onehot) + carry_scr[0:1, :]
    r1 = jnp.sum(jnp.where(hit1, prefix, 0.0), axis=-1, keepdims=True)
    r2 = jnp.sum(jnp.where(hit2, prefix, 0.0), axis=-1, keepdims=True)
    carry = carry_scr[0:1, :] + jnp.sum(onehot.astype(F32), axis=0, keepdims=True)
    carry_scr[0:1, :] = carry
    cnt_ref[...] = jnp.broadcast_to(carry, cnt_ref.shape)
    idsf = jnp.where(lane == 0, i1, jnp.where(lane == 1, i2,
                     jnp.where(lane == 2, r1, jnp.where(lane == 3, r2, 0.0))))
    ids_ref[...] = idsf.astype(jnp.int32)
    wts_ref[...] = jnp.where(lane == 0, w1, jnp.where(lane == 1, w2, 0.0))


def _mixer_call(x2, w_in, w_out, gn, ln_g, ln_b, wr_hi, wr_lo, br, consts, bsz, s_len):
    cdec, din, qdec, kdec, bias, pen, kmask, vmask, omask, tri = consts
    t_len = bsz * s_len
    ns = s_len // SEQ_TILE
    tok = lambda b, n, *_: (b * ns + n, 0)
    full2 = lambda b, n, *_: (0, 0)
    full3 = lambda b, n, *_: (0, 0, 0)
    grid_spec = pltpu.PrefetchScalarGridSpec(
        num_scalar_prefetch=1,
        grid=(bsz, ns),
        in_specs=[
            pl.BlockSpec((SEQ_TILE, D_MODEL), tok),
            pl.BlockSpec((D_MODEL, IN_WIDTH_TILED), full2),
            pl.BlockSpec((D_MODEL, D_MODEL), full2),
            pl.BlockSpec((1, RET_WIDTH), full2),
            pl.BlockSpec((1, D_MODEL), full2),
            pl.BlockSpec((1, D_MODEL), full2),
            pl.BlockSpec((D_MODEL, LANES), full2),
            pl.BlockSpec((D_MODEL, LANES), full2),
            pl.BlockSpec((1, LANES), full2),
            pl.BlockSpec((RET_HEADS, CHUNK, CHUNK), full3),
            pl.BlockSpec((RET_HEADS, CHUNK, RET_HEAD_DIM), full3),
            pl.BlockSpec((RET_HEADS, CHUNK, RET_HEAD_DIM), full3),
            pl.BlockSpec((SWA_KV_HEADS, SWA_GROUP * CHUNK, 2 * CHUNK), full3),
            pl.BlockSpec((1, 2 * CHUNK), full2),
            pl.BlockSpec((CHUNK, SWA_HEAD_DIM), full2),
            pl.BlockSpec((SWA_GROUP, 2 * CHUNK, SWA_VREP), full3),
            pl.BlockSpec((SWA_GROUP * 2 * CHUNK, SWA_VREP), full2),
            pl.BlockSpec((SEQ_TILE, SEQ_TILE), full2),
        ],
        out_specs=[
            pl.BlockSpec((SEQ_TILE * ROW_SUBLANES, LANES), tok),
            pl.BlockSpec((SEQ_TILE, LANES), tok),
            pl.BlockSpec((SEQ_TILE, LANES), tok),
            pl.BlockSpec((8, LANES), full2),
        ],
        scratch_shapes=[
            pltpu.VMEM((RET_HEADS, RET_HEAD_DIM, RET_HEAD_DIM), F32),
            pltpu.VMEM((CHUNK, SWA_KV_WIDTH), F32),
            pltpu.VMEM((CHUNK, SWA_KV_HEADS * SWA_VREP), F32),
            pltpu.VMEM((SEQ_TILE, D_MODEL), BF16),
            pltpu.VMEM((8, LANES), F32),
        ],
    )
    return pl.pallas_call(
        _mixer_kernel,
        grid_spec=grid_spec,
        out_shape=[
            jax.ShapeDtypeStruct((t_len * ROW_SUBLANES, LANES), F32),
            jax.ShapeDtypeStruct((t_len, LANES), jnp.int32),
            jax.ShapeDtypeStruct((t_len, LANES), F32),
            jax.ShapeDtypeStruct((8, LANES), F32),
        ],
        compiler_params=pltpu.CompilerParams(
            dimension_semantics=("arbitrary", "arbitrary"), vmem_limit_bytes=VMEM_LIMIT),
        name="mixer_router",
    )(cdec, x2, w_in, w_out, gn, ln_g, ln_b, wr_hi, wr_lo, br, din, qdec, kdec, bias, pen, kmask, vmask, omask, tri)


def _sc_mesh():
    return plsc.VectorSubcoreMesh(core_axis_name="core", subcore_axis_name="subcore")


def _sc_dispatch(h_rows, dest, n_rows):
    t_len = h_rows.shape[0]
    n_assign = dest.shape[0]
    per_worker = n_rows // SC_WORKERS
    n_chunks = per_worker // SC_CHUNK_ROWS
    assert n_chunks * SC_CHUNK_ROWS * SC_WORKERS == n_rows and n_rows < 3 * t_len
    assert per_worker % SC_LANES == 0 and n_assign % SC_LANES == 0 and n_assign == TOP_K * t_len

    @functools.partial(
        pl.kernel, mesh=_sc_mesh(), name="moe_dispatch_sc",
        compiler_params=pltpu.CompilerParams(needs_layout_passes=False),
        out_type=jax.ShapeDtypeStruct((n_rows, ROW_SUBLANES, LANES), F32),
        scratch_types=[pltpu.VMEM((n_assign,), jnp.int32),
                       pltpu.VMEM((per_worker,), jnp.int32),
                       pltpu.VMEM((SC_CHUNK_ROWS, ROW_SUBLANES, LANES), F32),
                       pltpu.SemaphoreType.DMA])
    def dispatch(h_hbm, dest_hbm, rows_hbm, dest_v, src_v, buf, sem):
        wid = lax.axis_index("subcore") * SC_CORES + lax.axis_index("core")
        base = wid * per_worker
        pltpu.sync_copy(dest_hbm, dest_v)
        lane = lax.iota(jnp.int32, SC_LANES)

        def wrap(a):
            a = jnp.where(a >= t_len, a - t_len, a)
            return jnp.where(a >= t_len, a - t_len, a)

        @pl.loop(0, per_worker // SC_LANES)
        def _(i):
            src_v[pl.ds(i * SC_LANES, SC_LANES)] = wrap(base + i * SC_LANES + lane)

        @pl.loop(0, n_assign // SC_LANES)
        def _(i):
            d = dest_v[pl.ds(i * SC_LANES, SC_LANES)] - base
            hit = (d >= 0) & (d < per_worker)
            plsc.store_scatter(src_v, [jnp.where(hit, d, 0)], wrap(i * SC_LANES + lane), mask=hit)

        @pl.loop(0, n_chunks)
        def _(c):
            off = pl.multiple_of(c * SC_CHUNK_ROWS, SC_CHUNK_ROWS)
            pltpu.async_copy(h_hbm.at[src_v.at[pl.ds(off, SC_CHUNK_ROWS)]], buf, sem).wait()
            pltpu.sync_copy(buf, rows_hbm.at[pl.ds(base + off, SC_CHUNK_ROWS)])

    return dispatch(h_rows, dest)


def _sc_gather_rows(table, idx, name):
    m = idx.shape[0]
    per_worker = m // (SC_CHUNK_ROWS * SC_WORKERS)
    assert per_worker * SC_CHUNK_ROWS * SC_WORKERS == m

    @functools.partial(
        pl.kernel, mesh=_sc_mesh(), name=name,
        out_type=jax.ShapeDtypeStruct((m, ROW_SUBLANES, LANES), F32),
        scratch_types=[pltpu.VMEM((m,), jnp.int32),
                       pltpu.VMEM((SC_CHUNK_ROWS, ROW_SUBLANES, LANES), F32),
                       pltpu.SemaphoreType.DMA])
    def gather(table_hbm, idx_hbm, out_hbm, idx_v, buf, sem):
        wid = lax.axis_index("subcore") * SC_CORES + lax.axis_index("core")
        pltpu.sync_copy(idx_hbm, idx_v)

        @pl.loop(0, per_worker)
        def _(j):
            off = pl.multiple_of((j * SC_WORKERS + wid) * SC_CHUNK_ROWS, SC_CHUNK_ROWS)
            pltpu.async_copy(table_hbm.at[idx_v.at[pl.ds(off, SC_CHUNK_ROWS)]], buf, sem).wait()
            pltpu.sync_copy(buf, out_hbm.at[pl.ds(off, SC_CHUNK_ROWS)])

    return gather(table, idx)


def _expert_kernel(bexp_ref, nused_ref, rows_ref, wg_ref, wu_ref, wd_ref, y_ref, wg_s, wu_s, wd_s):
    p = pl.program_id(0)
    prev = bexp_ref[jnp.maximum(p - 1, 0)]

    @pl.when((p == 0) | (bexp_ref[p] != prev))
    def _():
        wg_s[...] = wg_ref[0].astype(BF16)
        wu_s[...] = wu_ref[0].astype(BF16)
        wd_s[...] = wd_ref[0].astype(BF16)

    @pl.when(p < nused_ref[0])
    def _():
        xb = _load_row_tiles(rows_ref).astype(BF16)
        g = _dot(xb, wg_s[...])
        u = _dot(xb, wu_s[...])
        a = (_silu(g) * u).astype(BF16)
        _store_row_tiles(y_ref, _dot(a, wd_s[...]))

    @pl.when(p >= nused_ref[0])
    def _():
        y_ref[...] = jnp.zeros_like(y_ref)


def _expert_call(bexp, nused, rows, w_gate, w_up, w_down):
    n_rows = rows.shape[0] // ROW_SUBLANES
    n_blocks = n_rows // MOE_ROWS
    blk = lambda p, be, nu: (jnp.minimum(p, nu[0] - 1), 0)
    wsel = lambda p, be, nu: (be[p], 0, 0)
    grid_spec = pltpu.PrefetchScalarGridSpec(
        num_scalar_prefetch=2,
        grid=(n_blocks,),
        in_specs=[
            pl.BlockSpec((MOE_ROWS * ROW_SUBLANES, LANES), blk),
            pl.BlockSpec((1, D_MODEL, D_EXPERT), wsel),
            pl.BlockSpec((1, D_MODEL, D_EXPERT), wsel),
            pl.BlockSpec((1, D_EXPERT, D_MODEL), wsel),
        ],
        out_specs=pl.BlockSpec((MOE_ROWS * ROW_SUBLANES, LANES), lambda p, be, nu: (p, 0)),
        scratch_shapes=[
            pltpu.VMEM((D_MODEL, D_EXPERT), BF16),
            pltpu.VMEM((D_MODEL, D_EXPERT), BF16),
            pltpu.VMEM((D_EXPERT, D_MODEL), BF16),
        ],
    )
    return pl.pallas_call(
        _expert_kernel,
        grid_spec=grid_spec,
        out_shape=jax.ShapeDtypeStruct((n_rows * ROW_SUBLANES, LANES), F32),
        compiler_params=pltpu.CompilerParams(
            dimension_semantics=("arbitrary",), vmem_limit_bytes=VMEM_LIMIT),
        name="moe_experts",
    )(bexp, nused, rows, w_gate, w_up, w_down)


def _combine_kernel(h_ref, y0_ref, y1_ref, wts_ref, ln_g_ref, ln_b_ref, out_ref):
    wts = wts_ref[...]
    ffn = _load_row_tiles(y0_ref) * wts[:, 0:1] + _load_row_tiles(y1_ref) * wts[:, 1:2]
    out_ref[...] = _layer_norm(DEEPNORM_ALPHA * _load_row_tiles(h_ref) + ffn, ln_g_ref[...], ln_b_ref[...])


def _combine_call(h_rows, yk, wts, ln_g, ln_b):
    t_len = wts.shape[0]
    n_tiles = t_len // COMBINE_TILE
    tok = lambda i: (i, 0)
    full2 = lambda i: (0, 0)
    tiles = pl.BlockSpec((COMBINE_TILE * ROW_SUBLANES, LANES), tok)
    tiles_k1 = pl.BlockSpec((COMBINE_TILE * ROW_SUBLANES, LANES), lambda i: (i + n_tiles, 0))
    return pl.pallas_call(
        _combine_kernel,
        grid=(n_tiles,),
        in_specs=[tiles, tiles, tiles_k1,
                  pl.BlockSpec((COMBINE_TILE, LANES), tok),
                  pl.BlockSpec((1, D_MODEL), full2),
                  pl.BlockSpec((1, D_MODEL), full2)],
        out_specs=pl.BlockSpec((COMBINE_TILE, D_MODEL), tok),
        out_shape=jax.ShapeDtypeStruct((t_len, D_MODEL), F32),
        compiler_params=pltpu.CompilerParams(
            dimension_semantics=("arbitrary",), vmem_limit_bytes=VMEM_LIMIT),
        name="moe_combine_ln",
    )(h_rows, yk, yk, wts, ln_g, ln_b)


def _mixer_constants(attn_sinks):
    c = CHUNK
    log_g = jnp.log1p(-jnp.exp2(-5.0 - jnp.arange(RET_HEADS, dtype=F32)))
    idx = jnp.arange(c, dtype=F32)
    diff = idx[:, None] - idx[None, :]
    scale = RET_HEAD_DIM ** -0.5
    din = jnp.where(diff >= 0, jnp.exp(log_g[:, None, None] * jnp.maximum(diff, 0.0)), 0.0) * scale
    qdec = jnp.broadcast_to(jnp.exp(log_g[:, None] * (idx + 1.0))[:, :, None], (RET_HEADS, c, RET_HEAD_DIM))
    kdec = jnp.broadcast_to((jnp.exp(log_g[:, None] * (c - 1.0 - idx)) * scale)[:, :, None],
                            (RET_HEADS, c, RET_HEAD_DIM))
    cdec = jnp.exp(log_g * c)
    slopes = jnp.exp2(-8.0 * (jnp.arange(SWA_HEADS, dtype=F32) + 1.0) / SWA_HEADS)
    r = jnp.arange(c)[:, None]
    col = jnp.arange(c)[None, :]
    dist_prev = (r - col + c).astype(F32)
    dist_cur = (r - col).astype(F32)
    bprev = jnp.where((r < col)[None], -slopes[:, None, None] * dist_prev[None], NEG)
    bcur = jnp.where((r >= col)[None], -slopes[:, None, None] * dist_cur[None], NEG)
    bprev = bprev.at[:, :, 0].set(jnp.broadcast_to(attn_sinks.astype(F32)[:, None], (SWA_HEADS, c)))
    bias = jnp.concatenate([bprev, bcur], axis=-1).reshape(SWA_KV_HEADS, SWA_GROUP * c, 2 * c)
    key = jnp.arange(2 * c)
    pen = jnp.where((key >= 1) & (key < c), NEG, 0.0).astype(F32)[None, :]
    lane_head = jnp.arange(SWA_VREP) // SWA_HEAD_DIM
    own = lane_head[None, None, :] == jnp.arange(SWA_GROUP)[:, None, None]
    kmask = jnp.broadcast_to(jnp.arange(c)[:, None] > 0, (c, SWA_HEAD_DIM)).astype(BF16)
    vmask = (own & (key[None, :, None] > 0)).astype(BF16)
    omask = jnp.broadcast_to(own, (SWA_GROUP, 2 * c, SWA_VREP)).astype(BF16).reshape(
        SWA_GROUP * 2 * c, SWA_VREP)
    tr = jnp.arange(SEQ_TILE)
    tri = (tr[None, :] < tr[:, None]).astype(BF16)
    return cdec.astype(F32), din.astype(F32), qdec.astype(F32), kdec.astype(F32), \
        bias.astype(F32), pen, kmask, vmask, omask, tri


def _tile_v_columns(w_in):
    v_cols = w_in[:, _VA:].reshape(D_MODEL, SWA_KV_HEADS, 1, SWA_HEAD_DIM)
    v_cols = jnp.broadcast_to(v_cols, (D_MODEL, SWA_KV_HEADS, SWA_GROUP, SWA_HEAD_DIM))
    return jnp.concatenate([w_in[:, :_VA], v_cols.reshape(D_MODEL, SWA_KV_HEADS * SWA_VREP)], axis=1)


def _router_tables(w_group_router, b_group_router, w_expert_router, b_expert_router):
    w_e = jnp.transpose(w_expert_router, (1, 0, 2)).reshape(D_MODEL, N_EXPERTS)
    w = jnp.concatenate([w_e, w_group_router,
                         jnp.zeros((D_MODEL, LANES - N_EXPERTS - N_GROUPS), F32)], axis=1)
    bias = jnp.concatenate([b_expert_router.reshape(N_EXPERTS), b_group_router,
                            jnp.zeros((LANES - N_EXPERTS - N_GROUPS,), F32)])[None, :]
    w_hi = w.astype(BF16)
    w_lo = (w - w_hi.astype(F32)).astype(BF16)
    return w_hi, w_lo, bias


def kernel(x, w_in, ret_gn_g, attn_sinks, w_out, ln1_g, ln1_b, w_group_router, b_group_router,
           w_expert_router, b_expert_router, w_gate, w_up, w_down, ln2_g, ln2_b):
    bsz, s_len, d = x.shape
    assert d == D_MODEL and s_len % SEQ_TILE == 0 and w_in.shape[0] == DEPTH == 1
    t_len = bsz * s_len
    n_blocks = t_len * TOP_K // MOE_ROWS + N_EXPERTS
    n_rows = n_blocks * MOE_ROWS

    consts = _mixer_constants(attn_sinks[0])
    wr_hi, wr_lo, br = _router_tables(w_group_router[0], b_group_router[0],
                                      w_expert_router[0], b_expert_router[0])
    h_rows, ids, wts, cnt = _mixer_call(
        x.reshape(t_len, d), _tile_v_columns(w_in[0]).astype(BF16), w_out[0].astype(BF16), ret_gn_g[0][None, :],
        ln1_g[0][None, :], ln1_b[0][None, :], wr_hi, wr_lo, br, consts, bsz, s_len)

    counts = cnt[0, :N_EXPERTS].astype(jnp.int32)
    padded = (counts + MOE_ROWS - 1) // MOE_ROWS * MOE_ROWS
    pend = jnp.cumsum(padded)
    pstart = pend - padded
    onehot = ids[:, 0:TOP_K, None] == jnp.arange(N_EXPERTS, dtype=jnp.int32)
    dest = jnp.sum(jnp.where(onehot, pstart, 0), axis=-1) + ids[:, TOP_K:2 * TOP_K]
    dest = dest.T.reshape(-1)
    nused = (pend[-1:] // MOE_ROWS).astype(jnp.int32)
    blk_start = jnp.minimum(jnp.arange(n_blocks, dtype=jnp.int32), nused[0] - 1) * MOE_ROWS
    bexp = jnp.minimum(jnp.sum(pend[None, :] <= blk_start[:, None], axis=-1), N_EXPERTS - 1).astype(jnp.int32)

    rows = _sc_dispatch(h_rows.reshape(t_len, ROW_SUBLANES, LANES), dest, n_rows)
    y = _expert_call(bexp, nused, rows.reshape(n_rows * ROW_SUBLANES, LANES), w_gate[0], w_up[0], w_down[0])
    yk = _sc_gather_rows(y.reshape(n_rows, ROW_SUBLANES, LANES), dest, "moe_combine_sc")
    out = _combine_call(h_rows, yk.reshape(TOP_K * t_len * ROW_SUBLANES, LANES), wts,
                        ln2_g[0][None, :], ln2_b[0][None, :])
    return out.reshape(bsz, s_len, d)
```

```python
import functools

import jax
import jax.numpy as jnp
from jax import lax
from jax.experimental import pallas as pl
from jax.experimental.pallas import tpu as pltpu
from jax.experimental.pallas import tpu_sc as plsc

F32 = jnp.float32
BF16 = jnp.bfloat16

D_MODEL = 1024
RET_HEADS = 4
RET_HEAD_DIM = 128
RET_WIDTH = RET_HEADS * RET_HEAD_DIM
CHUNK = 128
SWA_HEADS = 8
SWA_KV_HEADS = 2
SWA_GROUP = SWA_HEADS // SWA_KV_HEADS
SWA_HEAD_DIM = 64
SWA_WIDTH = SWA_HEADS * SWA_HEAD_DIM
SWA_KV_WIDTH = SWA_KV_HEADS * SWA_HEAD_DIM
IN_WIDTH = 4 * RET_WIDTH + SWA_WIDTH + 2 * SWA_KV_WIDTH
N_GROUPS = 4
EXPERTS_PER_GROUP = 8
N_EXPERTS = N_GROUPS * EXPERTS_PER_GROUP
TOP_K = 2
D_EXPERT = 512
LN_EPS = 1e-5
GN_EPS = 1e-6
DEPTH = 1
DEEPNORM_ALPHA = (2 * DEPTH) ** 0.25
NEG = -1e30

LANES = 128
ROW_SUBLANES = D_MODEL // LANES
SEQ_TILE = 512
MOE_ROWS = 128
COMBINE_TILE = 256
SC_CORES = 2
SC_SUBCORES = 16
SC_WORKERS = SC_CORES * SC_SUBCORES
SC_LANES = 16
SC_CHUNK_ROWS = 32
VMEM_LIMIT = 56 * 1024 * 1024

_QR, _KR, _VR, _GR = 0, RET_WIDTH, 2 * RET_WIDTH, 3 * RET_WIDTH
_QA = 4 * RET_WIDTH
_KA = _QA + SWA_WIDTH
_VA = _KA + SWA_KV_WIDTH
SWA_VREP = SWA_GROUP * SWA_HEAD_DIM
IN_WIDTH_TILED = _VA + SWA_KV_HEADS * SWA_VREP
GROUP_LANE0 = N_EXPERTS


def _dot(a, b):
    return jnp.dot(a, b, preferred_element_type=F32)


def _dot_nt(a, b):
    return lax.dot_general(a, b, (((1,), (1,)), ((), ())), preferred_element_type=F32)


def _dot_tn(a, b):
    return lax.dot_general(a, b, (((0,), (0,)), ((), ())), preferred_element_type=F32)


def _layer_norm(z, g, b):
    mu = jnp.mean(z, axis=-1, keepdims=True)
    zc = z - mu
    var = jnp.mean(zc * zc, axis=-1, keepdims=True)
    return zc * lax.rsqrt(var + LN_EPS) * g + b


def _silu(g):
    return g / (1.0 + jnp.exp(-g))


def _store_row_tiles(ref, val):
    n = val.shape[0]
    for j in range(ROW_SUBLANES):
        ref[pl.ds(j, n, stride=ROW_SUBLANES), :] = val[:, j * LANES:(j + 1) * LANES]


def _load_row_tiles(ref):
    n = ref.shape[0] // ROW_SUBLANES
    return jnp.concatenate([ref[pl.ds(j, n, stride=ROW_SUBLANES), :] for j in range(ROW_SUBLANES)], axis=1)


def _mixer_kernel(cdec_ref, x_ref, w_in_ref, w_out_ref, gn_ref, ln_g_ref, ln_b_ref,
                  wr_hi_ref, wr_lo_ref, br_ref, din_ref, qdec_ref, kdec_ref,
                  bias_ref, pen_ref, kmask_ref, vmask_ref, omask_ref, tri_ref,
                  h3_ref, ids_ref, wts_ref, cnt_ref,
                  state_scr, kprev_scr, vprev_scr, o_scr, carry_scr):
    b = pl.program_id(0)
    n = pl.program_id(1)
    ts = x_ref.shape[0]

    @pl.when(n == 0)
    def _():
        state_scr[...] = jnp.zeros_like(state_scr)
        kprev_scr[...] = jnp.zeros_like(kprev_scr)
        vprev_scr[...] = jnp.zeros_like(vprev_scr)

    @pl.when((b == 0) & (n == 0))
    def _():
        carry_scr[...] = jnp.zeros_like(carry_scr)

    x = x_ref[...]
    xb = x.astype(BF16)

    def proj(lo, hi):
        return _dot(xb, w_in_ref[:, lo:hi])

    q_r, k_r, v_r, g_r = proj(_QR, _KR), proj(_KR, _VR), proj(_VR, _GR), proj(_GR, _QA)
    q_a, k_a = proj(_QA, _KA), proj(_KA, _VA)
    k_ab = k_a.astype(BF16)
    v_rep = proj(_VA, _VA + SWA_KV_HEADS * SWA_VREP).astype(BF16)
    first_pen = jnp.where(n == 0, pen_ref[...], 0.0)

    for c in range(ts // CHUNK):
        rs = slice(c * CHUNK, (c + 1) * CHUNK)
        for hd in range(RET_HEADS):
            cs = slice(hd * RET_HEAD_DIM, (hd + 1) * RET_HEAD_DIM)
            q = q_r[rs, cs]
            k = k_r[rs, cs]
            v = v_r[rs, cs].astype(BF16)
            scores = _dot_nt(q.astype(BF16), k.astype(BF16)) * din_ref[hd]
            inner = _dot(scores.astype(BF16), v)
            st = state_scr[hd]
            cross = _dot((q * qdec_ref[hd]).astype(BF16), st.astype(BF16))
            kv = _dot_tn((k * kdec_ref[hd]).astype(BF16), v)
            state_scr[hd] = st * cdec_ref[hd] + kv
            o = inner + cross
            mu = jnp.mean(o, axis=-1, keepdims=True)
            oc = o - mu
            var = jnp.mean(oc * oc, axis=-1, keepdims=True)
            on = oc * lax.rsqrt(var + GN_EPS) * gn_ref[:, cs] * _silu(g_r[rs, cs])
            o_scr[rs, cs] = on.astype(BF16)
        for j in range(SWA_KV_HEADS):
            ks = slice(j * SWA_HEAD_DIM, (j + 1) * SWA_HEAD_DIM)
            vs = slice(j * SWA_VREP, (j + 1) * SWA_VREP)
            if c == 0:
                kp = kprev_scr[:, ks].astype(BF16)
                vp = vprev_scr[:, vs].astype(BF16)
            else:
                ps = slice((c - 1) * CHUNK, c * CHUNK)
                kp = k_ab[ps, ks]
                vp = v_rep[ps, vs]
            kp = kp * kmask_ref[...]
            kcat = jnp.concatenate([kp, k_ab[rs, ks]], axis=0)
            vcat = jnp.concatenate([vp, v_rep[rs, vs]], axis=0)
            q0 = j * SWA_GROUP * SWA_HEAD_DIM
            qs = jnp.concatenate(
                [q_a[rs, q0 + g * SWA_HEAD_DIM:q0 + (g + 1) * SWA_HEAD_DIM] for g in range(SWA_GROUP)],
                axis=0)
            qs = (qs * (SWA_HEAD_DIM ** -0.5)).astype(BF16)
            s = _dot_nt(qs, kcat) + bias_ref[j]
            if c == 0:
                s = s + first_pen
            m = jnp.max(jnp.maximum(s[:, :CHUNK], s[:, CHUNK:]), axis=-1, keepdims=True)
            p = jnp.exp(s - m).astype(BF16)
            p_all = jnp.concatenate([p[g * CHUNK:(g + 1) * CHUNK] for g in range(SWA_GROUP)], axis=1)
            v_blk = jnp.concatenate([vcat * vmask_ref[g] for g in range(SWA_GROUP)], axis=0)
            num = _dot(p_all, v_blk)
            den = _dot(p_all, omask_ref[...])
            c0 = RET_WIDTH + j * SWA_GROUP * SWA_HEAD_DIM
            o_scr[rs, c0:c0 + SWA_GROUP * SWA_HEAD_DIM] = (num / den).astype(BF16)

    kprev_scr[...] = k_ab[ts - CHUNK:, :].astype(F32)
    vprev_scr[...] = v_rep[ts - CHUNK:, :].astype(F32)

    mix = _dot(o_scr[...], w_out_ref[...])
    h = _layer_norm(DEEPNORM_ALPHA * x + mix, ln_g_ref[...], ln_b_ref[...])
    _store_row_tiles(h3_ref, h)

    h_hi = h.astype(BF16)
    h_lo = (h - h_hi.astype(F32)).astype(BF16)
    logits = (_dot(h_hi, wr_hi_ref[...]) + _dot(h_lo, wr_hi_ref[...]) + _dot(h_hi, wr_lo_ref[...])
              + br_ref[...])
    lane = lax.broadcasted_iota(jnp.int32, (ts, LANES), 1).astype(F32)
    big = 1e9
    ninf = -jnp.inf
    gmask = (lane >= GROUP_LANE0) & (lane < GROUP_LANE0 + N_GROUPS)
    gl = jnp.where(gmask, logits, ninf)
    gmax = jnp.max(gl, axis=-1, keepdims=True)
    gidx = jnp.min(jnp.where(gl == gmax, lane, big), axis=-1, keepdims=True) - GROUP_LANE0
    g_w = 1.0 / jnp.sum(jnp.exp(gl - gmax), axis=-1, keepdims=True)
    lo = gidx * EXPERTS_PER_GROUP
    el = jnp.where((lane >= lo) & (lane < lo + EXPERTS_PER_GROUP), logits, ninf)
    m1 = jnp.max(el, axis=-1, keepdims=True)
    i1 = jnp.min(jnp.where(el == m1, lane, big), axis=-1, keepdims=True)
    el2 = jnp.where(lane == i1, ninf, el)
    m2 = jnp.max(el2, axis=-1, keepdims=True)
    i2 = jnp.min(jnp.where(el2 == m2, lane, big), axis=-1, keepdims=True)
    t = jnp.exp(m2 - m1)
    w1 = g_w / (1.0 + t)
    w2 = g_w * t / (1.0 + t)
    hit1 = lane == i1
    hit2 = lane == i2
    onehot = (hit1 | hit2).astype(BF16)
    prefix = _dot(tri_ref[...], onehot) + carry_scr[0:1, :]
    r1 = jnp.sum(jnp.where(hit1, prefix, 0.0), axis=-1, keepdims=True)
    r2 = jnp.sum(jnp.where(hit2, prefix, 0.0), axis=-1, keepdims=True)
    carry = carry_scr[0:1, :] + jnp.sum(onehot.astype(F32), axis=0, keepdims=True)
    carry_scr[0:1, :] = carry
    cnt_ref[...] = jnp.broadcast_to(carry, cnt_ref.shape)
    idsf = jnp.where(lane == 0, i1, jnp.where(lane == 1, i2,
                     jnp.where(lane == 2, r1, jnp.where(lane == 3, r2, 0.0))))
    ids_ref[...] = idsf.astype(jnp.int32)
    wts_ref[...] = jnp.where(lane == 0, w1, jnp.where(lane == 1, w2, 0.0))


def _mixer_call(x2, w_in, w_out, gn, ln_g, ln_b, wr_hi, wr_lo, br, consts, bsz, s_len):
    cdec, din, qdec, kdec, bias, pen, kmask, vmask, omask, tri = consts
    t_len = bsz * s_len
    ns = s_len // SEQ_TILE
    tok = lambda b, n, *_: (b * ns + n, 0)
    full2 = lambda b, n, *_: (0, 0)
    full3 = lambda b, n, *_: (0, 0, 0)
    grid_spec = pltpu.PrefetchScalarGridSpec(
        num_scalar_prefetch=1,
        grid=(bsz, ns),
        in_specs=[
            pl.BlockSpec((SEQ_TILE, D_MODEL), tok),
            pl.BlockSpec((D_MODEL, IN_WIDTH_TILED), full2),
            pl.BlockSpec((D_MODEL, D_MODEL), full2),
            pl.BlockSpec((1, RET_WIDTH), full2),
            pl.BlockSpec((1, D_MODEL), full2),
            pl.BlockSpec((1, D_MODEL), full2),
            pl.BlockSpec((D_MODEL, LANES), full2),
            pl.BlockSpec((D_MODEL, LANES), full2),
            pl.BlockSpec((1, LANES), full2),
            pl.BlockSpec((RET_HEADS, CHUNK, CHUNK), full3),
            pl.BlockSpec((RET_HEADS, CHUNK, RET_HEAD_DIM), full3),
            pl.BlockSpec((RET_HEADS, CHUNK, RET_HEAD_DIM), full3),
            pl.BlockSpec((SWA_KV_HEADS, SWA_GROUP * CHUNK, 2 * CHUNK), full3),
            pl.BlockSpec((1, 2 * CHUNK), full2),
            pl.BlockSpec((CHUNK, SWA_HEAD_DIM), full2),
            pl.BlockSpec((SWA_GROUP, 2 * CHUNK, SWA_VREP), full3),
            pl.BlockSpec((SWA_GROUP * 2 * CHUNK, SWA_VREP), full2),
            pl.BlockSpec((SEQ_TILE, SEQ_TILE), full2),
        ],
        out_specs=[
            pl.BlockSpec((SEQ_TILE * ROW_SUBLANES, LANES), tok),
            pl.BlockSpec((SEQ_TILE, LANES), tok),
            pl.BlockSpec((SEQ_TILE, LANES), tok),
            pl.BlockSpec((8, LANES), full2),
        ],
        scratch_shapes=[
            pltpu.VMEM((RET_HEADS, RET_HEAD_DIM, RET_HEAD_DIM), F32),
            pltpu.VMEM((CHUNK, SWA_KV_WIDTH), F32),
            pltpu.VMEM((CHUNK, SWA_KV_HEADS * SWA_VREP), F32),
            pltpu.VMEM((SEQ_TILE, D_MODEL), BF16),
            pltpu.VMEM((8, LANES), F32),
        ],
    )
    return pl.pallas_call(
        _mixer_kernel,
        grid_spec=grid_spec,
        out_shape=[
            jax.ShapeDtypeStruct((t_len * ROW_SUBLANES, LANES), F32),
            jax.ShapeDtypeStruct((t_len, LANES), jnp.int32),
            jax.ShapeDtypeStruct((t_len, LANES), F32),
            jax.ShapeDtypeStruct((8, LANES), F32),
        ],
        compiler_params=pltpu.CompilerParams(
            dimension_semantics=("arbitrary", "arbitrary"), vmem_limit_bytes=VMEM_LIMIT),
        name="mixer_router",
    )(cdec, x2, w_in, w_out, gn, ln_g, ln_b, wr_hi, wr_lo, br, din, qdec, kdec, bias, pen, kmask, vmask, omask, tri)


def _sc_mesh():
    return plsc.VectorSubcoreMesh(core_axis_name="core", subcore_axis_name="subcore")


def _sc_dispatch(h_rows, dest, n_rows):
    t_len = h_rows.shape[0]
    n_assign = dest.shape[0]
    per_worker = n_rows // SC_WORKERS
    n_chunks = per_worker // SC_CHUNK_ROWS
    assert n_chunks * SC_CHUNK_ROWS * SC_WORKERS == n_rows and n_rows < 3 * t_len
    assert per_worker % SC_LANES == 0 and n_assign % SC_LANES == 0 and n_assign == TOP_K * t_len

    @functools.partial(
        pl.kernel, mesh=_sc_mesh(), name="moe_dispatch_sc",
        compiler_params=pltpu.CompilerParams(needs_layout_passes=False),
        out_type=jax.ShapeDtypeStruct((n_rows, ROW_SUBLANES, LANES), F32),
        scratch_types=[pltpu.VMEM((n_assign,), jnp.int32),
                       pltpu.VMEM((per_worker,), jnp.int32),
                       pltpu.VMEM((SC_CHUNK_ROWS, ROW_SUBLANES, LANES), F32),
                       pltpu.SemaphoreType.DMA])
    def dispatch(h_hbm, dest_hbm, rows_hbm, dest_v, src_v, buf, sem):
        wid = lax.axis_index("subcore") * SC_CORES + lax.axis_index("core")
        base = wid * per_worker
        pltpu.sync_copy(dest_hbm, dest_v)
        lane = lax.iota(jnp.int32, SC_LANES)

        def wrap(a):
            a = jnp.where(a >= t_len, a - t_len, a)
            return jnp.where(a >= t_len, a - t_len, a)

        @pl.loop(0, per_worker // SC_LANES)
        def _(i):
            src_v[pl.ds(i * SC_LANES, SC_LANES)] = wrap(base + i * SC_LANES + lane)

        @pl.loop(0, n_assign // SC_LANES)
        def _(i):
            d = dest_v[pl.ds(i * SC_LANES, SC_LANES)] - base
            hit = (d >= 0) & (d < per_worker)
            plsc.store_scatter(src_v, [jnp.where(hit, d, 0)], wrap(i * SC_LANES + lane), mask=hit)

        @pl.loop(0, n_chunks)
        def _(c):
            off = pl.multiple_of(c * SC_CHUNK_ROWS, SC_CHUNK_ROWS)
            pltpu.async_copy(h_hbm.at[src_v.at[pl.ds(off, SC_CHUNK_ROWS)]], buf, sem).wait()
            pltpu.sync_copy(buf, rows_hbm.at[pl.ds(base + off, SC_CHUNK_ROWS)])

    return dispatch(h_rows, dest)


def _sc_gather_rows(table, idx, name):
    m = idx.shape[0]
    per_worker = m // (SC_CHUNK_ROWS * SC_WORKERS)
    assert per_worker * SC_CHUNK_ROWS * SC_WORKERS == m

    @functools.partial(
        pl.kernel, mesh=_sc_mesh(), name=name,
        out_type=jax.ShapeDtypeStruct((m, ROW_SUBLANES, LANES), F32),
        scratch_types=[pltpu.VMEM((m,), jnp.int32),
                       pltpu.VMEM((SC_CHUNK_ROWS, ROW_SUBLANES, LANES), F32),
                       pltpu.SemaphoreType.DMA])
    def gather(table_hbm, idx_hbm, out_hbm, idx_v, buf, sem):
        wid = lax.axis_index("subcore") * SC_CORES + lax.axis_index("core")
        pltpu.sync_copy(idx_hbm, idx_v)

        @pl.loop(0, per_worker)
        def _(j):
            off = pl.multiple_of((j * SC_WORKERS + wid) * SC_CHUNK_ROWS, SC_CHUNK_ROWS)
            pltpu.async_copy(table_hbm.at[idx_v.at[pl.ds(off, SC_CHUNK_ROWS)]], buf, sem).wait()
            pltpu.sync_copy(buf, out_hbm.at[pl.ds(off, SC_CHUNK_ROWS)])

    return gather(table, idx)


def _expert_kernel(bstart_ref, bcount_ref, wg_ref, wu_ref, wd_ref, rows_hbm, y_hbm, wg_s, wu_s, wd_s):
    e = pl.program_id(0)
    n_blk = bcount_ref[e]

    @pl.when(n_blk > 0)
    def _():
        wg_s[...] = wg_ref[0].astype(BF16)
        wu_s[...] = wu_ref[0].astype(BF16)
        wd_s[...] = wd_ref[0].astype(BF16)
        first = bstart_ref[e]

        def block(x_ref, y_ref):
            xb = _load_row_tiles(x_ref).astype(BF16)
            g = _dot(xb, wg_s[...])
            u = _dot(xb, wu_s[...])
            a = (_silu(g) * u).astype(BF16)
            _store_row_tiles(y_ref, _dot(a, wd_s[...]))

        spec = pl.BlockSpec((MOE_ROWS * ROW_SUBLANES, LANES), lambda i: (first + i, 0))
        pltpu.emit_pipeline(block, grid=(n_blk,), in_specs=[spec], out_specs=[spec])(rows_hbm, y_hbm)


def _expert_call(bstart, bcount, rows, w_gate, w_up, w_down):
    wsel = lambda e, *_: (e, 0, 0)
    grid_spec = pltpu.PrefetchScalarGridSpec(
        num_scalar_prefetch=2,
        grid=(N_EXPERTS,),
        in_specs=[
            pl.BlockSpec((1, D_MODEL, D_EXPERT), wsel),
            pl.BlockSpec((1, D_MODEL, D_EXPERT), wsel),
            pl.BlockSpec((1, D_EXPERT, D_MODEL), wsel),
            pl.BlockSpec(memory_space=pl.ANY),
        ],
        out_specs=pl.BlockSpec(memory_space=pl.ANY),
        scratch_shapes=[
            pltpu.VMEM((D_MODEL, D_EXPERT), BF16),
            pltpu.VMEM((D_MODEL, D_EXPERT), BF16),
            pltpu.VMEM((D_EXPERT, D_MODEL), BF16),
        ],
    )
    return pl.pallas_call(
        _expert_kernel,
        grid_spec=grid_spec,
        out_shape=jax.ShapeDtypeStruct(rows.shape, F32),
        compiler_params=pltpu.CompilerParams(
            dimension_semantics=("arbitrary",), vmem_limit_bytes=VMEM_LIMIT),
        name="moe_experts",
    )(bstart, bcount, w_gate, w_up, w_down, rows)


def _combine_kernel(h_ref, y0_ref, y1_ref, wts_ref, ln_g_ref, ln_b_ref, out_ref):
    wts = wts_ref[...]
    ffn = _load_row_tiles(y0_ref) * wts[:, 0:1] + _load_row_tiles(y1_ref) * wts[:, 1:2]
    out_ref[...] = _layer_norm(DEEPNORM_ALPHA * _load_row_tiles(h_ref) + ffn, ln_g_ref[...], ln_b_ref[...])


def _combine_call(h_rows, yk, wts, ln_g, ln_b):
    t_len = wts.shape[0]
    n_tiles = t_len // COMBINE_TILE
    tok = lambda i: (i, 0)
    full2 = lambda i: (0, 0)
    tiles = pl.BlockSpec((COMBINE_TILE * ROW_SUBLANES, LANES), tok)
    tiles_k1 = pl.BlockSpec((COMBINE_TILE * ROW_SUBLANES, LANES), lambda i: (i + n_tiles, 0))
    return pl.pallas_call(
        _combine_kernel,
        grid=(n_tiles,),
        in_specs=[tiles, tiles, tiles_k1,
                  pl.BlockSpec((COMBINE_TILE, LANES), tok),
                  pl.BlockSpec((1, D_MODEL), full2),
                  pl.BlockSpec((1, D_MODEL), full2)],
        out_specs=pl.BlockSpec((COMBINE_TILE, D_MODEL), tok),
        out_shape=jax.ShapeDtypeStruct((t_len, D_MODEL), F32),
        compiler_params=pltpu.CompilerParams(
            dimension_semantics=("arbitrary",), vmem_limit_bytes=VMEM_LIMIT),
        name="moe_combine_ln",
    )(h_rows, yk, yk, wts, ln_g, ln_b)


def _mixer_constants(attn_sinks):
    c = CHUNK
    log_g = jnp.log1p(-jnp.exp2(-5.0 - jnp.arange(RET_HEADS, dtype=F32)))
    idx = jnp.arange(c, dtype=F32)
    diff = idx[:, None] - idx[None, :]
    scale = RET_HEAD_DIM ** -0.5
    din = jnp.where(diff >= 0, jnp.exp(log_g[:, None, None] * jnp.maximum(diff, 0.0)), 0.0) * scale
    qdec = jnp.broadcast_to(jnp.exp(log_g[:, None] * (idx + 1.0))[:, :, None], (RET_HEADS, c, RET_HEAD_DIM))
    kdec = jnp.broadcast_to((jnp.exp(log_g[:, None] * (c - 1.0 - idx)) * scale)[:, :, None],
                            (RET_HEADS, c, RET_HEAD_DIM))
    cdec = jnp.exp(log_g * c)
    slopes = jnp.exp2(-8.0 * (jnp.arange(SWA_HEADS, dtype=F32) + 1.0) / SWA_HEADS)
    r = jnp.arange(c)[:, None]
    col = jnp.arange(c)[None, :]
    dist_prev = (r - col + c).astype(F32)
    dist_cur = (r - col).astype(F32)
    bprev = jnp.where((r < col)[None], -slopes[:, None, None] * dist_prev[None], NEG)
    bcur = jnp.where((r >= col)[None], -slopes[:, None, None] * dist_cur[None], NEG)
    bprev = bprev.at[:, :, 0].set(jnp.broadcast_to(attn_sinks.astype(F32)[:, None], (SWA_HEADS, c)))
    bias = jnp.concatenate([bprev, bcur], axis=-1).reshape(SWA_KV_HEADS, SWA_GROUP * c, 2 * c)
    key = jnp.arange(2 * c)
    pen = jnp.where((key >= 1) & (key < c), NEG, 0.0).astype(F32)[None, :]
    lane_head = jnp.arange(SWA_VREP) // SWA_HEAD_DIM
    own = lane_head[None, None, :] == jnp.arange(SWA_GROUP)[:, None, None]
    kmask = jnp.broadcast_to(jnp.arange(c)[:, None] > 0, (c, SWA_HEAD_DIM)).astype(BF16)
    vmask = (own & (key[None, :, None] > 0)).astype(BF16)
    omask = jnp.broadcast_to(own, (SWA_GROUP, 2 * c, SWA_VREP)).astype(BF16).reshape(
        SWA_GROUP * 2 * c, SWA_VREP)
    tr = jnp.arange(SEQ_TILE)
    tri = (tr[None, :] < tr[:, None]).astype(BF16)
    return cdec.astype(F32), din.astype(F32), qdec.astype(F32), kdec.astype(F32), \
        bias.astype(F32), pen, kmask, vmask, omask, tri


def _tile_v_columns(w_in):
    v_cols = w_in[:, _VA:].reshape(D_MODEL, SWA_KV_HEADS, 1, SWA_HEAD_DIM)
    v_cols = jnp.broadcast_to(v_cols, (D_MODEL, SWA_KV_HEADS, SWA_GROUP, SWA_HEAD_DIM))
    return jnp.concatenate([w_in[:, :_VA], v_cols.reshape(D_MODEL, SWA_KV_HEADS * SWA_VREP)], axis=1)


def _router_tables(w_group_router, b_group_router, w_expert_router, b_expert_router):
    w_e = jnp.transpose(w_expert_router, (1, 0, 2)).reshape(D_MODEL, N_EXPERTS)
    w = jnp.concatenate([w_e, w_group_router,
                         jnp.zeros((D_MODEL, LANES - N_EXPERTS - N_GROUPS), F32)], axis=1)
    bias = jnp.concatenate([b_expert_router.reshape(N_EXPERTS), b_group_router,
                            jnp.zeros((LANES - N_EXPERTS - N_GROUPS,), F32)])[None, :]
    w_hi = w.astype(BF16)
    w_lo = (w - w_hi.astype(F32)).astype(BF16)
    return w_hi, w_lo, bias


def kernel(x, w_in, ret_gn_g, attn_sinks, w_out, ln1_g, ln1_b, w_group_router, b_group_router,
           w_expert_router, b_expert_router, w_gate, w_up, w_down, ln2_g, ln2_b):
    bsz, s_len, d = x.shape
    assert d == D_MODEL and s_len % SEQ_TILE == 0 and w_in.shape[0] == DEPTH == 1
    t_len = bsz * s_len
    n_blocks = t_len * TOP_K // MOE_ROWS + N_EXPERTS
    n_rows = n_blocks * MOE_ROWS

    consts = _mixer_constants(attn_sinks[0])
    wr_hi, wr_lo, br = _router_tables(w_group_router[0], b_group_router[0],
                                      w_expert_router[0], b_expert_router[0])
    h_rows, ids, wts, cnt = _mixer_call(
        x.reshape(t_len, d), _tile_v_columns(w_in[0]).astype(BF16), w_out[0].astype(BF16), ret_gn_g[0][None, :],
        ln1_g[0][None, :], ln1_b[0][None, :], wr_hi, wr_lo, br, consts, bsz, s_len)

    counts = cnt[0, :N_EXPERTS].astype(jnp.int32)
    padded = (counts + MOE_ROWS - 1) // MOE_ROWS * MOE_ROWS
    pend = jnp.cumsum(padded)
    pstart = pend - padded
    onehot = ids[:, 0:TOP_K, None] == jnp.arange(N_EXPERTS, dtype=jnp.int32)
    dest = jnp.sum(jnp.where(onehot, pstart, 0), axis=-1) + ids[:, TOP_K:2 * TOP_K]
    dest = dest.T.reshape(-1)
    bstart = (pstart // MOE_ROWS).astype(jnp.int32)
    bcount = (padded // MOE_ROWS).astype(jnp.int32)

    rows = _sc_dispatch(h_rows.reshape(t_len, ROW_SUBLANES, LANES), dest, n_rows)
    y = _expert_call(bstart, bcount, rows.reshape(n_rows * ROW_SUBLANES, LANES), w_gate[0], w_up[0], w_down[0])
    yk = _sc_gather_rows(y.reshape(n_rows, ROW_SUBLANES, LANES), dest, "moe_combine_sc")
    out = _combine_call(h_rows, yk.reshape(TOP_K * t_len * ROW_SUBLANES, LANES), wts,
                        ln2_g[0][None, :], ln2_b[0][None, :])
    return out.reshape(bsz, s_len, d)
```

```python
import functools

import jax
import jax.numpy as jnp
from jax import lax
from jax.experimental import pallas as pl
from jax.experimental.pallas import tpu as pltpu
from jax.experimental.pallas import tpu_sc as plsc

F32 = jnp.float32
BF16 = jnp.bfloat16

D_MODEL = 1024
RET_HEADS = 4
RET_HEAD_DIM = 128
RET_WIDTH = RET_HEADS * RET_HEAD_DIM
CHUNK = 128
SWA_HEADS = 8
SWA_KV_HEADS = 2
SWA_GROUP = SWA_HEADS // SWA_KV_HEADS
SWA_HEAD_DIM = 64
SWA_WIDTH = SWA_HEADS * SWA_HEAD_DIM
SWA_KV_WIDTH = SWA_KV_HEADS * SWA_HEAD_DIM
IN_WIDTH = 4 * RET_WIDTH + SWA_WIDTH + 2 * SWA_KV_WIDTH
N_GROUPS = 4
EXPERTS_PER_GROUP = 8
N_EXPERTS = N_GROUPS * EXPERTS_PER_GROUP
TOP_K = 2
D_EXPERT = 512
LN_EPS = 1e-5
GN_EPS = 1e-6
DEPTH = 1
DEEPNORM_ALPHA = (2 * DEPTH) ** 0.25
NEG = -1e30

LANES = 128
ROW_SUBLANES = D_MODEL // LANES
SEQ_TILE = 512
MOE_ROWS = 256
COMBINE_TILE = 256
TOKEN_SPLITS = 2
SC_CORES = 2
SC_SUBCORES = 16
SC_WORKERS = SC_CORES * SC_SUBCORES
SC_LANES = 16
SC_CHUNK_ROWS = 32
VMEM_LIMIT = 56 * 1024 * 1024

_QR, _KR, _VR, _GR = 0, RET_WIDTH, 2 * RET_WIDTH, 3 * RET_WIDTH
_QA = 4 * RET_WIDTH
_KA = _QA + SWA_WIDTH
_VA = _KA + SWA_KV_WIDTH
SWA_VREP = SWA_GROUP * SWA_HEAD_DIM
IN_WIDTH_TILED = _VA + SWA_KV_HEADS * SWA_VREP
GROUP_LANE0 = N_EXPERTS


def _dot(a, b):
    return jnp.dot(a, b, preferred_element_type=F32)


def _dot_nt(a, b):
    return lax.dot_general(a, b, (((1,), (1,)), ((), ())), preferred_element_type=F32)


def _dot_tn(a, b):
    return lax.dot_general(a, b, (((0,), (0,)), ((), ())), preferred_element_type=F32)


def _layer_norm(z, g, b):
    mu = jnp.mean(z, axis=-1, keepdims=True)
    zc = z - mu
    var = jnp.mean(zc * zc, axis=-1, keepdims=True)
    return zc * lax.rsqrt(var + LN_EPS) * g + b


def _silu(g):
    return g / (1.0 + jnp.exp(-g))


def _store_row_tiles(ref, val):
    n = val.shape[0]
    for j in range(ROW_SUBLANES):
        ref[pl.ds(j, n, stride=ROW_SUBLANES), :] = val[:, j * LANES:(j + 1) * LANES]


def _load_row_tiles(ref):
    n = ref.shape[0] // ROW_SUBLANES
    return jnp.concatenate([ref[pl.ds(j, n, stride=ROW_SUBLANES), :] for j in range(ROW_SUBLANES)], axis=1)


def _mixer_kernel(cdec_ref, x_ref, w_in_ref, w_out_ref, gn_ref, ln_g_ref, ln_b_ref,
                  wr_hi_ref, wr_lo_ref, br_ref, din_ref, qdec_ref, kdec_ref,
                  bias_ref, pen_ref, kmask_ref, vmask_ref, omask_ref, tri_ref,
                  h3_ref, ids_ref, wts_ref, cnt_ref,
                  state_scr, kprev_scr, vprev_scr, o_scr, carry_scr):
    b = pl.program_id(0)
    n = pl.program_id(1)
    ts = x_ref.shape[0]

    @pl.when(n == 0)
    def _():
        state_scr[...] = jnp.zeros_like(state_scr)
        kprev_scr[...] = jnp.zeros_like(kprev_scr)
        vprev_scr[...] = jnp.zeros_like(vprev_scr)

    @pl.when((b == 0) & (n == 0))
    def _():
        carry_scr[...] = jnp.zeros_like(carry_scr)

    x = x_ref[...]
    xb = x.astype(BF16)

    def proj(lo, hi):
        return _dot(xb, w_in_ref[:, lo:hi])

    q_r, k_r, v_r, g_r = proj(_QR, _KR), proj(_KR, _VR), proj(_VR, _GR), proj(_GR, _QA)
    q_a, k_a = proj(_QA, _KA), proj(_KA, _VA)
    k_ab = k_a.astype(BF16)
    v_rep = proj(_VA, _VA + SWA_KV_HEADS * SWA_VREP).astype(BF16)
    first_pen = jnp.where(n == 0, pen_ref[...], 0.0)

    for c in range(ts // CHUNK):
        rs = slice(c * CHUNK, (c + 1) * CHUNK)
        for hd in range(RET_HEADS):
            cs = slice(hd * RET_HEAD_DIM, (hd + 1) * RET_HEAD_DIM)
            q = q_r[rs, cs]
            k = k_r[rs, cs]
            v = v_r[rs, cs].astype(BF16)
            scores = _dot_nt(q.astype(BF16), k.astype(BF16)) * din_ref[hd]
            inner = _dot(scores.astype(BF16), v)
            st = state_scr[hd]
            cross = _dot((q * qdec_ref[hd]).astype(BF16), st.astype(BF16))
            kv = _dot_tn((k * kdec_ref[hd]).astype(BF16), v)
            state_scr[hd] = st * cdec_ref[hd] + kv
            o = inner + cross
            mu = jnp.mean(o, axis=-1, keepdims=True)
            oc = o - mu
            var = jnp.mean(oc * oc, axis=-1, keepdims=True)
            on = oc * lax.rsqrt(var + GN_EPS) * gn_ref[:, cs] * _silu(g_r[rs, cs])
            o_scr[rs, cs] = on.astype(BF16)
        for j in range(SWA_KV_HEADS):
            ks = slice(j * SWA_HEAD_DIM, (j + 1) * SWA_HEAD_DIM)
            vs = slice(j * SWA_VREP, (j + 1) * SWA_VREP)
            if c == 0:
                kp = kprev_scr[:, ks].astype(BF16)
                vp = vprev_scr[:, vs].astype(BF16)
            else:
                ps = slice((c - 1) * CHUNK, c * CHUNK)
                kp = k_ab[ps, ks]
                vp = v_rep[ps, vs]
            kp = kp * kmask_ref[...]
            kcat = jnp.concatenate([kp, k_ab[rs, ks]], axis=0)
            vcat = jnp.concatenate([vp, v_rep[rs, vs]], axis=0)
            q0 = j * SWA_GROUP * SWA_HEAD_DIM
            qs = jnp.concatenate(
                [q_a[rs, q0 + g * SWA_HEAD_DIM:q0 + (g + 1) * SWA_HEAD_DIM] for g in range(SWA_GROUP)],
                axis=0)
            qs = (qs * (SWA_HEAD_DIM ** -0.5)).astype(BF16)
            s = _dot_nt(qs, kcat) + bias_ref[j]
            if c == 0:
                s = s + first_pen
            m = jnp.max(jnp.maximum(s[:, :CHUNK], s[:, CHUNK:]), axis=-1, keepdims=True)
            p = jnp.exp(s - m).astype(BF16)
            p_all = jnp.concatenate([p[g * CHUNK:(g + 1) * CHUNK] for g in range(SWA_GROUP)], axis=1)
            v_blk = jnp.concatenate([vcat * vmask_ref[g] for g in range(SWA_GROUP)], axis=0)
            num = _dot(p_all, v_blk)
            den = _dot(p_all, omask_ref[...])
            c0 = RET_WIDTH + j * SWA_GROUP * SWA_HEAD_DIM
            o_scr[rs, c0:c0 + SWA_GROUP * SWA_HEAD_DIM] = (num / den).astype(BF16)

    kprev_scr[...] = k_ab[ts - CHUNK:, :].astype(F32)
    vprev_scr[...] = v_rep[ts - CHUNK:, :].astype(F32)

    mix = _dot(o_scr[...], w_out_ref[...])
    h = _layer_norm(DEEPNORM_ALPHA * x + mix, ln_g_ref[...], ln_b_ref[...])
    _store_row_tiles(h3_ref, h)

    h_hi = h.astype(BF16)
    h_lo = (h - h_hi.astype(F32)).astype(BF16)
    logits = (_dot(h_hi, wr_hi_ref[...]) + _dot(h_lo, wr_hi_ref[...]) + _dot(h_hi, wr_lo_ref[...])
              + br_ref[...])
    lane = lax.broadcasted_iota(jnp.int32, (ts, LANES), 1).astype(F32)
    big = 1e9
    ninf = -jnp.inf
    gmask = (lane >= GROUP_LANE0) & (lane < GROUP_LANE0 + N_GROUPS)
    gl = jnp.where(gmask, logits, ninf)
    gmax = jnp.max(gl, axis=-1, keepdims=True)
    gidx = jnp.min(jnp.where(gl == gmax, lane, big), axis=-1, keepdims=True) - GROUP_LANE0
    g_w = 1.0 / jnp.sum(jnp.exp(gl - gmax), axis=-1, keepdims=True)
    lo = gidx * EXPERTS_PER_GROUP
    el = jnp.where((lane >= lo) & (lane < lo + EXPERTS_PER_GROUP), logits, ninf)
    m1 = jnp.max(el, axis=-1, keepdims=True)
    i1 = jnp.min(jnp.where(el == m1, lane, big), axis=-1, keepdims=True)
    el2 = jnp.where(lane == i1, ninf, el)
    m2 = jnp.max(el2, axis=-1, keepdims=True)
    i2 = jnp.min(jnp.where(el2 == m2, lane, big), axis=-1, keepdims=True)
    t = jnp.exp(m2 - m1)
    w1 = g_w / (1.0 + t)
    w2 = g_w * t / (1.0 + t)
    hit1 = lane == i1
    hit2 = lane == i2
    onehot = (hit1 | hit2).astype(BF16)
    prefix = _dot(tri_ref[...], onehot) + carry_scr[0:1, :]
    r1 = jnp.sum(jnp.where(hit1, prefix, 0.0), axis=-1, keepdims=True)
    r2 = jnp.sum(jnp.where(hit2, prefix, 0.0), axis=-1, keepdims=True)
    carry = carry_scr[0:1, :] + jnp.sum(onehot.astype(F32), axis=0, keepdims=True)
    carry_scr[0:1, :] = carry
    cnt_ref[...] = jnp.broadcast_to(carry, cnt_ref.shape)
    idsf = jnp.where(lane == 0, i1, jnp.where(lane == 1, i2,
                     jnp.where(lane == 2, r1, jnp.where(lane == 3, r2, 0.0))))
    ids_ref[...] = idsf.astype(jnp.int32)
    wts_ref[...] = jnp.where(lane == 0, w1, jnp.where(lane == 1, w2, 0.0))


def _mixer_call(x2, w_in, w_out, gn, ln_g, ln_b, wr_hi, wr_lo, br, consts, bsz, s_len, first_seq):
    cdec, din, qdec, kdec, bias, pen, kmask, vmask, omask, tri = consts
    t_len = bsz * s_len
    ns = s_len // SEQ_TILE
    tok = lambda b, n, *_: (b * ns + n, 0)
    tok_in = lambda b, n, *_: ((first_seq + b) * ns + n, 0)
    full2 = lambda b, n, *_: (0, 0)
    full3 = lambda b, n, *_: (0, 0, 0)
    grid_spec = pltpu.PrefetchScalarGridSpec(
        num_scalar_prefetch=1,
        grid=(bsz, ns),
        in_specs=[
            pl.BlockSpec((SEQ_TILE, D_MODEL), tok_in),
            pl.BlockSpec((D_MODEL, IN_WIDTH_TILED), full2),
            pl.BlockSpec((D_MODEL, D_MODEL), full2),
            pl.BlockSpec((1, RET_WIDTH), full2),
            pl.BlockSpec((1, D_MODEL), full2),
            pl.BlockSpec((1, D_MODEL), full2),
            pl.BlockSpec((D_MODEL, LANES), full2),
            pl.BlockSpec((D_MODEL, LANES), full2),
            pl.BlockSpec((1, LANES), full2),
            pl.BlockSpec((RET_HEADS, CHUNK, CHUNK), full3),
            pl.BlockSpec((RET_HEADS, CHUNK, RET_HEAD_DIM), full3),
            pl.BlockSpec((RET_HEADS, CHUNK, RET_HEAD_DIM), full3),
            pl.BlockSpec((SWA_KV_HEADS, SWA_GROUP * CHUNK, 2 * CHUNK), full3),
            pl.BlockSpec((1, 2 * CHUNK), full2),
            pl.BlockSpec((CHUNK, SWA_HEAD_DIM), full2),
            pl.BlockSpec((SWA_GROUP, 2 * CHUNK, SWA_VREP), full3),
            pl.BlockSpec((SWA_GROUP * 2 * CHUNK, SWA_VREP), full2),
            pl.BlockSpec((SEQ_TILE, SEQ_TILE), full2),
        ],
        out_specs=[
            pl.BlockSpec((SEQ_TILE * ROW_SUBLANES, LANES), tok),
            pl.BlockSpec((SEQ_TILE, LANES), tok),
            pl.BlockSpec((SEQ_TILE, LANES), tok),
            pl.BlockSpec((8, LANES), full2),
        ],
        scratch_shapes=[
            pltpu.VMEM((RET_HEADS, RET_HEAD_DIM, RET_HEAD_DIM), F32),
            pltpu.VMEM((CHUNK, SWA_KV_WIDTH), F32),
            pltpu.VMEM((CHUNK, SWA_KV_HEADS * SWA_VREP), F32),
            pltpu.VMEM((SEQ_TILE, D_MODEL), BF16),
            pltpu.VMEM((8, LANES), F32),
        ],
    )
    return pl.pallas_call(
        _mixer_kernel,
        grid_spec=grid_spec,
        out_shape=[
            jax.ShapeDtypeStruct((t_len * ROW_SUBLANES, LANES), F32),
            jax.ShapeDtypeStruct((t_len, LANES), jnp.int32),
            jax.ShapeDtypeStruct((t_len, LANES), F32),
            jax.ShapeDtypeStruct((8, LANES), F32),
        ],
        compiler_params=pltpu.CompilerParams(
            dimension_semantics=("arbitrary", "arbitrary"), vmem_limit_bytes=VMEM_LIMIT),
        name="mixer_router",
    )(cdec, x2, w_in, w_out, gn, ln_g, ln_b, wr_hi, wr_lo, br, din, qdec, kdec, bias, pen, kmask, vmask, omask, tri)


def _sc_mesh():
    return plsc.VectorSubcoreMesh(core_axis_name="core", subcore_axis_name="subcore")


def _sc_dispatch(h_rows, dest, n_rows):
    t_len = h_rows.shape[0]
    n_assign = dest.shape[0]
    per_worker = n_rows // SC_WORKERS
    n_chunks = per_worker // SC_CHUNK_ROWS
    assert n_chunks * SC_CHUNK_ROWS * SC_WORKERS == n_rows and n_rows <= 3 * t_len
    assert per_worker % SC_LANES == 0 and n_assign % SC_LANES == 0 and n_assign == TOP_K * t_len

    @functools.partial(
        pl.kernel, mesh=_sc_mesh(), name="moe_dispatch_sc",
        compiler_params=pltpu.CompilerParams(needs_layout_passes=False),
        out_type=jax.ShapeDtypeStruct((n_rows, ROW_SUBLANES, LANES), F32),
        scratch_types=[pltpu.VMEM((n_assign,), jnp.int32),
                       pltpu.VMEM((per_worker,), jnp.int32),
                       pltpu.VMEM((SC_CHUNK_ROWS, ROW_SUBLANES, LANES), F32),
                       pltpu.SemaphoreType.DMA])
    def dispatch(h_hbm, dest_hbm, rows_hbm, dest_v, src_v, buf, sem):
        wid = lax.axis_index("subcore") * SC_CORES + lax.axis_index("core")
        base = wid * per_worker
        pltpu.sync_copy(dest_hbm, dest_v)
        lane = lax.iota(jnp.int32, SC_LANES)

        def wrap(a):
            a = jnp.where(a >= t_len, a - t_len, a)
            return jnp.where(a >= t_len, a - t_len, a)

        @pl.loop(0, per_worker // SC_LANES)
        def _(i):
            src_v[pl.ds(i * SC_LANES, SC_LANES)] = wrap(base + i * SC_LANES + lane)

        @pl.loop(0, n_assign // SC_LANES)
        def _(i):
            d = dest_v[pl.ds(i * SC_LANES, SC_LANES)] - base
            hit = (d >= 0) & (d < per_worker)
            plsc.store_scatter(src_v, [jnp.where(hit, d, 0)], wrap(i * SC_LANES + lane), mask=hit)

        @pl.loop(0, n_chunks)
        def _(c):
            off = pl.multiple_of(c * SC_CHUNK_ROWS, SC_CHUNK_ROWS)
            pltpu.async_copy(h_hbm.at[src_v.at[pl.ds(off, SC_CHUNK_ROWS)]], buf, sem).wait()
            pltpu.sync_copy(buf, rows_hbm.at[pl.ds(base + off, SC_CHUNK_ROWS)])

    return dispatch(h_rows, dest)


def _sc_gather_rows(table, idx, name):
    m = idx.shape[0]
    per_worker = m // (SC_CHUNK_ROWS * SC_WORKERS)
    assert per_worker * SC_CHUNK_ROWS * SC_WORKERS == m

    @functools.partial(
        pl.kernel, mesh=_sc_mesh(), name=name,
        out_type=jax.ShapeDtypeStruct((m, ROW_SUBLANES, LANES), F32),
        scratch_types=[pltpu.VMEM((m,), jnp.int32),
                       pltpu.VMEM((SC_CHUNK_ROWS, ROW_SUBLANES, LANES), F32),
                       pltpu.SemaphoreType.DMA])
    def gather(table_hbm, idx_hbm, out_hbm, idx_v, buf, sem):
        wid = lax.axis_index("subcore") * SC_CORES + lax.axis_index("core")
        pltpu.sync_copy(idx_hbm, idx_v)

        @pl.loop(0, per_worker)
        def _(j):
            off = pl.multiple_of((j * SC_WORKERS + wid) * SC_CHUNK_ROWS, SC_CHUNK_ROWS)
            pltpu.async_copy(table_hbm.at[idx_v.at[pl.ds(off, SC_CHUNK_ROWS)]], buf, sem).wait()
            pltpu.sync_copy(buf, out_hbm.at[pl.ds(off, SC_CHUNK_ROWS)])

    return gather(table, idx)


def _expert_kernel(bexp_ref, nused_ref, rows_ref, wg_ref, wu_ref, wd_ref, y_ref, wg_s, wu_s, wd_s):
    p = pl.program_id(0)
    prev = bexp_ref[jnp.maximum(p - 1, 0)]

    @pl.when((p == 0) | (bexp_ref[p] != prev))
    def _():
        wg_s[...] = wg_ref[0].astype(BF16)
        wu_s[...] = wu_ref[0].astype(BF16)
        wd_s[...] = wd_ref[0].astype(BF16)

    @pl.when(p < nused_ref[0])
    def _():
        xb = _load_row_tiles(rows_ref).astype(BF16)
        g = _dot(xb, wg_s[...])
        u = _dot(xb, wu_s[...])
        a = (_silu(g) * u).astype(BF16)
        _store_row_tiles(y_ref, _dot(a, wd_s[...]))

    @pl.when(p >= nused_ref[0])
    def _():
        y_ref[...] = jnp.zeros_like(y_ref)


def _expert_call(bexp, nused, rows, w_gate, w_up, w_down):
    n_rows = rows.shape[0] // ROW_SUBLANES
    n_blocks = n_rows // MOE_ROWS
    blk = lambda p, be, nu: (jnp.minimum(p, nu[0] - 1), 0)
    wsel = lambda p, be, nu: (be[p], 0, 0)
    grid_spec = pltpu.PrefetchScalarGridSpec(
        num_scalar_prefetch=2,
        grid=(n_blocks,),
        in_specs=[
            pl.BlockSpec((MOE_ROWS * ROW_SUBLANES, LANES), blk),
            pl.BlockSpec((1, D_MODEL, D_EXPERT), wsel),
            pl.BlockSpec((1, D_MODEL, D_EXPERT), wsel),
            pl.BlockSpec((1, D_EXPERT, D_MODEL), wsel),
        ],
        out_specs=pl.BlockSpec((MOE_ROWS * ROW_SUBLANES, LANES), lambda p, be, nu: (p, 0)),
        scratch_shapes=[
            pltpu.VMEM((D_MODEL, D_EXPERT), BF16),
            pltpu.VMEM((D_MODEL, D_EXPERT), BF16),
            pltpu.VMEM((D_EXPERT, D_MODEL), BF16),
        ],
    )
    return pl.pallas_call(
        _expert_kernel,
        grid_spec=grid_spec,
        out_shape=jax.ShapeDtypeStruct((n_rows * ROW_SUBLANES, LANES), F32),
        compiler_params=pltpu.CompilerParams(
            dimension_semantics=("arbitrary",), vmem_limit_bytes=VMEM_LIMIT),
        name="moe_experts",
    )(bexp, nused, rows, w_gate, w_up, w_down)


def _combine_kernel(h_ref, y0_ref, y1_ref, wts_ref, ln_g_ref, ln_b_ref, *rest):
    out_ref = rest[-1]
    wts = wts_ref[...]
    ffn = _load_row_tiles(y0_ref) * wts[:, 0:1] + _load_row_tiles(y1_ref) * wts[:, 1:2]
    out_ref[...] = _layer_norm(DEEPNORM_ALPHA * _load_row_tiles(h_ref) + ffn, ln_g_ref[...], ln_b_ref[...])


def _combine_call(h_rows, yk, wts, ln_g, ln_b, out_prev, split, t_total):
    t_len = wts.shape[0]
    n_tiles = t_len // COMBINE_TILE
    tok = lambda i: (i, 0)
    full2 = lambda i: (0, 0)
    tiles = pl.BlockSpec((COMBINE_TILE * ROW_SUBLANES, LANES), tok)
    tiles_k1 = pl.BlockSpec((COMBINE_TILE * ROW_SUBLANES, LANES), lambda i: (i + n_tiles, 0))
    in_specs = [tiles, tiles, tiles_k1,
                pl.BlockSpec((COMBINE_TILE, LANES), tok),
                pl.BlockSpec((1, D_MODEL), full2),
                pl.BlockSpec((1, D_MODEL), full2)]
    args = [h_rows, yk, yk, wts, ln_g, ln_b]
    aliases = {}
    if out_prev is not None:
        in_specs.append(pl.BlockSpec(memory_space=pl.ANY))
        args.append(out_prev)
        aliases = {len(args) - 1: 0}
    return pl.pallas_call(
        _combine_kernel,
        grid=(n_tiles,),
        in_specs=in_specs,
        out_specs=pl.BlockSpec((COMBINE_TILE, D_MODEL), lambda i: (i + split * n_tiles, 0)),
        out_shape=jax.ShapeDtypeStruct((t_total, D_MODEL), F32),
        input_output_aliases=aliases,
        compiler_params=pltpu.CompilerParams(
            dimension_semantics=("arbitrary",), vmem_limit_bytes=VMEM_LIMIT),
        name="moe_combine_ln",
    )(*args)


def _mixer_constants(attn_sinks):
    c = CHUNK
    log_g = jnp.log1p(-jnp.exp2(-5.0 - jnp.arange(RET_HEADS, dtype=F32)))
    idx = jnp.arange(c, dtype=F32)
    diff = idx[:, None] - idx[None, :]
    scale = RET_HEAD_DIM ** -0.5
    din = jnp.where(diff >= 0, jnp.exp(log_g[:, None, None] * jnp.maximum(diff, 0.0)), 0.0) * scale
    qdec = jnp.broadcast_to(jnp.exp(log_g[:, None] * (idx + 1.0))[:, :, None], (RET_HEADS, c, RET_HEAD_DIM))
    kdec = jnp.broadcast_to((jnp.exp(log_g[:, None] * (c - 1.0 - idx)) * scale)[:, :, None],
                            (RET_HEADS, c, RET_HEAD_DIM))
    cdec = jnp.exp(log_g * c)
    slopes = jnp.exp2(-8.0 * (jnp.arange(SWA_HEADS, dtype=F32) + 1.0) / SWA_HEADS)
    r = jnp.arange(c)[:, None]
    col = jnp.arange(c)[None, :]
    dist_prev = (r - col + c).astype(F32)
    dist_cur = (r - col).astype(F32)
    bprev = jnp.where((r < col)[None], -slopes[:, None, None] * dist_prev[None], NEG)
    bcur = jnp.where((r >= col)[None], -slopes[:, None, None] * dist_cur[None], NEG)
    bprev = bprev.at[:, :, 0].set(jnp.broadcast_to(attn_sinks.astype(F32)[:, None], (SWA_HEADS, c)))
    bias = jnp.concatenate([bprev, bcur], axis=-1).reshape(SWA_KV_HEADS, SWA_GROUP * c, 2 * c)
    key = jnp.arange(2 * c)
    pen = jnp.where((key >= 1) & (key < c), NEG, 0.0).astype(F32)[None, :]
    lane_head = jnp.arange(SWA_VREP) // SWA_HEAD_DIM
    own = lane_head[None, None, :] == jnp.arange(SWA_GROUP)[:, None, None]
    kmask = jnp.broadcast_to(jnp.arange(c)[:, None] > 0, (c, SWA_HEAD_DIM)).astype(BF16)
    vmask = (own & (key[None, :, None] > 0)).astype(BF16)
    omask = jnp.broadcast_to(own, (SWA_GROUP, 2 * c, SWA_VREP)).astype(BF16).reshape(
        SWA_GROUP * 2 * c, SWA_VREP)
    tr = jnp.arange(SEQ_TILE)
    tri = (tr[None, :] < tr[:, None]).astype(BF16)
    return cdec.astype(F32), din.astype(F32), qdec.astype(F32), kdec.astype(F32), \
        bias.astype(F32), pen, kmask, vmask, omask, tri


def _tile_v_columns(w_in):
    v_cols = w_in[:, _VA:].reshape(D_MODEL, SWA_KV_HEADS, 1, SWA_HEAD_DIM)
    v_cols = jnp.broadcast_to(v_cols, (D_MODEL, SWA_KV_HEADS, SWA_GROUP, SWA_HEAD_DIM))
    return jnp.concatenate([w_in[:, :_VA], v_cols.reshape(D_MODEL, SWA_KV_HEADS * SWA_VREP)], axis=1)


def _router_tables(w_group_router, b_group_router, w_expert_router, b_expert_router):
    w_e = jnp.transpose(w_expert_router, (1, 0, 2)).reshape(D_MODEL, N_EXPERTS)
    w = jnp.concatenate([w_e, w_group_router,
                         jnp.zeros((D_MODEL, LANES - N_EXPERTS - N_GROUPS), F32)], axis=1)
    bias = jnp.concatenate([b_expert_router.reshape(N_EXPERTS), b_group_router,
                            jnp.zeros((LANES - N_EXPERTS - N_GROUPS,), F32)])[None, :]
    w_hi = w.astype(BF16)
    w_lo = (w - w_hi.astype(F32)).astype(BF16)
    return w_hi, w_lo, bias


def kernel(x, w_in, ret_gn_g, attn_sinks, w_out, ln1_g, ln1_b, w_group_router, b_group_router,
           w_expert_router, b_expert_router, w_gate, w_up, w_down, ln2_g, ln2_b):
    bsz, s_len, d = x.shape
    assert d == D_MODEL and s_len % SEQ_TILE == 0 and w_in.shape[0] == DEPTH == 1
    assert bsz % TOKEN_SPLITS == 0
    t_total = bsz * s_len
    bsz_s = bsz // TOKEN_SPLITS
    t_len = bsz_s * s_len
    n_blocks = t_len * TOP_K // MOE_ROWS + N_EXPERTS
    n_rows = n_blocks * MOE_ROWS

    consts = _mixer_constants(attn_sinks[0])
    wr_hi, wr_lo, br = _router_tables(w_group_router[0], b_group_router[0],
                                      w_expert_router[0], b_expert_router[0])
    w_in_b = _tile_v_columns(w_in[0]).astype(BF16)
    w_out_b = w_out[0].astype(BF16)
    x2 = x.reshape(t_total, d)

    out = None
    for sp in range(TOKEN_SPLITS):
        h_rows, ids, wts, cnt = _mixer_call(
            x2, w_in_b, w_out_b, ret_gn_g[0][None, :], ln1_g[0][None, :], ln1_b[0][None, :],
            wr_hi, wr_lo, br, consts, bsz_s, s_len, sp * bsz_s)

        counts = cnt[0, :N_EXPERTS].astype(jnp.int32)
        padded = (counts + MOE_ROWS - 1) // MOE_ROWS * MOE_ROWS
        pend = jnp.cumsum(padded)
        pstart = pend - padded
        onehot = ids[:, 0:TOP_K, None] == jnp.arange(N_EXPERTS, dtype=jnp.int32)
        dest = jnp.sum(jnp.where(onehot, pstart, 0), axis=-1) + ids[:, TOP_K:2 * TOP_K]
        dest = dest.T.reshape(-1)
        nused = (pend[-1:] // MOE_ROWS).astype(jnp.int32)
        blk_start = jnp.minimum(jnp.arange(n_blocks, dtype=jnp.int32), nused[0] - 1) * MOE_ROWS
        bexp = jnp.minimum(jnp.sum(pend[None, :] <= blk_start[:, None], axis=-1),
                           N_EXPERTS - 1).astype(jnp.int32)

        rows = _sc_dispatch(h_rows.reshape(t_len, ROW_SUBLANES, LANES), dest, n_rows)
        y = _expert_call(bexp, nused, rows.reshape(n_rows * ROW_SUBLANES, LANES), w_gate[0], w_up[0], w_down[0])
        yk = _sc_gather_rows(y.reshape(n_rows, ROW_SUBLANES, LANES), dest, "moe_combine_sc")
        out = _combine_call(h_rows, yk.reshape(TOP_K * t_len * ROW_SUBLANES, LANES), wts,
                            ln2_g[0][None, :], ln2_b[0][None, :], out, sp, t_total)
    return out.reshape(bsz, s_len, d)
```

```python
import functools

import jax
import jax.numpy as jnp
from jax import lax
from jax.experimental import pallas as pl
from jax.experimental.pallas import tpu as pltpu
from jax.experimental.pallas import tpu_sc as plsc

F32 = jnp.float32
BF16 = jnp.bfloat16

D_MODEL = 1024
RET_HEADS = 4
RET_HEAD_DIM = 128
RET_WIDTH = RET_HEADS * RET_HEAD_DIM
CHUNK = 128
SWA_HEADS = 8
SWA_KV_HEADS = 2
SWA_GROUP = SWA_HEADS // SWA_KV_HEADS
SWA_HEAD_DIM = 64
SWA_WIDTH = SWA_HEADS * SWA_HEAD_DIM
SWA_KV_WIDTH = SWA_KV_HEADS * SWA_HEAD_DIM
IN_WIDTH = 4 * RET_WIDTH + SWA_WIDTH + 2 * SWA_KV_WIDTH
N_GROUPS = 4
EXPERTS_PER_GROUP = 8
N_EXPERTS = N_GROUPS * EXPERTS_PER_GROUP
TOP_K = 2
D_EXPERT = 512
LN_EPS = 1e-5
GN_EPS = 1e-6
DEPTH = 1
DEEPNORM_ALPHA = (2 * DEPTH) ** 0.25
NEG = -1e30

LANES = 128
ROW_SUBLANES = D_MODEL // LANES
SEQ_TILE = 512
MOE_ROWS = 256
COMBINE_TILE = 256
TOKEN_SPLITS = 2
SC_CORES = 2
SC_SUBCORES = 16
SC_WORKERS = SC_CORES * SC_SUBCORES
SC_LANES = 16
SC_CHUNK_ROWS = 32
VMEM_LIMIT = 56 * 1024 * 1024

_QR, _KR, _VR, _GR = 0, RET_WIDTH, 2 * RET_WIDTH, 3 * RET_WIDTH
_QA = 4 * RET_WIDTH
_KA = _QA + SWA_WIDTH
_VA = _KA + SWA_KV_WIDTH
SWA_VREP = SWA_GROUP * SWA_HEAD_DIM
IN_WIDTH_TILED = _VA + SWA_KV_HEADS * SWA_VREP
GROUP_LANE0 = N_EXPERTS


def _dot(a, b):
    return jnp.dot(a, b, preferred_element_type=F32)


def _dot_nt(a, b):
    return lax.dot_general(a, b, (((1,), (1,)), ((), ())), preferred_element_type=F32)


def _dot_tn(a, b):
    return lax.dot_general(a, b, (((0,), (0,)), ((), ())), preferred_element_type=F32)


def _layer_norm(z, g, b):
    mu = jnp.mean(z, axis=-1, keepdims=True)
    zc = z - mu
    var = jnp.mean(zc * zc, axis=-1, keepdims=True)
    return zc * lax.rsqrt(var + LN_EPS) * g + b


def _silu(g):
    return g / (1.0 + jnp.exp(-g))


def _store_row_tiles(ref, val):
    n = val.shape[0]
    for j in range(ROW_SUBLANES):
        ref[pl.ds(j, n, stride=ROW_SUBLANES), :] = val[:, j * LANES:(j + 1) * LANES]


def _load_row_tiles(ref):
    n = ref.shape[0] // ROW_SUBLANES
    return jnp.concatenate([ref[pl.ds(j, n, stride=ROW_SUBLANES), :] for j in range(ROW_SUBLANES)], axis=1)


def _mixer_kernel(cdec_ref, x_ref, w_in_ref, w_out_ref, gn_ref, ln_g_ref, ln_b_ref,
                  wr_hi_ref, wr_lo_ref, br_ref, din_ref, qdec_ref, kdec_ref,
                  bias_ref, pen_ref, kmask_ref, vmask_ref, omask_ref, tri_ref,
                  h3_ref, ids_ref, wts_ref, cnt_ref,
                  state_scr, kprev_scr, vprev_scr, o_scr, carry_scr):
    b = pl.program_id(0)
    n = pl.program_id(1)
    ts = x_ref.shape[0]

    @pl.when(n == 0)
    def _():
        state_scr[...] = jnp.zeros_like(state_scr)
        kprev_scr[...] = jnp.zeros_like(kprev_scr)
        vprev_scr[...] = jnp.zeros_like(vprev_scr)

    @pl.when((b == 0) & (n == 0))
    def _():
        carry_scr[...] = jnp.zeros_like(carry_scr)

    x = x_ref[...]
    xb = x.astype(BF16)

    def proj(lo, hi):
        return _dot(xb, w_in_ref[:, lo:hi])

    q_r, k_r, v_r, g_r = proj(_QR, _KR), proj(_KR, _VR), proj(_VR, _GR), proj(_GR, _QA)
    q_a, k_a = proj(_QA, _KA), proj(_KA, _VA)
    k_ab = k_a.astype(BF16)
    v_rep = proj(_VA, _VA + SWA_KV_HEADS * SWA_VREP).astype(BF16)
    first_pen = jnp.where(n == 0, pen_ref[...], 0.0)

    for c in range(ts // CHUNK):
        rs = slice(c * CHUNK, (c + 1) * CHUNK)
        for hd in range(RET_HEADS):
            cs = slice(hd * RET_HEAD_DIM, (hd + 1) * RET_HEAD_DIM)
            q = q_r[rs, cs]
            k = k_r[rs, cs]
            v = v_r[rs, cs].astype(BF16)
            scores = _dot_nt(q.astype(BF16), k.astype(BF16)) * din_ref[hd]
            inner = _dot(scores.astype(BF16), v)
            st = state_scr[hd]
            cross = _dot((q * qdec_ref[hd]).astype(BF16), st.astype(BF16))
            kv = _dot_tn((k * kdec_ref[hd]).astype(BF16), v)
            state_scr[hd] = st * cdec_ref[hd] + kv
            o = inner + cross
            mu = jnp.mean(o, axis=-1, keepdims=True)
            oc = o - mu
            var = jnp.mean(oc * oc, axis=-1, keepdims=True)
            on = oc * lax.rsqrt(var + GN_EPS) * gn_ref[:, cs] * _silu(g_r[rs, cs])
            o_scr[rs, cs] = on.astype(BF16)
        for j in range(SWA_KV_HEADS):
            ks = slice(j * SWA_HEAD_DIM, (j + 1) * SWA_HEAD_DIM)
            vs = slice(j * SWA_VREP, (j + 1) * SWA_VREP)
            if c == 0:
                kp = kprev_scr[:, ks].astype(BF16)
                vp = vprev_scr[:, vs].astype(BF16)
            else:
                ps = slice((c - 1) * CHUNK, c * CHUNK)
                kp = k_ab[ps, ks]
                vp = v_rep[ps, vs]
            kp = kp * kmask_ref[...]
            kcat = jnp.concatenate([kp, k_ab[rs, ks]], axis=0)
            vcat = jnp.concatenate([vp, v_rep[rs, vs]], axis=0)
            q0 = j * SWA_GROUP * SWA_HEAD_DIM
            qs = jnp.concatenate(
                [q_a[rs, q0 + g * SWA_HEAD_DIM:q0 + (g + 1) * SWA_HEAD_DIM] for g in range(SWA_GROUP)],
                axis=0)
            qs = (qs * (SWA_HEAD_DIM ** -0.5)).astype(BF16)
            s = _dot_nt(qs, kcat) + bias_ref[j]
            if c == 0:
                s = s + first_pen
            m = jnp.max(jnp.maximum(s[:, :CHUNK], s[:, CHUNK:]), axis=-1, keepdims=True)
            p = jnp.exp(s - m).astype(BF16)
            p_all = jnp.concatenate([p[g * CHUNK:(g + 1) * CHUNK] for g in range(SWA_GROUP)], axis=1)
            v_blk = jnp.concatenate([vcat * vmask_ref[g] for g in range(SWA_GROUP)], axis=0)
            num = _dot(p_all, v_blk)
            den = _dot(p_all, omask_ref[...])
            c0 = RET_WIDTH + j * SWA_GROUP * SWA_HEAD_DIM
            o_scr[rs, c0:c0 + SWA_GROUP * SWA_HEAD_DIM] = (num / den).astype(BF16)

    kprev_scr[...] = k_ab[ts - CHUNK:, :].astype(F32)
    vprev_scr[...] = v_rep[ts - CHUNK:, :].astype(F32)

    mix = _dot(o_scr[...], w_out_ref[...])
    h = _layer_norm(DEEPNORM_ALPHA * x + mix, ln_g_ref[...], ln_b_ref[...])
    _store_row_tiles(h3_ref, h)

    h_hi = h.astype(BF16)
    h_lo = (h - h_hi.astype(F32)).astype(BF16)
    logits = (_dot(h_hi, wr_hi_ref[...]) + _dot(h_lo, wr_hi_ref[...]) + _dot(h_hi, wr_lo_ref[...])
              + br_ref[...])
    lane = lax.broadcasted_iota(jnp.int32, (ts, LANES), 1).astype(F32)
    big = 1e9
    ninf = -jnp.inf
    gmask = (lane >= GROUP_LANE0) & (lane < GROUP_LANE0 + N_GROUPS)
    gl = jnp.where(gmask, logits, ninf)
    gmax = jnp.max(gl, axis=-1, keepdims=True)
    gidx = jnp.min(jnp.where(gl == gmax, lane, big), axis=-1, keepdims=True) - GROUP_LANE0
    g_w = 1.0 / jnp.sum(jnp.exp(gl - gmax), axis=-1, keepdims=True)
    lo = gidx * EXPERTS_PER_GROUP
    el = jnp.where((lane >= lo) & (lane < lo + EXPERTS_PER_GROUP), logits, ninf)
    m1 = jnp.max(el, axis=-1, keepdims=True)
    i1 = jnp.min(jnp.where(el == m1, lane, big), axis=-1, keepdims=True)
    el2 = jnp.where(lane == i1, ninf, el)
    m2 = jnp.max(el2, axis=-1, keepdims=True)
    i2 = jnp.min(jnp.where(el2 == m2, lane, big), axis=-1, keepdims=True)
    t = jnp.exp(m2 - m1)
    w1 = g_w / (1.0 + t)
    w2 = g_w * t / (1.0 + t)
    hit1 = lane == i1
    hit2 = lane == i2
    onehot = (hit1 | hit2).astype(BF16)
    prefix = _dot(tri_ref[...], onehot) + carry_scr[0:1, :]
    r1 = jnp.sum(jnp.where(hit1, prefix, 0.0), axis=-1, keepdims=True)
    r2 = jnp.sum(jnp.where(hit2, prefix, 0.0), axis=-1, keepdims=True)
    carry = carry_scr[0:1, :] + jnp.sum(onehot.astype(F32), axis=0, keepdims=True)
    carry_scr[0:1, :] = carry
    cnt_ref[...] = jnp.broadcast_to(carry, cnt_ref.shape)
    idsf = jnp.where(lane == 0, i1, jnp.where(lane == 1, i2,
                     jnp.where(lane == 2, r1, jnp.where(lane == 3, r2, 0.0))))
    ids_ref[...] = idsf.astype(jnp.int32)
    wts_ref[...] = jnp.where(lane == 0, w1, jnp.where(lane == 1, w2, 0.0))


def _mixer_call(x2, w_in, w_out, gn, ln_g, ln_b, wr_hi, wr_lo, br, consts, bsz, s_len, first_seq):
    cdec, din, qdec, kdec, bias, pen, kmask, vmask, omask, tri = consts
    t_len = bsz * s_len
    ns = s_len // SEQ_TILE
    tok = lambda b, n, *_: (b * ns + n, 0)
    tok_in = lambda b, n, *_: ((first_seq + b) * ns + n, 0)
    full2 = lambda b, n, *_: (0, 0)
    full3 = lambda b, n, *_: (0, 0, 0)
    grid_spec = pltpu.PrefetchScalarGridSpec(
        num_scalar_prefetch=1,
        grid=(bsz, ns),
        in_specs=[
            pl.BlockSpec((SEQ_TILE, D_MODEL), tok_in),
            pl.BlockSpec((D_MODEL, IN_WIDTH_TILED), full2),
            pl.BlockSpec((D_MODEL, D_MODEL), full2),
            pl.BlockSpec((1, RET_WIDTH), full2),
            pl.BlockSpec((1, D_MODEL), full2),
            pl.BlockSpec((1, D_MODEL), full2),
            pl.BlockSpec((D_MODEL, LANES), full2),
            pl.BlockSpec((D_MODEL, LANES), full2),
            pl.BlockSpec((1, LANES), full2),
            pl.BlockSpec((RET_HEADS, CHUNK, CHUNK), full3),
            pl.BlockSpec((RET_HEADS, CHUNK, RET_HEAD_DIM), full3),
            pl.BlockSpec((RET_HEADS, CHUNK, RET_HEAD_DIM), full3),
            pl.BlockSpec((SWA_KV_HEADS, SWA_GROUP * CHUNK, 2 * CHUNK), full3),
            pl.BlockSpec((1, 2 * CHUNK), full2),
            pl.BlockSpec((CHUNK, SWA_HEAD_DIM), full2),
            pl.BlockSpec((SWA_GROUP, 2 * CHUNK, SWA_VREP), full3),
            pl.BlockSpec((SWA_GROUP * 2 * CHUNK, SWA_VREP), full2),
            pl.BlockSpec((SEQ_TILE, SEQ_TILE), full2),
        ],
        out_specs=[
            pl.BlockSpec((SEQ_TILE * ROW_SUBLANES, LANES), tok),
            pl.BlockSpec((SEQ_TILE, LANES), tok),
            pl.BlockSpec((SEQ_TILE, LANES), tok),
            pl.BlockSpec((8, LANES), full2),
        ],
        scratch_shapes=[
            pltpu.VMEM((RET_HEADS, RET_HEAD_DIM, RET_HEAD_DIM), F32),
            pltpu.VMEM((CHUNK, SWA_KV_WIDTH), F32),
            pltpu.VMEM((CHUNK, SWA_KV_HEADS * SWA_VREP), F32),
            pltpu.VMEM((SEQ_TILE, D_MODEL), BF16),
            pltpu.VMEM((8, LANES), F32),
        ],
    )
    return pl.pallas_call(
        _mixer_kernel,
        grid_spec=grid_spec,
        out_shape=[
            jax.ShapeDtypeStruct((t_len * ROW_SUBLANES, LANES), F32),
            jax.ShapeDtypeStruct((t_len, LANES), jnp.int32),
            jax.ShapeDtypeStruct((t_len, LANES), F32),
            jax.ShapeDtypeStruct((8, LANES), F32),
        ],
        compiler_params=pltpu.CompilerParams(
            dimension_semantics=("arbitrary", "arbitrary"), vmem_limit_bytes=VMEM_LIMIT),
        name="mixer_router",
    )(cdec, x2, w_in, w_out, gn, ln_g, ln_b, wr_hi, wr_lo, br, din, qdec, kdec, bias, pen, kmask, vmask, omask, tri)


def _sc_mesh():
    return plsc.VectorSubcoreMesh(core_axis_name="core", subcore_axis_name="subcore")


def _sc_dispatch(h_rows, dest, n_rows):
    t_len = h_rows.shape[0]
    n_assign = dest.shape[0]
    per_worker = n_rows // SC_WORKERS
    n_chunks = per_worker // SC_CHUNK_ROWS
    assert n_chunks * SC_CHUNK_ROWS * SC_WORKERS == n_rows and n_rows <= 3 * t_len
    assert per_worker % SC_LANES == 0 and n_assign % SC_LANES == 0 and n_assign == TOP_K * t_len

    @functools.partial(
        pl.kernel, mesh=_sc_mesh(), name="moe_dispatch_sc",
        compiler_params=pltpu.CompilerParams(needs_layout_passes=False),
        out_type=jax.ShapeDtypeStruct((n_rows, ROW_SUBLANES, LANES), F32),
        scratch_types=[pltpu.VMEM((n_assign,), jnp.int32),
                       pltpu.VMEM((per_worker,), jnp.int32),
                       pltpu.VMEM((SC_CHUNK_ROWS, ROW_SUBLANES, LANES), F32),
                       pltpu.SemaphoreType.DMA])
    def dispatch(h_hbm, dest_hbm, rows_hbm, dest_v, src_v, buf, sem):
        wid = lax.axis_index("subcore") * SC_CORES + lax.axis_index("core")
        base = wid * per_worker
        pltpu.sync_copy(dest_hbm, dest_v)
        lane = lax.iota(jnp.int32, SC_LANES)

        def wrap(a):
            a = jnp.where(a >= t_len, a - t_len, a)
            return jnp.where(a >= t_len, a - t_len, a)

        @pl.loop(0, per_worker // SC_LANES)
        def _(i):
            src_v[pl.ds(i * SC_LANES, SC_LANES)] = wrap(base + i * SC_LANES + lane)

        @pl.loop(0, n_assign // SC_LANES)
        def _(i):
            d = dest_v[pl.ds(i * SC_LANES, SC_LANES)] - base
            hit = (d >= 0) & (d < per_worker)
            plsc.store_scatter(src_v, [jnp.where(hit, d, 0)], wrap(i * SC_LANES + lane), mask=hit)

        @pl.loop(0, n_chunks)
        def _(c):
            off = pl.multiple_of(c * SC_CHUNK_ROWS, SC_CHUNK_ROWS)
            pltpu.async_copy(h_hbm.at[src_v.at[pl.ds(off, SC_CHUNK_ROWS)]], buf, sem).wait()
            pltpu.sync_copy(buf, rows_hbm.at[pl.ds(base + off, SC_CHUNK_ROWS)])

    return dispatch(h_rows, dest)


def _sc_gather_rows(table, idx, name):
    m = idx.shape[0]
    per_worker = m // (SC_CHUNK_ROWS * SC_WORKERS)
    assert per_worker * SC_CHUNK_ROWS * SC_WORKERS == m

    @functools.partial(
        pl.kernel, mesh=_sc_mesh(), name=name,
        out_type=jax.ShapeDtypeStruct((m, ROW_SUBLANES, LANES), F32),
        scratch_types=[pltpu.VMEM((m,), jnp.int32),
                       pltpu.VMEM((SC_CHUNK_ROWS, ROW_SUBLANES, LANES), F32),
                       pltpu.SemaphoreType.DMA])
    def gather(table_hbm, idx_hbm, out_hbm, idx_v, buf, sem):
        wid = lax.axis_index("subcore") * SC_CORES + lax.axis_index("core")
        pltpu.sync_copy(idx_hbm, idx_v)

        @pl.loop(0, per_worker)
        def _(j):
            off = pl.multiple_of((j * SC_WORKERS + wid) * SC_CHUNK_ROWS, SC_CHUNK_ROWS)
            pltpu.async_copy(table_hbm.at[idx_v.at[pl.ds(off, SC_CHUNK_ROWS)]], buf, sem).wait()
            pltpu.sync_copy(buf, out_hbm.at[pl.ds(off, SC_CHUNK_ROWS)])

    return gather(table, idx)


def _expert_kernel(sexp_ref, ssplit_ref, hold_ref, nsteps_ref, *refs):
    n = TOKEN_SPLITS
    row_refs, (wg_ref, wu_ref, wd_ref), y_refs = refs[:n], refs[n:n + 3], refs[n + 3:2 * n + 3]
    wg_s, wu_s, wd_s = refs[2 * n + 3:]
    p = pl.program_id(0)
    prev = sexp_ref[jnp.maximum(p - 1, 0)]

    @pl.when((p == 0) | (sexp_ref[p] != prev))
    def _():
        wg_s[...] = wg_ref[0].astype(BF16)
        wu_s[...] = wu_ref[0].astype(BF16)
        wd_s[...] = wd_ref[0].astype(BF16)

    for sp in range(n):
        @pl.when((p < nsteps_ref[0]) & (ssplit_ref[p] == sp))
        def _(x_ref=row_refs[sp], y_ref=y_refs[sp]):
            xb = _load_row_tiles(x_ref).astype(BF16)
            g = _dot(xb, wg_s[...])
            u = _dot(xb, wu_s[...])
            a = (_silu(g) * u).astype(BF16)
            _store_row_tiles(y_ref, _dot(a, wd_s[...]))


def _expert_steps(blk_counts, n_steps):
    n = blk_counts.shape[0]
    per_expert = jnp.sum(blk_counts, axis=0)
    cum = jnp.cumsum(per_expert)
    p = jnp.arange(n_steps, dtype=jnp.int32)
    sexp = jnp.minimum(jnp.sum(cum[None, :] <= p[:, None], axis=-1), N_EXPERTS - 1).astype(jnp.int32)
    onehot = sexp[:, None] == jnp.arange(N_EXPERTS, dtype=jnp.int32)[None, :]
    pick = lambda v: jnp.sum(jnp.where(onehot, v[None, :], 0), axis=-1)
    within = p - (pick(cum) - pick(per_expert))
    split_end = jnp.cumsum(blk_counts, axis=0)
    ssplit = jnp.zeros_like(p)
    for i in range(n - 1):
        ssplit = ssplit + (within >= pick(split_end[i])).astype(jnp.int32)
    valid = p < cum[-1]
    hold = jnp.stack([jnp.maximum(jnp.cumsum((valid & (ssplit == i)).astype(jnp.int32)) - 1, 0)
                      for i in range(n)])
    return sexp, ssplit.astype(jnp.int32), hold.reshape(-1).astype(jnp.int32), cum[-1:].astype(jnp.int32)


def _expert_call(blk_counts, rows, w_gate, w_up, w_down):
    n = len(rows)
    n_blocks = rows[0].shape[0] // (ROW_SUBLANES * MOE_ROWS)
    n_steps = n * n_blocks
    sexp, ssplit, hold, nsteps = _expert_steps(blk_counts, n_steps)

    def held_block(sp):
        return pl.BlockSpec((MOE_ROWS * ROW_SUBLANES, LANES),
                            lambda p, se, ss, hd, ns: (hd[sp * n_steps + p], 0))

    wsel = lambda p, se, ss, hd, ns: (se[p], 0, 0)
    grid_spec = pltpu.PrefetchScalarGridSpec(
        num_scalar_prefetch=4,
        grid=(n_steps,),
        in_specs=[held_block(sp) for sp in range(n)] + [
            pl.BlockSpec((1, D_MODEL, D_EXPERT), wsel),
            pl.BlockSpec((1, D_MODEL, D_EXPERT), wsel),
            pl.BlockSpec((1, D_EXPERT, D_MODEL), wsel),
        ],
        out_specs=[held_block(sp) for sp in range(n)],
        scratch_shapes=[
            pltpu.VMEM((D_MODEL, D_EXPERT), BF16),
            pltpu.VMEM((D_MODEL, D_EXPERT), BF16),
            pltpu.VMEM((D_EXPERT, D_MODEL), BF16),
        ],
    )
    return pl.pallas_call(
        _expert_kernel,
        grid_spec=grid_spec,
        out_shape=[jax.ShapeDtypeStruct(r.shape, F32) for r in rows],
        compiler_params=pltpu.CompilerParams(
            dimension_semantics=("arbitrary",), vmem_limit_bytes=VMEM_LIMIT),
        name="moe_experts",
    )(sexp, ssplit, hold, nsteps, *rows, w_gate, w_up, w_down)


def _combine_kernel(h_ref, y0_ref, y1_ref, wts_ref, ln_g_ref, ln_b_ref, *rest):
    out_ref = rest[-1]
    wts = wts_ref[...]
    ffn = _load_row_tiles(y0_ref) * wts[:, 0:1] + _load_row_tiles(y1_ref) * wts[:, 1:2]
    out_ref[...] = _layer_norm(DEEPNORM_ALPHA * _load_row_tiles(h_ref) + ffn, ln_g_ref[...], ln_b_ref[...])


def _combine_call(h_rows, yk, wts, ln_g, ln_b, out_prev, split, t_total):
    t_len = wts.shape[0]
    n_tiles = t_len // COMBINE_TILE
    tok = lambda i: (i, 0)
    full2 = lambda i: (0, 0)
    tiles = pl.BlockSpec((COMBINE_TILE * ROW_SUBLANES, LANES), tok)
    tiles_k1 = pl.BlockSpec((COMBINE_TILE * ROW_SUBLANES, LANES), lambda i: (i + n_tiles, 0))
    in_specs = [tiles, tiles, tiles_k1,
                pl.BlockSpec((COMBINE_TILE, LANES), tok),
                pl.BlockSpec((1, D_MODEL), full2),
                pl.BlockSpec((1, D_MODEL), full2)]
    args = [h_rows, yk, yk, wts, ln_g, ln_b]
    aliases = {}
    if out_prev is not None:
        in_specs.append(pl.BlockSpec(memory_space=pl.ANY))
        args.append(out_prev)
        aliases = {len(args) - 1: 0}
    return pl.pallas_call(
        _combine_kernel,
        grid=(n_tiles,),
        in_specs=in_specs,
        out_specs=pl.BlockSpec((COMBINE_TILE, D_MODEL), lambda i: (i + split * n_tiles, 0)),
        out_shape=jax.ShapeDtypeStruct((t_total, D_MODEL), F32),
        input_output_aliases=aliases,
        compiler_params=pltpu.CompilerParams(
            dimension_semantics=("arbitrary",), vmem_limit_bytes=VMEM_LIMIT),
        name="moe_combine_ln",
    )(*args)


def _mixer_constants(attn_sinks):
    c = CHUNK
    log_g = jnp.log1p(-jnp.exp2(-5.0 - jnp.arange(RET_HEADS, dtype=F32)))
    idx = jnp.arange(c, dtype=F32)
    diff = idx[:, None] - idx[None, :]
    scale = RET_HEAD_DIM ** -0.5
    din = jnp.where(diff >= 0, jnp.exp(log_g[:, None, None] * jnp.maximum(diff, 0.0)), 0.0) * scale
    qdec = jnp.broadcast_to(jnp.exp(log_g[:, None] * (idx + 1.0))[:, :, None], (RET_HEADS, c, RET_HEAD_DIM))
    kdec = jnp.broadcast_to((jnp.exp(log_g[:, None] * (c - 1.0 - idx)) * scale)[:, :, None],
                            (RET_HEADS, c, RET_HEAD_DIM))
    cdec = jnp.exp(log_g * c)
    slopes = jnp.exp2(-8.0 * (jnp.arange(SWA_HEADS, dtype=F32) + 1.0) / SWA_HEADS)
    r = jnp.arange(c)[:, None]
    col = jnp.arange(c)[None, :]
    dist_prev = (r - col + c).astype(F32)
    dist_cur = (r - col).astype(F32)
    bprev = jnp.where((r < col)[None], -slopes[:, None, None] * dist_prev[None], NEG)
    bcur = jnp.where((r >= col)[None], -slopes[:, None, None] * dist_cur[None], NEG)
    bprev = bprev.at[:, :, 0].set(jnp.broadcast_to(attn_sinks.astype(F32)[:, None], (SWA_HEADS, c)))
    bias = jnp.concatenate([bprev, bcur], axis=-1).reshape(SWA_KV_HEADS, SWA_GROUP * c, 2 * c)
    key = jnp.arange(2 * c)
    pen = jnp.where((key >= 1) & (key < c), NEG, 0.0).astype(F32)[None, :]
    lane_head = jnp.arange(SWA_VREP) // SWA_HEAD_DIM
    own = lane_head[None, None, :] == jnp.arange(SWA_GROUP)[:, None, None]
    kmask = jnp.broadcast_to(jnp.arange(c)[:, None] > 0, (c, SWA_HEAD_DIM)).astype(BF16)
    vmask = (own & (key[None, :, None] > 0)).astype(BF16)
    omask = jnp.broadcast_to(own, (SWA_GROUP, 2 * c, SWA_VREP)).astype(BF16).reshape(
        SWA_GROUP * 2 * c, SWA_VREP)
    tr = jnp.arange(SEQ_TILE)
    tri = (tr[None, :] < tr[:, None]).astype(BF16)
    return cdec.astype(F32), din.astype(F32), qdec.astype(F32), kdec.astype(F32), \
        bias.astype(F32), pen, kmask, vmask, omask, tri


def _tile_v_columns(w_in):
    v_cols = w_in[:, _VA:].reshape(D_MODEL, SWA_KV_HEADS, 1, SWA_HEAD_DIM)
    v_cols = jnp.broadcast_to(v_cols, (D_MODEL, SWA_KV_HEADS, SWA_GROUP, SWA_HEAD_DIM))
    return jnp.concatenate([w_in[:, :_VA], v_cols.reshape(D_MODEL, SWA_KV_HEADS * SWA_VREP)], axis=1)


def _router_tables(w_group_router, b_group_router, w_expert_router, b_expert_router):
    w_e = jnp.transpose(w_expert_router, (1, 0, 2)).reshape(D_MODEL, N_EXPERTS)
    w = jnp.concatenate([w_e, w_group_router,
                         jnp.zeros((D_MODEL, LANES - N_EXPERTS - N_GROUPS), F32)], axis=1)
    bias = jnp.concatenate([b_expert_router.reshape(N_EXPERTS), b_group_router,
                            jnp.zeros((LANES - N_EXPERTS - N_GROUPS,), F32)])[None, :]
    w_hi = w.astype(BF16)
    w_lo = (w - w_hi.astype(F32)).astype(BF16)
    return w_hi, w_lo, bias


def kernel(x, w_in, ret_gn_g, attn_sinks, w_out, ln1_g, ln1_b, w_group_router, b_group_router,
           w_expert_router, b_expert_router, w_gate, w_up, w_down, ln2_g, ln2_b):
    bsz, s_len, d = x.shape
    assert d == D_MODEL and s_len % SEQ_TILE == 0 and w_in.shape[0] == DEPTH == 1
    assert bsz % TOKEN_SPLITS == 0
    t_total = bsz * s_len
    bsz_s = bsz // TOKEN_SPLITS
    t_len = bsz_s * s_len
    n_blocks = t_len * TOP_K // MOE_ROWS + N_EXPERTS
    n_rows = n_blocks * MOE_ROWS

    consts = _mixer_constants(attn_sinks[0])
    wr_hi, wr_lo, br = _router_tables(w_group_router[0], b_group_router[0],
                                      w_expert_router[0], b_expert_router[0])
    w_in_b = _tile_v_columns(w_in[0]).astype(BF16)
    w_out_b = w_out[0].astype(BF16)
    x2 = x.reshape(t_total, d)

    routed, rows, blk_counts = [], [], []
    for sp in range(TOKEN_SPLITS):
        h_rows, ids, wts, cnt = _mixer_call(
            x2, w_in_b, w_out_b, ret_gn_g[0][None, :], ln1_g[0][None, :], ln1_b[0][None, :],
            wr_hi, wr_lo, br, consts, bsz_s, s_len, sp * bsz_s)

        counts = cnt[0, :N_EXPERTS].astype(jnp.int32)
        padded = (counts + MOE_ROWS - 1) // MOE_ROWS * MOE_ROWS
        pstart = jnp.cumsum(padded) - padded
        onehot = ids[:, 0:TOP_K, None] == jnp.arange(N_EXPERTS, dtype=jnp.int32)
        dest = jnp.sum(jnp.where(onehot, pstart, 0), axis=-1) + ids[:, TOP_K:2 * TOP_K]
        dest = dest.T.reshape(-1)

        rows.append(_sc_dispatch(h_rows.reshape(t_len, ROW_SUBLANES, LANES), dest, n_rows)
                    .reshape(n_rows * ROW_SUBLANES, LANES))
        blk_counts.append(padded // MOE_ROWS)
        routed.append((h_rows, wts, dest))

    ys = _expert_call(jnp.stack(blk_counts).astype(jnp.int32), rows, w_gate[0], w_up[0], w_down[0])

    out = None
    for sp, (h_rows, wts, dest) in enumerate(routed):
        yk = _sc_gather_rows(ys[sp].reshape(n_rows, ROW_SUBLANES, LANES), dest, "moe_combine_sc")
        out = _combine_call(h_rows, yk.reshape(TOP_K * t_len * ROW_SUBLANES, LANES), wts,
                            ln2_g[0][None, :], ln2_b[0][None, :], out, sp, t_total)
    return out.reshape(bsz, s_len, d)
```

```python
import functools

import jax
import jax.numpy as jnp
import numpy as np
from jax import lax
from jax.experimental import pallas as pl
from jax.experimental.pallas import tpu as pltpu
from jax.experimental.pallas import tpu_sc as plsc

F32 = jnp.float32
BF16 = jnp.bfloat16

D_MODEL = 1024
RET_HEADS = 4
RET_HEAD_DIM = 128
RET_WIDTH = RET_HEADS * RET_HEAD_DIM
CHUNK = 128
SWA_HEADS = 8
SWA_KV_HEADS = 2
SWA_GROUP = SWA_HEADS // SWA_KV_HEADS
SWA_HEAD_DIM = 64
SWA_WIDTH = SWA_HEADS * SWA_HEAD_DIM
SWA_KV_WIDTH = SWA_KV_HEADS * SWA_HEAD_DIM
IN_WIDTH = 4 * RET_WIDTH + SWA_WIDTH + 2 * SWA_KV_WIDTH
N_GROUPS = 4
EXPERTS_PER_GROUP = 8
N_EXPERTS = N_GROUPS * EXPERTS_PER_GROUP
TOP_K = 2
D_EXPERT = 512
LN_EPS = 1e-5
GN_EPS = 1e-6
DEPTH = 1
DEEPNORM_ALPHA = (2 * DEPTH) ** 0.25
NEG = -1e30

LANES = 128
ROW_SUBLANES = D_MODEL // LANES
SEQ_TILE = 512
MOE_ROWS = 256
COMBINE_TILE = 256
TOKEN_SPLITS = 2
SC_CORES = 2
SC_SUBCORES = 16
SC_WORKERS = SC_CORES * SC_SUBCORES
SC_LANES = 16
SC_CHUNK_ROWS = 32
VMEM_LIMIT = 56 * 1024 * 1024

_QR, _KR, _VR, _GR = 0, RET_WIDTH, 2 * RET_WIDTH, 3 * RET_WIDTH
_QA = 4 * RET_WIDTH
_KA = _QA + SWA_WIDTH
_VA = _KA + SWA_KV_WIDTH
SWA_VREP = SWA_GROUP * SWA_HEAD_DIM
IN_WIDTH_TILED = _VA + SWA_KV_HEADS * SWA_VREP
ROUTE_ROWS = 8
GROUP_LANE0 = N_EXPERTS


def _dot(a, b):
    return jnp.dot(a, b, preferred_element_type=F32)


def _dot_nt(a, b):
    return lax.dot_general(a, b, (((1,), (1,)), ((), ())), preferred_element_type=F32)


def _dot_tn(a, b):
    return lax.dot_general(a, b, (((0,), (0,)), ((), ())), preferred_element_type=F32)


def _layer_norm(z, g, b):
    mu = jnp.mean(z, axis=-1, keepdims=True)
    zc = z - mu
    var = jnp.mean(zc * zc, axis=-1, keepdims=True)
    return zc * lax.rsqrt(var + LN_EPS) * g + b


def _silu(g):
    return g / (1.0 + jnp.exp(-g))


def _store_row_tiles(ref, val):
    n = val.shape[0]
    for j in range(ROW_SUBLANES):
        ref[pl.ds(j, n, stride=ROW_SUBLANES), :] = val[:, j * LANES:(j + 1) * LANES]


def _load_row_tiles(ref):
    n = ref.shape[0] // ROW_SUBLANES
    return jnp.concatenate([ref[pl.ds(j, n, stride=ROW_SUBLANES), :] for j in range(ROW_SUBLANES)], axis=1)


def _mixer_kernel(cdec_ref, x_ref, w_in_ref, w_out_ref, gn_ref, ln_g_ref, ln_b_ref,
                  wr_hi_ref, wr_lo_ref, br_ref, din_ref, qdec_ref, kdec_ref,
                  bias_ref, pen_ref, kmask_ref, vmask_ref, omask_ref, tri_ref,
                  h3_ref, ids_ref, wts_ref, cnt_ref,
                  state_scr, kprev_scr, vprev_scr, o_scr, carry_scr):
    b = pl.program_id(0)
    n = pl.program_id(1)
    ts = x_ref.shape[0]

    @pl.when(n == 0)
    def _():
        state_scr[...] = jnp.zeros_like(state_scr)
        kprev_scr[...] = jnp.zeros_like(kprev_scr)
        vprev_scr[...] = jnp.zeros_like(vprev_scr)

    @pl.when((b == 0) & (n == 0))
    def _():
        carry_scr[...] = jnp.zeros_like(carry_scr)

    x = x_ref[...]
    xb = x.astype(BF16)

    def proj(lo, hi):
        return _dot(xb, w_in_ref[:, lo:hi])

    q_r, k_r, v_r, g_r = proj(_QR, _KR), proj(_KR, _VR), proj(_VR, _GR), proj(_GR, _QA)
    q_a, k_a = proj(_QA, _KA), proj(_KA, _VA)
    k_ab = k_a.astype(BF16)
    v_rep = proj(_VA, _VA + SWA_KV_HEADS * SWA_VREP).astype(BF16)
    first_pen = jnp.where(n == 0, pen_ref[...], 0.0)

    for c in range(ts // CHUNK):
        rs = slice(c * CHUNK, (c + 1) * CHUNK)
        for hd in range(RET_HEADS):
            cs = slice(hd * RET_HEAD_DIM, (hd + 1) * RET_HEAD_DIM)
            q = q_r[rs, cs]
            k = k_r[rs, cs]
            v = v_r[rs, cs].astype(BF16)
            scores = _dot_nt(q.astype(BF16), k.astype(BF16)) * din_ref[hd]
            inner = _dot(scores.astype(BF16), v)
            st = state_scr[hd]
            cross = _dot((q * qdec_ref[hd]).astype(BF16), st.astype(BF16))
            kv = _dot_tn((k * kdec_ref[hd]).astype(BF16), v)
            state_scr[hd] = st * cdec_ref[hd] + kv
            o = inner + cross
            mu = jnp.mean(o, axis=-1, keepdims=True)
            oc = o - mu
            var = jnp.mean(oc * oc, axis=-1, keepdims=True)
            on = oc * lax.rsqrt(var + GN_EPS) * gn_ref[:, cs] * _silu(g_r[rs, cs])
            o_scr[rs, cs] = on.astype(BF16)
        for j in range(SWA_KV_HEADS):
            ks = slice(j * SWA_HEAD_DIM, (j + 1) * SWA_HEAD_DIM)
            vs = slice(j * SWA_VREP, (j + 1) * SWA_VREP)
            if c == 0:
                kp = kprev_scr[:, ks].astype(BF16)
                vp = vprev_scr[:, vs].astype(BF16)
            else:
                ps = slice((c - 1) * CHUNK, c * CHUNK)
                kp = k_ab[ps, ks]
                vp = v_rep[ps, vs]
            kp = kp * kmask_ref[...]
            kcat = jnp.concatenate([kp, k_ab[rs, ks]], axis=0)
            vcat = jnp.concatenate([vp, v_rep[rs, vs]], axis=0)
            q0 = j * SWA_GROUP * SWA_HEAD_DIM
            qs = jnp.concatenate(
                [q_a[rs, q0 + g * SWA_HEAD_DIM:q0 + (g + 1) * SWA_HEAD_DIM] for g in range(SWA_GROUP)],
                axis=0)
            qs = (qs * (SWA_HEAD_DIM ** -0.5)).astype(BF16)
            s = _dot_nt(qs, kcat) + bias_ref[j]
            if c == 0:
                s = s + first_pen
            m = jnp.max(jnp.maximum(s[:, :CHUNK], s[:, CHUNK:]), axis=-1, keepdims=True)
            p = jnp.exp(s - m).astype(BF16)
            p_all = jnp.concatenate([p[g * CHUNK:(g + 1) * CHUNK] for g in range(SWA_GROUP)], axis=1)
            v_blk = jnp.concatenate([vcat * vmask_ref[g] for g in range(SWA_GROUP)], axis=0)
            num = _dot(p_all, v_blk)
            den = _dot(p_all, omask_ref[...])
            c0 = RET_WIDTH + j * SWA_GROUP * SWA_HEAD_DIM
            o_scr[rs, c0:c0 + SWA_GROUP * SWA_HEAD_DIM] = (num / den).astype(BF16)

    kprev_scr[...] = k_ab[ts - CHUNK:, :].astype(F32)
    vprev_scr[...] = v_rep[ts - CHUNK:, :].astype(F32)

    mix = _dot(o_scr[...], w_out_ref[...])
    h = _layer_norm(DEEPNORM_ALPHA * x + mix, ln_g_ref[...], ln_b_ref[...])
    _store_row_tiles(h3_ref, h)

    h_hi = h.astype(BF16)
    h_lo = (h - h_hi.astype(F32)).astype(BF16)
    logits = (_dot(h_hi, wr_hi_ref[...]) + _dot(h_lo, wr_hi_ref[...]) + _dot(h_hi, wr_lo_ref[...])
              + br_ref[...])
    lt = logits.T
    row = lax.broadcasted_iota(jnp.int32, (EXPERTS_PER_GROUP, ts), 0).astype(F32)
    big = 1e9
    ninf = -jnp.inf
    col_max = lambda v: jnp.max(v, axis=0, keepdims=True)
    first_at = lambda v, m: jnp.min(jnp.where(v == m, row, big), axis=0, keepdims=True)
    gl = jnp.where(row < N_GROUPS, lt[GROUP_LANE0:GROUP_LANE0 + EXPERTS_PER_GROUP], ninf)
    gmax = col_max(gl)
    gidx = first_at(gl, gmax)
    g_w = 1.0 / jnp.sum(jnp.exp(gl - gmax), axis=0, keepdims=True)
    el = lt[(N_GROUPS - 1) * EXPERTS_PER_GROUP:N_GROUPS * EXPERTS_PER_GROUP]
    for g in range(N_GROUPS - 2, -1, -1):
        el = jnp.where(gidx == g, lt[g * EXPERTS_PER_GROUP:(g + 1) * EXPERTS_PER_GROUP], el)
    m1 = col_max(el)
    i1 = first_at(el, m1)
    el2 = jnp.where(row == i1, ninf, el)
    m2 = col_max(el2)
    i2 = first_at(el2, m2)
    t = jnp.exp(m2 - m1)
    w1 = g_w / (1.0 + t)
    w2 = g_w * t / (1.0 + t)
    e1 = gidx * EXPERTS_PER_GROUP + i1
    e2 = gidx * EXPERTS_PER_GROUP + i2
    erow = lax.broadcasted_iota(jnp.int32, (N_EXPERTS, ts), 0).astype(F32)
    hit1 = erow == e1
    hit2 = erow == e2
    onehot = (hit1 | hit2).astype(BF16)
    prefix = _dot(onehot, tri_ref[...]) + carry_scr[:, 0:1]
    r1 = jnp.sum(jnp.where(hit1, prefix, 0.0), axis=0, keepdims=True)
    r2 = jnp.sum(jnp.where(hit2, prefix, 0.0), axis=0, keepdims=True)
    carry = carry_scr[:, 0:1] + jnp.sum(onehot.astype(F32), axis=1, keepdims=True)
    carry_scr[...] = jnp.broadcast_to(carry, carry_scr.shape)
    cnt_ref[...] = jnp.broadcast_to(carry, cnt_ref.shape)
    pick = lambda k, v, rest: jnp.where(row == k, v, rest)
    ids_ref[...] = pick(0, e1, pick(1, e2, pick(2, r1, pick(3, r2, 0.0)))).astype(jnp.int32)
    wts_ref[...] = pick(0, w1, pick(1, w2, 0.0))


def _mixer_call(x2, w_in, w_out, gn, ln_g, ln_b, wr_hi, wr_lo, br, consts, bsz, s_len, first_seq):
    cdec, din, qdec, kdec, bias, pen, kmask, vmask, omask, tri = consts
    t_len = bsz * s_len
    ns = s_len // SEQ_TILE
    tok = lambda b, n, *_: (b * ns + n, 0)
    tok_in = lambda b, n, *_: ((first_seq + b) * ns + n, 0)
    tok_t = lambda b, n, *_: (0, b * ns + n)
    full2 = lambda b, n, *_: (0, 0)
    full3 = lambda b, n, *_: (0, 0, 0)
    grid_spec = pltpu.PrefetchScalarGridSpec(
        num_scalar_prefetch=1,
        grid=(bsz, ns),
        in_specs=[
            pl.BlockSpec((SEQ_TILE, D_MODEL), tok_in),
            pl.BlockSpec((D_MODEL, IN_WIDTH_TILED), full2),
            pl.BlockSpec((D_MODEL, D_MODEL), full2),
            pl.BlockSpec((1, RET_WIDTH), full2),
            pl.BlockSpec((1, D_MODEL), full2),
            pl.BlockSpec((1, D_MODEL), full2),
            pl.BlockSpec((D_MODEL, LANES), full2),
            pl.BlockSpec((D_MODEL, LANES), full2),
            pl.BlockSpec((1, LANES), full2),
            pl.BlockSpec((RET_HEADS, CHUNK, CHUNK), full3),
            pl.BlockSpec((RET_HEADS, CHUNK, RET_HEAD_DIM), full3),
            pl.BlockSpec((RET_HEADS, CHUNK, RET_HEAD_DIM), full3),
            pl.BlockSpec((SWA_KV_HEADS, SWA_GROUP * CHUNK, 2 * CHUNK), full3),
            pl.BlockSpec((1, 2 * CHUNK), full2),
            pl.BlockSpec((CHUNK, SWA_HEAD_DIM), full2),
            pl.BlockSpec((SWA_GROUP, 2 * CHUNK, SWA_VREP), full3),
            pl.BlockSpec((SWA_GROUP * 2 * CHUNK, SWA_VREP), full2),
            pl.BlockSpec((SEQ_TILE, SEQ_TILE), full2),
        ],
        out_specs=[
            pl.BlockSpec((SEQ_TILE * ROW_SUBLANES, LANES), tok),
            pl.BlockSpec((ROUTE_ROWS, SEQ_TILE), tok_t),
            pl.BlockSpec((ROUTE_ROWS, SEQ_TILE), tok_t),
            pl.BlockSpec((N_EXPERTS, LANES), full2),
        ],
        scratch_shapes=[
            pltpu.VMEM((RET_HEADS, RET_HEAD_DIM, RET_HEAD_DIM), F32),
            pltpu.VMEM((CHUNK, SWA_KV_WIDTH), F32),
            pltpu.VMEM((CHUNK, SWA_KV_HEADS * SWA_VREP), F32),
            pltpu.VMEM((SEQ_TILE, D_MODEL), BF16),
            pltpu.VMEM((N_EXPERTS, LANES), F32),
        ],
    )
    return pl.pallas_call(
        _mixer_kernel,
        grid_spec=grid_spec,
        out_shape=[
            jax.ShapeDtypeStruct((t_len * ROW_SUBLANES, LANES), F32),
            jax.ShapeDtypeStruct((ROUTE_ROWS, t_len), jnp.int32),
            jax.ShapeDtypeStruct((ROUTE_ROWS, t_len), F32),
            jax.ShapeDtypeStruct((N_EXPERTS, LANES), F32),
        ],
        compiler_params=pltpu.CompilerParams(
            dimension_semantics=("arbitrary", "arbitrary"), vmem_limit_bytes=VMEM_LIMIT),
        name="mixer_router",
    )(cdec, x2, w_in, w_out, gn, ln_g, ln_b, wr_hi, wr_lo, br, din, qdec, kdec, bias, pen, kmask, vmask, omask, tri)


def _sc_mesh():
    return plsc.VectorSubcoreMesh(core_axis_name="core", subcore_axis_name="subcore")


def _sc_dispatch(h_rows, dest, n_rows):
    t_len = h_rows.shape[0]
    n_assign = dest.shape[0]
    per_worker = n_rows // SC_WORKERS
    n_chunks = per_worker // SC_CHUNK_ROWS
    assert n_chunks * SC_CHUNK_ROWS * SC_WORKERS == n_rows and n_rows <= 3 * t_len
    assert per_worker % SC_LANES == 0 and n_assign % SC_LANES == 0 and n_assign == TOP_K * t_len

    @functools.partial(
        pl.kernel, mesh=_sc_mesh(), name="moe_dispatch_sc",
        compiler_params=pltpu.CompilerParams(needs_layout_passes=False),
        out_type=jax.ShapeDtypeStruct((n_rows, ROW_SUBLANES, LANES), F32),
        scratch_types=[pltpu.VMEM((n_assign,), jnp.int32),
                       pltpu.VMEM((per_worker,), jnp.int32),
                       pltpu.VMEM((SC_CHUNK_ROWS, ROW_SUBLANES, LANES), F32),
                       pltpu.SemaphoreType.DMA])
    def dispatch(h_hbm, dest_hbm, rows_hbm, dest_v, src_v, buf, sem):
        wid = lax.axis_index("subcore") * SC_CORES + lax.axis_index("core")
        base = wid * per_worker
        pltpu.sync_copy(dest_hbm, dest_v)
        lane = lax.iota(jnp.int32, SC_LANES)

        def wrap(a):
            a = jnp.where(a >= t_len, a - t_len, a)
            return jnp.where(a >= t_len, a - t_len, a)

        @pl.loop(0, per_worker // SC_LANES)
        def _(i):
            src_v[pl.ds(i * SC_LANES, SC_LANES)] = wrap(base + i * SC_LANES + lane)

        @pl.loop(0, n_assign // SC_LANES)
        def _(i):
            d = dest_v[pl.ds(i * SC_LANES, SC_LANES)] - base
            hit = (d >= 0) & (d < per_worker)
            plsc.store_scatter(src_v, [jnp.where(hit, d, 0)], wrap(i * SC_LANES + lane), mask=hit)

        @pl.loop(0, n_chunks)
        def _(c):
            off = pl.multiple_of(c * SC_CHUNK_ROWS, SC_CHUNK_ROWS)
            pltpu.async_copy(h_hbm.at[src_v.at[pl.ds(off, SC_CHUNK_ROWS)]], buf, sem).wait()
            pltpu.sync_copy(buf, rows_hbm.at[pl.ds(base + off, SC_CHUNK_ROWS)])

    return dispatch(h_rows, dest)


def _sc_gather_rows(table, idx, name):
    m = idx.shape[0]
    per_worker = m // (SC_CHUNK_ROWS * SC_WORKERS)
    assert per_worker * SC_CHUNK_ROWS * SC_WORKERS == m

    @functools.partial(
        pl.kernel, mesh=_sc_mesh(), name=name,
        out_type=jax.ShapeDtypeStruct((m, ROW_SUBLANES, LANES), F32),
        scratch_types=[pltpu.VMEM((m,), jnp.int32),
                       pltpu.VMEM((SC_CHUNK_ROWS, ROW_SUBLANES, LANES), F32),
                       pltpu.SemaphoreType.DMA])
    def gather(table_hbm, idx_hbm, out_hbm, idx_v, buf, sem):
        wid = lax.axis_index("subcore") * SC_CORES + lax.axis_index("core")
        pltpu.sync_copy(idx_hbm, idx_v)

        @pl.loop(0, per_worker)
        def _(j):
            off = pl.multiple_of((j * SC_WORKERS + wid) * SC_CHUNK_ROWS, SC_CHUNK_ROWS)
            pltpu.async_copy(table_hbm.at[idx_v.at[pl.ds(off, SC_CHUNK_ROWS)]], buf, sem).wait()
            pltpu.sync_copy(buf, out_hbm.at[pl.ds(off, SC_CHUNK_ROWS)])

    return gather(table, idx)


def _expert_kernel(sexp_ref, ssplit_ref, hold_ref, nsteps_ref, *refs):
    n = TOKEN_SPLITS
    row_refs, (wg_ref, wu_ref, wd_ref), y_refs = refs[:n], refs[n:n + 3], refs[n + 3:2 * n + 3]
    wg_s, wu_s, wd_s = refs[2 * n + 3:]
    p = pl.program_id(0)
    prev = sexp_ref[jnp.maximum(p - 1, 0)]

    @pl.when((p == 0) | (sexp_ref[p] != prev))
    def _():
        wg_s[...] = wg_ref[0].astype(BF16)
        wu_s[...] = wu_ref[0].astype(BF16)
        wd_s[...] = wd_ref[0].astype(BF16)

    for sp in range(n):
        @pl.when((p < nsteps_ref[0]) & (ssplit_ref[p] == sp))
        def _(x_ref=row_refs[sp], y_ref=y_refs[sp]):
            xb = _load_row_tiles(x_ref).astype(BF16)
            g = _dot(xb, wg_s[...])
            u = _dot(xb, wu_s[...])
            a = (_silu(g) * u).astype(BF16)
            _store_row_tiles(y_ref, _dot(a, wd_s[...]))


def _expert_steps(blk_counts, n_steps):
    n = blk_counts.shape[0]
    per_expert = jnp.sum(blk_counts, axis=0)
    cum = jnp.cumsum(per_expert)
    p = jnp.arange(n_steps, dtype=jnp.int32)
    sexp = jnp.minimum(jnp.sum(cum[None, :] <= p[:, None], axis=-1), N_EXPERTS - 1).astype(jnp.int32)
    onehot = sexp[:, None] == jnp.arange(N_EXPERTS, dtype=jnp.int32)[None, :]
    pick = lambda v: jnp.sum(jnp.where(onehot, v[None, :], 0), axis=-1)
    within = p - (pick(cum) - pick(per_expert))
    split_end = jnp.cumsum(blk_counts, axis=0)
    ssplit = jnp.zeros_like(p)
    for i in range(n - 1):
        ssplit = ssplit + (within >= pick(split_end[i])).astype(jnp.int32)
    valid = p < cum[-1]
    hold = jnp.stack([jnp.maximum(jnp.cumsum((valid & (ssplit == i)).astype(jnp.int32)) - 1, 0)
                      for i in range(n)])
    return sexp, ssplit.astype(jnp.int32), hold.reshape(-1).astype(jnp.int32), cum[-1:].astype(jnp.int32)


def _expert_call(blk_counts, rows, w_gate, w_up, w_down):
    n = len(rows)
    n_blocks = rows[0].shape[0] // (ROW_SUBLANES * MOE_ROWS)
    n_steps = n * n_blocks
    sexp, ssplit, hold, nsteps = _expert_steps(blk_counts, n_steps)

    def held_block(sp):
        return pl.BlockSpec((MOE_ROWS * ROW_SUBLANES, LANES),
                            lambda p, se, ss, hd, ns: (hd[sp * n_steps + p], 0))

    wsel = lambda p, se, ss, hd, ns: (se[p], 0, 0)
    grid_spec = pltpu.PrefetchScalarGridSpec(
        num_scalar_prefetch=4,
        grid=(n_steps,),
        in_specs=[held_block(sp) for sp in range(n)] + [
            pl.BlockSpec((1, D_MODEL, D_EXPERT), wsel),
            pl.BlockSpec((1, D_MODEL, D_EXPERT), wsel),
            pl.BlockSpec((1, D_EXPERT, D_MODEL), wsel),
        ],
        out_specs=[held_block(sp) for sp in range(n)],
        scratch_shapes=[
            pltpu.VMEM((D_MODEL, D_EXPERT), BF16),
            pltpu.VMEM((D_MODEL, D_EXPERT), BF16),
            pltpu.VMEM((D_EXPERT, D_MODEL), BF16),
        ],
    )
    return pl.pallas_call(
        _expert_kernel,
        grid_spec=grid_spec,
        out_shape=[jax.ShapeDtypeStruct(r.shape, F32) for r in rows],
        compiler_params=pltpu.CompilerParams(
            dimension_semantics=("arbitrary",), vmem_limit_bytes=VMEM_LIMIT),
        name="moe_experts",
    )(sexp, ssplit, hold, nsteps, *rows, w_gate, w_up, w_down)


def _combine_kernel(h_ref, y0_ref, y1_ref, wts_ref, ln_g_ref, ln_b_ref, *rest):
    out_ref = rest[-1]
    wts = wts_ref[...].T
    ffn = _load_row_tiles(y0_ref) * wts[:, 0:1] + _load_row_tiles(y1_ref) * wts[:, 1:2]
    out_ref[...] = _layer_norm(DEEPNORM_ALPHA * _load_row_tiles(h_ref) + ffn, ln_g_ref[...], ln_b_ref[...])


def _combine_call(h_rows, yk, wts, ln_g, ln_b, out_prev, split, t_total):
    t_len = wts.shape[1]
    n_tiles = t_len // COMBINE_TILE
    tok = lambda i: (i, 0)
    full2 = lambda i: (0, 0)
    tiles = pl.BlockSpec((COMBINE_TILE * ROW_SUBLANES, LANES), tok)
    tiles_k1 = pl.BlockSpec((COMBINE_TILE * ROW_SUBLANES, LANES), lambda i: (i + n_tiles, 0))
    in_specs = [tiles, tiles, tiles_k1,
                pl.BlockSpec((ROUTE_ROWS, COMBINE_TILE), lambda i: (0, i)),
                pl.BlockSpec((1, D_MODEL), full2),
                pl.BlockSpec((1, D_MODEL), full2)]
    args = [h_rows, yk, yk, wts, ln_g, ln_b]
    aliases = {}
    if out_prev is not None:
        in_specs.append(pl.BlockSpec(memory_space=pl.ANY))
        args.append(out_prev)
        aliases = {len(args) - 1: 0}
    return pl.pallas_call(
        _combine_kernel,
        grid=(n_tiles,),
        in_specs=in_specs,
        out_specs=pl.BlockSpec((COMBINE_TILE, D_MODEL), lambda i: (i + split * n_tiles, 0)),
        out_shape=jax.ShapeDtypeStruct((t_total, D_MODEL), F32),
        input_output_aliases=aliases,
        compiler_params=pltpu.CompilerParams(
            dimension_semantics=("arbitrary",), vmem_limit_bytes=VMEM_LIMIT),
        name="moe_combine_ln",
    )(*args)


def _position_tables():
    c = CHUNK
    f = np.float32
    log_g = np.log1p(-np.exp2(-5.0 - np.arange(RET_HEADS, dtype=f))).astype(f)
    idx = np.arange(c, dtype=f)
    diff = idx[:, None] - idx[None, :]
    scale = f(RET_HEAD_DIM ** -0.5)
    din = np.where(diff >= 0, np.exp(log_g[:, None, None] * np.maximum(diff, 0.0)), 0.0).astype(f) * scale
    qdec = np.broadcast_to(np.exp(log_g[:, None] * (idx + 1.0))[:, :, None], (RET_HEADS, c, RET_HEAD_DIM))
    kdec = np.broadcast_to((np.exp(log_g[:, None] * (c - 1.0 - idx)) * scale)[:, :, None],
                           (RET_HEADS, c, RET_HEAD_DIM))
    cdec = np.exp(log_g * c)
    slopes = np.exp2(-8.0 * (np.arange(SWA_HEADS, dtype=f) + 1.0) / SWA_HEADS).astype(f)
    r = np.arange(c)[:, None]
    col = np.arange(c)[None, :]
    dist_prev = (r - col + c).astype(f)
    dist_cur = (r - col).astype(f)
    bprev = np.where((r < col)[None], -slopes[:, None, None] * dist_prev[None], NEG)
    bcur = np.where((r >= col)[None], -slopes[:, None, None] * dist_cur[None], NEG)
    bias = np.concatenate([bprev, bcur], axis=-1).reshape(SWA_KV_HEADS, SWA_GROUP * c, 2 * c)
    key = np.arange(2 * c)
    pen = np.where((key >= 1) & (key < c), NEG, 0.0)[None, :]
    lane_head = np.arange(SWA_VREP) // SWA_HEAD_DIM
    own = lane_head[None, None, :] == np.arange(SWA_GROUP)[:, None, None]
    kmask = np.broadcast_to(np.arange(c)[:, None] > 0, (c, SWA_HEAD_DIM))
    vmask = own & (key[None, :, None] > 0)
    omask = np.broadcast_to(own, (SWA_GROUP, 2 * c, SWA_VREP)).reshape(SWA_GROUP * 2 * c, SWA_VREP)
    tr = np.arange(SEQ_TILE)
    tri = tr[:, None] < tr[None, :]
    as_f32 = lambda v: np.ascontiguousarray(v, dtype=f)
    as_bf16 = lambda v: np.ascontiguousarray(v, dtype=f).astype(BF16)
    return (as_f32(cdec), as_f32(din), as_f32(qdec), as_f32(kdec), as_f32(bias), as_f32(pen),
            as_bf16(kmask), as_bf16(vmask), as_bf16(omask), as_bf16(tri))


def _mixer_constants(attn_sinks):
    cdec, din, qdec, kdec, bias, pen, kmask, vmask, omask, tri = _position_tables()
    sink = attn_sinks.astype(F32).reshape(SWA_KV_HEADS, SWA_GROUP, 1, 1)
    sink = jnp.broadcast_to(sink, (SWA_KV_HEADS, SWA_GROUP, CHUNK, 1)).reshape(SWA_KV_HEADS, SWA_GROUP * CHUNK, 1)
    is_slot = (np.arange(2 * CHUNK) == 0)[None, None, :]
    return cdec, din, qdec, kdec, jnp.where(is_slot, sink, bias), pen, kmask, vmask, omask, tri


def _tile_v_columns(w_in):
    v_cols = w_in[:, _VA:].reshape(D_MODEL, SWA_KV_HEADS, 1, SWA_HEAD_DIM)
    v_cols = jnp.broadcast_to(v_cols, (D_MODEL, SWA_KV_HEADS, SWA_GROUP, SWA_HEAD_DIM))
    return jnp.concatenate([w_in[:, :_VA], v_cols.reshape(D_MODEL, SWA_KV_HEADS * SWA_VREP)], axis=1)


def _router_tables(w_group_router, b_group_router, w_expert_router, b_expert_router):
    w_e = jnp.transpose(w_expert_router, (1, 0, 2)).reshape(D_MODEL, N_EXPERTS)
    w = jnp.concatenate([w_e, w_group_router,
                         jnp.zeros((D_MODEL, LANES - N_EXPERTS - N_GROUPS), F32)], axis=1)
    bias = jnp.concatenate([b_expert_router.reshape(N_EXPERTS), b_group_router,
                            jnp.zeros((LANES - N_EXPERTS - N_GROUPS,), F32)])[None, :]
    w_hi = w.astype(BF16)
    w_lo = (w - w_hi.astype(F32)).astype(BF16)
    return w_hi, w_lo, bias


def kernel(x, w_in, ret_gn_g, attn_sinks, w_out, ln1_g, ln1_b, w_group_router, b_group_router,
           w_expert_router, b_expert_router, w_gate, w_up, w_down, ln2_g, ln2_b):
    bsz, s_len, d = x.shape
    assert d == D_MODEL and s_len % SEQ_TILE == 0 and w_in.shape[0] == DEPTH == 1
    assert bsz % TOKEN_SPLITS == 0
    t_total = bsz * s_len
    bsz_s = bsz // TOKEN_SPLITS
    t_len = bsz_s * s_len
    n_blocks = t_len * TOP_K // MOE_ROWS + N_EXPERTS
    n_rows = n_blocks * MOE_ROWS

    consts = _mixer_constants(attn_sinks[0])
    wr_hi, wr_lo, br = _router_tables(w_group_router[0], b_group_router[0],
                                      w_expert_router[0], b_expert_router[0])
    w_in_b = _tile_v_columns(w_in[0]).astype(BF16)
    w_out_b = w_out[0].astype(BF16)
    x2 = x.reshape(t_total, d)

    routed, rows, blk_counts = [], [], []
    for sp in range(TOKEN_SPLITS):
        h_rows, ids, wts, cnt = _mixer_call(
            x2, w_in_b, w_out_b, ret_gn_g[0][None, :], ln1_g[0][None, :], ln1_b[0][None, :],
            wr_hi, wr_lo, br, consts, bsz_s, s_len, sp * bsz_s)

        counts = cnt[:, 0].astype(jnp.int32)
        padded = (counts + MOE_ROWS - 1) // MOE_ROWS * MOE_ROWS
        pstart = jnp.cumsum(padded) - padded
        onehot = ids[0:TOP_K, :, None] == jnp.arange(N_EXPERTS, dtype=jnp.int32)
        dest = jnp.sum(jnp.where(onehot, pstart, 0), axis=-1) + ids[TOP_K:2 * TOP_K]
        dest = dest.reshape(-1)

        rows.append(_sc_dispatch(h_rows.reshape(t_len, ROW_SUBLANES, LANES), dest, n_rows)
                    .reshape(n_rows * ROW_SUBLANES, LANES))
        blk_counts.append(padded // MOE_ROWS)
        routed.append((h_rows, wts, dest))

    ys = _expert_call(jnp.stack(blk_counts).astype(jnp.int32), rows, w_gate[0], w_up[0], w_down[0])

    out = None
    for sp, (h_rows, wts, dest) in enumerate(routed):
        yk = _sc_gather_rows(ys[sp].reshape(n_rows, ROW_SUBLANES, LANES), dest, "moe_combine_sc")
        out = _combine_call(h_rows, yk.reshape(TOP_K * t_len * ROW_SUBLANES, LANES), wts,
                            ln2_g[0][None, :], ln2_b[0][None, :], out, sp, t_total)
    return out.reshape(bsz, s_len, d)
```

```python
import functools

import jax
import jax.numpy as jnp
import numpy as np
from jax import lax
from jax.experimental import pallas as pl
from jax.experimental.pallas import tpu as pltpu
from jax.experimental.pallas import tpu_sc as plsc

F32 = jnp.float32
BF16 = jnp.bfloat16

D_MODEL = 1024
RET_HEADS = 4
RET_HEAD_DIM = 128
RET_WIDTH = RET_HEADS * RET_HEAD_DIM
CHUNK = 128
SWA_HEADS = 8
SWA_KV_HEADS = 2
SWA_GROUP = SWA_HEADS // SWA_KV_HEADS
SWA_HEAD_DIM = 64
SWA_WIDTH = SWA_HEADS * SWA_HEAD_DIM
SWA_KV_WIDTH = SWA_KV_HEADS * SWA_HEAD_DIM
IN_WIDTH = 4 * RET_WIDTH + SWA_WIDTH + 2 * SWA_KV_WIDTH
N_GROUPS = 4
EXPERTS_PER_GROUP = 8
N_EXPERTS = N_GROUPS * EXPERTS_PER_GROUP
TOP_K = 2
D_EXPERT = 512
LN_EPS = 1e-5
GN_EPS = 1e-6
DEPTH = 1
DEEPNORM_ALPHA = (2 * DEPTH) ** 0.25
NEG = -1e30

LANES = 128
ROW_SUBLANES = D_MODEL // LANES
SEQ_TILE = 512
MOE_ROWS = 256
COMBINE_TILE = 256
TOKEN_SPLITS = 2
SLOT_PARTS = 2
SC_CORES = 2
SC_SUBCORES = 16
SC_WORKERS = SC_CORES * SC_SUBCORES
SC_LANES = 16
SC_CHUNK_ROWS = 32
VMEM_LIMIT = 56 * 1024 * 1024

_QR, _KR, _VR, _GR = 0, RET_WIDTH, 2 * RET_WIDTH, 3 * RET_WIDTH
_QA = 4 * RET_WIDTH
_KA = _QA + SWA_WIDTH
_VA = _KA + SWA_KV_WIDTH
SWA_VREP = SWA_GROUP * SWA_HEAD_DIM
IN_WIDTH_TILED = _VA + SWA_KV_HEADS * SWA_VREP
ROUTE_ROWS = 8
GROUP_LANE0 = N_EXPERTS


def _dot(a, b):
    return jnp.dot(a, b, preferred_element_type=F32)


def _dot_nt(a, b):
    return lax.dot_general(a, b, (((1,), (1,)), ((), ())), preferred_element_type=F32)


def _dot_tn(a, b):
    return lax.dot_general(a, b, (((0,), (0,)), ((), ())), preferred_element_type=F32)


def _layer_norm(z, g, b):
    mu = jnp.mean(z, axis=-1, keepdims=True)
    zc = z - mu
    var = jnp.mean(zc * zc, axis=-1, keepdims=True)
    return zc * lax.rsqrt(var + LN_EPS) * g + b


def _silu(g):
    return g / (1.0 + jnp.exp(-g))


ROW_BYTES = 4 * D_MODEL


def _mixer_cost(t):
    proj = 2 * t * D_MODEL * (IN_WIDTH_TILED + D_MODEL + 3 * LANES)
    retention = RET_HEADS * 4 * 2 * t * CHUNK * RET_HEAD_DIM
    swa = SWA_KV_HEADS * 2 * t * SWA_GROUP * 2 * CHUNK * (SWA_HEAD_DIM + 2 * SWA_VREP)
    weights = 2 * D_MODEL * (IN_WIDTH_TILED + D_MODEL + 2 * LANES)
    return pl.CostEstimate(flops=proj + retention + swa + 2 * t * SEQ_TILE * N_EXPERTS,
                           transcendentals=t * (SWA_HEADS * 2 * CHUNK + RET_WIDTH + LANES),
                           bytes_accessed=2 * ROW_BYTES * t + weights)


def _expert_cost(n_rows):
    return pl.CostEstimate(flops=2 * 3 * n_rows * D_MODEL * D_EXPERT, transcendentals=n_rows * D_EXPERT,
                           bytes_accessed=2 * ROW_BYTES * n_rows + 4 * 3 * N_EXPERTS * D_MODEL * D_EXPERT)


def _row_move_cost(n_rows):
    return pl.CostEstimate(flops=0, transcendentals=0, bytes_accessed=2 * ROW_BYTES * n_rows)


def _store_row_tiles(ref, val):
    n = val.shape[0]
    for j in range(ROW_SUBLANES):
        ref[pl.ds(j, n, stride=ROW_SUBLANES), :] = val[:, j * LANES:(j + 1) * LANES]


def _load_row_tiles(ref):
    n = ref.shape[0] // ROW_SUBLANES
    return jnp.concatenate([ref[pl.ds(j, n, stride=ROW_SUBLANES), :] for j in range(ROW_SUBLANES)], axis=1)


def _mixer_kernel(cdec_ref, x_ref, w_in_ref, w_out_ref, gn_ref, ln_g_ref, ln_b_ref,
                  wr_hi_ref, wr_lo_ref, br_ref, din_ref, qdec_ref, kdec_ref,
                  bias_ref, pen_ref, kmask_ref, vmask_ref, omask_ref, tri_ref,
                  h3_ref, ids_ref, wts_ref, cnt_ref,
                  state_scr, kprev_scr, vprev_scr, o_scr, carry_scr):
    b = pl.program_id(0)
    n = pl.program_id(1)
    ts = x_ref.shape[0]

    @pl.when(n == 0)
    def _():
        state_scr[...] = jnp.zeros_like(state_scr)
        kprev_scr[...] = jnp.zeros_like(kprev_scr)
        vprev_scr[...] = jnp.zeros_like(vprev_scr)

    @pl.when((b == 0) & (n == 0))
    def _():
        carry_scr[...] = jnp.zeros_like(carry_scr)

    x = x_ref[...]
    xb = x.astype(BF16)

    def proj(lo, hi):
        return _dot(xb, w_in_ref[:, lo:hi])

    q_r, k_r, v_r, g_r = proj(_QR, _KR), proj(_KR, _VR), proj(_VR, _GR), proj(_GR, _QA)
    q_a, k_a = proj(_QA, _KA), proj(_KA, _VA)
    k_ab = k_a.astype(BF16)
    v_rep = proj(_VA, _VA + SWA_KV_HEADS * SWA_VREP).astype(BF16)
    first_pen = jnp.where(n == 0, pen_ref[...], 0.0)

    for c in range(ts // CHUNK):
        rs = slice(c * CHUNK, (c + 1) * CHUNK)
        for hd in range(RET_HEADS):
            cs = slice(hd * RET_HEAD_DIM, (hd + 1) * RET_HEAD_DIM)
            q = q_r[rs, cs]
            k = k_r[rs, cs]
            v = v_r[rs, cs].astype(BF16)
            scores = _dot_nt(q.astype(BF16), k.astype(BF16)) * din_ref[hd]
            inner = _dot(scores.astype(BF16), v)
            st = state_scr[hd]
            cross = _dot((q * qdec_ref[hd]).astype(BF16), st.astype(BF16))
            kv = _dot_tn((k * kdec_ref[hd]).astype(BF16), v)
            state_scr[hd] = st * cdec_ref[hd] + kv
            o = inner + cross
            mu = jnp.mean(o, axis=-1, keepdims=True)
            oc = o - mu
            var = jnp.mean(oc * oc, axis=-1, keepdims=True)
            on = oc * lax.rsqrt(var + GN_EPS) * gn_ref[:, cs] * _silu(g_r[rs, cs])
            o_scr[rs, cs] = on.astype(BF16)
        for j in range(SWA_KV_HEADS):
            ks = slice(j * SWA_HEAD_DIM, (j + 1) * SWA_HEAD_DIM)
            vs = slice(j * SWA_VREP, (j + 1) * SWA_VREP)
            if c == 0:
                kp = kprev_scr[:, ks].astype(BF16)
                vp = vprev_scr[:, vs].astype(BF16)
            else:
                ps = slice((c - 1) * CHUNK, c * CHUNK)
                kp = k_ab[ps, ks]
                vp = v_rep[ps, vs]
            kp = kp * kmask_ref[...]
            kcat = jnp.concatenate([kp, k_ab[rs, ks]], axis=0)
            vcat = jnp.concatenate([vp, v_rep[rs, vs]], axis=0)
            q0 = j * SWA_GROUP * SWA_HEAD_DIM
            qs = jnp.concatenate(
                [q_a[rs, q0 + g * SWA_HEAD_DIM:q0 + (g + 1) * SWA_HEAD_DIM] for g in range(SWA_GROUP)],
                axis=0)
            qs = (qs * (SWA_HEAD_DIM ** -0.5)).astype(BF16)
            s = _dot_nt(qs, kcat) + bias_ref[j]
            if c == 0:
                s = s + first_pen
            m = jnp.max(jnp.maximum(s[:, :CHUNK], s[:, CHUNK:]), axis=-1, keepdims=True)
            p = jnp.exp(s - m).astype(BF16)
            p_all = jnp.concatenate([p[g * CHUNK:(g + 1) * CHUNK] for g in range(SWA_GROUP)], axis=1)
            v_blk = jnp.concatenate([vcat * vmask_ref[g] for g in range(SWA_GROUP)], axis=0)
            num = _dot(p_all, v_blk)
            den = _dot(p_all, omask_ref[...])
            c0 = RET_WIDTH + j * SWA_GROUP * SWA_HEAD_DIM
            o_scr[rs, c0:c0 + SWA_GROUP * SWA_HEAD_DIM] = (num / den).astype(BF16)

    kprev_scr[...] = k_ab[ts - CHUNK:, :].astype(F32)
    vprev_scr[...] = v_rep[ts - CHUNK:, :].astype(F32)

    mix = _dot(o_scr[...], w_out_ref[...])
    h = _layer_norm(DEEPNORM_ALPHA * x + mix, ln_g_ref[...], ln_b_ref[...])
    _store_row_tiles(h3_ref, h)

    h_hi = h.astype(BF16)
    h_lo = (h - h_hi.astype(F32)).astype(BF16)
    logits = (_dot(h_hi, wr_hi_ref[...]) + _dot(h_lo, wr_hi_ref[...]) + _dot(h_hi, wr_lo_ref[...])
              + br_ref[...])
    lt = logits.T
    row = lax.broadcasted_iota(jnp.int32, (EXPERTS_PER_GROUP, ts), 0).astype(F32)
    big = 1e9
    ninf = -jnp.inf
    col_max = lambda v: jnp.max(v, axis=0, keepdims=True)
    first_at = lambda v, m: jnp.min(jnp.where(v == m, row, big), axis=0, keepdims=True)
    gl = jnp.where(row < N_GROUPS, lt[GROUP_LANE0:GROUP_LANE0 + EXPERTS_PER_GROUP], ninf)
    gmax = col_max(gl)
    gidx = first_at(gl, gmax)
    g_w = 1.0 / jnp.sum(jnp.exp(gl - gmax), axis=0, keepdims=True)
    el = lt[(N_GROUPS - 1) * EXPERTS_PER_GROUP:N_GROUPS * EXPERTS_PER_GROUP]
    for g in range(N_GROUPS - 2, -1, -1):
        el = jnp.where(gidx == g, lt[g * EXPERTS_PER_GROUP:(g + 1) * EXPERTS_PER_GROUP], el)
    m1 = col_max(el)
    i1 = first_at(el, m1)
    el2 = jnp.where(row == i1, ninf, el)
    m2 = col_max(el2)
    i2 = first_at(el2, m2)
    t = jnp.exp(m2 - m1)
    w1 = g_w / (1.0 + t)
    w2 = g_w * t / (1.0 + t)
    e1 = gidx * EXPERTS_PER_GROUP + i1
    e2 = gidx * EXPERTS_PER_GROUP + i2
    erow = lax.broadcasted_iota(jnp.int32, (N_EXPERTS, ts), 0).astype(F32)
    hit1 = erow == e1
    hit2 = erow == e2
    onehot = (hit1 | hit2).astype(BF16)
    prefix = _dot(onehot, tri_ref[...]) + carry_scr[:, 0:1]
    r1 = jnp.sum(jnp.where(hit1, prefix, 0.0), axis=0, keepdims=True)
    r2 = jnp.sum(jnp.where(hit2, prefix, 0.0), axis=0, keepdims=True)
    carry = carry_scr[:, 0:1] + jnp.sum(onehot.astype(F32), axis=1, keepdims=True)
    carry_scr[...] = jnp.broadcast_to(carry, carry_scr.shape)
    cnt_ref[...] = jnp.broadcast_to(carry, cnt_ref.shape)
    pick = lambda k, v, rest: jnp.where(row == k, v, rest)
    ids_ref[...] = pick(0, e1, pick(1, e2, pick(2, r1, pick(3, r2, 0.0)))).astype(jnp.int32)
    wts_ref[...] = pick(0, w1, pick(1, w2, 0.0))


def _mixer_call(x2, w_in, w_out, gn, ln_g, ln_b, wr_hi, wr_lo, br, consts, bsz, s_len, first_seq):
    cdec, din, qdec, kdec, bias, pen, kmask, vmask, omask, tri = consts
    t_len = bsz * s_len
    ns = s_len // SEQ_TILE
    tok = lambda b, n, *_: (b * ns + n, 0)
    tok_in = lambda b, n, *_: ((first_seq + b) * ns + n, 0)
    tok_t = lambda b, n, *_: (0, b * ns + n)
    full2 = lambda b, n, *_: (0, 0)
    full3 = lambda b, n, *_: (0, 0, 0)
    grid_spec = pltpu.PrefetchScalarGridSpec(
        num_scalar_prefetch=1,
        grid=(bsz, ns),
        in_specs=[
            pl.BlockSpec((SEQ_TILE, D_MODEL), tok_in),
            pl.BlockSpec((D_MODEL, IN_WIDTH_TILED), full2),
            pl.BlockSpec((D_MODEL, D_MODEL), full2),
            pl.BlockSpec((1, RET_WIDTH), full2),
            pl.BlockSpec((1, D_MODEL), full2),
            pl.BlockSpec((1, D_MODEL), full2),
            pl.BlockSpec((D_MODEL, LANES), full2),
            pl.BlockSpec((D_MODEL, LANES), full2),
            pl.BlockSpec((1, LANES), full2),
            pl.BlockSpec((RET_HEADS, CHUNK, CHUNK), full3),
            pl.BlockSpec((RET_HEADS, CHUNK, RET_HEAD_DIM), full3),
            pl.BlockSpec((RET_HEADS, CHUNK, RET_HEAD_DIM), full3),
            pl.BlockSpec((SWA_KV_HEADS, SWA_GROUP * CHUNK, 2 * CHUNK), full3),
            pl.BlockSpec((1, 2 * CHUNK), full2),
            pl.BlockSpec((CHUNK, SWA_HEAD_DIM), full2),
            pl.BlockSpec((SWA_GROUP, 2 * CHUNK, SWA_VREP), full3),
            pl.BlockSpec((SWA_GROUP * 2 * CHUNK, SWA_VREP), full2),
            pl.BlockSpec((SEQ_TILE, SEQ_TILE), full2),
        ],
        out_specs=[
            pl.BlockSpec((SEQ_TILE * ROW_SUBLANES, LANES), tok),
            pl.BlockSpec((ROUTE_ROWS, SEQ_TILE), tok_t),
            pl.BlockSpec((ROUTE_ROWS, SEQ_TILE), tok_t),
            pl.BlockSpec((N_EXPERTS, LANES), full2),
        ],
        scratch_shapes=[
            pltpu.VMEM((RET_HEADS, RET_HEAD_DIM, RET_HEAD_DIM), F32),
            pltpu.VMEM((CHUNK, SWA_KV_WIDTH), F32),
            pltpu.VMEM((CHUNK, SWA_KV_HEADS * SWA_VREP), F32),
            pltpu.VMEM((SEQ_TILE, D_MODEL), BF16),
            pltpu.VMEM((N_EXPERTS, LANES), F32),
        ],
    )
    return pl.pallas_call(
        _mixer_kernel,
        grid_spec=grid_spec,
        out_shape=[
            jax.ShapeDtypeStruct((t_len * ROW_SUBLANES, LANES), F32),
            jax.ShapeDtypeStruct((ROUTE_ROWS, t_len), jnp.int32),
            jax.ShapeDtypeStruct((ROUTE_ROWS, t_len), F32),
            jax.ShapeDtypeStruct((N_EXPERTS, LANES), F32),
        ],
        compiler_params=pltpu.CompilerParams(
            dimension_semantics=("arbitrary", "arbitrary"), vmem_limit_bytes=VMEM_LIMIT),
        name="mixer_router",
        cost_estimate=_mixer_cost(t_len),
    )(cdec, x2, w_in, w_out, gn, ln_g, ln_b, wr_hi, wr_lo, br, din, qdec, kdec, bias, pen, kmask, vmask, omask, tri)


def _sc_mesh():
    return plsc.VectorSubcoreMesh(core_axis_name="core", subcore_axis_name="subcore")


def _sc_dispatch(h_rows, dest, slot0, n_slots):
    t_len = h_rows.shape[0]
    n_assign = dest.shape[0]
    per_worker = n_slots // SC_WORKERS
    n_chunks = per_worker // SC_CHUNK_ROWS
    assert n_chunks * SC_CHUNK_ROWS * SC_WORKERS == n_slots and slot0 + n_slots <= 3 * t_len
    assert per_worker % SC_LANES == 0 and n_assign % SC_LANES == 0 and n_assign == TOP_K * t_len

    @functools.partial(
        pl.kernel, mesh=_sc_mesh(), name="moe_dispatch_sc", cost_estimate=_row_move_cost(n_slots),
        compiler_params=pltpu.CompilerParams(needs_layout_passes=False),
        out_type=jax.ShapeDtypeStruct((n_slots, ROW_SUBLANES, LANES), F32),
        scratch_types=[pltpu.VMEM((n_assign,), jnp.int32),
                       pltpu.VMEM((per_worker,), jnp.int32),
                       pltpu.VMEM((SC_CHUNK_ROWS, ROW_SUBLANES, LANES), F32),
                       pltpu.SemaphoreType.DMA])
    def dispatch(h_hbm, dest_hbm, rows_hbm, dest_v, src_v, buf, sem):
        wid = lax.axis_index("subcore") * SC_CORES + lax.axis_index("core")
        local = wid * per_worker
        base = slot0 + local
        pltpu.sync_copy(dest_hbm, dest_v)
        lane = lax.iota(jnp.int32, SC_LANES)

        def wrap(a):
            a = jnp.where(a >= t_len, a - t_len, a)
            return jnp.where(a >= t_len, a - t_len, a)

        @pl.loop(0, per_worker // SC_LANES)
        def _(i):
            src_v[pl.ds(i * SC_LANES, SC_LANES)] = wrap(base + i * SC_LANES + lane)

        @pl.loop(0, n_assign // SC_LANES)
        def _(i):
            d = dest_v[pl.ds(i * SC_LANES, SC_LANES)] - base
            hit = (d >= 0) & (d < per_worker)
            plsc.store_scatter(src_v, [jnp.where(hit, d, 0)], wrap(i * SC_LANES + lane), mask=hit)

        @pl.loop(0, n_chunks)
        def _(c):
            off = pl.multiple_of(c * SC_CHUNK_ROWS, SC_CHUNK_ROWS)
            pltpu.async_copy(h_hbm.at[src_v.at[pl.ds(off, SC_CHUNK_ROWS)]], buf, sem).wait()
            pltpu.sync_copy(buf, rows_hbm.at[pl.ds(local + off, SC_CHUNK_ROWS)])

    return dispatch(h_rows, dest)


def _sc_gather_rows(table, idx, name):
    m = idx.shape[0]
    per_worker = m // (SC_CHUNK_ROWS * SC_WORKERS)
    assert per_worker * SC_CHUNK_ROWS * SC_WORKERS == m

    @functools.partial(
        pl.kernel, mesh=_sc_mesh(), name=name, cost_estimate=_row_move_cost(m),
        out_type=jax.ShapeDtypeStruct((m, ROW_SUBLANES, LANES), F32),
        scratch_types=[pltpu.VMEM((m,), jnp.int32),
                       pltpu.VMEM((SC_CHUNK_ROWS, ROW_SUBLANES, LANES), F32),
                       pltpu.SemaphoreType.DMA])
    def gather(table_hbm, idx_hbm, out_hbm, idx_v, buf, sem):
        wid = lax.axis_index("subcore") * SC_CORES + lax.axis_index("core")
        pltpu.sync_copy(idx_hbm, idx_v)

        @pl.loop(0, per_worker)
        def _(j):
            off = pl.multiple_of((j * SC_WORKERS + wid) * SC_CHUNK_ROWS, SC_CHUNK_ROWS)
            pltpu.async_copy(table_hbm.at[idx_v.at[pl.ds(off, SC_CHUNK_ROWS)]], buf, sem).wait()
            pltpu.sync_copy(buf, out_hbm.at[pl.ds(off, SC_CHUNK_ROWS)])

    return gather(table, idx)


def _expert_kernel(bexp_ref, nused_ref, rows_ref, wg_ref, wu_ref, wd_ref, *rest, first_block):
    y_ref, wg_s, wu_s, wd_s = rest[-4:]
    p = pl.program_id(0)
    blk = first_block + p
    prev = bexp_ref[jnp.maximum(blk - 1, 0)]

    @pl.when((p == 0) | (bexp_ref[blk] != prev))
    def _():
        wg_s[...] = wg_ref[0].astype(BF16)
        wu_s[...] = wu_ref[0].astype(BF16)
        wd_s[...] = wd_ref[0].astype(BF16)

    @pl.when(blk < nused_ref[0])
    def _():
        xb = _load_row_tiles(rows_ref).astype(BF16)
        g = _dot(xb, wg_s[...])
        u = _dot(xb, wu_s[...])
        a = (_silu(g) * u).astype(BF16)
        _store_row_tiles(y_ref, _dot(a, wd_s[...]))

    @pl.when(blk >= nused_ref[0])
    def _():
        y_ref[...] = jnp.zeros_like(y_ref)


def _expert_call(bexp, nused, rows, w_gate, w_up, w_down, y_prev, part, n_rows):
    part_blocks = rows.shape[0] // (ROW_SUBLANES * MOE_ROWS)
    first_block = part * part_blocks
    last_used = lambda p, nu: jnp.clip(first_block + p, first_block, jnp.maximum(nu[0] - 1, first_block))
    blk = lambda p, be, nu: (last_used(p, nu) - first_block, 0)
    wsel = lambda p, be, nu: (be[last_used(p, nu)], 0, 0)
    in_specs = [
        pl.BlockSpec((MOE_ROWS * ROW_SUBLANES, LANES), blk),
        pl.BlockSpec((1, D_MODEL, D_EXPERT), wsel),
        pl.BlockSpec((1, D_MODEL, D_EXPERT), wsel),
        pl.BlockSpec((1, D_EXPERT, D_MODEL), wsel),
    ]
    args = [rows, w_gate, w_up, w_down]
    aliases = {}
    if y_prev is not None:
        in_specs.append(pl.BlockSpec(memory_space=pl.ANY))
        args.append(y_prev)
        aliases = {2 + len(args) - 1: 0}
    grid_spec = pltpu.PrefetchScalarGridSpec(
        num_scalar_prefetch=2,
        grid=(part_blocks,),
        in_specs=in_specs,
        out_specs=pl.BlockSpec((MOE_ROWS * ROW_SUBLANES, LANES), lambda p, be, nu: (first_block + p, 0)),
        scratch_shapes=[
            pltpu.VMEM((D_MODEL, D_EXPERT), BF16),
            pltpu.VMEM((D_MODEL, D_EXPERT), BF16),
            pltpu.VMEM((D_EXPERT, D_MODEL), BF16),
        ],
    )
    return pl.pallas_call(
        functools.partial(_expert_kernel, first_block=first_block),
        grid_spec=grid_spec,
        out_shape=jax.ShapeDtypeStruct((n_rows * ROW_SUBLANES, LANES), F32),
        input_output_aliases=aliases,
        compiler_params=pltpu.CompilerParams(
            dimension_semantics=("arbitrary",), vmem_limit_bytes=VMEM_LIMIT),
        name="moe_experts",
        cost_estimate=_expert_cost(part_blocks * MOE_ROWS),
    )(bexp, nused, *args)


def _combine_kernel(h_ref, y0_ref, y1_ref, wts_ref, ln_g_ref, ln_b_ref, *rest):
    out_ref = rest[-1]
    wts = wts_ref[...].T
    ffn = _load_row_tiles(y0_ref) * wts[:, 0:1] + _load_row_tiles(y1_ref) * wts[:, 1:2]
    out_ref[...] = _layer_norm(DEEPNORM_ALPHA * _load_row_tiles(h_ref) + ffn, ln_g_ref[...], ln_b_ref[...])


def _combine_call(h_rows, yk, wts, ln_g, ln_b, out_prev, split, t_total):
    t_len = t_total // TOKEN_SPLITS
    n_tiles = t_len // COMBINE_TILE
    tok = lambda i: (i, 0)
    tok_split = lambda i: (i + split * n_tiles, 0)
    full2 = lambda i: (0, 0)
    tiles = pl.BlockSpec((COMBINE_TILE * ROW_SUBLANES, LANES), tok)
    tiles_k1 = pl.BlockSpec((COMBINE_TILE * ROW_SUBLANES, LANES), lambda i: (i + n_tiles, 0))
    in_specs = [pl.BlockSpec((COMBINE_TILE * ROW_SUBLANES, LANES), tok_split), tiles, tiles_k1,
                pl.BlockSpec((ROUTE_ROWS, COMBINE_TILE), lambda i: (0, i + split * n_tiles)),
                pl.BlockSpec((1, D_MODEL), full2),
                pl.BlockSpec((1, D_MODEL), full2)]
    args = [h_rows, yk, yk, wts, ln_g, ln_b]
    aliases = {}
    if out_prev is not None:
        in_specs.append(pl.BlockSpec(memory_space=pl.ANY))
        args.append(out_prev)
        aliases = {len(args) - 1: 0}
    return pl.pallas_call(
        _combine_kernel,
        grid=(n_tiles,),
        in_specs=in_specs,
        out_specs=pl.BlockSpec((COMBINE_TILE, D_MODEL), tok_split),
        out_shape=jax.ShapeDtypeStruct((t_total, D_MODEL), F32),
        input_output_aliases=aliases,
        compiler_params=pltpu.CompilerParams(
            dimension_semantics=("arbitrary",), vmem_limit_bytes=VMEM_LIMIT),
        name="moe_combine_ln",
        cost_estimate=pl.CostEstimate(flops=8 * t_len * D_MODEL, transcendentals=t_len,
                                      bytes_accessed=4 * ROW_BYTES * t_len),
    )(*args)


def _position_tables():
    c = CHUNK
    f = np.float32
    log_g = np.log1p(-np.exp2(-5.0 - np.arange(RET_HEADS, dtype=f))).astype(f)
    idx = np.arange(c, dtype=f)
    diff = idx[:, None] - idx[None, :]
    scale = f(RET_HEAD_DIM ** -0.5)
    din = np.where(diff >= 0, np.exp(log_g[:, None, None] * np.maximum(diff, 0.0)), 0.0).astype(f) * scale
    qdec = np.broadcast_to(np.exp(log_g[:, None] * (idx + 1.0))[:, :, None], (RET_HEADS, c, RET_HEAD_DIM))
    kdec = np.broadcast_to((np.exp(log_g[:, None] * (c - 1.0 - idx)) * scale)[:, :, None],
                           (RET_HEADS, c, RET_HEAD_DIM))
    cdec = np.exp(log_g * c)
    slopes = np.exp2(-8.0 * (np.arange(SWA_HEADS, dtype=f) + 1.0) / SWA_HEADS).astype(f)
    r = np.arange(c)[:, None]
    col = np.arange(c)[None, :]
    dist_prev = (r - col + c).astype(f)
    dist_cur = (r - col).astype(f)
    bprev = np.where((r < col)[None], -slopes[:, None, None] * dist_prev[None], NEG)
    bcur = np.where((r >= col)[None], -slopes[:, None, None] * dist_cur[None], NEG)
    bias = np.concatenate([bprev, bcur], axis=-1).reshape(SWA_KV_HEADS, SWA_GROUP * c, 2 * c)
    key = np.arange(2 * c)
    pen = np.where((key >= 1) & (key < c), NEG, 0.0)[None, :]
    lane_head = np.arange(SWA_VREP) // SWA_HEAD_DIM
    own = lane_head[None, None, :] == np.arange(SWA_GROUP)[:, None, None]
    kmask = np.broadcast_to(np.arange(c)[:, None] > 0, (c, SWA_HEAD_DIM))
    vmask = own & (key[None, :, None] > 0)
    omask = np.broadcast_to(own, (SWA_GROUP, 2 * c, SWA_VREP)).reshape(SWA_GROUP * 2 * c, SWA_VREP)
    tr = np.arange(SEQ_TILE)
    tri = tr[:, None] < tr[None, :]
    as_f32 = lambda v: np.ascontiguousarray(v, dtype=f)
    as_bf16 = lambda v: np.ascontiguousarray(v, dtype=f).astype(BF16)
    return (as_f32(cdec), as_f32(din), as_f32(qdec), as_f32(kdec), as_f32(bias), as_f32(pen),
            as_bf16(kmask), as_bf16(vmask), as_bf16(omask), as_bf16(tri))


def _mixer_constants(attn_sinks):
    cdec, din, qdec, kdec, bias, pen, kmask, vmask, omask, tri = _position_tables()
    sink = attn_sinks.astype(F32).reshape(SWA_KV_HEADS, SWA_GROUP, 1, 1)
    sink = jnp.broadcast_to(sink, (SWA_KV_HEADS, SWA_GROUP, CHUNK, 1)).reshape(SWA_KV_HEADS, SWA_GROUP * CHUNK, 1)
    is_slot = (np.arange(2 * CHUNK) == 0)[None, None, :]
    return cdec, din, qdec, kdec, jnp.where(is_slot, sink, bias), pen, kmask, vmask, omask, tri


def _tile_v_columns(w_in):
    v_cols = w_in[:, _VA:].reshape(D_MODEL, SWA_KV_HEADS, 1, SWA_HEAD_DIM)
    v_cols = jnp.broadcast_to(v_cols, (D_MODEL, SWA_KV_HEADS, SWA_GROUP, SWA_HEAD_DIM))
    return jnp.concatenate([w_in[:, :_VA], v_cols.reshape(D_MODEL, SWA_KV_HEADS * SWA_VREP)], axis=1)


def _router_tables(w_group_router, b_group_router, w_expert_router, b_expert_router):
    w_e = jnp.transpose(w_expert_router, (1, 0, 2)).reshape(D_MODEL, N_EXPERTS)
    w = jnp.concatenate([w_e, w_group_router,
                         jnp.zeros((D_MODEL, LANES - N_EXPERTS - N_GROUPS), F32)], axis=1)
    bias = jnp.concatenate([b_expert_router.reshape(N_EXPERTS), b_group_router,
                            jnp.zeros((LANES - N_EXPERTS - N_GROUPS,), F32)])[None, :]
    w_hi = w.astype(BF16)
    w_lo = (w - w_hi.astype(F32)).astype(BF16)
    return w_hi, w_lo, bias


def kernel(x, w_in, ret_gn_g, attn_sinks, w_out, ln1_g, ln1_b, w_group_router, b_group_router,
           w_expert_router, b_expert_router, w_gate, w_up, w_down, ln2_g, ln2_b):
    bsz, s_len, d = x.shape
    assert d == D_MODEL and s_len % SEQ_TILE == 0 and w_in.shape[0] == DEPTH == 1
    assert bsz % TOKEN_SPLITS == 0
    t_len = bsz * s_len
    t_split = t_len // TOKEN_SPLITS
    n_blocks = t_len * TOP_K // MOE_ROWS + N_EXPERTS
    n_rows = n_blocks * MOE_ROWS
    part_rows = n_rows // SLOT_PARTS

    consts = _mixer_constants(attn_sinks[0])
    wr_hi, wr_lo, br = _router_tables(w_group_router[0], b_group_router[0],
                                      w_expert_router[0], b_expert_router[0])
    h_rows, ids, wts, cnt = _mixer_call(
        x.reshape(t_len, d), _tile_v_columns(w_in[0]).astype(BF16), w_out[0].astype(BF16),
        ret_gn_g[0][None, :], ln1_g[0][None, :], ln1_b[0][None, :], wr_hi, wr_lo, br, consts, bsz, s_len, 0)

    counts = cnt[:, 0].astype(jnp.int32)
    padded = (counts + MOE_ROWS - 1) // MOE_ROWS * MOE_ROWS
    pend = jnp.cumsum(padded)
    pstart = pend - padded
    onehot = ids[0:TOP_K, :, None] == jnp.arange(N_EXPERTS, dtype=jnp.int32)
    dest2 = jnp.sum(jnp.where(onehot, pstart, 0), axis=-1) + ids[TOP_K:2 * TOP_K]
    dest = dest2.reshape(-1)
    nused = (pend[-1:] // MOE_ROWS).astype(jnp.int32)
    blk_start = jnp.minimum(jnp.arange(n_blocks, dtype=jnp.int32), nused[0] - 1) * MOE_ROWS
    bexp = jnp.minimum(jnp.sum(pend[None, :] <= blk_start[:, None], axis=-1), N_EXPERTS - 1).astype(jnp.int32)

    h_tiles = h_rows.reshape(t_len, ROW_SUBLANES, LANES)
    y = None
    for part in range(SLOT_PARTS):
        rows = _sc_dispatch(h_tiles, dest, part * part_rows, part_rows)
        y = _expert_call(bexp, nused, rows.reshape(part_rows * ROW_SUBLANES, LANES),
                         w_gate[0], w_up[0], w_down[0], y, part, n_rows)

    y_tiles = y.reshape(n_rows, ROW_SUBLANES, LANES)
    out = None
    for sp in range(TOKEN_SPLITS):
        idx = dest2[:, sp * t_split:(sp + 1) * t_split].reshape(-1)
        yk = _sc_gather_rows(y_tiles, idx, "moe_combine_sc")
        out = _combine_call(h_rows, yk.reshape(TOP_K * t_split * ROW_SUBLANES, LANES), wts,
                            ln2_g[0][None, :], ln2_b[0][None, :], out, sp, t_len)
    return out.reshape(bsz, s_len, d)
```

```python
import functools

import jax
import jax.numpy as jnp
import numpy as np
from jax import lax
from jax.experimental import pallas as pl
from jax.experimental.pallas import tpu as pltpu
from jax.experimental.pallas import tpu_sc as plsc

F32 = jnp.float32
BF16 = jnp.bfloat16
U32 = jnp.uint32

D_MODEL = 1024
RET_HEADS = 4
RET_HEAD_DIM = 128
RET_WIDTH = RET_HEADS * RET_HEAD_DIM
CHUNK = 128
SWA_HEADS = 8
SWA_KV_HEADS = 2
SWA_GROUP = SWA_HEADS // SWA_KV_HEADS
SWA_HEAD_DIM = 64
SWA_WIDTH = SWA_HEADS * SWA_HEAD_DIM
SWA_KV_WIDTH = SWA_KV_HEADS * SWA_HEAD_DIM
IN_WIDTH = 4 * RET_WIDTH + SWA_WIDTH + 2 * SWA_KV_WIDTH
N_GROUPS = 4
EXPERTS_PER_GROUP = 8
N_EXPERTS = N_GROUPS * EXPERTS_PER_GROUP
TOP_K = 2
D_EXPERT = 512
LN_EPS = 1e-5
GN_EPS = 1e-6
DEPTH = 1
DEEPNORM_ALPHA = (2 * DEPTH) ** 0.25
NEG = -1e30

LANES = 128
ROW_SUBLANES = D_MODEL // 2 // LANES
SEQ_TILE = 512
MOE_ROWS = 256
COMBINE_TILE = 256
TOKEN_SPLITS = 2
SLOT_PARTS = 2
SC_CORES = 2
SC_SUBCORES = 16
SC_WORKERS = SC_CORES * SC_SUBCORES
SC_LANES = 16
SC_CHUNK_ROWS = 64
VMEM_LIMIT = 56 * 1024 * 1024

_QR, _KR, _VR, _GR = 0, RET_WIDTH, 2 * RET_WIDTH, 3 * RET_WIDTH
_QA = 4 * RET_WIDTH
_KA = _QA + SWA_WIDTH
_VA = _KA + SWA_KV_WIDTH
SWA_VREP = SWA_GROUP * SWA_HEAD_DIM
IN_WIDTH_TILED = _VA + SWA_KV_HEADS * SWA_VREP
ROUTE_ROWS = 8
GROUP_LANE0 = N_EXPERTS


def _dot(a, b):
    return jnp.dot(a, b, preferred_element_type=F32)


def _dot_nt(a, b):
    return lax.dot_general(a, b, (((1,), (1,)), ((), ())), preferred_element_type=F32)


def _dot_tn(a, b):
    return lax.dot_general(a, b, (((0,), (0,)), ((), ())), preferred_element_type=F32)


def _layer_norm(z, g, b):
    mu = jnp.mean(z, axis=-1, keepdims=True)
    zc = z - mu
    var = jnp.mean(zc * zc, axis=-1, keepdims=True)
    return zc * lax.rsqrt(var + LN_EPS) * g + b


def _silu(g):
    return g / (1.0 + jnp.exp(-g))


ROW_BYTES = 4 * ROW_SUBLANES * LANES


def _mixer_cost(t):
    proj = 2 * t * D_MODEL * (IN_WIDTH_TILED + D_MODEL + 3 * LANES)
    retention = RET_HEADS * 4 * 2 * t * CHUNK * RET_HEAD_DIM
    swa = SWA_KV_HEADS * 2 * t * SWA_GROUP * 2 * CHUNK * (SWA_HEAD_DIM + 2 * SWA_VREP)
    weights = 2 * D_MODEL * (IN_WIDTH_TILED + D_MODEL + 2 * LANES)
    return pl.CostEstimate(flops=proj + retention + swa + 2 * t * SEQ_TILE * N_EXPERTS,
                           transcendentals=t * (SWA_HEADS * 2 * CHUNK + RET_WIDTH + LANES),
                           bytes_accessed=(2 * 4 * D_MODEL + ROW_BYTES) * t + weights)


def _expert_cost(n_rows):
    return pl.CostEstimate(flops=2 * 3 * n_rows * D_MODEL * D_EXPERT, transcendentals=n_rows * D_EXPERT,
                           bytes_accessed=2 * ROW_BYTES * n_rows + 4 * 3 * N_EXPERTS * D_MODEL * D_EXPERT)


def _row_move_cost(n_rows):
    return pl.CostEstimate(flops=0, transcendentals=0, bytes_accessed=2 * ROW_BYTES * n_rows)


def _pack_rows(ref, val):
    n = val.shape[0]
    half = D_MODEL // 2
    hi = lax.bitcast_convert_type(val[:, :half].astype(BF16).astype(F32), U32)
    lo = lax.bitcast_convert_type(val[:, half:].astype(BF16).astype(F32), U32)
    word = hi | (lo >> 16)
    for j in range(ROW_SUBLANES):
        ref[pl.ds(j, n, stride=ROW_SUBLANES), :] = word[:, j * LANES:(j + 1) * LANES]


def _unpack_rows(ref):
    n = ref.shape[0] // ROW_SUBLANES
    word = jnp.concatenate([ref[pl.ds(j, n, stride=ROW_SUBLANES), :] for j in range(ROW_SUBLANES)], axis=1)
    hi = lax.bitcast_convert_type(word & jnp.uint32(0xFFFF0000), F32)
    lo = lax.bitcast_convert_type(word << 16, F32)
    return jnp.concatenate([hi, lo], axis=1)


def _mixer_kernel(cdec_ref, x_ref, w_in_ref, w_out_ref, gn_ref, ln_g_ref, ln_b_ref,
                  wr_hi_ref, wr_lo_ref, br_ref, din_ref, qdec_ref, kdec_ref,
                  bias_ref, pen_ref, kmask_ref, vmask_ref, omask_ref, tri_ref,
                  h_ref, hp_ref, ids_ref, wts_ref, cnt_ref,
                  state_scr, kprev_scr, vprev_scr, o_scr, carry_scr):
    b = pl.program_id(0)
    n = pl.program_id(1)
    ts = x_ref.shape[0]

    @pl.when(n == 0)
    def _():
        state_scr[...] = jnp.zeros_like(state_scr)
        kprev_scr[...] = jnp.zeros_like(kprev_scr)
        vprev_scr[...] = jnp.zeros_like(vprev_scr)

    @pl.when((b == 0) & (n == 0))
    def _():
        carry_scr[...] = jnp.zeros_like(carry_scr)

    x = x_ref[...]
    xb = x.astype(BF16)

    def proj(lo, hi):
        return _dot(xb, w_in_ref[:, lo:hi])

    q_r, k_r, v_r, g_r = proj(_QR, _KR), proj(_KR, _VR), proj(_VR, _GR), proj(_GR, _QA)
    q_a, k_a = proj(_QA, _KA), proj(_KA, _VA)
    k_ab = k_a.astype(BF16)
    v_rep = proj(_VA, _VA + SWA_KV_HEADS * SWA_VREP).astype(BF16)
    first_pen = jnp.where(n == 0, pen_ref[...], 0.0)

    for c in range(ts // CHUNK):
        rs = slice(c * CHUNK, (c + 1) * CHUNK)
        for hd in range(RET_HEADS):
            cs = slice(hd * RET_HEAD_DIM, (hd + 1) * RET_HEAD_DIM)
            q = q_r[rs, cs]
            k = k_r[rs, cs]
            v = v_r[rs, cs].astype(BF16)
            scores = _dot_nt(q.astype(BF16), k.astype(BF16)) * din_ref[hd]
            inner = _dot(scores.astype(BF16), v)
            st = state_scr[hd]
            cross = _dot((q * qdec_ref[hd]).astype(BF16), st.astype(BF16))
            kv = _dot_tn((k * kdec_ref[hd]).astype(BF16), v)
            state_scr[hd] = st * cdec_ref[hd] + kv
            o = inner + cross
            mu = jnp.mean(o, axis=-1, keepdims=True)
            oc = o - mu
            var = jnp.mean(oc * oc, axis=-1, keepdims=True)
            on = oc * lax.rsqrt(var + GN_EPS) * gn_ref[:, cs] * _silu(g_r[rs, cs])
            o_scr[rs, cs] = on.astype(BF16)
        for j in range(SWA_KV_HEADS):
            ks = slice(j * SWA_HEAD_DIM, (j + 1) * SWA_HEAD_DIM)
            vs = slice(j * SWA_VREP, (j + 1) * SWA_VREP)
            if c == 0:
                kp = kprev_scr[:, ks].astype(BF16)
                vp = vprev_scr[:, vs].astype(BF16)
            else:
                ps = slice((c - 1) * CHUNK, c * CHUNK)
                kp = k_ab[ps, ks]
                vp = v_rep[ps, vs]
            kp = kp * kmask_ref[...]
            kcat = jnp.concatenate([kp, k_ab[rs, ks]], axis=0)
            vcat = jnp.concatenate([vp, v_rep[rs, vs]], axis=0)
            q0 = j * SWA_GROUP * SWA_HEAD_DIM
            qs = jnp.concatenate(
                [q_a[rs, q0 + g * SWA_HEAD_DIM:q0 + (g + 1) * SWA_HEAD_DIM] for g in range(SWA_GROUP)],
                axis=0)
            qs = (qs * (SWA_HEAD_DIM ** -0.5)).astype(BF16)
            s = _dot_nt(qs, kcat) + bias_ref[j]
            if c == 0:
                s = s + first_pen
            m = jnp.max(jnp.maximum(s[:, :CHUNK], s[:, CHUNK:]), axis=-1, keepdims=True)
            p = jnp.exp(s - m).astype(BF16)
            p_all = jnp.concatenate([p[g * CHUNK:(g + 1) * CHUNK] for g in range(SWA_GROUP)], axis=1)
            v_blk = jnp.concatenate([vcat * vmask_ref[g] for g in range(SWA_GROUP)], axis=0)
            num = _dot(p_all, v_blk)
            den = _dot(p_all, omask_ref[...])
            c0 = RET_WIDTH + j * SWA_GROUP * SWA_HEAD_DIM
            o_scr[rs, c0:c0 + SWA_GROUP * SWA_HEAD_DIM] = (num / den).astype(BF16)

    kprev_scr[...] = k_ab[ts - CHUNK:, :].astype(F32)
    vprev_scr[...] = v_rep[ts - CHUNK:, :].astype(F32)

    mix = _dot(o_scr[...], w_out_ref[...])
    h = _layer_norm(DEEPNORM_ALPHA * x + mix, ln_g_ref[...], ln_b_ref[...])
    h_ref[...] = h
    _pack_rows(hp_ref, h)

    h_hi = h.astype(BF16)
    h_lo = (h - h_hi.astype(F32)).astype(BF16)
    logits = (_dot(h_hi, wr_hi_ref[...]) + _dot(h_lo, wr_hi_ref[...]) + _dot(h_hi, wr_lo_ref[...])
              + br_ref[...])
    lt = logits.T
    row = lax.broadcasted_iota(jnp.int32, (EXPERTS_PER_GROUP, ts), 0).astype(F32)
    big = 1e9
    ninf = -jnp.inf
    col_max = lambda v: jnp.max(v, axis=0, keepdims=True)
    first_at = lambda v, m: jnp.min(jnp.where(v == m, row, big), axis=0, keepdims=True)
    gl = jnp.where(row < N_GROUPS, lt[GROUP_LANE0:GROUP_LANE0 + EXPERTS_PER_GROUP], ninf)
    gmax = col_max(gl)
    gidx = first_at(gl, gmax)
    g_w = 1.0 / jnp.sum(jnp.exp(gl - gmax), axis=0, keepdims=True)
    el = lt[(N_GROUPS - 1) * EXPERTS_PER_GROUP:N_GROUPS * EXPERTS_PER_GROUP]
    for g in range(N_GROUPS - 2, -1, -1):
        el = jnp.where(gidx == g, lt[g * EXPERTS_PER_GROUP:(g + 1) * EXPERTS_PER_GROUP], el)
    m1 = col_max(el)
    i1 = first_at(el, m1)
    el2 = jnp.where(row == i1, ninf, el)
    m2 = col_max(el2)
    i2 = first_at(el2, m2)
    t = jnp.exp(m2 - m1)
    w1 = g_w / (1.0 + t)
    w2 = g_w * t / (1.0 + t)
    e1 = gidx * EXPERTS_PER_GROUP + i1
    e2 = gidx * EXPERTS_PER_GROUP + i2
    erow = lax.broadcasted_iota(jnp.int32, (N_EXPERTS, ts), 0).astype(F32)
    hit1 = erow == e1
    hit2 = erow == e2
    onehot = (hit1 | hit2).astype(BF16)
    prefix = _dot(onehot, tri_ref[...]) + carry_scr[:, 0:1]
    r1 = jnp.sum(jnp.where(hit1, prefix, 0.0), axis=0, keepdims=True)
    r2 = jnp.sum(jnp.where(hit2, prefix, 0.0), axis=0, keepdims=True)
    carry = carry_scr[:, 0:1] + jnp.sum(onehot.astype(F32), axis=1, keepdims=True)
    carry_scr[...] = jnp.broadcast_to(carry, carry_scr.shape)
    cnt_ref[...] = jnp.broadcast_to(carry, cnt_ref.shape)
    pick = lambda k, v, rest: jnp.where(row == k, v, rest)
    ids_ref[...] = pick(0, e1, pick(1, e2, pick(2, r1, pick(3, r2, 0.0)))).astype(jnp.int32)
    wts_ref[...] = pick(0, w1, pick(1, w2, 0.0))


def _mixer_call(x2, w_in, w_out, gn, ln_g, ln_b, wr_hi, wr_lo, br, consts, bsz, s_len, first_seq):
    cdec, din, qdec, kdec, bias, pen, kmask, vmask, omask, tri = consts
    t_len = bsz * s_len
    ns = s_len // SEQ_TILE
    tok = lambda b, n, *_: (b * ns + n, 0)
    tok_in = lambda b, n, *_: ((first_seq + b) * ns + n, 0)
    tok_t = lambda b, n, *_: (0, b * ns + n)
    full2 = lambda b, n, *_: (0, 0)
    full3 = lambda b, n, *_: (0, 0, 0)
    grid_spec = pltpu.PrefetchScalarGridSpec(
        num_scalar_prefetch=1,
        grid=(bsz, ns),
        in_specs=[
            pl.BlockSpec((SEQ_TILE, D_MODEL), tok_in),
            pl.BlockSpec((D_MODEL, IN_WIDTH_TILED), full2),
            pl.BlockSpec((D_MODEL, D_MODEL), full2),
            pl.BlockSpec((1, RET_WIDTH), full2),
            pl.BlockSpec((1, D_MODEL), full2),
            pl.BlockSpec((1, D_MODEL), full2),
            pl.BlockSpec((D_MODEL, LANES), full2),
            pl.BlockSpec((D_MODEL, LANES), full2),
            pl.BlockSpec((1, LANES), full2),
            pl.BlockSpec((RET_HEADS, CHUNK, CHUNK), full3),
            pl.BlockSpec((RET_HEADS, CHUNK, RET_HEAD_DIM), full3),
            pl.BlockSpec((RET_HEADS, CHUNK, RET_HEAD_DIM), full3),
            pl.BlockSpec((SWA_KV_HEADS, SWA_GROUP * CHUNK, 2 * CHUNK), full3),
            pl.BlockSpec((1, 2 * CHUNK), full2),
            pl.BlockSpec((CHUNK, SWA_HEAD_DIM), full2),
            pl.BlockSpec((SWA_GROUP, 2 * CHUNK, SWA_VREP), full3),
            pl.BlockSpec((SWA_GROUP * 2 * CHUNK, SWA_VREP), full2),
            pl.BlockSpec((SEQ_TILE, SEQ_TILE), full2),
        ],
        out_specs=[
            pl.BlockSpec((SEQ_TILE, D_MODEL), tok),
            pl.BlockSpec((SEQ_TILE * ROW_SUBLANES, LANES), tok),
            pl.BlockSpec((ROUTE_ROWS, SEQ_TILE), tok_t),
            pl.BlockSpec((ROUTE_ROWS, SEQ_TILE), tok_t),
            pl.BlockSpec((N_EXPERTS, LANES), full2),
        ],
        scratch_shapes=[
            pltpu.VMEM((RET_HEADS, RET_HEAD_DIM, RET_HEAD_DIM), F32),
            pltpu.VMEM((CHUNK, SWA_KV_WIDTH), F32),
            pltpu.VMEM((CHUNK, SWA_KV_HEADS * SWA_VREP), F32),
            pltpu.VMEM((SEQ_TILE, D_MODEL), BF16),
            pltpu.VMEM((N_EXPERTS, LANES), F32),
        ],
    )
    return pl.pallas_call(
        _mixer_kernel,
        grid_spec=grid_spec,
        out_shape=[
            jax.ShapeDtypeStruct((t_len, D_MODEL), F32),
            jax.ShapeDtypeStruct((t_len * ROW_SUBLANES, LANES), U32),
            jax.ShapeDtypeStruct((ROUTE_ROWS, t_len), jnp.int32),
            jax.ShapeDtypeStruct((ROUTE_ROWS, t_len), F32),
            jax.ShapeDtypeStruct((N_EXPERTS, LANES), F32),
        ],
        compiler_params=pltpu.CompilerParams(
            dimension_semantics=("arbitrary", "arbitrary"), vmem_limit_bytes=VMEM_LIMIT),
        name="mixer_router",
        cost_estimate=_mixer_cost(t_len),
    )(cdec, x2, w_in, w_out, gn, ln_g, ln_b, wr_hi, wr_lo, br, din, qdec, kdec, bias, pen, kmask, vmask, omask, tri)


def _sc_mesh():
    return plsc.VectorSubcoreMesh(core_axis_name="core", subcore_axis_name="subcore")


def _sc_dispatch(h_rows, dest, slot0, n_slots):
    t_len = h_rows.shape[0]
    n_assign = dest.shape[0]
    per_worker = n_slots // SC_WORKERS
    n_chunks = per_worker // SC_CHUNK_ROWS
    assert n_chunks * SC_CHUNK_ROWS * SC_WORKERS == n_slots and slot0 + n_slots <= 3 * t_len
    assert per_worker % SC_LANES == 0 and n_assign % SC_LANES == 0 and n_assign == TOP_K * t_len

    @functools.partial(
        pl.kernel, mesh=_sc_mesh(), name="moe_dispatch_sc", cost_estimate=_row_move_cost(n_slots),
        compiler_params=pltpu.CompilerParams(needs_layout_passes=False),
        out_type=jax.ShapeDtypeStruct((n_slots, ROW_SUBLANES, LANES), U32),
        scratch_types=[pltpu.VMEM((n_assign,), jnp.int32),
                       pltpu.VMEM((per_worker,), jnp.int32),
                       pltpu.VMEM((SC_CHUNK_ROWS, ROW_SUBLANES, LANES), U32),
                       pltpu.SemaphoreType.DMA])
    def dispatch(h_hbm, dest_hbm, rows_hbm, dest_v, src_v, buf, sem):
        wid = lax.axis_index("subcore") * SC_CORES + lax.axis_index("core")
        local = wid * per_worker
        base = slot0 + local
        pltpu.sync_copy(dest_hbm, dest_v)
        lane = lax.iota(jnp.int32, SC_LANES)

        def wrap(a):
            a = jnp.where(a >= t_len, a - t_len, a)
            return jnp.where(a >= t_len, a - t_len, a)

        @pl.loop(0, per_worker // SC_LANES)
        def _(i):
            src_v[pl.ds(i * SC_LANES, SC_LANES)] = wrap(base + i * SC_LANES + lane)

        @pl.loop(0, n_assign // SC_LANES)
        def _(i):
            d = dest_v[pl.ds(i * SC_LANES, SC_LANES)] - base
            hit = (d >= 0) & (d < per_worker)
            plsc.store_scatter(src_v, [jnp.where(hit, d, 0)], wrap(i * SC_LANES + lane), mask=hit)

        @pl.loop(0, n_chunks)
        def _(c):
            off = pl.multiple_of(c * SC_CHUNK_ROWS, SC_CHUNK_ROWS)
            pltpu.async_copy(h_hbm.at[src_v.at[pl.ds(off, SC_CHUNK_ROWS)]], buf, sem).wait()
            pltpu.sync_copy(buf, rows_hbm.at[pl.ds(local + off, SC_CHUNK_ROWS)])

    return dispatch(h_rows, dest)


def _sc_gather_rows(table, idx, name):
    m = idx.shape[0]
    per_worker = m // (SC_CHUNK_ROWS * SC_WORKERS)
    assert per_worker * SC_CHUNK_ROWS * SC_WORKERS == m

    @functools.partial(
        pl.kernel, mesh=_sc_mesh(), name=name, cost_estimate=_row_move_cost(m),
        out_type=jax.ShapeDtypeStruct((m, ROW_SUBLANES, LANES), U32),
        scratch_types=[pltpu.VMEM((m,), jnp.int32),
                       pltpu.VMEM((SC_CHUNK_ROWS, ROW_SUBLANES, LANES), U32),
                       pltpu.SemaphoreType.DMA])
    def gather(table_hbm, idx_hbm, out_hbm, idx_v, buf, sem):
        wid = lax.axis_index("subcore") * SC_CORES + lax.axis_index("core")
        pltpu.sync_copy(idx_hbm, idx_v)

        @pl.loop(0, per_worker)
        def _(j):
            off = pl.multiple_of((j * SC_WORKERS + wid) * SC_CHUNK_ROWS, SC_CHUNK_ROWS)
            pltpu.async_copy(table_hbm.at[idx_v.at[pl.ds(off, SC_CHUNK_ROWS)]], buf, sem).wait()
            pltpu.sync_copy(buf, out_hbm.at[pl.ds(off, SC_CHUNK_ROWS)])

    return gather(table, idx)


def _expert_kernel(bexp_ref, nused_ref, rows_ref, wg_ref, wu_ref, wd_ref, *rest, first_block):
    y_ref, wg_s, wu_s, wd_s = rest[-4:]
    p = pl.program_id(0)
    blk = first_block + p
    prev = bexp_ref[jnp.maximum(blk - 1, 0)]

    @pl.when((p == 0) | (bexp_ref[blk] != prev))
    def _():
        wg_s[...] = wg_ref[0].astype(BF16)
        wu_s[...] = wu_ref[0].astype(BF16)
        wd_s[...] = wd_ref[0].astype(BF16)

    @pl.when(blk < nused_ref[0])
    def _():
        xb = _unpack_rows(rows_ref).astype(BF16)
        g = _dot(xb, wg_s[...])
        u = _dot(xb, wu_s[...])
        a = (_silu(g) * u).astype(BF16)
        _pack_rows(y_ref, _dot(a, wd_s[...]))

    @pl.when(blk >= nused_ref[0])
    def _():
        y_ref[...] = jnp.zeros_like(y_ref)


def _expert_call(bexp, nused, rows, w_gate, w_up, w_down, y_prev, part, n_rows):
    part_blocks = rows.shape[0] // (ROW_SUBLANES * MOE_ROWS)
    first_block = part * part_blocks
    last_used = lambda p, nu: jnp.clip(first_block + p, first_block, jnp.maximum(nu[0] - 1, first_block))
    blk = lambda p, be, nu: (last_used(p, nu) - first_block, 0)
    wsel = lambda p, be, nu: (be[last_used(p, nu)], 0, 0)
    in_specs = [
        pl.BlockSpec((MOE_ROWS * ROW_SUBLANES, LANES), blk),
        pl.BlockSpec((1, D_MODEL, D_EXPERT), wsel),
        pl.BlockSpec((1, D_MODEL, D_EXPERT), wsel),
        pl.BlockSpec((1, D_EXPERT, D_MODEL), wsel),
    ]
    args = [rows, w_gate, w_up, w_down]
    aliases = {}
    if y_prev is not None:
        in_specs.append(pl.BlockSpec(memory_space=pl.ANY))
        args.append(y_prev)
        aliases = {2 + len(args) - 1: 0}
    grid_spec = pltpu.PrefetchScalarGridSpec(
        num_scalar_prefetch=2,
        grid=(part_blocks,),
        in_specs=in_specs,
        out_specs=pl.BlockSpec((MOE_ROWS * ROW_SUBLANES, LANES), lambda p, be, nu: (first_block + p, 0)),
        scratch_shapes=[
            pltpu.VMEM((D_MODEL, D_EXPERT), BF16),
            pltpu.VMEM((D_MODEL, D_EXPERT), BF16),
            pltpu.VMEM((D_EXPERT, D_MODEL), BF16),
        ],
    )
    return pl.pallas_call(
        functools.partial(_expert_kernel, first_block=first_block),
        grid_spec=grid_spec,
        out_shape=jax.ShapeDtypeStruct((n_rows * ROW_SUBLANES, LANES), U32),
        input_output_aliases=aliases,
        compiler_params=pltpu.CompilerParams(
            dimension_semantics=("arbitrary",), vmem_limit_bytes=VMEM_LIMIT),
        name="moe_experts",
        cost_estimate=_expert_cost(part_blocks * MOE_ROWS),
    )(bexp, nused, *args)


def _combine_kernel(h_ref, y0_ref, y1_ref, wts_ref, ln_g_ref, ln_b_ref, *rest):
    out_ref = rest[-1]
    wts = wts_ref[...].T
    ffn = _unpack_rows(y0_ref) * wts[:, 0:1] + _unpack_rows(y1_ref) * wts[:, 1:2]
    out_ref[...] = _layer_norm(DEEPNORM_ALPHA * h_ref[...] + ffn, ln_g_ref[...], ln_b_ref[...])


def _combine_call(h, yk, wts, ln_g, ln_b, out_prev, split, t_total):
    t_len = t_total // TOKEN_SPLITS
    n_tiles = t_len // COMBINE_TILE
    tok = lambda i: (i, 0)
    tok_split = lambda i: (i + split * n_tiles, 0)
    full2 = lambda i: (0, 0)
    tiles = pl.BlockSpec((COMBINE_TILE * ROW_SUBLANES, LANES), tok)
    tiles_k1 = pl.BlockSpec((COMBINE_TILE * ROW_SUBLANES, LANES), lambda i: (i + n_tiles, 0))
    in_specs = [pl.BlockSpec((COMBINE_TILE, D_MODEL), tok_split), tiles, tiles_k1,
                pl.BlockSpec((ROUTE_ROWS, COMBINE_TILE), lambda i: (0, i + split * n_tiles)),
                pl.BlockSpec((1, D_MODEL), full2),
                pl.BlockSpec((1, D_MODEL), full2)]
    args = [h, yk, yk, wts, ln_g, ln_b]
    aliases = {}
    if out_prev is not None:
        in_specs.append(pl.BlockSpec(memory_space=pl.ANY))
        args.append(out_prev)
        aliases = {len(args) - 1: 0}
    return pl.pallas_call(
        _combine_kernel,
        grid=(n_tiles,),
        in_specs=in_specs,
        out_specs=pl.BlockSpec((COMBINE_TILE, D_MODEL), tok_split),
        out_shape=jax.ShapeDtypeStruct((t_total, D_MODEL), F32),
        input_output_aliases=aliases,
        compiler_params=pltpu.CompilerParams(
            dimension_semantics=("arbitrary",), vmem_limit_bytes=VMEM_LIMIT),
        name="moe_combine_ln",
        cost_estimate=pl.CostEstimate(flops=8 * t_len * D_MODEL, transcendentals=t_len,
                                      bytes_accessed=(2 * 4 * D_MODEL + TOP_K * ROW_BYTES) * t_len),
    )(*args)


def _position_tables():
    c = CHUNK
    f = np.float32
    log_g = np.log1p(-np.exp2(-5.0 - np.arange(RET_HEADS, dtype=f))).astype(f)
    idx = np.arange(c, dtype=f)
    diff = idx[:, None] - idx[None, :]
    scale = f(RET_HEAD_DIM ** -0.5)
    din = np.where(diff >= 0, np.exp(log_g[:, None, None] * np.maximum(diff, 0.0)), 0.0).astype(f) * scale
    qdec = np.broadcast_to(np.exp(log_g[:, None] * (idx + 1.0))[:, :, None], (RET_HEADS, c, RET_HEAD_DIM))
    kdec = np.broadcast_to((np.exp(log_g[:, None] * (c - 1.0 - idx)) * scale)[:, :, None],
                           (RET_HEADS, c, RET_HEAD_DIM))
    cdec = np.exp(log_g * c)
    slopes = np.exp2(-8.0 * (np.arange(SWA_HEADS, dtype=f) + 1.0) / SWA_HEADS).astype(f)
    r = np.arange(c)[:, None]
    col = np.arange(c)[None, :]
    dist_prev = (r - col + c).astype(f)
    dist_cur = (r - col).astype(f)
    bprev = np.where((r < col)[None], -slopes[:, None, None] * dist_prev[None], NEG)
    bcur = np.where((r >= col)[None], -slopes[:, None, None] * dist_cur[None], NEG)
    bias = np.concatenate([bprev, bcur], axis=-1).reshape(SWA_KV_HEADS, SWA_GROUP * c, 2 * c)
    key = np.arange(2 * c)
    pen = np.where((key >= 1) & (key < c), NEG, 0.0)[None, :]
    lane_head = np.arange(SWA_VREP) // SWA_HEAD_DIM
    own = lane_head[None, None, :] == np.arange(SWA_GROUP)[:, None, None]
    kmask = np.broadcast_to(np.arange(c)[:, None] > 0, (c, SWA_HEAD_DIM))
    vmask = own & (key[None, :, None] > 0)
    omask = np.broadcast_to(own, (SWA_GROUP, 2 * c, SWA_VREP)).reshape(SWA_GROUP * 2 * c, SWA_VREP)
    tr = np.arange(SEQ_TILE)
    tri = tr[:, None] < tr[None, :]
    as_f32 = lambda v: np.ascontiguousarray(v, dtype=f)
    as_bf16 = lambda v: np.ascontiguousarray(v, dtype=f).astype(BF16)
    return (as_f32(cdec), as_f32(din), as_f32(qdec), as_f32(kdec), as_f32(bias), as_f32(pen),
            as_bf16(kmask), as_bf16(vmask), as_bf16(omask), as_bf16(tri))


def _mixer_constants(attn_sinks):
    cdec, din, qdec, kdec, bias, pen, kmask, vmask, omask, tri = _position_tables()
    sink = attn_sinks.astype(F32).reshape(SWA_KV_HEADS, SWA_GROUP, 1, 1)
    sink = jnp.broadcast_to(sink, (SWA_KV_HEADS, SWA_GROUP, CHUNK, 1)).reshape(SWA_KV_HEADS, SWA_GROUP * CHUNK, 1)
    is_slot = (np.arange(2 * CHUNK) == 0)[None, None, :]
    return cdec, din, qdec, kdec, jnp.where(is_slot, sink, bias), pen, kmask, vmask, omask, tri


def _tile_v_columns(w_in):
    v_cols = w_in[:, _VA:].reshape(D_MODEL, SWA_KV_HEADS, 1, SWA_HEAD_DIM)
    v_cols = jnp.broadcast_to(v_cols, (D_MODEL, SWA_KV_HEADS, SWA_GROUP, SWA_HEAD_DIM))
    return jnp.concatenate([w_in[:, :_VA], v_cols.reshape(D_MODEL, SWA_KV_HEADS * SWA_VREP)], axis=1)


def _router_tables(w_group_router, b_group_router, w_expert_router, b_expert_router):
    w_e = jnp.transpose(w_expert_router, (1, 0, 2)).reshape(D_MODEL, N_EXPERTS)
    w = jnp.concatenate([w_e, w_group_router,
                         jnp.zeros((D_MODEL, LANES - N_EXPERTS - N_GROUPS), F32)], axis=1)
    bias = jnp.concatenate([b_expert_router.reshape(N_EXPERTS), b_group_router,
                            jnp.zeros((LANES - N_EXPERTS - N_GROUPS,), F32)])[None, :]
    w_hi = w.astype(BF16)
    w_lo = (w - w_hi.astype(F32)).astype(BF16)
    return w_hi, w_lo, bias


def kernel(x, w_in, ret_gn_g, attn_sinks, w_out, ln1_g, ln1_b, w_group_router, b_group_router,
           w_expert_router, b_expert_router, w_gate, w_up, w_down, ln2_g, ln2_b):
    bsz, s_len, d = x.shape
    assert d == D_MODEL and s_len % SEQ_TILE == 0 and w_in.shape[0] == DEPTH == 1
    assert bsz % TOKEN_SPLITS == 0
    t_len = bsz * s_len
    t_split = t_len // TOKEN_SPLITS
    n_blocks = t_len * TOP_K // MOE_ROWS + N_EXPERTS
    n_rows = n_blocks * MOE_ROWS
    part_rows = n_rows // SLOT_PARTS

    consts = _mixer_constants(attn_sinks[0])
    wr_hi, wr_lo, br = _router_tables(w_group_router[0], b_group_router[0],
                                      w_expert_router[0], b_expert_router[0])
    h, h_packed, ids, wts, cnt = _mixer_call(
        x.reshape(t_len, d), _tile_v_columns(w_in[0]).astype(BF16), w_out[0].astype(BF16),
        ret_gn_g[0][None, :], ln1_g[0][None, :], ln1_b[0][None, :], wr_hi, wr_lo, br, consts, bsz, s_len, 0)

    counts = cnt[:, 0].astype(jnp.int32)
    padded = (counts + MOE_ROWS - 1) // MOE_ROWS * MOE_ROWS
    pend = jnp.cumsum(padded)
    pstart = pend - padded
    onehot = ids[0:TOP_K, :, None] == jnp.arange(N_EXPERTS, dtype=jnp.int32)
    dest2 = jnp.sum(jnp.where(onehot, pstart, 0), axis=-1) + ids[TOP_K:2 * TOP_K]
    dest = dest2.reshape(-1)
    nused = (pend[-1:] // MOE_ROWS).astype(jnp.int32)
    blk_start = jnp.minimum(jnp.arange(n_blocks, dtype=jnp.int32), nused[0] - 1) * MOE_ROWS
    bexp = jnp.minimum(jnp.sum(pend[None, :] <= blk_start[:, None], axis=-1), N_EXPERTS - 1).astype(jnp.int32)

    h_tiles = h_packed.reshape(t_len, ROW_SUBLANES, LANES)
    y = None
    for part in range(SLOT_PARTS):
        rows = _sc_dispatch(h_tiles, dest, part * part_rows, part_rows)
        y = _expert_call(bexp, nused, rows.reshape(part_rows * ROW_SUBLANES, LANES),
                         w_gate[0], w_up[0], w_down[0], y, part, n_rows)

    y_tiles = y.reshape(n_rows, ROW_SUBLANES, LANES)
    out = None
    for sp in range(TOKEN_SPLITS):
        idx = dest2[:, sp * t_split:(sp + 1) * t_split].reshape(-1)
        yk = _sc_gather_rows(y_tiles, idx, "moe_combine_sc")
        out = _combine_call(h, yk.reshape(TOP_K * t_split * ROW_SUBLANES, LANES), wts,
                            ln2_g[0][None, :], ln2_b[0][None, :], out, sp, t_len)
    return out.reshape(bsz, s_len, d)
```

```python
import functools

import jax
import jax.numpy as jnp
import numpy as np
from jax import lax
from jax.experimental import pallas as pl
from jax.experimental.pallas import tpu as pltpu
from jax.experimental.pallas import tpu_sc as plsc

F32 = jnp.float32
BF16 = jnp.bfloat16
U32 = jnp.uint32

D_MODEL = 1024
RET_HEADS = 4
RET_HEAD_DIM = 128
RET_WIDTH = RET_HEADS * RET_HEAD_DIM
CHUNK = 128
SWA_HEADS = 8
SWA_KV_HEADS = 2
SWA_GROUP = SWA_HEADS // SWA_KV_HEADS
SWA_HEAD_DIM = 64
SWA_WIDTH = SWA_HEADS * SWA_HEAD_DIM
SWA_KV_WIDTH = SWA_KV_HEADS * SWA_HEAD_DIM
IN_WIDTH = 4 * RET_WIDTH + SWA_WIDTH + 2 * SWA_KV_WIDTH
N_GROUPS = 4
EXPERTS_PER_GROUP = 8
N_EXPERTS = N_GROUPS * EXPERTS_PER_GROUP
TOP_K = 2
D_EXPERT = 512
LN_EPS = 1e-5
GN_EPS = 1e-6
DEPTH = 1
DEEPNORM_ALPHA = (2 * DEPTH) ** 0.25
NEG = -1e30

LANES = 128
ROW_SUBLANES = D_MODEL // 2 // LANES
SEQ_TILE = 512
PROJ_ROWS = 256
MOE_ROWS = 512
COMBINE_TILE = 256
TOKEN_SPLITS = 2
SLOT_PARTS = 2
SC_CORES = 2
SC_SUBCORES = 16
SC_WORKERS = SC_CORES * SC_SUBCORES
SC_LANES = 16
SC_CHUNK_ROWS = 64
VMEM_LIMIT = 56 * 1024 * 1024

_QR, _KR, _VR, _GR = 0, RET_WIDTH, 2 * RET_WIDTH, 3 * RET_WIDTH
_QA = 4 * RET_WIDTH
_KA = _QA + SWA_WIDTH
_VA = _KA + SWA_KV_WIDTH
SWA_VREP = SWA_GROUP * SWA_HEAD_DIM
IN_WIDTH_TILED = _VA + SWA_KV_HEADS * SWA_VREP
ROUTE_ROWS = 8
GROUP_LANE0 = N_EXPERTS


def _dot(a, b):
    return jnp.dot(a, b, preferred_element_type=F32)


def _dot_nt(a, b):
    return lax.dot_general(a, b, (((1,), (1,)), ((), ())), preferred_element_type=F32)


def _dot_tn(a, b):
    return lax.dot_general(a, b, (((0,), (0,)), ((), ())), preferred_element_type=F32)


def _layer_norm(z, g, b):
    mu = jnp.mean(z, axis=-1, keepdims=True)
    zc = z - mu
    var = jnp.mean(zc * zc, axis=-1, keepdims=True)
    return zc * lax.rsqrt(var + LN_EPS) * g + b


def _silu(g):
    return g / (1.0 + jnp.exp(-g))


ROW_BYTES = 4 * ROW_SUBLANES * LANES


def _mixer_cost(t):
    proj = 2 * t * D_MODEL * (IN_WIDTH_TILED + D_MODEL + 3 * LANES)
    retention = RET_HEADS * 4 * 2 * t * CHUNK * RET_HEAD_DIM
    swa = SWA_KV_HEADS * 2 * t * SWA_GROUP * 2 * CHUNK * (SWA_HEAD_DIM + 2 * SWA_VREP)
    weights = 2 * D_MODEL * (IN_WIDTH_TILED + D_MODEL + 2 * LANES)
    return pl.CostEstimate(flops=proj + retention + swa + 2 * t * SEQ_TILE * N_EXPERTS,
                           transcendentals=t * (SWA_HEADS * 2 * CHUNK + RET_WIDTH + LANES),
                           bytes_accessed=(2 * 4 * D_MODEL + ROW_BYTES) * t + weights)


def _expert_cost(n_rows):
    return pl.CostEstimate(flops=2 * 3 * n_rows * D_MODEL * D_EXPERT, transcendentals=n_rows * D_EXPERT,
                           bytes_accessed=2 * ROW_BYTES * n_rows + 4 * 3 * N_EXPERTS * D_MODEL * D_EXPERT)


def _row_move_cost(n_rows):
    return pl.CostEstimate(flops=0, transcendentals=0, bytes_accessed=2 * ROW_BYTES * n_rows)


def _pack_rows(ref, val):
    n = val.shape[0]
    half = D_MODEL // 2
    hi = lax.bitcast_convert_type(val[:, :half].astype(BF16).astype(F32), U32)
    lo = lax.bitcast_convert_type(val[:, half:].astype(BF16).astype(F32), U32)
    word = hi | (lo >> 16)
    for j in range(ROW_SUBLANES):
        ref[pl.ds(j, n, stride=ROW_SUBLANES), :] = word[:, j * LANES:(j + 1) * LANES]


def _unpack_rows(ref):
    n = ref.shape[0] // ROW_SUBLANES
    word = jnp.concatenate([ref[pl.ds(j, n, stride=ROW_SUBLANES), :] for j in range(ROW_SUBLANES)], axis=1)
    hi = lax.bitcast_convert_type(word & jnp.uint32(0xFFFF0000), F32)
    lo = lax.bitcast_convert_type(word << 16, F32)
    return jnp.concatenate([hi, lo], axis=1)


def _mixer_kernel(cdec_ref, x_ref, w_in_ref, w_out_ref, gn_ref, ln_g_ref, ln_b_ref,
                  wr_cat_ref, br_ref, din_ref, qdec_ref, kdec_ref,
                  bias_ref, pen_ref, kmask_ref, vmask_ref, hlane_ref, tri_ref,
                  h_ref, hp_ref, ids_ref, wts_ref, cnt_ref,
                  state_scr, kprev_scr, vprev_scr, o_scr, carry_scr):
    b = pl.program_id(0)
    n = pl.program_id(1)
    ts = x_ref.shape[0]

    @pl.when(n == 0)
    def _():
        state_scr[...] = jnp.zeros_like(state_scr)
        kprev_scr[...] = jnp.zeros_like(kprev_scr)
        vprev_scr[...] = jnp.zeros_like(vprev_scr)

    @pl.when((b == 0) & (n == 0))
    def _():
        carry_scr[...] = jnp.zeros_like(carry_scr)

    x = x_ref[...]
    xb = x.astype(BF16)

    chunks_per_part = PROJ_ROWS // CHUNK
    parts = []

    def project_part(part):
        xp = xb[part * PROJ_ROWS:(part + 1) * PROJ_ROWS]
        proj = lambda lo, hi: _dot(xp, w_in_ref[:, lo:hi])
        parts.append(dict(
            q_r=proj(_QR, _KR), k_r=proj(_KR, _VR), v_r=proj(_VR, _GR), g_r=proj(_GR, _QA),
            q_a=proj(_QA, _KA), k_ab=proj(_KA, _VA).astype(BF16),
            v_rep=proj(_VA, _VA + SWA_KV_HEADS * SWA_VREP).astype(BF16)))

    project_part(0)

    def chunk_of(name, c):
        lo = (c % chunks_per_part) * CHUNK
        return parts[c // chunks_per_part][name][lo:lo + CHUNK]
    first_pen = jnp.where(n == 0, pen_ref[...], 0.0)

    for c in range(ts // CHUNK):
        rs = slice(c * CHUNK, (c + 1) * CHUNK)
        q_r, k_r, v_r, g_r = (chunk_of(name, c) for name in ("q_r", "k_r", "v_r", "g_r"))
        q_a, k_ab, v_rep = (chunk_of(name, c) for name in ("q_a", "k_ab", "v_rep"))
        for hd in range(RET_HEADS):
            cs = slice(hd * RET_HEAD_DIM, (hd + 1) * RET_HEAD_DIM)
            q = q_r[:, cs]
            k = k_r[:, cs]
            v = v_r[:, cs].astype(BF16)
            scores = _dot_nt(q.astype(BF16), k.astype(BF16)) * din_ref[hd]
            st = state_scr[hd]
            o = _dot(jnp.concatenate([scores.astype(BF16), (q * qdec_ref[hd]).astype(BF16)], axis=1),
                     jnp.concatenate([v, st.astype(BF16)], axis=0))
            kv = _dot_tn((k * kdec_ref[hd]).astype(BF16), v)
            state_scr[hd] = st * cdec_ref[hd] + kv
            mu = jnp.mean(o, axis=-1, keepdims=True)
            oc = o - mu
            var = jnp.mean(oc * oc, axis=-1, keepdims=True)
            on = oc * lax.rsqrt(var + GN_EPS) * gn_ref[:, cs] * _silu(g_r[:, cs])
            o_scr[rs, cs] = on.astype(BF16)
        for j in range(SWA_KV_HEADS):
            ks = slice(j * SWA_HEAD_DIM, (j + 1) * SWA_HEAD_DIM)
            vs = slice(j * SWA_VREP, (j + 1) * SWA_VREP)
            if c == 0:
                kp = kprev_scr[:, ks].astype(BF16)
                vp = vprev_scr[:, vs].astype(BF16)
            else:
                kp = chunk_of("k_ab", c - 1)[:, ks]
                vp = chunk_of("v_rep", c - 1)[:, vs]
            kp = kp * kmask_ref[...]
            kcat = jnp.concatenate([kp, k_ab[:, ks]], axis=0)
            vcat = jnp.concatenate([vp, v_rep[:, vs]], axis=0)
            q0 = j * SWA_GROUP * SWA_HEAD_DIM
            qs = jnp.concatenate(
                [q_a[:, q0 + g * SWA_HEAD_DIM:q0 + (g + 1) * SWA_HEAD_DIM] for g in range(SWA_GROUP)],
                axis=0)
            qs = (qs * (SWA_HEAD_DIM ** -0.5)).astype(BF16)
            s = _dot_nt(qs, kcat) + bias_ref[j]
            if c == 0:
                s = s + first_pen
            m = jnp.max(jnp.maximum(s[:, :CHUNK], s[:, CHUNK:]), axis=-1, keepdims=True)
            p32 = jnp.exp(s - m)
            den_col = jnp.sum(p32[:, :CHUNK] + p32[:, CHUNK:], axis=-1, keepdims=True)
            p = p32.astype(BF16)
            p_all = jnp.concatenate([p[g * CHUNK:(g + 1) * CHUNK] for g in range(SWA_GROUP)], axis=1)
            v_blk = jnp.concatenate([vcat * vmask_ref[g] for g in range(SWA_GROUP)], axis=0)
            num = _dot(p_all, v_blk)
            den = den_col[(SWA_GROUP - 1) * CHUNK:]
            for g in range(SWA_GROUP - 2, -1, -1):
                den = jnp.where(hlane_ref[...] == g, den_col[g * CHUNK:(g + 1) * CHUNK], den)
            c0 = RET_WIDTH + j * SWA_GROUP * SWA_HEAD_DIM
            o_scr[rs, c0:c0 + SWA_GROUP * SWA_HEAD_DIM] = (num / den).astype(BF16)
        if c % chunks_per_part == 0 and len(parts) < ts // PROJ_ROWS:
            project_part(len(parts))

    kprev_scr[...] = chunk_of("k_ab", ts // CHUNK - 1).astype(F32)
    vprev_scr[...] = chunk_of("v_rep", ts // CHUNK - 1).astype(F32)

    mix = _dot(o_scr[...], w_out_ref[...])
    h = _layer_norm(DEEPNORM_ALPHA * x + mix, ln_g_ref[...], ln_b_ref[...])
    h_ref[...] = h
    _pack_rows(hp_ref, h)

    h_hi = h.astype(BF16)
    h_lo = (h - h_hi.astype(F32)).astype(BF16)
    hi_terms = _dot(h_hi, wr_cat_ref[...])
    logits = hi_terms[:, :LANES] + hi_terms[:, LANES:] + _dot(h_lo, wr_cat_ref[:, :LANES]) + br_ref[...]
    lt = logits.T
    row = lax.broadcasted_iota(jnp.int32, (EXPERTS_PER_GROUP, ts), 0).astype(F32)
    big = 1e9
    ninf = -jnp.inf
    col_max = lambda v: jnp.max(v, axis=0, keepdims=True)
    first_at = lambda v, m: jnp.min(jnp.where(v == m, row, big), axis=0, keepdims=True)
    gl = jnp.where(row < N_GROUPS, lt[GROUP_LANE0:GROUP_LANE0 + EXPERTS_PER_GROUP], ninf)
    gmax = col_max(gl)
    gidx = first_at(gl, gmax)
    g_w = 1.0 / jnp.sum(jnp.exp(gl - gmax), axis=0, keepdims=True)
    el = lt[(N_GROUPS - 1) * EXPERTS_PER_GROUP:N_GROUPS * EXPERTS_PER_GROUP]
    for g in range(N_GROUPS - 2, -1, -1):
        el = jnp.where(gidx == g, lt[g * EXPERTS_PER_GROUP:(g + 1) * EXPERTS_PER_GROUP], el)
    m1 = col_max(el)
    i1 = first_at(el, m1)
    el2 = jnp.where(row == i1, ninf, el)
    m2 = col_max(el2)
    i2 = first_at(el2, m2)
    t = jnp.exp(m2 - m1)
    w1 = g_w / (1.0 + t)
    w2 = g_w * t / (1.0 + t)
    e1 = gidx * EXPERTS_PER_GROUP + i1
    e2 = gidx * EXPERTS_PER_GROUP + i2
    erow = lax.broadcasted_iota(jnp.int32, (N_EXPERTS, ts), 0).astype(F32)
    hit1 = erow == e1
    hit2 = erow == e2
    onehot = (hit1 | hit2).astype(BF16)
    prefix = _dot(onehot, tri_ref[...]) + carry_scr[:, 0:1]
    r1 = jnp.sum(jnp.where(hit1, prefix, 0.0), axis=0, keepdims=True)
    r2 = jnp.sum(jnp.where(hit2, prefix, 0.0), axis=0, keepdims=True)
    carry = carry_scr[:, 0:1] + jnp.sum(onehot.astype(F32), axis=1, keepdims=True)
    carry_scr[...] = jnp.broadcast_to(carry, carry_scr.shape)
    cnt_ref[...] = jnp.broadcast_to(carry, cnt_ref.shape)
    pick = lambda k, v, rest: jnp.where(row == k, v, rest)
    ids_ref[...] = pick(0, e1, pick(1, e2, pick(2, r1, pick(3, r2, 0.0)))).astype(jnp.int32)
    wts_ref[...] = pick(0, w1, pick(1, w2, 0.0))


def _mixer_call(x2, w_in, w_out, gn, ln_g, ln_b, wr_cat, br, consts, bsz, s_len, first_seq):
    cdec, din, qdec, kdec, bias, pen, kmask, vmask, hlane, tri = consts
    t_len = bsz * s_len
    ns = s_len // SEQ_TILE
    tok = lambda b, n, *_: (b * ns + n, 0)
    tok_in = lambda b, n, *_: ((first_seq + b) * ns + n, 0)
    tok_t = lambda b, n, *_: (0, b * ns + n)
    full2 = lambda b, n, *_: (0, 0)
    full3 = lambda b, n, *_: (0, 0, 0)
    grid_spec = pltpu.PrefetchScalarGridSpec(
        num_scalar_prefetch=1,
        grid=(bsz, ns),
        in_specs=[
            pl.BlockSpec((SEQ_TILE, D_MODEL), tok_in),
            pl.BlockSpec((D_MODEL, IN_WIDTH_TILED), full2),
            pl.BlockSpec((D_MODEL, D_MODEL), full2),
            pl.BlockSpec((1, RET_WIDTH), full2),
            pl.BlockSpec((1, D_MODEL), full2),
            pl.BlockSpec((1, D_MODEL), full2),
            pl.BlockSpec((D_MODEL, 2 * LANES), full2),
            pl.BlockSpec((1, LANES), full2),
            pl.BlockSpec((RET_HEADS, CHUNK, CHUNK), full3),
            pl.BlockSpec((RET_HEADS, CHUNK, RET_HEAD_DIM), full3),
            pl.BlockSpec((RET_HEADS, CHUNK, RET_HEAD_DIM), full3),
            pl.BlockSpec((SWA_KV_HEADS, SWA_GROUP * CHUNK, 2 * CHUNK), full3),
            pl.BlockSpec((1, 2 * CHUNK), full2),
            pl.BlockSpec((CHUNK, SWA_HEAD_DIM), full2),
            pl.BlockSpec((SWA_GROUP, 2 * CHUNK, SWA_VREP), full3),
            pl.BlockSpec((1, SWA_VREP), full2),
            pl.BlockSpec((SEQ_TILE, SEQ_TILE), full2),
        ],
        out_specs=[
            pl.BlockSpec((SEQ_TILE, D_MODEL), tok),
            pl.BlockSpec((SEQ_TILE * ROW_SUBLANES, LANES), tok),
            pl.BlockSpec((ROUTE_ROWS, SEQ_TILE), tok_t),
            pl.BlockSpec((ROUTE_ROWS, SEQ_TILE), tok_t),
            pl.BlockSpec((N_EXPERTS, LANES), full2),
        ],
        scratch_shapes=[
            pltpu.VMEM((RET_HEADS, RET_HEAD_DIM, RET_HEAD_DIM), F32),
            pltpu.VMEM((CHUNK, SWA_KV_WIDTH), F32),
            pltpu.VMEM((CHUNK, SWA_KV_HEADS * SWA_VREP), F32),
            pltpu.VMEM((SEQ_TILE, D_MODEL), BF16),
            pltpu.VMEM((N_EXPERTS, LANES), F32),
        ],
    )
    return pl.pallas_call(
        _mixer_kernel,
        grid_spec=grid_spec,
        out_shape=[
            jax.ShapeDtypeStruct((t_len, D_MODEL), F32),
            jax.ShapeDtypeStruct((t_len * ROW_SUBLANES, LANES), U32),
            jax.ShapeDtypeStruct((ROUTE_ROWS, t_len), jnp.int32),
            jax.ShapeDtypeStruct((ROUTE_ROWS, t_len), F32),
            jax.ShapeDtypeStruct((N_EXPERTS, LANES), F32),
        ],
        compiler_params=pltpu.CompilerParams(
            dimension_semantics=("arbitrary", "arbitrary"), vmem_limit_bytes=VMEM_LIMIT),
        name="mixer_router",
        cost_estimate=_mixer_cost(t_len),
    )(cdec, x2, w_in, w_out, gn, ln_g, ln_b, wr_cat, br, din, qdec, kdec, bias, pen, kmask, vmask, hlane, tri)


def _sc_mesh():
    return plsc.VectorSubcoreMesh(core_axis_name="core", subcore_axis_name="subcore")


def _sc_dispatch(h_rows, dest, slot0, n_slots):
    t_len = h_rows.shape[0]
    n_assign = dest.shape[0]
    per_worker = n_slots // SC_WORKERS
    n_chunks = per_worker // SC_CHUNK_ROWS
    assert n_chunks * SC_CHUNK_ROWS * SC_WORKERS == n_slots and slot0 + n_slots <= 3 * t_len
    assert per_worker % SC_LANES == 0 and n_assign % SC_LANES == 0 and n_assign == TOP_K * t_len

    @functools.partial(
        pl.kernel, mesh=_sc_mesh(), name="moe_dispatch_sc", cost_estimate=_row_move_cost(n_slots),
        compiler_params=pltpu.CompilerParams(needs_layout_passes=False),
        out_type=jax.ShapeDtypeStruct((n_slots, ROW_SUBLANES, LANES), U32),
        scratch_types=[pltpu.VMEM((n_assign,), jnp.int32),
                       pltpu.VMEM((per_worker,), jnp.int32),
                       pltpu.VMEM((SC_CHUNK_ROWS, ROW_SUBLANES, LANES), U32),
                       pltpu.SemaphoreType.DMA])
    def dispatch(h_hbm, dest_hbm, rows_hbm, dest_v, src_v, buf, sem):
        wid = lax.axis_index("subcore") * SC_CORES + lax.axis_index("core")
        local = wid * per_worker
        base = slot0 + local
        pltpu.sync_copy(dest_hbm, dest_v)
        lane = lax.iota(jnp.int32, SC_LANES)

        def wrap(a):
            a = jnp.where(a >= t_len, a - t_len, a)
            return jnp.where(a >= t_len, a - t_len, a)

        @pl.loop(0, per_worker // SC_LANES)
        def _(i):
            src_v[pl.ds(i * SC_LANES, SC_LANES)] = wrap(base + i * SC_LANES + lane)

        @pl.loop(0, n_assign // SC_LANES)
        def _(i):
            d = dest_v[pl.ds(i * SC_LANES, SC_LANES)] - base
            hit = (d >= 0) & (d < per_worker)
            plsc.store_scatter(src_v, [jnp.where(hit, d, 0)], wrap(i * SC_LANES + lane), mask=hit)

        @pl.loop(0, n_chunks)
        def _(c):
            off = pl.multiple_of(c * SC_CHUNK_ROWS, SC_CHUNK_ROWS)
            pltpu.async_copy(h_hbm.at[src_v.at[pl.ds(off, SC_CHUNK_ROWS)]], buf, sem).wait()
            pltpu.sync_copy(buf, rows_hbm.at[pl.ds(local + off, SC_CHUNK_ROWS)])

    return dispatch(h_rows, dest)


def _sc_gather_rows(table, idx, name):
    m = idx.shape[0]
    per_worker = m // (SC_CHUNK_ROWS * SC_WORKERS)
    assert per_worker * SC_CHUNK_ROWS * SC_WORKERS == m

    @functools.partial(
        pl.kernel, mesh=_sc_mesh(), name=name, cost_estimate=_row_move_cost(m),
        out_type=jax.ShapeDtypeStruct((m, ROW_SUBLANES, LANES), U32),
        scratch_types=[pltpu.VMEM((m,), jnp.int32),
                       pltpu.VMEM((SC_CHUNK_ROWS, ROW_SUBLANES, LANES), U32),
                       pltpu.SemaphoreType.DMA])
    def gather(table_hbm, idx_hbm, out_hbm, idx_v, buf, sem):
        wid = lax.axis_index("subcore") * SC_CORES + lax.axis_index("core")
        pltpu.sync_copy(idx_hbm, idx_v)

        @pl.loop(0, per_worker)
        def _(j):
            off = pl.multiple_of((j * SC_WORKERS + wid) * SC_CHUNK_ROWS, SC_CHUNK_ROWS)
            pltpu.async_copy(table_hbm.at[idx_v.at[pl.ds(off, SC_CHUNK_ROWS)]], buf, sem).wait()
            pltpu.sync_copy(buf, out_hbm.at[pl.ds(off, SC_CHUNK_ROWS)])

    return gather(table, idx)


def _expert_kernel(bexp_ref, nused_ref, rows_ref, wg_ref, wu_ref, wd_ref, *rest, first_block):
    y_ref, wg_s, wu_s, wd_s = rest[-4:]
    p = pl.program_id(0)
    blk = first_block + p
    prev = bexp_ref[jnp.maximum(blk - 1, 0)]

    @pl.when((p == 0) | (bexp_ref[blk] != prev))
    def _():
        wg_s[...] = wg_ref[0].astype(BF16)
        wu_s[...] = wu_ref[0].astype(BF16)
        wd_s[...] = wd_ref[0].astype(BF16)

    @pl.when(blk < nused_ref[0])
    def _():
        xb = _unpack_rows(rows_ref).astype(BF16)
        g = _dot(xb, wg_s[...])
        u = _dot(xb, wu_s[...])
        a = (_silu(g) * u).astype(BF16)
        _pack_rows(y_ref, _dot(a, wd_s[...]))

    @pl.when(blk >= nused_ref[0])
    def _():
        y_ref[...] = jnp.zeros_like(y_ref)


def _expert_call(bexp, nused, rows, w_gate, w_up, w_down, y_prev, part, n_rows):
    part_blocks = rows.shape[0] // (ROW_SUBLANES * MOE_ROWS)
    first_block = part * part_blocks
    last_used = lambda p, nu: jnp.clip(first_block + p, first_block, jnp.maximum(nu[0] - 1, first_block))
    blk = lambda p, be, nu: (last_used(p, nu) - first_block, 0)
    wsel = lambda p, be, nu: (be[last_used(p, nu)], 0, 0)
    in_specs = [
        pl.BlockSpec((MOE_ROWS * ROW_SUBLANES, LANES), blk),
        pl.BlockSpec((1, D_MODEL, D_EXPERT), wsel),
        pl.BlockSpec((1, D_MODEL, D_EXPERT), wsel),
        pl.BlockSpec((1, D_EXPERT, D_MODEL), wsel),
    ]
    args = [rows, w_gate, w_up, w_down]
    aliases = {}
    if y_prev is not None:
        in_specs.append(pl.BlockSpec(memory_space=pl.ANY))
        args.append(y_prev)
        aliases = {2 + len(args) - 1: 0}
    grid_spec = pltpu.PrefetchScalarGridSpec(
        num_scalar_prefetch=2,
        grid=(part_blocks,),
        in_specs=in_specs,
        out_specs=pl.BlockSpec((MOE_ROWS * ROW_SUBLANES, LANES), lambda p, be, nu: (first_block + p, 0)),
        scratch_shapes=[
            pltpu.VMEM((D_MODEL, D_EXPERT), BF16),
            pltpu.VMEM((D_MODEL, D_EXPERT), BF16),
            pltpu.VMEM((D_EXPERT, D_MODEL), BF16),
        ],
    )
    return pl.pallas_call(
        functools.partial(_expert_kernel, first_block=first_block),
        grid_spec=grid_spec,
        out_shape=jax.ShapeDtypeStruct((n_rows * ROW_SUBLANES, LANES), U32),
        input_output_aliases=aliases,
        compiler_params=pltpu.CompilerParams(
            dimension_semantics=("arbitrary",), vmem_limit_bytes=VMEM_LIMIT),
        name="moe_experts",
        cost_estimate=_expert_cost(part_blocks * MOE_ROWS),
    )(bexp, nused, *args)


def _combine_kernel(h_ref, y0_ref, y1_ref, wts_ref, ln_g_ref, ln_b_ref, *rest):
    out_ref = rest[-1]
    wts = wts_ref[...].T
    ffn = _unpack_rows(y0_ref) * wts[:, 0:1] + _unpack_rows(y1_ref) * wts[:, 1:2]
    out_ref[...] = _layer_norm(DEEPNORM_ALPHA * h_ref[...] + ffn, ln_g_ref[...], ln_b_ref[...])


def _combine_call(h, yk, wts, ln_g, ln_b, out_prev, split, t_total):
    t_len = t_total // TOKEN_SPLITS
    n_tiles = t_len // COMBINE_TILE
    tok = lambda i: (i, 0)
    tok_split = lambda i: (i + split * n_tiles, 0)
    full2 = lambda i: (0, 0)
    tiles = pl.BlockSpec((COMBINE_TILE * ROW_SUBLANES, LANES), tok)
    tiles_k1 = pl.BlockSpec((COMBINE_TILE * ROW_SUBLANES, LANES), lambda i: (i + n_tiles, 0))
    in_specs = [pl.BlockSpec((COMBINE_TILE, D_MODEL), tok_split), tiles, tiles_k1,
                pl.BlockSpec((ROUTE_ROWS, COMBINE_TILE), lambda i: (0, i + split * n_tiles)),
                pl.BlockSpec((1, D_MODEL), full2),
                pl.BlockSpec((1, D_MODEL), full2)]
    args = [h, yk, yk, wts, ln_g, ln_b]
    aliases = {}
    if out_prev is not None:
        in_specs.append(pl.BlockSpec(memory_space=pl.ANY))
        args.append(out_prev)
        aliases = {len(args) - 1: 0}
    return pl.pallas_call(
        _combine_kernel,
        grid=(n_tiles,),
        in_specs=in_specs,
        out_specs=pl.BlockSpec((COMBINE_TILE, D_MODEL), tok_split),
        out_shape=jax.ShapeDtypeStruct((t_total, D_MODEL), F32),
        input_output_aliases=aliases,
        compiler_params=pltpu.CompilerParams(
            dimension_semantics=("arbitrary",), vmem_limit_bytes=VMEM_LIMIT),
        name="moe_combine_ln",
        cost_estimate=pl.CostEstimate(flops=8 * t_len * D_MODEL, transcendentals=t_len,
                                      bytes_accessed=(2 * 4 * D_MODEL + TOP_K * ROW_BYTES) * t_len),
    )(*args)


def _position_tables():
    c = CHUNK
    f = np.float32
    log_g = np.log1p(-np.exp2(-5.0 - np.arange(RET_HEADS, dtype=f))).astype(f)
    idx = np.arange(c, dtype=f)
    diff = idx[:, None] - idx[None, :]
    scale = f(RET_HEAD_DIM ** -0.5)
    din = np.where(diff >= 0, np.exp(log_g[:, None, None] * np.maximum(diff, 0.0)), 0.0).astype(f) * scale
    qdec = np.broadcast_to(np.exp(log_g[:, None] * (idx + 1.0))[:, :, None], (RET_HEADS, c, RET_HEAD_DIM))
    kdec = np.broadcast_to((np.exp(log_g[:, None] * (c - 1.0 - idx)) * scale)[:, :, None],
                           (RET_HEADS, c, RET_HEAD_DIM))
    cdec = np.exp(log_g * c)
    slopes = np.exp2(-8.0 * (np.arange(SWA_HEADS, dtype=f) + 1.0) / SWA_HEADS).astype(f)
    r = np.arange(c)[:, None]
    col = np.arange(c)[None, :]
    dist_prev = (r - col + c).astype(f)
    dist_cur = (r - col).astype(f)
    bprev = np.where((r < col)[None], -slopes[:, None, None] * dist_prev[None], NEG)
    bcur = np.where((r >= col)[None], -slopes[:, None, None] * dist_cur[None], NEG)
    bias = np.concatenate([bprev, bcur], axis=-1).reshape(SWA_KV_HEADS, SWA_GROUP * c, 2 * c)
    key = np.arange(2 * c)
    pen = np.where((key >= 1) & (key < c), NEG, 0.0)[None, :]
    lane_head = np.arange(SWA_VREP) // SWA_HEAD_DIM
    own = lane_head[None, None, :] == np.arange(SWA_GROUP)[:, None, None]
    kmask = np.broadcast_to(np.arange(c)[:, None] > 0, (c, SWA_HEAD_DIM))
    vmask = own & (key[None, :, None] > 0)
    hlane = lane_head[None, :]
    tr = np.arange(SEQ_TILE)
    tri = tr[:, None] < tr[None, :]
    as_f32 = lambda v: np.ascontiguousarray(v, dtype=f)
    as_bf16 = lambda v: np.ascontiguousarray(v, dtype=f).astype(BF16)
    return (as_f32(cdec), as_f32(din), as_f32(qdec), as_f32(kdec), as_f32(bias), as_f32(pen),
            as_bf16(kmask), as_bf16(vmask), as_f32(hlane), as_bf16(tri))


def _mixer_constants(attn_sinks):
    cdec, din, qdec, kdec, bias, pen, kmask, vmask, hlane, tri = _position_tables()
    sink = attn_sinks.astype(F32).reshape(SWA_KV_HEADS, SWA_GROUP, 1, 1)
    sink = jnp.broadcast_to(sink, (SWA_KV_HEADS, SWA_GROUP, CHUNK, 1)).reshape(SWA_KV_HEADS, SWA_GROUP * CHUNK, 1)
    is_slot = (np.arange(2 * CHUNK) == 0)[None, None, :]
    return cdec, din, qdec, kdec, jnp.where(is_slot, sink, bias), pen, kmask, vmask, hlane, tri


def _tile_v_columns(w_in):
    v_cols = w_in[:, _VA:].reshape(D_MODEL, SWA_KV_HEADS, 1, SWA_HEAD_DIM)
    v_cols = jnp.broadcast_to(v_cols, (D_MODEL, SWA_KV_HEADS, SWA_GROUP, SWA_HEAD_DIM))
    return jnp.concatenate([w_in[:, :_VA], v_cols.reshape(D_MODEL, SWA_KV_HEADS * SWA_VREP)], axis=1)


def _router_tables(w_group_router, b_group_router, w_expert_router, b_expert_router):
    w_e = jnp.transpose(w_expert_router, (1, 0, 2)).reshape(D_MODEL, N_EXPERTS)
    w = jnp.concatenate([w_e, w_group_router,
                         jnp.zeros((D_MODEL, LANES - N_EXPERTS - N_GROUPS), F32)], axis=1)
    bias = jnp.concatenate([b_expert_router.reshape(N_EXPERTS), b_group_router,
                            jnp.zeros((LANES - N_EXPERTS - N_GROUPS,), F32)])[None, :]
    w_hi = w.astype(BF16)
    w_lo = (w - w_hi.astype(F32)).astype(BF16)
    return jnp.concatenate([w_hi, w_lo], axis=1), bias


def kernel(x, w_in, ret_gn_g, attn_sinks, w_out, ln1_g, ln1_b, w_group_router, b_group_router,
           w_expert_router, b_expert_router, w_gate, w_up, w_down, ln2_g, ln2_b):
    bsz, s_len, d = x.shape
    assert d == D_MODEL and s_len % SEQ_TILE == 0 and w_in.shape[0] == DEPTH == 1
    assert bsz % TOKEN_SPLITS == 0
    t_len = bsz * s_len
    t_split = t_len // TOKEN_SPLITS
    n_blocks = t_len * TOP_K // MOE_ROWS + N_EXPERTS
    n_rows = n_blocks * MOE_ROWS
    part_rows = n_rows // SLOT_PARTS

    consts = _mixer_constants(attn_sinks[0])
    wr_cat, br = _router_tables(w_group_router[0], b_group_router[0],
                                      w_expert_router[0], b_expert_router[0])
    h, h_packed, ids, wts, cnt = _mixer_call(
        x.reshape(t_len, d), _tile_v_columns(w_in[0]).astype(BF16), w_out[0].astype(BF16),
        ret_gn_g[0][None, :], ln1_g[0][None, :], ln1_b[0][None, :], wr_cat, br, consts, bsz, s_len, 0)

    counts = cnt[:, 0].astype(jnp.int32)
    padded = (counts + MOE_ROWS - 1) // MOE_ROWS * MOE_ROWS
    pend = jnp.cumsum(padded)
    pstart = pend - padded
    onehot = ids[0:TOP_K, :, None] == jnp.arange(N_EXPERTS, dtype=jnp.int32)
    dest2 = jnp.sum(jnp.where(onehot, pstart, 0), axis=-1) + ids[TOP_K:2 * TOP_K]
    dest = dest2.reshape(-1)
    nused = (pend[-1:] // MOE_ROWS).astype(jnp.int32)
    blk_start = jnp.minimum(jnp.arange(n_blocks, dtype=jnp.int32), nused[0] - 1) * MOE_ROWS
    bexp = jnp.minimum(jnp.sum(pend[None, :] <= blk_start[:, None], axis=-1), N_EXPERTS - 1).astype(jnp.int32)

    h_tiles = h_packed.reshape(t_len, ROW_SUBLANES, LANES)
    y = None
    for part in range(SLOT_PARTS):
        rows = _sc_dispatch(h_tiles, dest, part * part_rows, part_rows)
        y = _expert_call(bexp, nused, rows.reshape(part_rows * ROW_SUBLANES, LANES),
                         w_gate[0], w_up[0], w_down[0], y, part, n_rows)

    y_tiles = y.reshape(n_rows, ROW_SUBLANES, LANES)
    out = None
    for sp in range(TOKEN_SPLITS):
        idx = dest2[:, sp * t_split:(sp + 1) * t_split].reshape(-1)
        yk = _sc_gather_rows(y_tiles, idx, "moe_combine_sc")
        out = _combine_call(h, yk.reshape(TOP_K * t_split * ROW_SUBLANES, LANES), wts,
                            ln2_g[0][None, :], ln2_b[0][None, :], out, sp, t_len)
    return out.reshape(bsz, s_len, d)
```

```python
import functools

import jax
import jax.numpy as jnp
import numpy as np
from jax import lax
from jax.experimental import pallas as pl
from jax.experimental.pallas import tpu as pltpu
from jax.experimental.pallas import tpu_sc as plsc

F32 = jnp.float32
BF16 = jnp.bfloat16
U32 = jnp.uint32

D_MODEL = 1024
RET_HEADS = 4
RET_HEAD_DIM = 128
RET_WIDTH = RET_HEADS * RET_HEAD_DIM
CHUNK = 128
SWA_HEADS = 8
SWA_KV_HEADS = 2
SWA_GROUP = SWA_HEADS // SWA_KV_HEADS
SWA_HEAD_DIM = 64
SWA_WIDTH = SWA_HEADS * SWA_HEAD_DIM
SWA_KV_WIDTH = SWA_KV_HEADS * SWA_HEAD_DIM
IN_WIDTH = 4 * RET_WIDTH + SWA_WIDTH + 2 * SWA_KV_WIDTH
N_GROUPS = 4
EXPERTS_PER_GROUP = 8
N_EXPERTS = N_GROUPS * EXPERTS_PER_GROUP
TOP_K = 2
D_EXPERT = 512
LN_EPS = 1e-5
GN_EPS = 1e-6
DEPTH = 1
DEEPNORM_ALPHA = (2 * DEPTH) ** 0.25
NEG = -1e30

LANES = 128
ROW_SUBLANES = D_MODEL // 2 // LANES
SEQ_TILE = 512
PROJ_ROWS = 256
MOE_ROWS = 512
COMBINE_TILE = 256
TOKEN_SPLITS = 2
SLOT_PARTS = 2
SC_CORES = 2
SC_SUBCORES = 16
SC_WORKERS = SC_CORES * SC_SUBCORES
SC_LANES = 16
SC_CHUNK_ROWS = 64
VMEM_LIMIT = 56 * 1024 * 1024

_QR, _KR, _VR, _GR = 0, RET_WIDTH, 2 * RET_WIDTH, 3 * RET_WIDTH
_QA = 4 * RET_WIDTH
_KA = _QA + SWA_WIDTH
_VA = _KA + SWA_KV_WIDTH
SWA_VREP = SWA_GROUP * SWA_HEAD_DIM
IN_WIDTH_TILED = _VA + SWA_KV_HEADS * SWA_VREP
ROUTE_ROWS = 8
GROUP_LANE0 = N_EXPERTS


def _dot(a, b):
    return jnp.dot(a, b, preferred_element_type=F32)


def _dot_nt(a, b):
    return lax.dot_general(a, b, (((1,), (1,)), ((), ())), preferred_element_type=F32)


def _dot_tn(a, b):
    return lax.dot_general(a, b, (((0,), (0,)), ((), ())), preferred_element_type=F32)


def _layer_norm(z, g, b):
    mu = jnp.mean(z, axis=-1, keepdims=True)
    zc = z - mu
    var = jnp.mean(zc * zc, axis=-1, keepdims=True)
    return zc * lax.rsqrt(var + LN_EPS) * g + b


def _silu(g):
    return g / (1.0 + jnp.exp(-g))


ROW_BYTES = 4 * ROW_SUBLANES * LANES


def _mixer_cost(t):
    proj = 2 * t * D_MODEL * (IN_WIDTH_TILED + D_MODEL + 3 * LANES)
    retention = RET_HEADS * 4 * 2 * t * CHUNK * RET_HEAD_DIM
    swa = SWA_KV_HEADS * 2 * t * SWA_GROUP * 2 * CHUNK * (SWA_HEAD_DIM + 2 * SWA_VREP)
    weights = 2 * D_MODEL * (IN_WIDTH_TILED + D_MODEL + 2 * LANES)
    return pl.CostEstimate(flops=proj + retention + swa + 2 * t * SEQ_TILE * N_EXPERTS,
                           transcendentals=t * (SWA_HEADS * 2 * CHUNK + RET_WIDTH + LANES),
                           bytes_accessed=(2 * 4 * D_MODEL + ROW_BYTES) * t + weights)


def _expert_cost(n_rows):
    return pl.CostEstimate(flops=2 * 3 * n_rows * D_MODEL * D_EXPERT, transcendentals=n_rows * D_EXPERT,
                           bytes_accessed=2 * ROW_BYTES * n_rows + 4 * 3 * N_EXPERTS * D_MODEL * D_EXPERT)


def _row_move_cost(n_rows):
    return pl.CostEstimate(flops=0, transcendentals=0, bytes_accessed=2 * ROW_BYTES * n_rows)


def _pack_rows(ref, val):
    n = val.shape[0]
    half = D_MODEL // 2
    hi = lax.bitcast_convert_type(val[:, :half].astype(BF16).astype(F32), U32)
    lo = lax.bitcast_convert_type(val[:, half:].astype(BF16).astype(F32), U32)
    word = hi | (lo >> 16)
    for j in range(ROW_SUBLANES):
        ref[pl.ds(j, n, stride=ROW_SUBLANES), :] = word[:, j * LANES:(j + 1) * LANES]


def _unpack_rows(ref):
    n = ref.shape[0] // ROW_SUBLANES
    word = jnp.concatenate([ref[pl.ds(j, n, stride=ROW_SUBLANES), :] for j in range(ROW_SUBLANES)], axis=1)
    hi = lax.bitcast_convert_type(word & jnp.uint32(0xFFFF0000), F32)
    lo = lax.bitcast_convert_type(word << 16, F32)
    return jnp.concatenate([hi, lo], axis=1)


def _mixer_kernel(cdec_ref, x_ref, w_in_ref, w_out_ref, gn_ref, ln_g_ref, ln_b_ref,
                  wr_cat_ref, br_ref, din_ref, qdec_ref, kdec_ref,
                  bias_ref, pen_ref, kmask_ref, vmask_ref, hlane_ref, tri_ref,
                  h_ref, hp_ref, ids_ref, wts_ref, cnt_ref,
                  state_scr, kprev_scr, vprev_scr, o_scr, carry_scr):
    b = pl.program_id(0)
    n = pl.program_id(1)
    ts = x_ref.shape[0]

    @pl.when(n == 0)
    def _():
        state_scr[...] = jnp.zeros_like(state_scr)
        kprev_scr[...] = jnp.zeros_like(kprev_scr)
        vprev_scr[...] = jnp.zeros_like(vprev_scr)

    @pl.when((b == 0) & (n == 0))
    def _():
        carry_scr[...] = jnp.zeros_like(carry_scr)

    x = x_ref[...]
    xb = x.astype(BF16)

    chunks_per_part = PROJ_ROWS // CHUNK
    parts = []

    def project_part(part):
        xp = xb[part * PROJ_ROWS:(part + 1) * PROJ_ROWS]
        proj = lambda lo, hi: _dot(xp, w_in_ref[:, lo:hi])
        parts.append(dict(
            q_r=proj(_QR, _KR), k_r=proj(_KR, _VR), v_r=proj(_VR, _GR), g_r=proj(_GR, _QA),
            q_a=proj(_QA, _KA), k_ab=proj(_KA, _VA).astype(BF16),
            v_rep=proj(_VA, _VA + SWA_KV_HEADS * SWA_VREP).astype(BF16)))

    project_part(0)

    def chunk_of(name, c):
        lo = (c % chunks_per_part) * CHUNK
        return parts[c // chunks_per_part][name][lo:lo + CHUNK]
    first_pen = jnp.where(n == 0, pen_ref[...], 0.0)

    for c in range(ts // CHUNK):
        rs = slice(c * CHUNK, (c + 1) * CHUNK)
        q_r, k_r, v_r, g_r = (chunk_of(name, c) for name in ("q_r", "k_r", "v_r", "g_r"))
        q_a, k_ab, v_rep = (chunk_of(name, c) for name in ("q_a", "k_ab", "v_rep"))
        for hd in range(RET_HEADS):
            cs = slice(hd * RET_HEAD_DIM, (hd + 1) * RET_HEAD_DIM)
            q = q_r[:, cs]
            k = k_r[:, cs]
            v = v_r[:, cs].astype(BF16)
            scores = _dot_nt(q.astype(BF16), k.astype(BF16)) * din_ref[hd]
            st = state_scr[hd]
            o = _dot(jnp.concatenate([scores.astype(BF16), (q * qdec_ref[hd]).astype(BF16)], axis=1),
                     jnp.concatenate([v, st.astype(BF16)], axis=0))
            kv = _dot_tn((k * kdec_ref[hd]).astype(BF16), v)
            state_scr[hd] = st * cdec_ref[hd] + kv
            mu = jnp.mean(o, axis=-1, keepdims=True)
            oc = o - mu
            var = jnp.mean(oc * oc, axis=-1, keepdims=True)
            on = oc * lax.rsqrt(var + GN_EPS) * gn_ref[:, cs] * _silu(g_r[:, cs])
            o_scr[rs, cs] = on.astype(BF16)
        for j in range(SWA_KV_HEADS):
            ks = slice(j * SWA_HEAD_DIM, (j + 1) * SWA_HEAD_DIM)
            vs = slice(j * SWA_VREP, (j + 1) * SWA_VREP)
            if c == 0:
                kp = kprev_scr[:, ks].astype(BF16)
                vp = vprev_scr[:, vs].astype(BF16)
            else:
                kp = chunk_of("k_ab", c - 1)[:, ks]
                vp = chunk_of("v_rep", c - 1)[:, vs]
            kp = kp * kmask_ref[...]
            kcat = jnp.concatenate([kp, k_ab[:, ks]], axis=0)
            vcat = jnp.concatenate([vp, v_rep[:, vs]], axis=0)
            q0 = j * SWA_GROUP * SWA_HEAD_DIM
            qs = jnp.concatenate(
                [q_a[:, q0 + g * SWA_HEAD_DIM:q0 + (g + 1) * SWA_HEAD_DIM] for g in range(SWA_GROUP)],
                axis=0)
            qs = (qs * (SWA_HEAD_DIM ** -0.5)).astype(BF16)
            s = _dot_nt(qs, kcat) + bias_ref[j]
            if c == 0:
                s = s + first_pen
            m = jnp.max(jnp.maximum(s[:, :CHUNK], s[:, CHUNK:]), axis=-1, keepdims=True)
            p32 = jnp.exp(s - m)
            den_col = jnp.sum(p32[:, :CHUNK] + p32[:, CHUNK:], axis=-1, keepdims=True)
            p = p32.astype(BF16)
            p_all = jnp.concatenate([p[g * CHUNK:(g + 1) * CHUNK] for g in range(SWA_GROUP)], axis=1)
            v_blk = jnp.concatenate([vcat * vmask_ref[g] for g in range(SWA_GROUP)], axis=0)
            num = _dot(p_all, v_blk)
            den = den_col[(SWA_GROUP - 1) * CHUNK:]
            for g in range(SWA_GROUP - 2, -1, -1):
                den = jnp.where(hlane_ref[...] == g, den_col[g * CHUNK:(g + 1) * CHUNK], den)
            c0 = RET_WIDTH + j * SWA_GROUP * SWA_HEAD_DIM
            o_scr[rs, c0:c0 + SWA_GROUP * SWA_HEAD_DIM] = (num / den).astype(BF16)
        if c % chunks_per_part == 0 and len(parts) < ts // PROJ_ROWS:
            project_part(len(parts))

    kprev_scr[...] = chunk_of("k_ab", ts // CHUNK - 1).astype(F32)
    vprev_scr[...] = chunk_of("v_rep", ts // CHUNK - 1).astype(F32)

    mix = _dot(o_scr[...], w_out_ref[...])
    h = _layer_norm(DEEPNORM_ALPHA * x + mix, ln_g_ref[...], ln_b_ref[...])
    h_ref[...] = h
    _pack_rows(hp_ref, h)

    h_hi = h.astype(BF16)
    h_lo = (h - h_hi.astype(F32)).astype(BF16)
    hi_terms = _dot(h_hi, wr_cat_ref[...])
    logits = hi_terms[:, :LANES] + hi_terms[:, LANES:] + _dot(h_lo, wr_cat_ref[:, :LANES]) + br_ref[...]
    lt = logits.T
    row = lax.broadcasted_iota(jnp.int32, (EXPERTS_PER_GROUP, ts), 0).astype(F32)
    big = 1e9
    ninf = -jnp.inf
    col_max = lambda v: jnp.max(v, axis=0, keepdims=True)
    first_at = lambda v, m: jnp.min(jnp.where(v == m, row, big), axis=0, keepdims=True)
    gl = jnp.where(row < N_GROUPS, lt[GROUP_LANE0:GROUP_LANE0 + EXPERTS_PER_GROUP], ninf)
    gmax = col_max(gl)
    gidx = first_at(gl, gmax)
    g_w = 1.0 / jnp.sum(jnp.exp(gl - gmax), axis=0, keepdims=True)
    el = lt[(N_GROUPS - 1) * EXPERTS_PER_GROUP:N_GROUPS * EXPERTS_PER_GROUP]
    for g in range(N_GROUPS - 2, -1, -1):
        el = jnp.where(gidx == g, lt[g * EXPERTS_PER_GROUP:(g + 1) * EXPERTS_PER_GROUP], el)
    m1 = col_max(el)
    i1 = first_at(el, m1)
    el2 = jnp.where(row == i1, ninf, el)
    m2 = col_max(el2)
    i2 = first_at(el2, m2)
    t = jnp.exp(m2 - m1)
    w1 = g_w / (1.0 + t)
    w2 = g_w * t / (1.0 + t)
    e1 = gidx * EXPERTS_PER_GROUP + i1
    e2 = gidx * EXPERTS_PER_GROUP + i2
    erow = lax.broadcasted_iota(jnp.int32, (N_EXPERTS, ts), 0).astype(F32)
    hit1 = erow == e1
    hit2 = erow == e2
    onehot = (hit1 | hit2).astype(BF16)
    prefix = _dot(onehot, tri_ref[...]) + carry_scr[:, 0:1]
    r1 = jnp.sum(jnp.where(hit1, prefix, 0.0), axis=0, keepdims=True)
    r2 = jnp.sum(jnp.where(hit2, prefix, 0.0), axis=0, keepdims=True)
    carry = carry_scr[:, 0:1] + jnp.sum(onehot.astype(F32), axis=1, keepdims=True)
    carry_scr[...] = jnp.broadcast_to(carry, carry_scr.shape)
    cnt_ref[...] = jnp.broadcast_to(carry, cnt_ref.shape)
    pick = lambda k, v, rest: jnp.where(row == k, v, rest)
    ids_ref[...] = pick(0, e1, pick(1, e2, pick(2, r1, pick(3, r2, 0.0)))).astype(jnp.int32)
    wts_ref[...] = pick(0, w1, pick(1, w2, 0.0))


def _mixer_call(x2, w_in, w_out, gn, ln_g, ln_b, wr_cat, br, consts, bsz, s_len, first_seq):
    cdec, din, qdec, kdec, bias, pen, kmask, vmask, hlane, tri = consts
    t_len = bsz * s_len
    ns = s_len // SEQ_TILE
    tok = lambda b, n, *_: (b * ns + n, 0)
    tok_in = lambda b, n, *_: ((first_seq + b) * ns + n, 0)
    tok_t = lambda b, n, *_: (0, b * ns + n)
    full2 = lambda b, n, *_: (0, 0)
    full3 = lambda b, n, *_: (0, 0, 0)
    grid_spec = pltpu.PrefetchScalarGridSpec(
        num_scalar_prefetch=1,
        grid=(bsz, ns),
        in_specs=[
            pl.BlockSpec((SEQ_TILE, D_MODEL), tok_in),
            pl.BlockSpec((D_MODEL, IN_WIDTH_TILED), full2),
            pl.BlockSpec((D_MODEL, D_MODEL), full2),
            pl.BlockSpec((1, RET_WIDTH), full2),
            pl.BlockSpec((1, D_MODEL), full2),
            pl.BlockSpec((1, D_MODEL), full2),
            pl.BlockSpec((D_MODEL, 2 * LANES), full2),
            pl.BlockSpec((1, LANES), full2),
            pl.BlockSpec((RET_HEADS, CHUNK, CHUNK), full3),
            pl.BlockSpec((RET_HEADS, CHUNK, RET_HEAD_DIM), full3),
            pl.BlockSpec((RET_HEADS, CHUNK, RET_HEAD_DIM), full3),
            pl.BlockSpec((SWA_KV_HEADS, SWA_GROUP * CHUNK, 2 * CHUNK), full3),
            pl.BlockSpec((1, 2 * CHUNK), full2),
            pl.BlockSpec((CHUNK, SWA_HEAD_DIM), full2),
            pl.BlockSpec((SWA_GROUP, 2 * CHUNK, SWA_VREP), full3),
            pl.BlockSpec((1, SWA_VREP), full2),
            pl.BlockSpec((SEQ_TILE, SEQ_TILE), full2),
        ],
        out_specs=[
            pl.BlockSpec((SEQ_TILE, D_MODEL), tok),
            pl.BlockSpec((SEQ_TILE * ROW_SUBLANES, LANES), tok),
            pl.BlockSpec((ROUTE_ROWS, SEQ_TILE), tok_t),
            pl.BlockSpec((ROUTE_ROWS, SEQ_TILE), tok_t),
            pl.BlockSpec((N_EXPERTS, LANES), full2),
        ],
        scratch_shapes=[
            pltpu.VMEM((RET_HEADS, RET_HEAD_DIM, RET_HEAD_DIM), F32),
            pltpu.VMEM((CHUNK, SWA_KV_WIDTH), F32),
            pltpu.VMEM((CHUNK, SWA_KV_HEADS * SWA_VREP), F32),
            pltpu.VMEM((SEQ_TILE, D_MODEL), BF16),
            pltpu.VMEM((N_EXPERTS, LANES), F32),
        ],
    )
    return pl.pallas_call(
        _mixer_kernel,
        grid_spec=grid_spec,
        out_shape=[
            jax.ShapeDtypeStruct((t_len, D_MODEL), F32),
            jax.ShapeDtypeStruct((t_len * ROW_SUBLANES, LANES), U32),
            jax.ShapeDtypeStruct((ROUTE_ROWS, t_len), jnp.int32),
            jax.ShapeDtypeStruct((ROUTE_ROWS, t_len), F32),
            jax.ShapeDtypeStruct((N_EXPERTS, LANES), F32),
        ],
        compiler_params=pltpu.CompilerParams(
            dimension_semantics=("arbitrary", "arbitrary"), vmem_limit_bytes=VMEM_LIMIT),
        name="mixer_router",
        cost_estimate=_mixer_cost(t_len),
    )(cdec, x2, w_in, w_out, gn, ln_g, ln_b, wr_cat, br, din, qdec, kdec, bias, pen, kmask, vmask, hlane, tri)


def _sc_mesh():
    return plsc.VectorSubcoreMesh(core_axis_name="core", subcore_axis_name="subcore")


def _sc_dispatch(h_rows, dest, slot0, n_slots):
    t_len = h_rows.shape[0]
    n_assign = dest.shape[0]
    per_worker = n_slots // SC_WORKERS
    n_chunks = per_worker // SC_CHUNK_ROWS
    assert n_chunks * SC_CHUNK_ROWS * SC_WORKERS == n_slots and slot0 + n_slots <= 3 * t_len
    assert per_worker % SC_LANES == 0 and n_assign % SC_LANES == 0 and n_assign == TOP_K * t_len

    @functools.partial(
        pl.kernel, mesh=_sc_mesh(), name="moe_dispatch_sc", cost_estimate=_row_move_cost(n_slots),
        compiler_params=pltpu.CompilerParams(needs_layout_passes=False),
        out_type=jax.ShapeDtypeStruct((n_slots, ROW_SUBLANES, LANES), U32),
        scratch_types=[pltpu.VMEM((n_assign,), jnp.int32),
                       pltpu.VMEM((per_worker,), jnp.int32),
                       pltpu.VMEM((SC_CHUNK_ROWS, ROW_SUBLANES, LANES), U32),
                       pltpu.SemaphoreType.DMA])
    def dispatch(h_hbm, dest_hbm, rows_hbm, dest_v, src_v, buf, sem):
        wid = lax.axis_index("subcore") * SC_CORES + lax.axis_index("core")
        local = wid * per_worker
        base = slot0 + local
        pltpu.sync_copy(dest_hbm, dest_v)
        lane = lax.iota(jnp.int32, SC_LANES)

        def wrap(a):
            a = jnp.where(a >= t_len, a - t_len, a)
            return jnp.where(a >= t_len, a - t_len, a)

        @pl.loop(0, per_worker // SC_LANES)
        def _(i):
            src_v[pl.ds(i * SC_LANES, SC_LANES)] = wrap(base + i * SC_LANES + lane)

        @pl.loop(0, n_assign // SC_LANES)
        def _(i):
            d = dest_v[pl.ds(i * SC_LANES, SC_LANES)] - base
            hit = (d >= 0) & (d < per_worker)
            plsc.store_scatter(src_v, [jnp.where(hit, d, 0)], wrap(i * SC_LANES + lane), mask=hit)

        @pl.loop(0, n_chunks)
        def _(c):
            off = pl.multiple_of(c * SC_CHUNK_ROWS, SC_CHUNK_ROWS)
            pltpu.async_copy(h_hbm.at[src_v.at[pl.ds(off, SC_CHUNK_ROWS)]], buf, sem).wait()
            pltpu.sync_copy(buf, rows_hbm.at[pl.ds(local + off, SC_CHUNK_ROWS)])

    return dispatch(h_rows, dest)


def _sc_gather_rows(table, idx, name):
    m = idx.shape[0]
    per_worker = m // (SC_CHUNK_ROWS * SC_WORKERS)
    assert per_worker * SC_CHUNK_ROWS * SC_WORKERS == m

    @functools.partial(
        pl.kernel, mesh=_sc_mesh(), name=name, cost_estimate=_row_move_cost(m),
        out_type=jax.ShapeDtypeStruct((m, ROW_SUBLANES, LANES), U32),
        scratch_types=[pltpu.VMEM((m,), jnp.int32),
                       pltpu.VMEM((SC_CHUNK_ROWS, ROW_SUBLANES, LANES), U32),
                       pltpu.SemaphoreType.DMA])
    def gather(table_hbm, idx_hbm, out_hbm, idx_v, buf, sem):
        wid = lax.axis_index("subcore") * SC_CORES + lax.axis_index("core")
        pltpu.sync_copy(idx_hbm, idx_v)

        @pl.loop(0, per_worker)
        def _(j):
            off = pl.multiple_of((j * SC_WORKERS + wid) * SC_CHUNK_ROWS, SC_CHUNK_ROWS)
            pltpu.async_copy(table_hbm.at[idx_v.at[pl.ds(off, SC_CHUNK_ROWS)]], buf, sem).wait()
            pltpu.sync_copy(buf, out_hbm.at[pl.ds(off, SC_CHUNK_ROWS)])

    return gather(table, idx)


def _expert_kernel(bexp_ref, nused_ref, nxt_ref, slot_ref, rows_ref, wg_hbm, wu_hbm, wd_hbm, *rest, first_block):
    y_ref, stage_g, stage_u, stage_d, wg_s, wu_s, wd_s, sem = rest[-8:]
    p = pl.program_id(0)
    blk = first_block + p
    expert = bexp_ref[blk]
    slot = slot_ref[blk]

    def weight_copies(e, s):
        return [pltpu.make_async_copy(w.at[e], stage.at[s], sem.at[s, i])
                for i, (w, stage) in enumerate(((wg_hbm, stage_g), (wu_hbm, stage_u), (wd_hbm, stage_d)))]

    @pl.when(p == 0)
    def _():
        for cp in weight_copies(expert, slot):
            cp.start()

    @pl.when((p == 0) | (expert != bexp_ref[jnp.maximum(blk - 1, 0)]))
    def _():
        for cp in weight_copies(expert, slot):
            cp.wait()
        wg_s[...] = stage_g[slot].astype(BF16)
        wu_s[...] = stage_u[slot].astype(BF16)
        wd_s[...] = stage_d[slot].astype(BF16)

        @pl.when(nxt_ref[blk] >= 0)
        def _():
            for cp in weight_copies(nxt_ref[blk], 1 - slot):
                cp.start()

    @pl.when(blk < nused_ref[0])
    def _():
        xb = _unpack_rows(rows_ref).astype(BF16)
        g = _dot(xb, wg_s[...])
        u = _dot(xb, wu_s[...])
        a = (_silu(g) * u).astype(BF16)
        _pack_rows(y_ref, _dot(a, wd_s[...]))

    @pl.when(blk >= nused_ref[0])
    def _():
        y_ref[...] = jnp.zeros_like(y_ref)


def _expert_call(bexp, nused, seg_end, rows, w_gate, w_up, w_down, y_prev, part, n_rows):
    part_blocks = rows.shape[0] // (ROW_SUBLANES * MOE_ROWS)
    n_blocks = bexp.shape[0]
    first_block = part * part_blocks
    limit = jnp.minimum(nused[0], first_block + part_blocks)
    nxt = jnp.where(seg_end < limit, bexp[jnp.minimum(seg_end, n_blocks - 1)], -1).astype(jnp.int32)
    starts = jnp.concatenate([jnp.ones((1,), jnp.int32), (bexp[1:] != bexp[:-1]).astype(jnp.int32)])
    slot = (jnp.cumsum(starts) % 2).astype(jnp.int32)
    last_used = lambda p, nu: jnp.clip(first_block + p, first_block, jnp.maximum(nu[0] - 1, first_block))
    blk = lambda p, be, nu, nx, sl: (last_used(p, nu) - first_block, 0)
    hbm = pl.BlockSpec(memory_space=pl.ANY)
    in_specs = [pl.BlockSpec((MOE_ROWS * ROW_SUBLANES, LANES), blk), hbm, hbm, hbm]
    args = [rows, w_gate, w_up, w_down]
    aliases = {}
    if y_prev is not None:
        in_specs.append(hbm)
        args.append(y_prev)
        aliases = {4 + len(args) - 1: 0}
    grid_spec = pltpu.PrefetchScalarGridSpec(
        num_scalar_prefetch=4,
        grid=(part_blocks,),
        in_specs=in_specs,
        out_specs=pl.BlockSpec((MOE_ROWS * ROW_SUBLANES, LANES), lambda p, be, nu, nx, sl: (first_block + p, 0)),
        scratch_shapes=[
            pltpu.VMEM((2, D_MODEL, D_EXPERT), F32),
            pltpu.VMEM((2, D_MODEL, D_EXPERT), F32),
            pltpu.VMEM((2, D_EXPERT, D_MODEL), F32),
            pltpu.VMEM((D_MODEL, D_EXPERT), BF16),
            pltpu.VMEM((D_MODEL, D_EXPERT), BF16),
            pltpu.VMEM((D_EXPERT, D_MODEL), BF16),
            pltpu.SemaphoreType.DMA((2, 3)),
        ],
    )
    return pl.pallas_call(
        functools.partial(_expert_kernel, first_block=first_block),
        grid_spec=grid_spec,
        out_shape=jax.ShapeDtypeStruct((n_rows * ROW_SUBLANES, LANES), U32),
        input_output_aliases=aliases,
        compiler_params=pltpu.CompilerParams(
            dimension_semantics=("arbitrary",), vmem_limit_bytes=VMEM_LIMIT),
        name="moe_experts",
        cost_estimate=_expert_cost(part_blocks * MOE_ROWS),
    )(bexp, nused, nxt, slot, *args)


def _combine_kernel(h_ref, y0_ref, y1_ref, wts_ref, ln_g_ref, ln_b_ref, *rest):
    out_ref = rest[-1]
    wts = wts_ref[...].T
    ffn = _unpack_rows(y0_ref) * wts[:, 0:1] + _unpack_rows(y1_ref) * wts[:, 1:2]
    out_ref[...] = _layer_norm(DEEPNORM_ALPHA * h_ref[...] + ffn, ln_g_ref[...], ln_b_ref[...])


def _combine_call(h, yk, wts, ln_g, ln_b, out_prev, split, t_total):
    t_len = t_total // TOKEN_SPLITS
    n_tiles = t_len // COMBINE_TILE
    tok = lambda i: (i, 0)
    tok_split = lambda i: (i + split * n_tiles, 0)
    full2 = lambda i: (0, 0)
    tiles = pl.BlockSpec((COMBINE_TILE * ROW_SUBLANES, LANES), tok)
    tiles_k1 = pl.BlockSpec((COMBINE_TILE * ROW_SUBLANES, LANES), lambda i: (i + n_tiles, 0))
    in_specs = [pl.BlockSpec((COMBINE_TILE, D_MODEL), tok_split), tiles, tiles_k1,
                pl.BlockSpec((ROUTE_ROWS, COMBINE_TILE), lambda i: (0, i + split * n_tiles)),
                pl.BlockSpec((1, D_MODEL), full2),
                pl.BlockSpec((1, D_MODEL), full2)]
    args = [h, yk, yk, wts, ln_g, ln_b]
    aliases = {}
    if out_prev is not None:
        in_specs.append(pl.BlockSpec(memory_space=pl.ANY))
        args.append(out_prev)
        aliases = {len(args) - 1: 0}
    return pl.pallas_call(
        _combine_kernel,
        grid=(n_tiles,),
        in_specs=in_specs,
        out_specs=pl.BlockSpec((COMBINE_TILE, D_MODEL), tok_split),
        out_shape=jax.ShapeDtypeStruct((t_total, D_MODEL), F32),
        input_output_aliases=aliases,
        compiler_params=pltpu.CompilerParams(
            dimension_semantics=("arbitrary",), vmem_limit_bytes=VMEM_LIMIT),
        name="moe_combine_ln",
        cost_estimate=pl.CostEstimate(flops=8 * t_len * D_MODEL, transcendentals=t_len,
                                      bytes_accessed=(2 * 4 * D_MODEL + TOP_K * ROW_BYTES) * t_len),
    )(*args)


def _position_tables():
    c = CHUNK
    f = np.float32
    log_g = np.log1p(-np.exp2(-5.0 - np.arange(RET_HEADS, dtype=f))).astype(f)
    idx = np.arange(c, dtype=f)
    diff = idx[:, None] - idx[None, :]
    scale = f(RET_HEAD_DIM ** -0.5)
    din = np.where(diff >= 0, np.exp(log_g[:, None, None] * np.maximum(diff, 0.0)), 0.0).astype(f) * scale
    qdec = np.broadcast_to(np.exp(log_g[:, None] * (idx + 1.0))[:, :, None], (RET_HEADS, c, RET_HEAD_DIM))
    kdec = np.broadcast_to((np.exp(log_g[:, None] * (c - 1.0 - idx)) * scale)[:, :, None],
                           (RET_HEADS, c, RET_HEAD_DIM))
    cdec = np.exp(log_g * c)
    slopes = np.exp2(-8.0 * (np.arange(SWA_HEADS, dtype=f) + 1.0) / SWA_HEADS).astype(f)
    r = np.arange(c)[:, None]
    col = np.arange(c)[None, :]
    dist_prev = (r - col + c).astype(f)
    dist_cur = (r - col).astype(f)
    bprev = np.where((r < col)[None], -slopes[:, None, None] * dist_prev[None], NEG)
    bcur = np.where((r >= col)[None], -slopes[:, None, None] * dist_cur[None], NEG)
    bias = np.concatenate([bprev, bcur], axis=-1).reshape(SWA_KV_HEADS, SWA_GROUP * c, 2 * c)
    key = np.arange(2 * c)
    pen = np.where((key >= 1) & (key < c), NEG, 0.0)[None, :]
    lane_head = np.arange(SWA_VREP) // SWA_HEAD_DIM
    own = lane_head[None, None, :] == np.arange(SWA_GROUP)[:, None, None]
    kmask = np.broadcast_to(np.arange(c)[:, None] > 0, (c, SWA_HEAD_DIM))
    vmask = own & (key[None, :, None] > 0)
    hlane = lane_head[None, :]
    tr = np.arange(SEQ_TILE)
    tri = tr[:, None] < tr[None, :]
    as_f32 = lambda v: np.ascontiguousarray(v, dtype=f)
    as_bf16 = lambda v: np.ascontiguousarray(v, dtype=f).astype(BF16)
    return (as_f32(cdec), as_f32(din), as_f32(qdec), as_f32(kdec), as_f32(bias), as_f32(pen),
            as_bf16(kmask), as_bf16(vmask), as_f32(hlane), as_bf16(tri))


def _mixer_constants(attn_sinks):
    cdec, din, qdec, kdec, bias, pen, kmask, vmask, hlane, tri = _position_tables()
    sink = attn_sinks.astype(F32).reshape(SWA_KV_HEADS, SWA_GROUP, 1, 1)
    sink = jnp.broadcast_to(sink, (SWA_KV_HEADS, SWA_GROUP, CHUNK, 1)).reshape(SWA_KV_HEADS, SWA_GROUP * CHUNK, 1)
    is_slot = (np.arange(2 * CHUNK) == 0)[None, None, :]
    return cdec, din, qdec, kdec, jnp.where(is_slot, sink, bias), pen, kmask, vmask, hlane, tri


def _tile_v_columns(w_in):
    v_cols = w_in[:, _VA:].reshape(D_MODEL, SWA_KV_HEADS, 1, SWA_HEAD_DIM)
    v_cols = jnp.broadcast_to(v_cols, (D_MODEL, SWA_KV_HEADS, SWA_GROUP, SWA_HEAD_DIM))
    return jnp.concatenate([w_in[:, :_VA], v_cols.reshape(D_MODEL, SWA_KV_HEADS * SWA_VREP)], axis=1)


def _router_tables(w_group_router, b_group_router, w_expert_router, b_expert_router):
    w_e = jnp.transpose(w_expert_router, (1, 0, 2)).reshape(D_MODEL, N_EXPERTS)
    w = jnp.concatenate([w_e, w_group_router,
                         jnp.zeros((D_MODEL, LANES - N_EXPERTS - N_GROUPS), F32)], axis=1)
    bias = jnp.concatenate([b_expert_router.reshape(N_EXPERTS), b_group_router,
                            jnp.zeros((LANES - N_EXPERTS - N_GROUPS,), F32)])[None, :]
    w_hi = w.astype(BF16)
    w_lo = (w - w_hi.astype(F32)).astype(BF16)
    return jnp.concatenate([w_hi, w_lo], axis=1), bias


def kernel(x, w_in, ret_gn_g, attn_sinks, w_out, ln1_g, ln1_b, w_group_router, b_group_router,
           w_expert_router, b_expert_router, w_gate, w_up, w_down, ln2_g, ln2_b):
    bsz, s_len, d = x.shape
    assert d == D_MODEL and s_len % SEQ_TILE == 0 and w_in.shape[0] == DEPTH == 1
    assert bsz % TOKEN_SPLITS == 0
    t_len = bsz * s_len
    t_split = t_len // TOKEN_SPLITS
    n_blocks = t_len * TOP_K // MOE_ROWS + N_EXPERTS
    n_rows = n_blocks * MOE_ROWS
    part_rows = n_rows // SLOT_PARTS

    consts = _mixer_constants(attn_sinks[0])
    wr_cat, br = _router_tables(w_group_router[0], b_group_router[0],
                                      w_expert_router[0], b_expert_router[0])
    h, h_packed, ids, wts, cnt = _mixer_call(
        x.reshape(t_len, d), _tile_v_columns(w_in[0]).astype(BF16), w_out[0].astype(BF16),
        ret_gn_g[0][None, :], ln1_g[0][None, :], ln1_b[0][None, :], wr_cat, br, consts, bsz, s_len, 0)

    counts = cnt[:, 0].astype(jnp.int32)
    padded = (counts + MOE_ROWS - 1) // MOE_ROWS * MOE_ROWS
    pend = jnp.cumsum(padded)
    pstart = pend - padded
    onehot = ids[0:TOP_K, :, None] == jnp.arange(N_EXPERTS, dtype=jnp.int32)
    dest2 = jnp.sum(jnp.where(onehot, pstart, 0), axis=-1) + ids[TOP_K:2 * TOP_K]
    dest = dest2.reshape(-1)
    nused = (pend[-1:] // MOE_ROWS).astype(jnp.int32)
    blk_start = jnp.minimum(jnp.arange(n_blocks, dtype=jnp.int32), nused[0] - 1) * MOE_ROWS
    bexp = jnp.minimum(jnp.sum(pend[None, :] <= blk_start[:, None], axis=-1), N_EXPERTS - 1).astype(jnp.int32)
    seg_end = jnp.sum(jnp.where(bexp[:, None] == jnp.arange(N_EXPERTS, dtype=jnp.int32), pend // MOE_ROWS, 0),
                      axis=-1).astype(jnp.int32)

    h_tiles = h_packed.reshape(t_len, ROW_SUBLANES, LANES)
    y = None
    for part in range(SLOT_PARTS):
        rows = _sc_dispatch(h_tiles, dest, part * part_rows, part_rows)
        y = _expert_call(bexp, nused, seg_end, rows.reshape(part_rows * ROW_SUBLANES, LANES),
                         w_gate[0], w_up[0], w_down[0], y, part, n_rows)

    y_tiles = y.reshape(n_rows, ROW_SUBLANES, LANES)
    out = None
    for sp in range(TOKEN_SPLITS):
        idx = dest2[:, sp * t_split:(sp + 1) * t_split].reshape(-1)
        yk = _sc_gather_rows(y_tiles, idx, "moe_combine_sc")
        out = _combine_call(h, yk.reshape(TOP_K * t_split * ROW_SUBLANES, LANES), wts,
                            ln2_g[0][None, :], ln2_b[0][None, :], out, sp, t_len)
    return out.reshape(bsz, s_len, d)
```

```python
import functools

import jax
import jax.numpy as jnp
import numpy as np
from jax import lax
from jax.experimental import pallas as pl
from jax.experimental.pallas import tpu as pltpu
from jax.experimental.pallas import tpu_sc as plsc

F32 = jnp.float32
BF16 = jnp.bfloat16
U32 = jnp.uint32

D_MODEL = 1024
RET_HEADS = 4
RET_HEAD_DIM = 128
RET_WIDTH = RET_HEADS * RET_HEAD_DIM
CHUNK = 128
SWA_HEADS = 8
SWA_KV_HEADS = 2
SWA_GROUP = SWA_HEADS // SWA_KV_HEADS
SWA_HEAD_DIM = 64
SWA_WIDTH = SWA_HEADS * SWA_HEAD_DIM
SWA_KV_WIDTH = SWA_KV_HEADS * SWA_HEAD_DIM
IN_WIDTH = 4 * RET_WIDTH + SWA_WIDTH + 2 * SWA_KV_WIDTH
N_GROUPS = 4
EXPERTS_PER_GROUP = 8
N_EXPERTS = N_GROUPS * EXPERTS_PER_GROUP
TOP_K = 2
D_EXPERT = 512
LN_EPS = 1e-5
GN_EPS = 1e-6
DEPTH = 1
DEEPNORM_ALPHA = (2 * DEPTH) ** 0.25
NEG = -1e30

LANES = 128
ROW_SUBLANES = D_MODEL // 2 // LANES
SEQ_TILE = 512
PROJ_ROWS = 256
MOE_ROWS = 512
COMBINE_TILE = 256
TOKEN_SPLITS = 2
SLOT_PARTS = 2
SC_CORES = 2
SC_SUBCORES = 16
SC_WORKERS = SC_CORES * SC_SUBCORES
SC_LANES = 16
SC_DEST_PIECES = 8
SC_SCAN_UNROLL = 8
SC_CHUNK_ROWS = 64
VMEM_LIMIT = 56 * 1024 * 1024

_QR, _KR, _VR, _GR = 0, RET_WIDTH, 2 * RET_WIDTH, 3 * RET_WIDTH
_QA = 4 * RET_WIDTH
_KA = _QA + SWA_WIDTH
_VA = _KA + SWA_KV_WIDTH
SWA_VREP = SWA_GROUP * SWA_HEAD_DIM
IN_WIDTH_TILED = _VA + SWA_KV_HEADS * SWA_VREP
ROUTE_ROWS = 8
GROUP_LANE0 = N_EXPERTS


def _dot(a, b):
    return jnp.dot(a, b, preferred_element_type=F32)


def _dot_nt(a, b):
    return lax.dot_general(a, b, (((1,), (1,)), ((), ())), preferred_element_type=F32)


def _dot_tn(a, b):
    return lax.dot_general(a, b, (((0,), (0,)), ((), ())), preferred_element_type=F32)


def _layer_norm(z, g, b):
    mu = jnp.mean(z, axis=-1, keepdims=True)
    zc = z - mu
    var = jnp.mean(zc * zc, axis=-1, keepdims=True)
    return zc * lax.rsqrt(var + LN_EPS) * g + b


def _silu(g):
    return g / (1.0 + jnp.exp(-g))


ROW_BYTES = 4 * ROW_SUBLANES * LANES


def _mixer_cost(t):
    proj = 2 * t * D_MODEL * (IN_WIDTH_TILED + D_MODEL + 3 * LANES)
    retention = RET_HEADS * 4 * 2 * t * CHUNK * RET_HEAD_DIM
    swa = SWA_KV_HEADS * 2 * t * SWA_GROUP * 2 * CHUNK * (SWA_HEAD_DIM + 2 * SWA_VREP)
    weights = 2 * D_MODEL * (IN_WIDTH_TILED + D_MODEL + 2 * LANES)
    return pl.CostEstimate(flops=proj + retention + swa + 2 * t * SEQ_TILE * N_EXPERTS,
                           transcendentals=t * (SWA_HEADS * 2 * CHUNK + RET_WIDTH + LANES),
                           bytes_accessed=(2 * 4 * D_MODEL + ROW_BYTES) * t + weights)


def _expert_cost(n_rows):
    return pl.CostEstimate(flops=2 * 3 * n_rows * D_MODEL * D_EXPERT, transcendentals=n_rows * D_EXPERT,
                           bytes_accessed=2 * ROW_BYTES * n_rows + 4 * 3 * N_EXPERTS * D_MODEL * D_EXPERT)


def _row_move_cost(n_rows):
    return pl.CostEstimate(flops=0, transcendentals=0, bytes_accessed=2 * ROW_BYTES * n_rows)


def _pack_rows(ref, val):
    n = val.shape[0]
    half = D_MODEL // 2
    hi = lax.bitcast_convert_type(val[:, :half].astype(BF16).astype(F32), U32)
    lo = lax.bitcast_convert_type(val[:, half:].astype(BF16).astype(F32), U32)
    word = hi | (lo >> 16)
    for j in range(ROW_SUBLANES):
        ref[pl.ds(j, n, stride=ROW_SUBLANES), :] = word[:, j * LANES:(j + 1) * LANES]


def _unpack_rows(ref):
    n = ref.shape[0] // ROW_SUBLANES
    word = jnp.concatenate([ref[pl.ds(j, n, stride=ROW_SUBLANES), :] for j in range(ROW_SUBLANES)], axis=1)
    hi = lax.bitcast_convert_type(word & jnp.uint32(0xFFFF0000), F32)
    lo = lax.bitcast_convert_type(word << 16, F32)
    return jnp.concatenate([hi, lo], axis=1)


def _mixer_kernel(cdec_ref, x_ref, w_in_ref, w_out_ref, gn_ref, ln_g_ref, ln_b_ref,
                  wr_cat_ref, br_ref, din_ref, qdec_ref, kdec_ref,
                  bias_ref, pen_ref, kmask_ref, vmask_ref, hlane_ref, tri_ref,
                  h_ref, hp_ref, ids_ref, wts_ref, cnt_ref,
                  state_scr, kprev_scr, vprev_scr, o_scr, carry_scr):
    b = pl.program_id(0)
    n = pl.program_id(1)
    ts = x_ref.shape[0]

    @pl.when(n == 0)
    def _():
        state_scr[...] = jnp.zeros_like(state_scr)
        kprev_scr[...] = jnp.zeros_like(kprev_scr)
        vprev_scr[...] = jnp.zeros_like(vprev_scr)

    @pl.when((b == 0) & (n == 0))
    def _():
        carry_scr[...] = jnp.zeros_like(carry_scr)

    x = x_ref[...]
    xb = x.astype(BF16)

    chunks_per_part = PROJ_ROWS // CHUNK
    parts = []

    def project_part(part):
        xp = xb[part * PROJ_ROWS:(part + 1) * PROJ_ROWS]
        proj = lambda lo, hi: _dot(xp, w_in_ref[:, lo:hi])
        parts.append(dict(
            q_r=proj(_QR, _KR), k_r=proj(_KR, _VR), v_r=proj(_VR, _GR), g_r=proj(_GR, _QA),
            q_a=proj(_QA, _KA), k_ab=proj(_KA, _VA).astype(BF16),
            v_rep=proj(_VA, _VA + SWA_KV_HEADS * SWA_VREP).astype(BF16)))

    project_part(0)

    def chunk_of(name, c):
        lo = (c % chunks_per_part) * CHUNK
        return parts[c // chunks_per_part][name][lo:lo + CHUNK]
    first_pen = jnp.where(n == 0, pen_ref[...], 0.0)

    for c in range(ts // CHUNK):
        rs = slice(c * CHUNK, (c + 1) * CHUNK)
        q_r, k_r, v_r, g_r = (chunk_of(name, c) for name in ("q_r", "k_r", "v_r", "g_r"))
        q_a, k_ab, v_rep = (chunk_of(name, c) for name in ("q_a", "k_ab", "v_rep"))
        for hd in range(RET_HEADS):
            cs = slice(hd * RET_HEAD_DIM, (hd + 1) * RET_HEAD_DIM)
            q = q_r[:, cs]
            k = k_r[:, cs]
            v = v_r[:, cs].astype(BF16)
            scores = _dot_nt(q.astype(BF16), k.astype(BF16)) * din_ref[hd]
            st = state_scr[hd]
            o = _dot(jnp.concatenate([scores.astype(BF16), (q * qdec_ref[hd]).astype(BF16)], axis=1),
                     jnp.concatenate([v, st.astype(BF16)], axis=0))
            kv = _dot_tn((k * kdec_ref[hd]).astype(BF16), v)
            state_scr[hd] = st * cdec_ref[hd] + kv
            mu = jnp.mean(o, axis=-1, keepdims=True)
            oc = o - mu
            var = jnp.mean(oc * oc, axis=-1, keepdims=True)
            on = oc * lax.rsqrt(var + GN_EPS) * gn_ref[:, cs] * _silu(g_r[:, cs])
            o_scr[rs, cs] = on.astype(BF16)
        for j in range(SWA_KV_HEADS):
            ks = slice(j * SWA_HEAD_DIM, (j + 1) * SWA_HEAD_DIM)
            vs = slice(j * SWA_VREP, (j + 1) * SWA_VREP)
            if c == 0:
                kp = kprev_scr[:, ks].astype(BF16)
                vp = vprev_scr[:, vs].astype(BF16)
            else:
                kp = chunk_of("k_ab", c - 1)[:, ks]
                vp = chunk_of("v_rep", c - 1)[:, vs]
            kp = kp * kmask_ref[...]
            kcat = jnp.concatenate([kp, k_ab[:, ks]], axis=0)
            vcat = jnp.concatenate([vp, v_rep[:, vs]], axis=0)
            q0 = j * SWA_GROUP * SWA_HEAD_DIM
            qs = jnp.concatenate(
                [q_a[:, q0 + g * SWA_HEAD_DIM:q0 + (g + 1) * SWA_HEAD_DIM] for g in range(SWA_GROUP)],
                axis=0)
            qs = (qs * (SWA_HEAD_DIM ** -0.5)).astype(BF16)
            s = _dot_nt(qs, kcat) + bias_ref[j]
            if c == 0:
                s = s + first_pen
            m = jnp.max(jnp.maximum(s[:, :CHUNK], s[:, CHUNK:]), axis=-1, keepdims=True)
            p32 = jnp.exp(s - m)
            den_col = jnp.sum(p32[:, :CHUNK] + p32[:, CHUNK:], axis=-1, keepdims=True)
            p = p32.astype(BF16)
            p_all = jnp.concatenate([p[g * CHUNK:(g + 1) * CHUNK] for g in range(SWA_GROUP)], axis=1)
            v_blk = jnp.concatenate([vcat * vmask_ref[g] for g in range(SWA_GROUP)], axis=0)
            num = _dot(p_all, v_blk)
            den = den_col[(SWA_GROUP - 1) * CHUNK:]
            for g in range(SWA_GROUP - 2, -1, -1):
                den = jnp.where(hlane_ref[...] == g, den_col[g * CHUNK:(g + 1) * CHUNK], den)
            c0 = RET_WIDTH + j * SWA_GROUP * SWA_HEAD_DIM
            o_scr[rs, c0:c0 + SWA_GROUP * SWA_HEAD_DIM] = (num / den).astype(BF16)
        if c % chunks_per_part == 0 and len(parts) < ts // PROJ_ROWS:
            project_part(len(parts))

    kprev_scr[...] = chunk_of("k_ab", ts // CHUNK - 1).astype(F32)
    vprev_scr[...] = chunk_of("v_rep", ts // CHUNK - 1).astype(F32)

    mix = _dot(o_scr[...], w_out_ref[...])
    h = _layer_norm(DEEPNORM_ALPHA * x + mix, ln_g_ref[...], ln_b_ref[...])
    h_ref[...] = h
    _pack_rows(hp_ref, h)

    h_hi = h.astype(BF16)
    h_lo = (h - h_hi.astype(F32)).astype(BF16)
    hi_terms = _dot(h_hi, wr_cat_ref[...])
    logits = hi_terms[:, :LANES] + hi_terms[:, LANES:] + _dot(h_lo, wr_cat_ref[:, :LANES]) + br_ref[...]
    lt = logits.T
    row = lax.broadcasted_iota(jnp.int32, (EXPERTS_PER_GROUP, ts), 0).astype(F32)
    big = 1e9
    ninf = -jnp.inf
    col_max = lambda v: jnp.max(v, axis=0, keepdims=True)
    first_at = lambda v, m: jnp.min(jnp.where(v == m, row, big), axis=0, keepdims=True)
    gl = jnp.where(row < N_GROUPS, lt[GROUP_LANE0:GROUP_LANE0 + EXPERTS_PER_GROUP], ninf)
    gmax = col_max(gl)
    gidx = first_at(gl, gmax)
    g_w = 1.0 / jnp.sum(jnp.exp(gl - gmax), axis=0, keepdims=True)
    el = lt[(N_GROUPS - 1) * EXPERTS_PER_GROUP:N_GROUPS * EXPERTS_PER_GROUP]
    for g in range(N_GROUPS - 2, -1, -1):
        el = jnp.where(gidx == g, lt[g * EXPERTS_PER_GROUP:(g + 1) * EXPERTS_PER_GROUP], el)
    m1 = col_max(el)
    i1 = first_at(el, m1)
    el2 = jnp.where(row == i1, ninf, el)
    m2 = col_max(el2)
    i2 = first_at(el2, m2)
    t = jnp.exp(m2 - m1)
    w1 = g_w / (1.0 + t)
    w2 = g_w * t / (1.0 + t)
    e1 = gidx * EXPERTS_PER_GROUP + i1
    e2 = gidx * EXPERTS_PER_GROUP + i2
    erow = lax.broadcasted_iota(jnp.int32, (N_EXPERTS, ts), 0).astype(F32)
    hit1 = erow == e1
    hit2 = erow == e2
    onehot = (hit1 | hit2).astype(BF16)
    prefix = _dot(onehot, tri_ref[...]) + carry_scr[:, 0:1]
    r1 = jnp.sum(jnp.where(hit1, prefix, 0.0), axis=0, keepdims=True)
    r2 = jnp.sum(jnp.where(hit2, prefix, 0.0), axis=0, keepdims=True)
    carry = carry_scr[:, 0:1] + jnp.sum(onehot.astype(F32), axis=1, keepdims=True)
    carry_scr[...] = jnp.broadcast_to(carry, carry_scr.shape)
    cnt_ref[...] = jnp.broadcast_to(carry, cnt_ref.shape)
    pick = lambda k, v, rest: jnp.where(row == k, v, rest)
    ids_ref[...] = pick(0, e1, pick(1, e2, pick(2, r1, pick(3, r2, 0.0)))).astype(jnp.int32)
    wts_ref[...] = pick(0, w1, pick(1, w2, 0.0))


def _mixer_call(x2, w_in, w_out, gn, ln_g, ln_b, wr_cat, br, consts, bsz, s_len, first_seq):
    cdec, din, qdec, kdec, bias, pen, kmask, vmask, hlane, tri = consts
    t_len = bsz * s_len
    ns = s_len // SEQ_TILE
    tok = lambda b, n, *_: (b * ns + n, 0)
    tok_in = lambda b, n, *_: ((first_seq + b) * ns + n, 0)
    tok_t = lambda b, n, *_: (0, b * ns + n)
    full2 = lambda b, n, *_: (0, 0)
    full3 = lambda b, n, *_: (0, 0, 0)
    grid_spec = pltpu.PrefetchScalarGridSpec(
        num_scalar_prefetch=1,
        grid=(bsz, ns),
        in_specs=[
            pl.BlockSpec((SEQ_TILE, D_MODEL), tok_in),
            pl.BlockSpec((D_MODEL, IN_WIDTH_TILED), full2),
            pl.BlockSpec((D_MODEL, D_MODEL), full2),
            pl.BlockSpec((1, RET_WIDTH), full2),
            pl.BlockSpec((1, D_MODEL), full2),
            pl.BlockSpec((1, D_MODEL), full2),
            pl.BlockSpec((D_MODEL, 2 * LANES), full2),
            pl.BlockSpec((1, LANES), full2),
            pl.BlockSpec((RET_HEADS, CHUNK, CHUNK), full3),
            pl.BlockSpec((RET_HEADS, CHUNK, RET_HEAD_DIM), full3),
            pl.BlockSpec((RET_HEADS, CHUNK, RET_HEAD_DIM), full3),
            pl.BlockSpec((SWA_KV_HEADS, SWA_GROUP * CHUNK, 2 * CHUNK), full3),
            pl.BlockSpec((1, 2 * CHUNK), full2),
            pl.BlockSpec((CHUNK, SWA_HEAD_DIM), full2),
            pl.BlockSpec((SWA_GROUP, 2 * CHUNK, SWA_VREP), full3),
            pl.BlockSpec((1, SWA_VREP), full2),
            pl.BlockSpec((SEQ_TILE, SEQ_TILE), full2),
        ],
        out_specs=[
            pl.BlockSpec((SEQ_TILE, D_MODEL), tok),
            pl.BlockSpec((SEQ_TILE * ROW_SUBLANES, LANES), tok),
            pl.BlockSpec((ROUTE_ROWS, SEQ_TILE), tok_t),
            pl.BlockSpec((ROUTE_ROWS, SEQ_TILE), tok_t),
            pl.BlockSpec((N_EXPERTS, LANES), full2),
        ],
        scratch_shapes=[
            pltpu.VMEM((RET_HEADS, RET_HEAD_DIM, RET_HEAD_DIM), F32),
            pltpu.VMEM((CHUNK, SWA_KV_WIDTH), F32),
            pltpu.VMEM((CHUNK, SWA_KV_HEADS * SWA_VREP), F32),
            pltpu.VMEM((SEQ_TILE, D_MODEL), BF16),
            pltpu.VMEM((N_EXPERTS, LANES), F32),
        ],
    )
    return pl.pallas_call(
        _mixer_kernel,
        grid_spec=grid_spec,
        out_shape=[
            jax.ShapeDtypeStruct((t_len, D_MODEL), F32),
            jax.ShapeDtypeStruct((t_len * ROW_SUBLANES, LANES), U32),
            jax.ShapeDtypeStruct((ROUTE_ROWS, t_len), jnp.int32),
            jax.ShapeDtypeStruct((ROUTE_ROWS, t_len), F32),
            jax.ShapeDtypeStruct((N_EXPERTS, LANES), F32),
        ],
        compiler_params=pltpu.CompilerParams(
            dimension_semantics=("arbitrary", "arbitrary"), vmem_limit_bytes=VMEM_LIMIT),
        name="mixer_router",
        cost_estimate=_mixer_cost(t_len),
    )(cdec, x2, w_in, w_out, gn, ln_g, ln_b, wr_cat, br, din, qdec, kdec, bias, pen, kmask, vmask, hlane, tri)


def _sc_mesh():
    return plsc.VectorSubcoreMesh(core_axis_name="core", subcore_axis_name="subcore")


def _sc_stream_rows(table_hbm, idx_v, out_hbm, out_base, n_chunks, bufs, sems):
    assert n_chunks % 2 == 0

    def gather(j, b):
        off = pl.multiple_of(j * SC_CHUNK_ROWS, SC_CHUNK_ROWS)
        return pltpu.make_async_copy(table_hbm.at[idx_v.at[pl.ds(off, SC_CHUNK_ROWS)]], bufs[b], sems[b])

    gather(0, 0).start()

    @pl.loop(0, n_chunks, step=2)
    def _(j0):
        for b in range(2):
            j = j0 + b
            gather(j, b).wait()

            @pl.when(j + 1 < n_chunks)
            def _():
                gather(j + 1, 1 - b).start()

            off = pl.multiple_of(j * SC_CHUNK_ROWS, SC_CHUNK_ROWS)
            pltpu.sync_copy(bufs[b], out_hbm.at[pl.ds(out_base + off, SC_CHUNK_ROWS)])


def _sc_dispatch(h_rows, dest, slot0, n_slots):
    t_len = h_rows.shape[0]
    n_assign = dest.shape[0]
    per_worker = n_slots // SC_WORKERS
    n_chunks = per_worker // SC_CHUNK_ROWS
    assert n_chunks * SC_CHUNK_ROWS * SC_WORKERS == n_slots and slot0 + n_slots <= 3 * t_len
    assert per_worker % SC_LANES == 0 and t_len % (SC_LANES * SC_SCAN_UNROLL) == 0 and n_assign == TOP_K * t_len

    @functools.partial(
        pl.kernel, mesh=_sc_mesh(), name="moe_dispatch_sc", cost_estimate=_row_move_cost(n_slots),
        compiler_params=pltpu.CompilerParams(needs_layout_passes=False),
        out_type=jax.ShapeDtypeStruct((n_slots, ROW_SUBLANES, LANES), U32),
        scratch_types=[pltpu.VMEM((n_assign,), jnp.int32),
                       pltpu.VMEM((per_worker,), jnp.int32),
                       pltpu.VMEM((SC_CHUNK_ROWS, ROW_SUBLANES, LANES), U32),
                       pltpu.VMEM((SC_CHUNK_ROWS, ROW_SUBLANES, LANES), U32),
                       pltpu.SemaphoreType.DMA, pltpu.SemaphoreType.DMA])
    def dispatch(h_hbm, dest_hbm, rows_hbm, dest_v, src_v, buf0, buf1, sem, sem1):
        wid = lax.axis_index("subcore") * SC_CORES + lax.axis_index("core")
        local = wid * per_worker
        base = slot0 + local
        piece = n_assign // SC_DEST_PIECES
        copies = []
        for c in range(SC_DEST_PIECES):
            off = pl.multiple_of(lax.rem(c + wid, SC_DEST_PIECES) * piece, SC_LANES)
            copies.append(pltpu.async_copy(dest_hbm.at[pl.ds(off, piece)], dest_v.at[pl.ds(off, piece)], sem))
        for cp in copies:
            cp.wait()
        lane = lax.iota(jnp.int32, SC_LANES)

        def wrap(a):
            a = jnp.where(a >= t_len, a - t_len, a)
            return jnp.where(a >= t_len, a - t_len, a)

        @pl.loop(0, per_worker // SC_LANES)
        def _(i):
            src_v[pl.ds(i * SC_LANES, SC_LANES)] = wrap(base + i * SC_LANES + lane)

        for k in range(TOP_K):
            @pl.loop(0, t_len // (SC_LANES * SC_SCAN_UNROLL))
            def _(i):
                for u in range(SC_SCAN_UNROLL):
                    tok0 = (i * SC_SCAN_UNROLL + u) * SC_LANES
                    d = dest_v[pl.ds(k * t_len + tok0, SC_LANES)] - base
                    hit = (d >= 0) & (d < per_worker)
                    plsc.store_scatter(src_v, [jnp.where(hit, d, 0)], tok0 + lane, mask=hit)

        _sc_stream_rows(h_hbm, src_v, rows_hbm, local, n_chunks, (buf0, buf1), (sem, sem1))

    return dispatch(h_rows, dest)


def _sc_gather_rows(table, idx, name):
    m = idx.shape[0]
    per_worker = m // (SC_CHUNK_ROWS * SC_WORKERS)
    assert per_worker * SC_CHUNK_ROWS * SC_WORKERS == m

    @functools.partial(
        pl.kernel, mesh=_sc_mesh(), name=name, cost_estimate=_row_move_cost(m),
        out_type=jax.ShapeDtypeStruct((m, ROW_SUBLANES, LANES), U32),
        scratch_types=[pltpu.VMEM((per_worker * SC_CHUNK_ROWS,), jnp.int32),
                       pltpu.VMEM((SC_CHUNK_ROWS, ROW_SUBLANES, LANES), U32),
                       pltpu.VMEM((SC_CHUNK_ROWS, ROW_SUBLANES, LANES), U32),
                       pltpu.SemaphoreType.DMA, pltpu.SemaphoreType.DMA])
    def gather(table_hbm, idx_hbm, out_hbm, idx_v, buf0, buf1, sem0, sem1):
        wid = lax.axis_index("subcore") * SC_CORES + lax.axis_index("core")
        base = pl.multiple_of(wid * (per_worker * SC_CHUNK_ROWS), SC_CHUNK_ROWS)
        pltpu.sync_copy(idx_hbm.at[pl.ds(base, per_worker * SC_CHUNK_ROWS)], idx_v)

        _sc_stream_rows(table_hbm, idx_v, out_hbm, base, per_worker, (buf0, buf1), (sem0, sem1))

    return gather(table, idx)


def _expert_kernel(bexp_ref, nused_ref, nxt_ref, slot_ref, rows_ref, wg_hbm, wu_hbm, wd_hbm, *rest, first_block):
    y_ref, stage_g, stage_u, stage_d, wg_s, wu_s, wd_s, sem = rest[-8:]
    p = pl.program_id(0)
    blk = first_block + p
    expert = bexp_ref[blk]
    slot = slot_ref[blk]

    def weight_copies(e, s):
        return [pltpu.make_async_copy(w.at[e], stage.at[s], sem.at[s, i])
                for i, (w, stage) in enumerate(((wg_hbm, stage_g), (wu_hbm, stage_u), (wd_hbm, stage_d)))]

    @pl.when(p == 0)
    def _():
        for cp in weight_copies(expert, slot):
            cp.start()

    @pl.when((p == 0) | (expert != bexp_ref[jnp.maximum(blk - 1, 0)]))
    def _():
        for cp in weight_copies(expert, slot):
            cp.wait()
        wg_s[...] = stage_g[slot].astype(BF16)
        wu_s[...] = stage_u[slot].astype(BF16)
        wd_s[...] = stage_d[slot].astype(BF16)

        @pl.when(nxt_ref[blk] >= 0)
        def _():
            for cp in weight_copies(nxt_ref[blk], 1 - slot):
                cp.start()

    @pl.when(blk < nused_ref[0])
    def _():
        xb = _unpack_rows(rows_ref).astype(BF16)
        g = _dot(xb, wg_s[...])
        u = _dot(xb, wu_s[...])
        a = (_silu(g) * u).astype(BF16)
        _pack_rows(y_ref, _dot(a, wd_s[...]))

    @pl.when(blk >= nused_ref[0])
    def _():
        y_ref[...] = jnp.zeros_like(y_ref)


def _expert_call(bexp, nused, seg_end, rows, w_gate, w_up, w_down, y_prev, part, n_rows):
    part_blocks = rows.shape[0] // (ROW_SUBLANES * MOE_ROWS)
    n_blocks = bexp.shape[0]
    first_block = part * part_blocks
    limit = jnp.minimum(nused[0], first_block + part_blocks)
    nxt = jnp.where(seg_end < limit, bexp[jnp.minimum(seg_end, n_blocks - 1)], -1).astype(jnp.int32)
    starts = jnp.concatenate([jnp.ones((1,), jnp.int32), (bexp[1:] != bexp[:-1]).astype(jnp.int32)])
    slot = (jnp.cumsum(starts) % 2).astype(jnp.int32)
    last_used = lambda p, nu: jnp.clip(first_block + p, first_block, jnp.maximum(nu[0] - 1, first_block))
    blk = lambda p, be, nu, nx, sl: (last_used(p, nu) - first_block, 0)
    hbm = pl.BlockSpec(memory_space=pl.ANY)
    in_specs = [pl.BlockSpec((MOE_ROWS * ROW_SUBLANES, LANES), blk), hbm, hbm, hbm]
    args = [rows, w_gate, w_up, w_down]
    aliases = {}
    if y_prev is not None:
        in_specs.append(hbm)
        args.append(y_prev)
        aliases = {4 + len(args) - 1: 0}
    grid_spec = pltpu.PrefetchScalarGridSpec(
        num_scalar_prefetch=4,
        grid=(part_blocks,),
        in_specs=in_specs,
        out_specs=pl.BlockSpec((MOE_ROWS * ROW_SUBLANES, LANES), lambda p, be, nu, nx, sl: (first_block + p, 0)),
        scratch_shapes=[
            pltpu.VMEM((2, D_MODEL, D_EXPERT), F32),
            pltpu.VMEM((2, D_MODEL, D_EXPERT), F32),
            pltpu.VMEM((2, D_EXPERT, D_MODEL), F32),
            pltpu.VMEM((D_MODEL, D_EXPERT), BF16),
            pltpu.VMEM((D_MODEL, D_EXPERT), BF16),
            pltpu.VMEM((D_EXPERT, D_MODEL), BF16),
            pltpu.SemaphoreType.DMA((2, 3)),
        ],
    )
    return pl.pallas_call(
        functools.partial(_expert_kernel, first_block=first_block),
        grid_spec=grid_spec,
        out_shape=jax.ShapeDtypeStruct((n_rows * ROW_SUBLANES, LANES), U32),
        input_output_aliases=aliases,
        compiler_params=pltpu.CompilerParams(
            dimension_semantics=("arbitrary",), vmem_limit_bytes=VMEM_LIMIT),
        name="moe_experts",
        cost_estimate=_expert_cost(part_blocks * MOE_ROWS),
    )(bexp, nused, nxt, slot, *args)


def _combine_kernel(h_ref, y0_ref, y1_ref, wts_ref, ln_g_ref, ln_b_ref, *rest):
    out_ref = rest[-1]
    wts = wts_ref[...].T
    ffn = _unpack_rows(y0_ref) * wts[:, 0:1] + _unpack_rows(y1_ref) * wts[:, 1:2]
    out_ref[...] = _layer_norm(DEEPNORM_ALPHA * h_ref[...] + ffn, ln_g_ref[...], ln_b_ref[...])


def _combine_call(h, yk, wts, ln_g, ln_b, out_prev, split, t_total):
    t_len = t_total // TOKEN_SPLITS
    n_tiles = t_len // COMBINE_TILE
    tok = lambda i: (i, 0)
    tok_split = lambda i: (i + split * n_tiles, 0)
    full2 = lambda i: (0, 0)
    tiles = pl.BlockSpec((COMBINE_TILE * ROW_SUBLANES, LANES), tok)
    tiles_k1 = pl.BlockSpec((COMBINE_TILE * ROW_SUBLANES, LANES), lambda i: (i + n_tiles, 0))
    in_specs = [pl.BlockSpec((COMBINE_TILE, D_MODEL), tok_split), tiles, tiles_k1,
                pl.BlockSpec((ROUTE_ROWS, COMBINE_TILE), lambda i: (0, i + split * n_tiles)),
                pl.BlockSpec((1, D_MODEL), full2),
                pl.BlockSpec((1, D_MODEL), full2)]
    args = [h, yk, yk, wts, ln_g, ln_b]
    aliases = {}
    if out_prev is not None:
        in_specs.append(pl.BlockSpec(memory_space=pl.ANY))
        args.append(out_prev)
        aliases = {len(args) - 1: 0}
    return pl.pallas_call(
        _combine_kernel,
        grid=(n_tiles,),
        in_specs=in_specs,
        out_specs=pl.BlockSpec((COMBINE_TILE, D_MODEL), tok_split),
        out_shape=jax.ShapeDtypeStruct((t_total, D_MODEL), F32),
        input_output_aliases=aliases,
        compiler_params=pltpu.CompilerParams(
            dimension_semantics=("arbitrary",), vmem_limit_bytes=VMEM_LIMIT),
        name="moe_combine_ln",
        cost_estimate=pl.CostEstimate(flops=8 * t_len * D_MODEL, transcendentals=t_len,
                                      bytes_accessed=(2 * 4 * D_MODEL + TOP_K * ROW_BYTES) * t_len),
    )(*args)


def _position_tables():
    c = CHUNK
    f = np.float32
    log_g = np.log1p(-np.exp2(-5.0 - np.arange(RET_HEADS, dtype=f))).astype(f)
    idx = np.arange(c, dtype=f)
    diff = idx[:, None] - idx[None, :]
    scale = f(RET_HEAD_DIM ** -0.5)
    din = np.where(diff >= 0, np.exp(log_g[:, None, None] * np.maximum(diff, 0.0)), 0.0).astype(f) * scale
    qdec = np.broadcast_to(np.exp(log_g[:, None] * (idx + 1.0))[:, :, None], (RET_HEADS, c, RET_HEAD_DIM))
    kdec = np.broadcast_to((np.exp(log_g[:, None] * (c - 1.0 - idx)) * scale)[:, :, None],
                           (RET_HEADS, c, RET_HEAD_DIM))
    cdec = np.exp(log_g * c)
    slopes = np.exp2(-8.0 * (np.arange(SWA_HEADS, dtype=f) + 1.0) / SWA_HEADS).astype(f)
    r = np.arange(c)[:, None]
    col = np.arange(c)[None, :]
    dist_prev = (r - col + c).astype(f)
    dist_cur = (r - col).astype(f)
    bprev = np.where((r < col)[None], -slopes[:, None, None] * dist_prev[None], NEG)
    bcur = np.where((r >= col)[None], -slopes[:, None, None] * dist_cur[None], NEG)
    bias = np.concatenate([bprev, bcur], axis=-1).reshape(SWA_KV_HEADS, SWA_GROUP * c, 2 * c)
    key = np.arange(2 * c)
    pen = np.where((key >= 1) & (key < c), NEG, 0.0)[None, :]
    lane_head = np.arange(SWA_VREP) // SWA_HEAD_DIM
    own = lane_head[None, None, :] == np.arange(SWA_GROUP)[:, None, None]
    kmask = np.broadcast_to(np.arange(c)[:, None] > 0, (c, SWA_HEAD_DIM))
    vmask = own & (key[None, :, None] > 0)
    hlane = lane_head[None, :]
    tr = np.arange(SEQ_TILE)
    tri = tr[:, None] < tr[None, :]
    as_f32 = lambda v: np.ascontiguousarray(v, dtype=f)
    as_bf16 = lambda v: np.ascontiguousarray(v, dtype=f).astype(BF16)
    return (as_f32(cdec), as_f32(din), as_f32(qdec), as_f32(kdec), as_f32(bias), as_f32(pen),
            as_bf16(kmask), as_bf16(vmask), as_f32(hlane), as_bf16(tri))


def _mixer_constants(attn_sinks):
    cdec, din, qdec, kdec, bias, pen, kmask, vmask, hlane, tri = _position_tables()
    sink = attn_sinks.astype(F32).reshape(SWA_KV_HEADS, SWA_GROUP, 1, 1)
    sink = jnp.broadcast_to(sink, (SWA_KV_HEADS, SWA_GROUP, CHUNK, 1)).reshape(SWA_KV_HEADS, SWA_GROUP * CHUNK, 1)
    is_slot = (np.arange(2 * CHUNK) == 0)[None, None, :]
    return cdec, din, qdec, kdec, jnp.where(is_slot, sink, bias), pen, kmask, vmask, hlane, tri


def _tile_v_columns(w_in):
    v_cols = w_in[:, _VA:].reshape(D_MODEL, SWA_KV_HEADS, 1, SWA_HEAD_DIM)
    v_cols = jnp.broadcast_to(v_cols, (D_MODEL, SWA_KV_HEADS, SWA_GROUP, SWA_HEAD_DIM))
    return jnp.concatenate([w_in[:, :_VA], v_cols.reshape(D_MODEL, SWA_KV_HEADS * SWA_VREP)], axis=1)


def _router_tables(w_group_router, b_group_router, w_expert_router, b_expert_router):
    w_e = jnp.transpose(w_expert_router, (1, 0, 2)).reshape(D_MODEL, N_EXPERTS)
    w = jnp.concatenate([w_e, w_group_router,
                         jnp.zeros((D_MODEL, LANES - N_EXPERTS - N_GROUPS), F32)], axis=1)
    bias = jnp.concatenate([b_expert_router.reshape(N_EXPERTS), b_group_router,
                            jnp.zeros((LANES - N_EXPERTS - N_GROUPS,), F32)])[None, :]
    w_hi = w.astype(BF16)
    w_lo = (w - w_hi.astype(F32)).astype(BF16)
    return jnp.concatenate([w_hi, w_lo], axis=1), bias


def kernel(x, w_in, ret_gn_g, attn_sinks, w_out, ln1_g, ln1_b, w_group_router, b_group_router,
           w_expert_router, b_expert_router, w_gate, w_up, w_down, ln2_g, ln2_b):
    bsz, s_len, d = x.shape
    assert d == D_MODEL and s_len % SEQ_TILE == 0 and w_in.shape[0] == DEPTH == 1
    assert bsz % TOKEN_SPLITS == 0
    t_len = bsz * s_len
    t_split = t_len // TOKEN_SPLITS
    n_blocks = t_len * TOP_K // MOE_ROWS + N_EXPERTS
    n_rows = n_blocks * MOE_ROWS
    part_rows = n_rows // SLOT_PARTS

    consts = _mixer_constants(attn_sinks[0])
    wr_cat, br = _router_tables(w_group_router[0], b_group_router[0],
                                      w_expert_router[0], b_expert_router[0])
    h, h_packed, ids, wts, cnt = _mixer_call(
        x.reshape(t_len, d), _tile_v_columns(w_in[0]).astype(BF16), w_out[0].astype(BF16),
        ret_gn_g[0][None, :], ln1_g[0][None, :], ln1_b[0][None, :], wr_cat, br, consts, bsz, s_len, 0)

    counts = cnt[:, 0].astype(jnp.int32)
    padded = (counts + MOE_ROWS - 1) // MOE_ROWS * MOE_ROWS
    pend = jnp.cumsum(padded)
    pstart = pend - padded
    onehot = ids[0:TOP_K, :, None] == jnp.arange(N_EXPERTS, dtype=jnp.int32)
    dest2 = jnp.sum(jnp.where(onehot, pstart, 0), axis=-1) + ids[TOP_K:2 * TOP_K]
    dest = dest2.reshape(-1)
    nused = (pend[-1:] // MOE_ROWS).astype(jnp.int32)
    blk_start = jnp.minimum(jnp.arange(n_blocks, dtype=jnp.int32), nused[0] - 1) * MOE_ROWS
    bexp = jnp.minimum(jnp.sum(pend[None, :] <= blk_start[:, None], axis=-1), N_EXPERTS - 1).astype(jnp.int32)
    seg_end = jnp.sum(jnp.where(bexp[:, None] == jnp.arange(N_EXPERTS, dtype=jnp.int32), pend // MOE_ROWS, 0),
                      axis=-1).astype(jnp.int32)

    h_tiles = h_packed.reshape(t_len, ROW_SUBLANES, LANES)
    y = None
    for part in range(SLOT_PARTS):
        rows = _sc_dispatch(h_tiles, dest, part * part_rows, part_rows)
        y = _expert_call(bexp, nused, seg_end, rows.reshape(part_rows * ROW_SUBLANES, LANES),
                         w_gate[0], w_up[0], w_down[0], y, part, n_rows)

    y_tiles = y.reshape(n_rows, ROW_SUBLANES, LANES)
    out = None
    for sp in range(TOKEN_SPLITS):
        idx = dest2[:, sp * t_split:(sp + 1) * t_split].reshape(-1)
        yk = _sc_gather_rows(y_tiles, idx, "moe_combine_sc")
        out = _combine_call(h, yk.reshape(TOP_K * t_split * ROW_SUBLANES, LANES), wts,
                            ln2_g[0][None, :], ln2_b[0][None, :], out, sp, t_len)
    return out.reshape(bsz, s_len, d)
```

```python
import functools

import jax
import jax.numpy as jnp
import numpy as np
from jax import lax
from jax.experimental import pallas as pl
from jax.experimental.pallas import tpu as pltpu
from jax.experimental.pallas import tpu_sc as plsc

F32 = jnp.float32
BF16 = jnp.bfloat16
U32 = jnp.uint32

D_MODEL = 1024
RET_HEADS = 4
RET_HEAD_DIM = 128
RET_WIDTH = RET_HEADS * RET_HEAD_DIM
CHUNK = 128
SWA_HEADS = 8
SWA_KV_HEADS = 2
SWA_GROUP = SWA_HEADS // SWA_KV_HEADS
SWA_HEAD_DIM = 64
SWA_WIDTH = SWA_HEADS * SWA_HEAD_DIM
SWA_KV_WIDTH = SWA_KV_HEADS * SWA_HEAD_DIM
IN_WIDTH = 4 * RET_WIDTH + SWA_WIDTH + 2 * SWA_KV_WIDTH
N_GROUPS = 4
EXPERTS_PER_GROUP = 8
N_EXPERTS = N_GROUPS * EXPERTS_PER_GROUP
TOP_K = 2
D_EXPERT = 512
LN_EPS = 1e-5
GN_EPS = 1e-6
DEPTH = 1
DEEPNORM_ALPHA = (2 * DEPTH) ** 0.25
NEG = -1e30

LANES = 128
ROW_SUBLANES = D_MODEL // 2 // LANES
SEQ_TILE = 1024
PROJ_ROWS = 256
MOE_ROWS = 512
COMBINE_TILE = 512
TOKEN_SPLITS = 2
SLOT_PARTS = 2
SC_CORES = 2
SC_SUBCORES = 16
SC_WORKERS = SC_CORES * SC_SUBCORES
SC_LANES = 16
SC_DEST_PIECES = 8
SC_SCAN_UNROLL = 8
SC_CHUNK_ROWS = 64
VMEM_LIMIT = 56 * 1024 * 1024

_QR, _KR, _VR, _GR = 0, RET_WIDTH, 2 * RET_WIDTH, 3 * RET_WIDTH
_QA = 4 * RET_WIDTH
_KA = _QA + SWA_WIDTH
_VA = _KA + SWA_KV_WIDTH
SWA_VREP = SWA_GROUP * SWA_HEAD_DIM
IN_WIDTH_TILED = _VA + SWA_KV_HEADS * SWA_VREP
ROUTE_ROWS = 8
GROUP_LANE0 = N_EXPERTS


def _dot(a, b):
    return jnp.dot(a, b, preferred_element_type=F32)


def _dot_nt(a, b):
    return lax.dot_general(a, b, (((1,), (1,)), ((), ())), preferred_element_type=F32)


def _dot_tn(a, b):
    return lax.dot_general(a, b, (((0,), (0,)), ((), ())), preferred_element_type=F32)


def _layer_norm(z, g, b):
    mu = jnp.mean(z, axis=-1, keepdims=True)
    zc = z - mu
    var = jnp.mean(zc * zc, axis=-1, keepdims=True)
    return zc * lax.rsqrt(var + LN_EPS) * g + b


def _silu(g):
    return g / (1.0 + jnp.exp(-g))


ROW_BYTES = 4 * ROW_SUBLANES * LANES


def _mixer_cost(t):
    proj = 2 * t * D_MODEL * (IN_WIDTH_TILED + D_MODEL + 3 * LANES)
    retention = RET_HEADS * 4 * 2 * t * CHUNK * RET_HEAD_DIM
    swa = SWA_KV_HEADS * 2 * t * SWA_GROUP * 2 * CHUNK * (SWA_HEAD_DIM + 2 * SWA_VREP)
    weights = 2 * D_MODEL * (IN_WIDTH_TILED + D_MODEL + 2 * LANES)
    return pl.CostEstimate(flops=proj + retention + swa + 2 * t * SEQ_TILE * N_EXPERTS,
                           transcendentals=t * (SWA_HEADS * 2 * CHUNK + RET_WIDTH + LANES),
                           bytes_accessed=(2 * 4 * D_MODEL + ROW_BYTES) * t + weights)


def _expert_cost(n_rows):
    return pl.CostEstimate(flops=2 * 3 * n_rows * D_MODEL * D_EXPERT, transcendentals=n_rows * D_EXPERT,
                           bytes_accessed=2 * ROW_BYTES * n_rows + 4 * 3 * N_EXPERTS * D_MODEL * D_EXPERT)


def _row_move_cost(n_rows):
    return pl.CostEstimate(flops=0, transcendentals=0, bytes_accessed=2 * ROW_BYTES * n_rows)


def _pack_rows(ref, val):
    n = val.shape[0]
    half = D_MODEL // 2
    hi = lax.bitcast_convert_type(val[:, :half].astype(BF16).astype(F32), U32)
    lo = lax.bitcast_convert_type(val[:, half:].astype(BF16).astype(F32), U32)
    word = hi | (lo >> 16)
    for j in range(ROW_SUBLANES):
        ref[pl.ds(j, n, stride=ROW_SUBLANES), :] = word[:, j * LANES:(j + 1) * LANES]


def _unpack_rows(ref):
    n = ref.shape[0] // ROW_SUBLANES
    word = jnp.concatenate([ref[pl.ds(j, n, stride=ROW_SUBLANES), :] for j in range(ROW_SUBLANES)], axis=1)
    hi = lax.bitcast_convert_type(word & jnp.uint32(0xFFFF0000), F32)
    lo = lax.bitcast_convert_type(word << 16, F32)
    return jnp.concatenate([hi, lo], axis=1)


def _mixer_kernel(cdec_ref, x_ref, w_in_ref, w_out_ref, gn_ref, ln_g_ref, ln_b_ref,
                  wr_cat_ref, br_ref, din_ref, qdec_ref, kdec_ref,
                  bias_ref, pen_ref, kmask_ref, vmask_ref, hlane_ref, tri_ref,
                  h_ref, hp_ref, ids_ref, wts_ref, cnt_ref,
                  state_scr, kprev_scr, vprev_scr, o_scr, carry_scr):
    b = pl.program_id(0)
    n = pl.program_id(1)
    ts = x_ref.shape[0]

    @pl.when(n == 0)
    def _():
        state_scr[...] = jnp.zeros_like(state_scr)
        kprev_scr[...] = jnp.zeros_like(kprev_scr)
        vprev_scr[...] = jnp.zeros_like(vprev_scr)

    @pl.when((b == 0) & (n == 0))
    def _():
        carry_scr[...] = jnp.zeros_like(carry_scr)

    x = x_ref[...]
    xb = x.astype(BF16)

    chunks_per_part = PROJ_ROWS // CHUNK
    parts = []

    def project_part(part):
        xp = xb[part * PROJ_ROWS:(part + 1) * PROJ_ROWS]
        proj = lambda lo, hi: _dot(xp, w_in_ref[:, lo:hi])
        parts.append(dict(
            q_r=proj(_QR, _KR), k_r=proj(_KR, _VR), v_r=proj(_VR, _GR), g_r=proj(_GR, _QA),
            q_a=proj(_QA, _KA), k_ab=proj(_KA, _VA).astype(BF16),
            v_rep=proj(_VA, _VA + SWA_KV_HEADS * SWA_VREP).astype(BF16)))

    project_part(0)

    def chunk_of(name, c):
        lo = (c % chunks_per_part) * CHUNK
        return parts[c // chunks_per_part][name][lo:lo + CHUNK]
    first_pen = jnp.where(n == 0, pen_ref[...], 0.0)

    for c in range(ts // CHUNK):
        rs = slice(c * CHUNK, (c + 1) * CHUNK)
        q_r, k_r, v_r, g_r = (chunk_of(name, c) for name in ("q_r", "k_r", "v_r", "g_r"))
        q_a, k_ab, v_rep = (chunk_of(name, c) for name in ("q_a", "k_ab", "v_rep"))
        for hd in range(RET_HEADS):
            cs = slice(hd * RET_HEAD_DIM, (hd + 1) * RET_HEAD_DIM)
            q = q_r[:, cs]
            k = k_r[:, cs]
            v = v_r[:, cs].astype(BF16)
            scores = _dot_nt(q.astype(BF16), k.astype(BF16)) * din_ref[hd]
            st = state_scr[hd]
            o = _dot(jnp.concatenate([scores.astype(BF16), (q * qdec_ref[hd]).astype(BF16)], axis=1),
                     jnp.concatenate([v, st.astype(BF16)], axis=0))
            kv = _dot_tn((k * kdec_ref[hd]).astype(BF16), v)
            state_scr[hd] = st * cdec_ref[hd] + kv
            mu = jnp.mean(o, axis=-1, keepdims=True)
            oc = o - mu
            var = jnp.mean(oc * oc, axis=-1, keepdims=True)
            on = oc * lax.rsqrt(var + GN_EPS) * gn_ref[:, cs] * _silu(g_r[:, cs])
            o_scr[rs, cs] = on.astype(BF16)
        for j in range(SWA_KV_HEADS):
            ks = slice(j * SWA_HEAD_DIM, (j + 1) * SWA_HEAD_DIM)
            vs = slice(j * SWA_VREP, (j + 1) * SWA_VREP)
            if c == 0:
                kp = kprev_scr[:, ks].astype(BF16)
                vp = vprev_scr[:, vs].astype(BF16)
            else:
                kp = chunk_of("k_ab", c - 1)[:, ks]
                vp = chunk_of("v_rep", c - 1)[:, vs]
            kp = kp * kmask_ref[...]
            kcat = jnp.concatenate([kp, k_ab[:, ks]], axis=0)
            vcat = jnp.concatenate([vp, v_rep[:, vs]], axis=0)
            q0 = j * SWA_GROUP * SWA_HEAD_DIM
            qs = jnp.concatenate(
                [q_a[:, q0 + g * SWA_HEAD_DIM:q0 + (g + 1) * SWA_HEAD_DIM] for g in range(SWA_GROUP)],
                axis=0)
            qs = (qs * (SWA_HEAD_DIM ** -0.5)).astype(BF16)
            s = _dot_nt(qs, kcat) + bias_ref[j]
            if c == 0:
                s = s + first_pen
            m = jnp.max(jnp.maximum(s[:, :CHUNK], s[:, CHUNK:]), axis=-1, keepdims=True)
            p32 = jnp.exp(s - m)
            den_col = jnp.sum(p32[:, :CHUNK] + p32[:, CHUNK:], axis=-1, keepdims=True)
            p = p32.astype(BF16)
            p_all = jnp.concatenate([p[g * CHUNK:(g + 1) * CHUNK] for g in range(SWA_GROUP)], axis=1)
            v_blk = jnp.concatenate([vcat * vmask_ref[g] for g in range(SWA_GROUP)], axis=0)
            num = _dot(p_all, v_blk)
            den = den_col[(SWA_GROUP - 1) * CHUNK:]
            for g in range(SWA_GROUP - 2, -1, -1):
                den = jnp.where(hlane_ref[...] == g, den_col[g * CHUNK:(g + 1) * CHUNK], den)
            c0 = RET_WIDTH + j * SWA_GROUP * SWA_HEAD_DIM
            o_scr[rs, c0:c0 + SWA_GROUP * SWA_HEAD_DIM] = (num / den).astype(BF16)
        if c % chunks_per_part == 0 and len(parts) < ts // PROJ_ROWS:
            project_part(len(parts))

    kprev_scr[...] = chunk_of("k_ab", ts // CHUNK - 1).astype(F32)
    vprev_scr[...] = chunk_of("v_rep", ts // CHUNK - 1).astype(F32)

    mix = _dot(o_scr[...], w_out_ref[...])
    h = _layer_norm(DEEPNORM_ALPHA * x + mix, ln_g_ref[...], ln_b_ref[...])
    h_ref[...] = h
    _pack_rows(hp_ref, h)

    h_hi = h.astype(BF16)
    h_lo = (h - h_hi.astype(F32)).astype(BF16)
    hi_terms = _dot(h_hi, wr_cat_ref[...])
    logits = hi_terms[:, :LANES] + hi_terms[:, LANES:] + _dot(h_lo, wr_cat_ref[:, :LANES]) + br_ref[...]
    lt = logits.T
    row = lax.broadcasted_iota(jnp.int32, (EXPERTS_PER_GROUP, ts), 0).astype(F32)
    big = 1e9
    ninf = -jnp.inf
    col_max = lambda v: jnp.max(v, axis=0, keepdims=True)
    first_at = lambda v, m: jnp.min(jnp.where(v == m, row, big), axis=0, keepdims=True)
    gl = jnp.where(row < N_GROUPS, lt[GROUP_LANE0:GROUP_LANE0 + EXPERTS_PER_GROUP], ninf)
    gmax = col_max(gl)
    gidx = first_at(gl, gmax)
    g_w = 1.0 / jnp.sum(jnp.exp(gl - gmax), axis=0, keepdims=True)
    el = lt[(N_GROUPS - 1) * EXPERTS_PER_GROUP:N_GROUPS * EXPERTS_PER_GROUP]
    for g in range(N_GROUPS - 2, -1, -1):
        el = jnp.where(gidx == g, lt[g * EXPERTS_PER_GROUP:(g + 1) * EXPERTS_PER_GROUP], el)
    m1 = col_max(el)
    i1 = first_at(el, m1)
    el2 = jnp.where(row == i1, ninf, el)
    m2 = col_max(el2)
    i2 = first_at(el2, m2)
    t = jnp.exp(m2 - m1)
    w1 = g_w / (1.0 + t)
    w2 = g_w * t / (1.0 + t)
    e1 = gidx * EXPERTS_PER_GROUP + i1
    e2 = gidx * EXPERTS_PER_GROUP + i2
    erow = lax.broadcasted_iota(jnp.int32, (N_EXPERTS, ts), 0).astype(F32)
    hit1 = erow == e1
    hit2 = erow == e2
    onehot = (hit1 | hit2).astype(BF16)
    prefix = _dot(onehot, tri_ref[...]) + carry_scr[:, 0:1]
    r1 = jnp.sum(jnp.where(hit1, prefix, 0.0), axis=0, keepdims=True)
    r2 = jnp.sum(jnp.where(hit2, prefix, 0.0), axis=0, keepdims=True)
    carry = carry_scr[:, 0:1] + jnp.sum(onehot.astype(F32), axis=1, keepdims=True)
    carry_scr[...] = jnp.broadcast_to(carry, carry_scr.shape)
    cnt_ref[...] = jnp.broadcast_to(carry, cnt_ref.shape)
    pick = lambda k, v, rest: jnp.where(row == k, v, rest)
    ids_ref[...] = pick(0, e1, pick(1, e2, pick(2, r1, pick(3, r2, 0.0)))).astype(jnp.int32)
    wts_ref[...] = pick(0, w1, pick(1, w2, 0.0))


def _mixer_call(x2, w_in, w_out, gn, ln_g, ln_b, wr_cat, br, consts, bsz, s_len, first_seq):
    cdec, din, qdec, kdec, bias, pen, kmask, vmask, hlane, tri = consts
    t_len = bsz * s_len
    ns = s_len // SEQ_TILE
    tok = lambda b, n, *_: (b * ns + n, 0)
    tok_in = lambda b, n, *_: ((first_seq + b) * ns + n, 0)
    tok_t = lambda b, n, *_: (0, b * ns + n)
    full2 = lambda b, n, *_: (0, 0)
    full3 = lambda b, n, *_: (0, 0, 0)
    grid_spec = pltpu.PrefetchScalarGridSpec(
        num_scalar_prefetch=1,
        grid=(bsz, ns),
        in_specs=[
            pl.BlockSpec((SEQ_TILE, D_MODEL), tok_in),
            pl.BlockSpec((D_MODEL, IN_WIDTH_TILED), full2),
            pl.BlockSpec((D_MODEL, D_MODEL), full2),
            pl.BlockSpec((1, RET_WIDTH), full2),
            pl.BlockSpec((1, D_MODEL), full2),
            pl.BlockSpec((1, D_MODEL), full2),
            pl.BlockSpec((D_MODEL, 2 * LANES), full2),
            pl.BlockSpec((1, LANES), full2),
            pl.BlockSpec((RET_HEADS, CHUNK, CHUNK), full3),
            pl.BlockSpec((RET_HEADS, CHUNK, RET_HEAD_DIM), full3),
            pl.BlockSpec((RET_HEADS, CHUNK, RET_HEAD_DIM), full3),
            pl.BlockSpec((SWA_KV_HEADS, SWA_GROUP * CHUNK, 2 * CHUNK), full3),
            pl.BlockSpec((1, 2 * CHUNK), full2),
            pl.BlockSpec((CHUNK, SWA_HEAD_DIM), full2),
            pl.BlockSpec((SWA_GROUP, 2 * CHUNK, SWA_VREP), full3),
            pl.BlockSpec((1, SWA_VREP), full2),
            pl.BlockSpec((SEQ_TILE, SEQ_TILE), full2),
        ],
        out_specs=[
            pl.BlockSpec((SEQ_TILE, D_MODEL), tok),
            pl.BlockSpec((SEQ_TILE * ROW_SUBLANES, LANES), tok),
            pl.BlockSpec((ROUTE_ROWS, SEQ_TILE), tok_t),
            pl.BlockSpec((ROUTE_ROWS, SEQ_TILE), tok_t),
            pl.BlockSpec((N_EXPERTS, LANES), full2),
        ],
        scratch_shapes=[
            pltpu.VMEM((RET_HEADS, RET_HEAD_DIM, RET_HEAD_DIM), F32),
            pltpu.VMEM((CHUNK, SWA_KV_WIDTH), F32),
            pltpu.VMEM((CHUNK, SWA_KV_HEADS * SWA_VREP), F32),
            pltpu.VMEM((SEQ_TILE, D_MODEL), BF16),
            pltpu.VMEM((N_EXPERTS, LANES), F32),
        ],
    )
    return pl.pallas_call(
        _mixer_kernel,
        grid_spec=grid_spec,
        out_shape=[
            jax.ShapeDtypeStruct((t_len, D_MODEL), F32),
            jax.ShapeDtypeStruct((t_len * ROW_SUBLANES, LANES), U32),
            jax.ShapeDtypeStruct((ROUTE_ROWS, t_len), jnp.int32),
            jax.ShapeDtypeStruct((ROUTE_ROWS, t_len), F32),
            jax.ShapeDtypeStruct((N_EXPERTS, LANES), F32),
        ],
        compiler_params=pltpu.CompilerParams(
            dimension_semantics=("arbitrary", "arbitrary"), vmem_limit_bytes=VMEM_LIMIT),
        name="mixer_router",
        cost_estimate=_mixer_cost(t_len),
    )(cdec, x2, w_in, w_out, gn, ln_g, ln_b, wr_cat, br, din, qdec, kdec, bias, pen, kmask, vmask, hlane, tri)


def _sc_mesh():
    return plsc.VectorSubcoreMesh(core_axis_name="core", subcore_axis_name="subcore")


def _sc_stream_rows(table_hbm, idx_v, out_hbm, out_base, n_chunks, bufs, sems):
    assert n_chunks % 2 == 0

    def gather(j, b):
        off = pl.multiple_of(j * SC_CHUNK_ROWS, SC_CHUNK_ROWS)
        return pltpu.make_async_copy(table_hbm.at[idx_v.at[pl.ds(off, SC_CHUNK_ROWS)]], bufs[b], sems[b])

    gather(0, 0).start()

    @pl.loop(0, n_chunks, step=2)
    def _(j0):
        for b in range(2):
            j = j0 + b
            gather(j, b).wait()

            @pl.when(j + 1 < n_chunks)
            def _():
                gather(j + 1, 1 - b).start()

            off = pl.multiple_of(j * SC_CHUNK_ROWS, SC_CHUNK_ROWS)
            pltpu.sync_copy(bufs[b], out_hbm.at[pl.ds(out_base + off, SC_CHUNK_ROWS)])


def _sc_dispatch(h_rows, dest, slot0, n_slots):
    t_len = h_rows.shape[0]
    n_assign = dest.shape[0]
    per_worker = n_slots // SC_WORKERS
    n_chunks = per_worker // SC_CHUNK_ROWS
    assert n_chunks * SC_CHUNK_ROWS * SC_WORKERS == n_slots and slot0 + n_slots <= 3 * t_len
    assert per_worker % SC_LANES == 0 and t_len % (SC_LANES * SC_SCAN_UNROLL) == 0 and n_assign == TOP_K * t_len

    @functools.partial(
        pl.kernel, mesh=_sc_mesh(), name="moe_dispatch_sc", cost_estimate=_row_move_cost(n_slots),
        compiler_params=pltpu.CompilerParams(needs_layout_passes=False),
        out_type=jax.ShapeDtypeStruct((n_slots, ROW_SUBLANES, LANES), U32),
        scratch_types=[pltpu.VMEM((n_assign,), jnp.int32),
                       pltpu.VMEM((per_worker,), jnp.int32),
                       pltpu.VMEM((SC_CHUNK_ROWS, ROW_SUBLANES, LANES), U32),
                       pltpu.VMEM((SC_CHUNK_ROWS, ROW_SUBLANES, LANES), U32),
                       pltpu.SemaphoreType.DMA, pltpu.SemaphoreType.DMA])
    def dispatch(h_hbm, dest_hbm, rows_hbm, dest_v, src_v, buf0, buf1, sem, sem1):
        wid = lax.axis_index("subcore") * SC_CORES + lax.axis_index("core")
        local = wid * per_worker
        base = slot0 + local
        piece = n_assign // SC_DEST_PIECES
        copies = []
        for c in range(SC_DEST_PIECES):
            off = pl.multiple_of(lax.rem(c + wid, SC_DEST_PIECES) * piece, SC_LANES)
            copies.append(pltpu.async_copy(dest_hbm.at[pl.ds(off, piece)], dest_v.at[pl.ds(off, piece)], sem))
        for cp in copies:
            cp.wait()
        lane = lax.iota(jnp.int32, SC_LANES)

        def wrap(a):
            a = jnp.where(a >= t_len, a - t_len, a)
            return jnp.where(a >= t_len, a - t_len, a)

        @pl.loop(0, per_worker // SC_LANES)
        def _(i):
            src_v[pl.ds(i * SC_LANES, SC_LANES)] = wrap(base + i * SC_LANES + lane)

        for k in range(TOP_K):
            @pl.loop(0, t_len // (SC_LANES * SC_SCAN_UNROLL))
            def _(i):
                for u in range(SC_SCAN_UNROLL):
                    tok0 = (i * SC_SCAN_UNROLL + u) * SC_LANES
                    d = dest_v[pl.ds(k * t_len + tok0, SC_LANES)] - base
                    hit = (d >= 0) & (d < per_worker)
                    plsc.store_scatter(src_v, [jnp.where(hit, d, 0)], tok0 + lane, mask=hit)

        _sc_stream_rows(h_hbm, src_v, rows_hbm, local, n_chunks, (buf0, buf1), (sem, sem1))

    return dispatch(h_rows, dest)


def _sc_gather_rows(table, idx, name):
    m = idx.shape[0]
    per_worker = m // (SC_CHUNK_ROWS * SC_WORKERS)
    assert per_worker * SC_CHUNK_ROWS * SC_WORKERS == m

    @functools.partial(
        pl.kernel, mesh=_sc_mesh(), name=name, cost_estimate=_row_move_cost(m),
        out_type=jax.ShapeDtypeStruct((m, ROW_SUBLANES, LANES), U32),
        scratch_types=[pltpu.VMEM((per_worker * SC_CHUNK_ROWS,), jnp.int32),
                       pltpu.VMEM((SC_CHUNK_ROWS, ROW_SUBLANES, LANES), U32),
                       pltpu.VMEM((SC_CHUNK_ROWS, ROW_SUBLANES, LANES), U32),
                       pltpu.SemaphoreType.DMA, pltpu.SemaphoreType.DMA])
    def gather(table_hbm, idx_hbm, out_hbm, idx_v, buf0, buf1, sem0, sem1):
        wid = lax.axis_index("subcore") * SC_CORES + lax.axis_index("core")
        base = pl.multiple_of(wid * (per_worker * SC_CHUNK_ROWS), SC_CHUNK_ROWS)
        pltpu.sync_copy(idx_hbm.at[pl.ds(base, per_worker * SC_CHUNK_ROWS)], idx_v)

        _sc_stream_rows(table_hbm, idx_v, out_hbm, base, per_worker, (buf0, buf1), (sem0, sem1))

    return gather(table, idx)


def _expert_kernel(bexp_ref, nused_ref, nxt_ref, slot_ref, rows_ref, wg_hbm, wu_hbm, wd_hbm, *rest, first_block):
    y_ref, stage_g, stage_u, stage_d, wg_s, wu_s, wd_s, sem = rest[-8:]
    p = pl.program_id(0)
    blk = first_block + p
    expert = bexp_ref[blk]
    slot = slot_ref[blk]

    def weight_copies(e, s):
        return [pltpu.make_async_copy(w.at[e], stage.at[s], sem.at[s, i])
                for i, (w, stage) in enumerate(((wg_hbm, stage_g), (wu_hbm, stage_u), (wd_hbm, stage_d)))]

    @pl.when(p == 0)
    def _():
        for cp in weight_copies(expert, slot):
            cp.start()

    @pl.when((p == 0) | (expert != bexp_ref[jnp.maximum(blk - 1, 0)]))
    def _():
        for cp in weight_copies(expert, slot):
            cp.wait()
        wg_s[...] = stage_g[slot].astype(BF16)
        wu_s[...] = stage_u[slot].astype(BF16)
        wd_s[...] = stage_d[slot].astype(BF16)

        @pl.when(nxt_ref[blk] >= 0)
        def _():
            for cp in weight_copies(nxt_ref[blk], 1 - slot):
                cp.start()

    @pl.when(blk < nused_ref[0])
    def _():
        xb = _unpack_rows(rows_ref).astype(BF16)
        g = _dot(xb, wg_s[...])
        u = _dot(xb, wu_s[...])
        a = (_silu(g) * u).astype(BF16)
        _pack_rows(y_ref, _dot(a, wd_s[...]))

    @pl.when(blk >= nused_ref[0])
    def _():
        y_ref[...] = jnp.zeros_like(y_ref)


def _expert_call(bexp, nused, seg_end, rows, w_gate, w_up, w_down, y_prev, part, n_rows):
    part_blocks = rows.shape[0] // (ROW_SUBLANES * MOE_ROWS)
    n_blocks = bexp.shape[0]
    first_block = part * part_blocks
    limit = jnp.minimum(nused[0], first_block + part_blocks)
    nxt = jnp.where(seg_end < limit, bexp[jnp.minimum(seg_end, n_blocks - 1)], -1).astype(jnp.int32)
    starts = jnp.concatenate([jnp.ones((1,), jnp.int32), (bexp[1:] != bexp[:-1]).astype(jnp.int32)])
    slot = (jnp.cumsum(starts) % 2).astype(jnp.int32)
    last_used = lambda p, nu: jnp.clip(first_block + p, first_block, jnp.maximum(nu[0] - 1, first_block))
    blk = lambda p, be, nu, nx, sl: (last_used(p, nu) - first_block, 0)
    hbm = pl.BlockSpec(memory_space=pl.ANY)
    in_specs = [pl.BlockSpec((MOE_ROWS * ROW_SUBLANES, LANES), blk), hbm, hbm, hbm]
    args = [rows, w_gate, w_up, w_down]
    aliases = {}
    if y_prev is not None:
        in_specs.append(hbm)
        args.append(y_prev)
        aliases = {4 + len(args) - 1: 0}
    grid_spec = pltpu.PrefetchScalarGridSpec(
        num_scalar_prefetch=4,
        grid=(part_blocks,),
        in_specs=in_specs,
        out_specs=pl.BlockSpec((MOE_ROWS * ROW_SUBLANES, LANES), lambda p, be, nu, nx, sl: (first_block + p, 0)),
        scratch_shapes=[
            pltpu.VMEM((2, D_MODEL, D_EXPERT), F32),
            pltpu.VMEM((2, D_MODEL, D_EXPERT), F32),
            pltpu.VMEM((2, D_EXPERT, D_MODEL), F32),
            pltpu.VMEM((D_MODEL, D_EXPERT), BF16),
            pltpu.VMEM((D_MODEL, D_EXPERT), BF16),
            pltpu.VMEM((D_EXPERT, D_MODEL), BF16),
            pltpu.SemaphoreType.DMA((2, 3)),
        ],
    )
    return pl.pallas_call(
        functools.partial(_expert_kernel, first_block=first_block),
        grid_spec=grid_spec,
        out_shape=jax.ShapeDtypeStruct((n_rows * ROW_SUBLANES, LANES), U32),
        input_output_aliases=aliases,
        compiler_params=pltpu.CompilerParams(
            dimension_semantics=("arbitrary",), vmem_limit_bytes=VMEM_LIMIT),
        name="moe_experts",
        cost_estimate=_expert_cost(part_blocks * MOE_ROWS),
    )(bexp, nused, nxt, slot, *args)


def _combine_kernel(h_ref, y0_ref, y1_ref, wts_ref, ln_g_ref, ln_b_ref, *rest):
    out_ref = rest[-1]
    wts = wts_ref[...].T
    ffn = _unpack_rows(y0_ref) * wts[:, 0:1] + _unpack_rows(y1_ref) * wts[:, 1:2]
    out_ref[...] = _layer_norm(DEEPNORM_ALPHA * h_ref[...] + ffn, ln_g_ref[...], ln_b_ref[...])


def _combine_call(h, yk, wts, ln_g, ln_b, out_prev, split, t_total):
    t_len = t_total // TOKEN_SPLITS
    n_tiles = t_len // COMBINE_TILE
    tok = lambda i: (i, 0)
    tok_split = lambda i: (i + split * n_tiles, 0)
    full2 = lambda i: (0, 0)
    tiles = pl.BlockSpec((COMBINE_TILE * ROW_SUBLANES, LANES), tok)
    tiles_k1 = pl.BlockSpec((COMBINE_TILE * ROW_SUBLANES, LANES), lambda i: (i + n_tiles, 0))
    in_specs = [pl.BlockSpec((COMBINE_TILE, D_MODEL), tok_split), tiles, tiles_k1,
                pl.BlockSpec((ROUTE_ROWS, COMBINE_TILE), lambda i: (0, i + split * n_tiles)),
                pl.BlockSpec((1, D_MODEL), full2),
                pl.BlockSpec((1, D_MODEL), full2)]
    args = [h, yk, yk, wts, ln_g, ln_b]
    aliases = {}
    if out_prev is not None:
        in_specs.append(pl.BlockSpec(memory_space=pl.ANY))
        args.append(out_prev)
        aliases = {len(args) - 1: 0}
    return pl.pallas_call(
        _combine_kernel,
        grid=(n_tiles,),
        in_specs=in_specs,
        out_specs=pl.BlockSpec((COMBINE_TILE, D_MODEL), tok_split),
        out_shape=jax.ShapeDtypeStruct((t_total, D_MODEL), F32),
        input_output_aliases=aliases,
        compiler_params=pltpu.CompilerParams(
            dimension_semantics=("arbitrary",), vmem_limit_bytes=VMEM_LIMIT),
        name="moe_combine_ln",
        cost_estimate=pl.CostEstimate(flops=8 * t_len * D_MODEL, transcendentals=t_len,
                                      bytes_accessed=(2 * 4 * D_MODEL + TOP_K * ROW_BYTES) * t_len),
    )(*args)


def _position_tables():
    c = CHUNK
    f = np.float32
    log_g = np.log1p(-np.exp2(-5.0 - np.arange(RET_HEADS, dtype=f))).astype(f)
    idx = np.arange(c, dtype=f)
    diff = idx[:, None] - idx[None, :]
    scale = f(RET_HEAD_DIM ** -0.5)
    din = np.where(diff >= 0, np.exp(log_g[:, None, None] * np.maximum(diff, 0.0)), 0.0).astype(f) * scale
    qdec = np.broadcast_to(np.exp(log_g[:, None] * (idx + 1.0))[:, :, None], (RET_HEADS, c, RET_HEAD_DIM))
    kdec = np.broadcast_to((np.exp(log_g[:, None] * (c - 1.0 - idx)) * scale)[:, :, None],
                           (RET_HEADS, c, RET_HEAD_DIM))
    cdec = np.exp(log_g * c)
    slopes = np.exp2(-8.0 * (np.arange(SWA_HEADS, dtype=f) + 1.0) / SWA_HEADS).astype(f)
    r = np.arange(c)[:, None]
    col = np.arange(c)[None, :]
    dist_prev = (r - col + c).astype(f)
    dist_cur = (r - col).astype(f)
    bprev = np.where((r < col)[None], -slopes[:, None, None] * dist_prev[None], NEG)
    bcur = np.where((r >= col)[None], -slopes[:, None, None] * dist_cur[None], NEG)
    bias = np.concatenate([bprev, bcur], axis=-1).reshape(SWA_KV_HEADS, SWA_GROUP * c, 2 * c)
    key = np.arange(2 * c)
    pen = np.where((key >= 1) & (key < c), NEG, 0.0)[None, :]
    lane_head = np.arange(SWA_VREP) // SWA_HEAD_DIM
    own = lane_head[None, None, :] == np.arange(SWA_GROUP)[:, None, None]
    kmask = np.broadcast_to(np.arange(c)[:, None] > 0, (c, SWA_HEAD_DIM))
    vmask = own & (key[None, :, None] > 0)
    hlane = lane_head[None, :]
    tr = np.arange(SEQ_TILE)
    tri = tr[:, None] < tr[None, :]
    as_f32 = lambda v: np.ascontiguousarray(v, dtype=f)
    as_bf16 = lambda v: np.ascontiguousarray(v, dtype=f).astype(BF16)
    return (as_f32(cdec), as_f32(din), as_f32(qdec), as_f32(kdec), as_f32(bias), as_f32(pen),
            as_bf16(kmask), as_bf16(vmask), as_f32(hlane), as_bf16(tri))


def _mixer_constants(attn_sinks):
    cdec, din, qdec, kdec, bias, pen, kmask, vmask, hlane, tri = _position_tables()
    sink = attn_sinks.astype(F32).reshape(SWA_KV_HEADS, SWA_GROUP, 1, 1)
    sink = jnp.broadcast_to(sink, (SWA_KV_HEADS, SWA_GROUP, CHUNK, 1)).reshape(SWA_KV_HEADS, SWA_GROUP * CHUNK, 1)
    is_slot = (np.arange(2 * CHUNK) == 0)[None, None, :]
    return cdec, din, qdec, kdec, jnp.where(is_slot, sink, bias), pen, kmask, vmask, hlane, tri


def _tile_v_columns(w_in):
    v_cols = w_in[:, _VA:].reshape(D_MODEL, SWA_KV_HEADS, 1, SWA_HEAD_DIM)
    v_cols = jnp.broadcast_to(v_cols, (D_MODEL, SWA_KV_HEADS, SWA_GROUP, SWA_HEAD_DIM))
    return jnp.concatenate([w_in[:, :_VA], v_cols.reshape(D_MODEL, SWA_KV_HEADS * SWA_VREP)], axis=1)


def _router_tables(w_group_router, b_group_router, w_expert_router, b_expert_router):
    w_e = jnp.transpose(w_expert_router, (1, 0, 2)).reshape(D_MODEL, N_EXPERTS)
    w = jnp.concatenate([w_e, w_group_router,
                         jnp.zeros((D_MODEL, LANES - N_EXPERTS - N_GROUPS), F32)], axis=1)
    bias = jnp.concatenate([b_expert_router.reshape(N_EXPERTS), b_group_router,
                            jnp.zeros((LANES - N_EXPERTS - N_GROUPS,), F32)])[None, :]
    w_hi = w.astype(BF16)
    w_lo = (w - w_hi.astype(F32)).astype(BF16)
    return jnp.concatenate([w_hi, w_lo], axis=1), bias


def kernel(x, w_in, ret_gn_g, attn_sinks, w_out, ln1_g, ln1_b, w_group_router, b_group_router,
           w_expert_router, b_expert_router, w_gate, w_up, w_down, ln2_g, ln2_b):
    bsz, s_len, d = x.shape
    assert d == D_MODEL and s_len % SEQ_TILE == 0 and w_in.shape[0] == DEPTH == 1
    assert bsz % TOKEN_SPLITS == 0
    t_len = bsz * s_len
    t_split = t_len // TOKEN_SPLITS
    n_blocks = t_len * TOP_K // MOE_ROWS + N_EXPERTS
    n_rows = n_blocks * MOE_ROWS
    part_rows = n_rows // SLOT_PARTS

    consts = _mixer_constants(attn_sinks[0])
    wr_cat, br = _router_tables(w_group_router[0], b_group_router[0],
                                      w_expert_router[0], b_expert_router[0])
    h, h_packed, ids, wts, cnt = _mixer_call(
        x.reshape(t_len, d), _tile_v_columns(w_in[0]).astype(BF16), w_out[0].astype(BF16),
        ret_gn_g[0][None, :], ln1_g[0][None, :], ln1_b[0][None, :], wr_cat, br, consts, bsz, s_len, 0)

    counts = cnt[:, 0].astype(jnp.int32)
    padded = (counts + MOE_ROWS - 1) // MOE_ROWS * MOE_ROWS
    pend = jnp.cumsum(padded)
    pstart = pend - padded
    onehot = ids[0:TOP_K, :, None] == jnp.arange(N_EXPERTS, dtype=jnp.int32)
    dest2 = jnp.sum(jnp.where(onehot, pstart, 0), axis=-1) + ids[TOP_K:2 * TOP_K]
    dest = dest2.reshape(-1)
    nused = (pend[-1:] // MOE_ROWS).astype(jnp.int32)
    blk_start = jnp.minimum(jnp.arange(n_blocks, dtype=jnp.int32), nused[0] - 1) * MOE_ROWS
    bexp = jnp.minimum(jnp.sum(pend[None, :] <= blk_start[:, None], axis=-1), N_EXPERTS - 1).astype(jnp.int32)
    seg_end = jnp.sum(jnp.where(bexp[:, None] == jnp.arange(N_EXPERTS, dtype=jnp.int32), pend // MOE_ROWS, 0),
                      axis=-1).astype(jnp.int32)

    h_tiles = h_packed.reshape(t_len, ROW_SUBLANES, LANES)
    y = None
    for part in range(SLOT_PARTS):
        rows = _sc_dispatch(h_tiles, dest, part * part_rows, part_rows)
        y = _expert_call(bexp, nused, seg_end, rows.reshape(part_rows * ROW_SUBLANES, LANES),
                         w_gate[0], w_up[0], w_down[0], y, part, n_rows)

    y_tiles = y.reshape(n_rows, ROW_SUBLANES, LANES)
    out = None
    for sp in range(TOKEN_SPLITS):
        idx = dest2[:, sp * t_split:(sp + 1) * t_split].reshape(-1)
        yk = _sc_gather_rows(y_tiles, idx, "moe_combine_sc")
        out = _combine_call(h, yk.reshape(TOP_K * t_split * ROW_SUBLANES, LANES), wts,
                            ln2_g[0][None, :], ln2_b[0][None, :], out, sp, t_len)
    return out.reshape(bsz, s_len, d)
```

```python
import functools

import jax
import jax.numpy as jnp
import numpy as np
from jax import lax
from jax.experimental import pallas as pl
from jax.experimental.pallas import tpu as pltpu
from jax.experimental.pallas import tpu_sc as plsc

F32 = jnp.float32
BF16 = jnp.bfloat16
U32 = jnp.uint32

D_MODEL = 1024
RET_HEADS = 4
RET_HEAD_DIM = 128
RET_WIDTH = RET_HEADS * RET_HEAD_DIM
CHUNK = 128
SWA_HEADS = 8
SWA_KV_HEADS = 2
SWA_GROUP = SWA_HEADS // SWA_KV_HEADS
SWA_HEAD_DIM = 64
SWA_WIDTH = SWA_HEADS * SWA_HEAD_DIM
SWA_KV_WIDTH = SWA_KV_HEADS * SWA_HEAD_DIM
IN_WIDTH = 4 * RET_WIDTH + SWA_WIDTH + 2 * SWA_KV_WIDTH
N_GROUPS = 4
EXPERTS_PER_GROUP = 8
N_EXPERTS = N_GROUPS * EXPERTS_PER_GROUP
TOP_K = 2
D_EXPERT = 512
LN_EPS = 1e-5
GN_EPS = 1e-6
DEPTH = 1
DEEPNORM_ALPHA = (2 * DEPTH) ** 0.25
NEG = -1e30

LANES = 128
ROW_SUBLANES = D_MODEL // 2 // LANES
SEQ_TILE = 1024
PROJ_ROWS = 256
MOE_ROWS = 512
COMBINE_TILE = 512
TOKEN_SPLITS = 2
SLOT_PARTS = 2
SC_CORES = 2
SC_SUBCORES = 16
SC_WORKERS = SC_CORES * SC_SUBCORES
SC_LANES = 16
SC_DEST_PIECES = 8
SC_SCAN_UNROLL = 8
SC_CHUNK_ROWS = 64
VMEM_LIMIT = 56 * 1024 * 1024

_QR, _KR, _VR, _GR = 0, RET_WIDTH, 2 * RET_WIDTH, 3 * RET_WIDTH
_QA = 4 * RET_WIDTH
_KA = _QA + SWA_WIDTH
_VA = _KA + SWA_KV_WIDTH
SWA_VREP = SWA_GROUP * SWA_HEAD_DIM
ROUTE_ROWS = 8
GROUP_LANE0 = N_EXPERTS


def _dot(a, b):
    return jnp.dot(a, b, preferred_element_type=F32)


def _dot_nt(a, b):
    return lax.dot_general(a, b, (((1,), (1,)), ((), ())), preferred_element_type=F32)


def _dot_tn(a, b):
    return lax.dot_general(a, b, (((0,), (0,)), ((), ())), preferred_element_type=F32)


def _layer_norm(z, g, b):
    mu = jnp.mean(z, axis=-1, keepdims=True)
    zc = z - mu
    var = jnp.mean(zc * zc, axis=-1, keepdims=True)
    return zc * lax.rsqrt(var + LN_EPS) * g + b


def _silu(g):
    return g / (1.0 + jnp.exp(-g))


def _tile_heads(v):
    return jnp.concatenate([v[:, j * SWA_HEAD_DIM:(j + 1) * SWA_HEAD_DIM]
                            for j in range(SWA_KV_HEADS) for _ in range(SWA_GROUP)], axis=1)


ROW_BYTES = 4 * ROW_SUBLANES * LANES


def _mixer_cost(t):
    proj = 2 * t * D_MODEL * (IN_WIDTH + D_MODEL + 3 * LANES)
    retention = RET_HEADS * 4 * 2 * t * CHUNK * RET_HEAD_DIM
    swa = SWA_KV_HEADS * 2 * t * SWA_GROUP * 2 * CHUNK * (SWA_HEAD_DIM + 2 * SWA_VREP)
    weights = 2 * D_MODEL * (IN_WIDTH + D_MODEL + 2 * LANES)
    return pl.CostEstimate(flops=proj + retention + swa + 2 * t * SEQ_TILE * N_EXPERTS,
                           transcendentals=t * (SWA_HEADS * 2 * CHUNK + RET_WIDTH + LANES),
                           bytes_accessed=(2 * 4 * D_MODEL + ROW_BYTES) * t + weights)


def _expert_cost(n_rows):
    return pl.CostEstimate(flops=2 * 3 * n_rows * D_MODEL * D_EXPERT, transcendentals=n_rows * D_EXPERT,
                           bytes_accessed=2 * ROW_BYTES * n_rows + 4 * 3 * N_EXPERTS * D_MODEL * D_EXPERT)


def _row_move_cost(n_rows):
    return pl.CostEstimate(flops=0, transcendentals=0, bytes_accessed=2 * ROW_BYTES * n_rows)


def _pack_rows(ref, val):
    n = val.shape[0]
    half = D_MODEL // 2
    hi = lax.bitcast_convert_type(val[:, :half].astype(BF16).astype(F32), U32)
    lo = lax.bitcast_convert_type(val[:, half:].astype(BF16).astype(F32), U32)
    word = hi | (lo >> 16)
    for j in range(ROW_SUBLANES):
        ref[pl.ds(j, n, stride=ROW_SUBLANES), :] = word[:, j * LANES:(j + 1) * LANES]


def _unpack_rows(ref):
    n = ref.shape[0] // ROW_SUBLANES
    word = jnp.concatenate([ref[pl.ds(j, n, stride=ROW_SUBLANES), :] for j in range(ROW_SUBLANES)], axis=1)
    hi = lax.bitcast_convert_type(word & jnp.uint32(0xFFFF0000), F32)
    lo = lax.bitcast_convert_type(word << 16, F32)
    return jnp.concatenate([hi, lo], axis=1)


def _mixer_kernel(cdec_ref, x_ref, w_in_ref, w_out_ref, gn_ref, ln_g_ref, ln_b_ref,
                  wr_cat_ref, br_ref, din_ref, qdec_ref, kdec_ref,
                  bias_ref, pen_ref, kmask_ref, vmask_ref, hlane_ref, tri_ref,
                  h_ref, hp_ref, ids_ref, wts_ref, cnt_ref,
                  state_scr, kprev_scr, vprev_scr, o_scr, carry_scr):
    b = pl.program_id(0)
    n = pl.program_id(1)
    ts = x_ref.shape[0]

    @pl.when(n == 0)
    def _():
        state_scr[...] = jnp.zeros_like(state_scr)
        kprev_scr[...] = jnp.zeros_like(kprev_scr)
        vprev_scr[...] = jnp.zeros_like(vprev_scr)

    @pl.when((b == 0) & (n == 0))
    def _():
        carry_scr[...] = jnp.zeros_like(carry_scr)

    x = x_ref[...]
    xb = x.astype(BF16)

    chunks_per_part = PROJ_ROWS // CHUNK
    parts = []

    def project_part(part):
        xp = xb[part * PROJ_ROWS:(part + 1) * PROJ_ROWS]
        proj = lambda lo, hi: _dot(xp, w_in_ref[:, lo:hi])
        parts.append(dict(
            q_r=proj(_QR, _KR), k_r=proj(_KR, _VR), v_r=proj(_VR, _GR), g_r=proj(_GR, _QA),
            q_a=proj(_QA, _KA), k_ab=proj(_KA, _VA).astype(BF16),
            v_rep=_tile_heads(proj(_VA, IN_WIDTH).astype(BF16))))

    project_part(0)

    def chunk_of(name, c):
        lo = (c % chunks_per_part) * CHUNK
        return parts[c // chunks_per_part][name][lo:lo + CHUNK]
    first_pen = jnp.where(n == 0, pen_ref[...], 0.0)

    for c in range(ts // CHUNK):
        rs = slice(c * CHUNK, (c + 1) * CHUNK)
        q_r, k_r, v_r, g_r = (chunk_of(name, c) for name in ("q_r", "k_r", "v_r", "g_r"))
        q_a, k_ab, v_rep = (chunk_of(name, c) for name in ("q_a", "k_ab", "v_rep"))
        for hd in range(RET_HEADS):
            cs = slice(hd * RET_HEAD_DIM, (hd + 1) * RET_HEAD_DIM)
            q = q_r[:, cs]
            k = k_r[:, cs]
            v = v_r[:, cs].astype(BF16)
            scores = _dot_nt(q.astype(BF16), k.astype(BF16)) * din_ref[hd]
            st = state_scr[hd]
            o = _dot(jnp.concatenate([scores.astype(BF16), (q * qdec_ref[hd]).astype(BF16)], axis=1),
                     jnp.concatenate([v, st.astype(BF16)], axis=0))
            kv = _dot_tn((k * kdec_ref[hd]).astype(BF16), v)
            state_scr[hd] = st * cdec_ref[hd] + kv
            mu = jnp.mean(o, axis=-1, keepdims=True)
            oc = o - mu
            var = jnp.mean(oc * oc, axis=-1, keepdims=True)
            on = oc * lax.rsqrt(var + GN_EPS) * gn_ref[:, cs] * _silu(g_r[:, cs])
            o_scr[rs, cs] = on.astype(BF16)
        for j in range(SWA_KV_HEADS):
            ks = slice(j * SWA_HEAD_DIM, (j + 1) * SWA_HEAD_DIM)
            vs = slice(j * SWA_VREP, (j + 1) * SWA_VREP)
            if c == 0:
                kp = kprev_scr[:, ks].astype(BF16)
                vp = vprev_scr[:, vs].astype(BF16)
            else:
                kp = chunk_of("k_ab", c - 1)[:, ks]
                vp = chunk_of("v_rep", c - 1)[:, vs]
            kp = kp * kmask_ref[...]
            kcat = jnp.concatenate([kp, k_ab[:, ks]], axis=0)
            vcat = jnp.concatenate([vp, v_rep[:, vs]], axis=0)
            q0 = j * SWA_GROUP * SWA_HEAD_DIM
            qs = jnp.concatenate(
                [q_a[:, q0 + g * SWA_HEAD_DIM:q0 + (g + 1) * SWA_HEAD_DIM] for g in range(SWA_GROUP)],
                axis=0)
            qs = (qs * (SWA_HEAD_DIM ** -0.5)).astype(BF16)
            s = _dot_nt(qs, kcat) + bias_ref[j]
            if c == 0:
                s = s + first_pen
            m = jnp.max(jnp.maximum(s[:, :CHUNK], s[:, CHUNK:]), axis=-1, keepdims=True)
            p32 = jnp.exp(s - m)
            den_col = jnp.sum(p32[:, :CHUNK] + p32[:, CHUNK:], axis=-1, keepdims=True)
            p = p32.astype(BF16)
            p_all = jnp.concatenate([p[g * CHUNK:(g + 1) * CHUNK] for g in range(SWA_GROUP)], axis=1)
            v_blk = jnp.concatenate([vcat * vmask_ref[g] for g in range(SWA_GROUP)], axis=0)
            num = _dot(p_all, v_blk)
            den = den_col[(SWA_GROUP - 1) * CHUNK:]
            for g in range(SWA_GROUP - 2, -1, -1):
                den = jnp.where(hlane_ref[...] == g, den_col[g * CHUNK:(g + 1) * CHUNK], den)
            c0 = RET_WIDTH + j * SWA_GROUP * SWA_HEAD_DIM
            o_scr[rs, c0:c0 + SWA_GROUP * SWA_HEAD_DIM] = (num / den).astype(BF16)
        if c % chunks_per_part == 0 and len(parts) < ts // PROJ_ROWS:
            project_part(len(parts))

    kprev_scr[...] = chunk_of("k_ab", ts // CHUNK - 1).astype(F32)
    vprev_scr[...] = chunk_of("v_rep", ts // CHUNK - 1).astype(F32)

    mix = _dot(o_scr[...], w_out_ref[...])
    h = _layer_norm(DEEPNORM_ALPHA * x + mix, ln_g_ref[...], ln_b_ref[...])
    h_ref[...] = h
    _pack_rows(hp_ref, h)

    h_hi = h.astype(BF16)
    h_lo = (h - h_hi.astype(F32)).astype(BF16)
    hi_terms = _dot(h_hi, wr_cat_ref[...])
    logits = hi_terms[:, :LANES] + hi_terms[:, LANES:] + _dot(h_lo, wr_cat_ref[:, :LANES]) + br_ref[...]
    lt = logits.T
    row = lax.broadcasted_iota(jnp.int32, (EXPERTS_PER_GROUP, ts), 0).astype(F32)
    big = 1e9
    ninf = -jnp.inf
    col_max = lambda v: jnp.max(v, axis=0, keepdims=True)
    first_at = lambda v, m: jnp.min(jnp.where(v == m, row, big), axis=0, keepdims=True)
    gl = jnp.where(row < N_GROUPS, lt[GROUP_LANE0:GROUP_LANE0 + EXPERTS_PER_GROUP], ninf)
    gmax = col_max(gl)
    gidx = first_at(gl, gmax)
    g_w = 1.0 / jnp.sum(jnp.exp(gl - gmax), axis=0, keepdims=True)
    el = lt[(N_GROUPS - 1) * EXPERTS_PER_GROUP:N_GROUPS * EXPERTS_PER_GROUP]
    for g in range(N_GROUPS - 2, -1, -1):
        el = jnp.where(gidx == g, lt[g * EXPERTS_PER_GROUP:(g + 1) * EXPERTS_PER_GROUP], el)
    m1 = col_max(el)
    i1 = first_at(el, m1)
    el2 = jnp.where(row == i1, ninf, el)
    m2 = col_max(el2)
    i2 = first_at(el2, m2)
    t = jnp.exp(m2 - m1)
    w1 = g_w / (1.0 + t)
    w2 = g_w * t / (1.0 + t)
    e1 = gidx * EXPERTS_PER_GROUP + i1
    e2 = gidx * EXPERTS_PER_GROUP + i2
    erow = lax.broadcasted_iota(jnp.int32, (N_EXPERTS, ts), 0).astype(F32)
    hit1 = erow == e1
    hit2 = erow == e2
    onehot = (hit1 | hit2).astype(BF16)
    prefix = _dot(onehot, tri_ref[...]) + carry_scr[:, 0:1]
    r1 = jnp.sum(jnp.where(hit1, prefix, 0.0), axis=0, keepdims=True)
    r2 = jnp.sum(jnp.where(hit2, prefix, 0.0), axis=0, keepdims=True)
    carry = carry_scr[:, 0:1] + jnp.sum(onehot.astype(F32), axis=1, keepdims=True)
    carry_scr[...] = jnp.broadcast_to(carry, carry_scr.shape)
    cnt_ref[...] = jnp.broadcast_to(carry, cnt_ref.shape)
    pick = lambda k, v, rest: jnp.where(row == k, v, rest)
    ids_ref[...] = pick(0, e1, pick(1, e2, pick(2, r1, pick(3, r2, 0.0)))).astype(jnp.int32)
    wts_ref[...] = pick(0, w1, pick(1, w2, 0.0))


def _mixer_call(x2, w_in, w_out, gn, ln_g, ln_b, wr_cat, br, consts, bsz, s_len, first_seq):
    cdec, din, qdec, kdec, bias, pen, kmask, vmask, hlane, tri = consts
    t_len = bsz * s_len
    ns = s_len // SEQ_TILE
    tok = lambda b, n, *_: (b * ns + n, 0)
    tok_in = lambda b, n, *_: ((first_seq + b) * ns + n, 0)
    tok_t = lambda b, n, *_: (0, b * ns + n)
    full2 = lambda b, n, *_: (0, 0)
    full3 = lambda b, n, *_: (0, 0, 0)
    grid_spec = pltpu.PrefetchScalarGridSpec(
        num_scalar_prefetch=1,
        grid=(bsz, ns),
        in_specs=[
            pl.BlockSpec((SEQ_TILE, D_MODEL), tok_in),
            pl.BlockSpec((D_MODEL, IN_WIDTH), full2),
            pl.BlockSpec((D_MODEL, D_MODEL), full2),
            pl.BlockSpec((1, RET_WIDTH), full2),
            pl.BlockSpec((1, D_MODEL), full2),
            pl.BlockSpec((1, D_MODEL), full2),
            pl.BlockSpec((D_MODEL, 2 * LANES), full2),
            pl.BlockSpec((1, LANES), full2),
            pl.BlockSpec((RET_HEADS, CHUNK, CHUNK), full3),
            pl.BlockSpec((RET_HEADS, CHUNK, RET_HEAD_DIM), full3),
            pl.BlockSpec((RET_HEADS, CHUNK, RET_HEAD_DIM), full3),
            pl.BlockSpec((SWA_KV_HEADS, SWA_GROUP * CHUNK, 2 * CHUNK), full3),
            pl.BlockSpec((1, 2 * CHUNK), full2),
            pl.BlockSpec((CHUNK, SWA_HEAD_DIM), full2),
            pl.BlockSpec((SWA_GROUP, 2 * CHUNK, SWA_VREP), full3),
            pl.BlockSpec((1, SWA_VREP), full2),
            pl.BlockSpec((SEQ_TILE, SEQ_TILE), full2),
        ],
        out_specs=[
            pl.BlockSpec((SEQ_TILE, D_MODEL), tok),
            pl.BlockSpec((SEQ_TILE * ROW_SUBLANES, LANES), tok),
            pl.BlockSpec((ROUTE_ROWS, SEQ_TILE), tok_t),
            pl.BlockSpec((ROUTE_ROWS, SEQ_TILE), tok_t),
            pl.BlockSpec((N_EXPERTS, LANES), full2),
        ],
        scratch_shapes=[
            pltpu.VMEM((RET_HEADS, RET_HEAD_DIM, RET_HEAD_DIM), F32),
            pltpu.VMEM((CHUNK, SWA_KV_WIDTH), F32),
            pltpu.VMEM((CHUNK, SWA_KV_HEADS * SWA_VREP), F32),
            pltpu.VMEM((SEQ_TILE, D_MODEL), BF16),
            pltpu.VMEM((N_EXPERTS, LANES), F32),
        ],
    )
    return pl.pallas_call(
        _mixer_kernel,
        grid_spec=grid_spec,
        out_shape=[
            jax.ShapeDtypeStruct((t_len, D_MODEL), F32),
            jax.ShapeDtypeStruct((t_len * ROW_SUBLANES, LANES), U32),
            jax.ShapeDtypeStruct((ROUTE_ROWS, t_len), jnp.int32),
            jax.ShapeDtypeStruct((ROUTE_ROWS, t_len), F32),
            jax.ShapeDtypeStruct((N_EXPERTS, LANES), F32),
        ],
        compiler_params=pltpu.CompilerParams(
            dimension_semantics=("arbitrary", "arbitrary"), vmem_limit_bytes=VMEM_LIMIT),
        name="mixer_router",
        cost_estimate=_mixer_cost(t_len),
    )(cdec, x2, w_in, w_out, gn, ln_g, ln_b, wr_cat, br, din, qdec, kdec, bias, pen, kmask, vmask, hlane, tri)


def _sc_mesh():
    return plsc.VectorSubcoreMesh(core_axis_name="core", subcore_axis_name="subcore")


def _sc_stream_rows(table_hbm, idx_v, out_hbm, out_base, n_chunks, bufs, sems):
    assert n_chunks % 2 == 0

    def gather(j, b):
        off = pl.multiple_of(j * SC_CHUNK_ROWS, SC_CHUNK_ROWS)
        return pltpu.make_async_copy(table_hbm.at[idx_v.at[pl.ds(off, SC_CHUNK_ROWS)]], bufs[b], sems[b])

    gather(0, 0).start()

    @pl.loop(0, n_chunks, step=2)
    def _(j0):
        for b in range(2):
            j = j0 + b
            gather(j, b).wait()

            @pl.when(j + 1 < n_chunks)
            def _():
                gather(j + 1, 1 - b).start()

            off = pl.multiple_of(j * SC_CHUNK_ROWS, SC_CHUNK_ROWS)
            pltpu.sync_copy(bufs[b], out_hbm.at[pl.ds(out_base + off, SC_CHUNK_ROWS)])


def _sc_dispatch(h_rows, dest, slot0, n_slots):
    t_len = h_rows.shape[0]
    n_assign = dest.shape[0]
    per_worker = n_slots // SC_WORKERS
    n_chunks = per_worker // SC_CHUNK_ROWS
    assert n_chunks * SC_CHUNK_ROWS * SC_WORKERS == n_slots and slot0 + n_slots <= 3 * t_len
    assert per_worker % SC_LANES == 0 and t_len % (SC_LANES * SC_SCAN_UNROLL) == 0 and n_assign == TOP_K * t_len

    @functools.partial(
        pl.kernel, mesh=_sc_mesh(), name="moe_dispatch_sc", cost_estimate=_row_move_cost(n_slots),
        compiler_params=pltpu.CompilerParams(needs_layout_passes=False),
        out_type=jax.ShapeDtypeStruct((n_slots, ROW_SUBLANES, LANES), U32),
        scratch_types=[pltpu.VMEM((n_assign,), jnp.int32),
                       pltpu.VMEM((per_worker,), jnp.int32),
                       pltpu.VMEM((SC_CHUNK_ROWS, ROW_SUBLANES, LANES), U32),
                       pltpu.VMEM((SC_CHUNK_ROWS, ROW_SUBLANES, LANES), U32),
                       pltpu.SemaphoreType.DMA, pltpu.SemaphoreType.DMA])
    def dispatch(h_hbm, dest_hbm, rows_hbm, dest_v, src_v, buf0, buf1, sem, sem1):
        wid = lax.axis_index("subcore") * SC_CORES + lax.axis_index("core")
        local = wid * per_worker
        base = slot0 + local
        piece = n_assign // SC_DEST_PIECES
        copies = []
        for c in range(SC_DEST_PIECES):
            off = pl.multiple_of(lax.rem(c + wid, SC_DEST_PIECES) * piece, SC_LANES)
            copies.append(pltpu.async_copy(dest_hbm.at[pl.ds(off, piece)], dest_v.at[pl.ds(off, piece)], sem))
        for cp in copies:
            cp.wait()
        lane = lax.iota(jnp.int32, SC_LANES)

        def wrap(a):
            a = jnp.where(a >= t_len, a - t_len, a)
            return jnp.where(a >= t_len, a - t_len, a)

        @pl.loop(0, per_worker // SC_LANES)
        def _(i):
            src_v[pl.ds(i * SC_LANES, SC_LANES)] = wrap(base + i * SC_LANES + lane)

        for k in range(TOP_K):
            @pl.loop(0, t_len // (SC_LANES * SC_SCAN_UNROLL))
            def _(i):
                for u in range(SC_SCAN_UNROLL):
                    tok0 = (i * SC_SCAN_UNROLL + u) * SC_LANES
                    d = dest_v[pl.ds(k * t_len + tok0, SC_LANES)] - base
                    hit = (d >= 0) & (d < per_worker)
                    plsc.store_scatter(src_v, [jnp.where(hit, d, 0)], tok0 + lane, mask=hit)

        _sc_stream_rows(h_hbm, src_v, rows_hbm, local, n_chunks, (buf0, buf1), (sem, sem1))

    return dispatch(h_rows, dest)


def _sc_gather_rows(table, idx, name):
    m = idx.shape[0]
    per_worker = m // (SC_CHUNK_ROWS * SC_WORKERS)
    assert per_worker * SC_CHUNK_ROWS * SC_WORKERS == m

    @functools.partial(
        pl.kernel, mesh=_sc_mesh(), name=name, cost_estimate=_row_move_cost(m),
        out_type=jax.ShapeDtypeStruct((m, ROW_SUBLANES, LANES), U32),
        scratch_types=[pltpu.VMEM((per_worker * SC_CHUNK_ROWS,), jnp.int32),
                       pltpu.VMEM((SC_CHUNK_ROWS, ROW_SUBLANES, LANES), U32),
                       pltpu.VMEM((SC_CHUNK_ROWS, ROW_SUBLANES, LANES), U32),
                       pltpu.SemaphoreType.DMA, pltpu.SemaphoreType.DMA])
    def gather(table_hbm, idx_hbm, out_hbm, idx_v, buf0, buf1, sem0, sem1):
        wid = lax.axis_index("subcore") * SC_CORES + lax.axis_index("core")
        base = pl.multiple_of(wid * (per_worker * SC_CHUNK_ROWS), SC_CHUNK_ROWS)
        pltpu.sync_copy(idx_hbm.at[pl.ds(base, per_worker * SC_CHUNK_ROWS)], idx_v)

        _sc_stream_rows(table_hbm, idx_v, out_hbm, base, per_worker, (buf0, buf1), (sem0, sem1))

    return gather(table, idx)


def _expert_kernel(bexp_ref, nused_ref, nxt_ref, slot_ref, rows_ref, wg_hbm, wu_hbm, wd_hbm, *rest, first_block):
    y_ref, stage_g, stage_u, stage_d, wg_s, wu_s, wd_s, sem = rest[-8:]
    p = pl.program_id(0)
    blk = first_block + p
    expert = bexp_ref[blk]
    slot = slot_ref[blk]

    def weight_copies(e, s):
        return [pltpu.make_async_copy(w.at[e], stage.at[s], sem.at[s, i])
                for i, (w, stage) in enumerate(((wg_hbm, stage_g), (wu_hbm, stage_u), (wd_hbm, stage_d)))]

    @pl.when(p == 0)
    def _():
        for cp in weight_copies(expert, slot):
            cp.start()

    @pl.when((p == 0) | (expert != bexp_ref[jnp.maximum(blk - 1, 0)]))
    def _():
        for cp in weight_copies(expert, slot):
            cp.wait()
        wg_s[...] = stage_g[slot].astype(BF16)
        wu_s[...] = stage_u[slot].astype(BF16)
        wd_s[...] = stage_d[slot].astype(BF16)

        @pl.when(nxt_ref[blk] >= 0)
        def _():
            for cp in weight_copies(nxt_ref[blk], 1 - slot):
                cp.start()

    @pl.when(blk < nused_ref[0])
    def _():
        xb = _unpack_rows(rows_ref).astype(BF16)
        g = _dot(xb, wg_s[...])
        u = _dot(xb, wu_s[...])
        a = (_silu(g) * u).astype(BF16)
        _pack_rows(y_ref, _dot(a, wd_s[...]))


def _expert_call(bexp, nused, seg_end, rows, w_gate, w_up, w_down, y_prev, part, n_rows):
    part_blocks = rows.shape[0] // (ROW_SUBLANES * MOE_ROWS)
    n_blocks = bexp.shape[0]
    first_block = part * part_blocks
    limit = jnp.minimum(nused[0], first_block + part_blocks)
    nxt = jnp.where(seg_end < limit, bexp[jnp.minimum(seg_end, n_blocks - 1)], -1).astype(jnp.int32)
    starts = jnp.concatenate([jnp.ones((1,), jnp.int32), (bexp[1:] != bexp[:-1]).astype(jnp.int32)])
    slot = (jnp.cumsum(starts) % 2).astype(jnp.int32)
    last_used = lambda p, nu: jnp.clip(first_block + p, first_block, jnp.maximum(nu[0] - 1, first_block))
    blk = lambda p, be, nu, nx, sl: (last_used(p, nu) - first_block, 0)
    hbm = pl.BlockSpec(memory_space=pl.ANY)
    in_specs = [pl.BlockSpec((MOE_ROWS * ROW_SUBLANES, LANES), blk), hbm, hbm, hbm]
    args = [rows, w_gate, w_up, w_down]
    aliases = {}
    if y_prev is not None:
        in_specs.append(hbm)
        args.append(y_prev)
        aliases = {4 + len(args) - 1: 0}
    grid_spec = pltpu.PrefetchScalarGridSpec(
        num_scalar_prefetch=4,
        grid=(part_blocks,),
        in_specs=in_specs,
        out_specs=pl.BlockSpec((MOE_ROWS * ROW_SUBLANES, LANES), lambda p, be, nu, nx, sl: (last_used(p, nu), 0)),
        scratch_shapes=[
            pltpu.VMEM((2, D_MODEL, D_EXPERT), F32),
            pltpu.VMEM((2, D_MODEL, D_EXPERT), F32),
            pltpu.VMEM((2, D_EXPERT, D_MODEL), F32),
            pltpu.VMEM((D_MODEL, D_EXPERT), BF16),
            pltpu.VMEM((D_MODEL, D_EXPERT), BF16),
            pltpu.VMEM((D_EXPERT, D_MODEL), BF16),
            pltpu.SemaphoreType.DMA((2, 3)),
        ],
    )
    return pl.pallas_call(
        functools.partial(_expert_kernel, first_block=first_block),
        grid_spec=grid_spec,
        out_shape=jax.ShapeDtypeStruct((n_rows * ROW_SUBLANES, LANES), U32),
        input_output_aliases=aliases,
        compiler_params=pltpu.CompilerParams(
            dimension_semantics=("arbitrary",), vmem_limit_bytes=VMEM_LIMIT),
        name="moe_experts",
        cost_estimate=_expert_cost(part_blocks * MOE_ROWS),
    )(bexp, nused, nxt, slot, *args)


def _combine_kernel(h_ref, y0_ref, y1_ref, wts_ref, ln_g_ref, ln_b_ref, *rest):
    out_ref = rest[-1]
    wts = wts_ref[...].T
    ffn = _unpack_rows(y0_ref) * wts[:, 0:1] + _unpack_rows(y1_ref) * wts[:, 1:2]
    out_ref[...] = _layer_norm(DEEPNORM_ALPHA * h_ref[...] + ffn, ln_g_ref[...], ln_b_ref[...])


def _combine_call(h, yk, wts, ln_g, ln_b, out_prev, split, t_total):
    t_len = t_total // TOKEN_SPLITS
    n_tiles = t_len // COMBINE_TILE
    tok = lambda i: (i, 0)
    tok_split = lambda i: (i + split * n_tiles, 0)
    full2 = lambda i: (0, 0)
    tiles = pl.BlockSpec((COMBINE_TILE * ROW_SUBLANES, LANES), tok)
    tiles_k1 = pl.BlockSpec((COMBINE_TILE * ROW_SUBLANES, LANES), lambda i: (i + n_tiles, 0))
    in_specs = [pl.BlockSpec((COMBINE_TILE, D_MODEL), tok_split), tiles, tiles_k1,
                pl.BlockSpec((ROUTE_ROWS, COMBINE_TILE), lambda i: (0, i + split * n_tiles)),
                pl.BlockSpec((1, D_MODEL), full2),
                pl.BlockSpec((1, D_MODEL), full2)]
    args = [h, yk, yk, wts, ln_g, ln_b]
    aliases = {}
    if out_prev is not None:
        in_specs.append(pl.BlockSpec(memory_space=pl.ANY))
        args.append(out_prev)
        aliases = {len(args) - 1: 0}
    return pl.pallas_call(
        _combine_kernel,
        grid=(n_tiles,),
        in_specs=in_specs,
        out_specs=pl.BlockSpec((COMBINE_TILE, D_MODEL), tok_split),
        out_shape=jax.ShapeDtypeStruct((t_total, D_MODEL), F32),
        input_output_aliases=aliases,
        compiler_params=pltpu.CompilerParams(
            dimension_semantics=("arbitrary",), vmem_limit_bytes=VMEM_LIMIT),
        name="moe_combine_ln",
        cost_estimate=pl.CostEstimate(flops=8 * t_len * D_MODEL, transcendentals=t_len,
                                      bytes_accessed=(2 * 4 * D_MODEL + TOP_K * ROW_BYTES) * t_len),
    )(*args)


def _position_tables():
    c = CHUNK
    f = np.float32
    log_g = np.log1p(-np.exp2(-5.0 - np.arange(RET_HEADS, dtype=f))).astype(f)
    idx = np.arange(c, dtype=f)
    diff = idx[:, None] - idx[None, :]
    scale = f(RET_HEAD_DIM ** -0.5)
    din = np.where(diff >= 0, np.exp(log_g[:, None, None] * np.maximum(diff, 0.0)), 0.0).astype(f) * scale
    qdec = np.broadcast_to(np.exp(log_g[:, None] * (idx + 1.0))[:, :, None], (RET_HEADS, c, RET_HEAD_DIM))
    kdec = np.broadcast_to((np.exp(log_g[:, None] * (c - 1.0 - idx)) * scale)[:, :, None],
                           (RET_HEADS, c, RET_HEAD_DIM))
    cdec = np.exp(log_g * c)
    slopes = np.exp2(-8.0 * (np.arange(SWA_HEADS, dtype=f) + 1.0) / SWA_HEADS).astype(f)
    r = np.arange(c)[:, None]
    col = np.arange(c)[None, :]
    dist_prev = (r - col + c).astype(f)
    dist_cur = (r - col).astype(f)
    bprev = np.where((r < col)[None], -slopes[:, None, None] * dist_prev[None], NEG)
    bcur = np.where((r >= col)[None], -slopes[:, None, None] * dist_cur[None], NEG)
    bias = np.concatenate([bprev, bcur], axis=-1).reshape(SWA_KV_HEADS, SWA_GROUP * c, 2 * c)
    key = np.arange(2 * c)
    pen = np.where((key >= 1) & (key < c), NEG, 0.0)[None, :]
    lane_head = np.arange(SWA_VREP) // SWA_HEAD_DIM
    own = lane_head[None, None, :] == np.arange(SWA_GROUP)[:, None, None]
    kmask = np.broadcast_to(np.arange(c)[:, None] > 0, (c, SWA_HEAD_DIM))
    vmask = own & (key[None, :, None] > 0)
    hlane = lane_head[None, :]
    tr = np.arange(SEQ_TILE)
    tri = tr[:, None] < tr[None, :]
    as_f32 = lambda v: np.ascontiguousarray(v, dtype=f)
    as_bf16 = lambda v: np.ascontiguousarray(v, dtype=f).astype(BF16)
    return (as_f32(cdec), as_f32(din), as_f32(qdec), as_f32(kdec), as_f32(bias), as_f32(pen),
            as_bf16(kmask), as_bf16(vmask), as_f32(hlane), as_bf16(tri))


def _mixer_constants(attn_sinks):
    cdec, din, qdec, kdec, bias, pen, kmask, vmask, hlane, tri = _position_tables()
    sink = attn_sinks.astype(F32).reshape(SWA_KV_HEADS, SWA_GROUP, 1, 1)
    sink = jnp.broadcast_to(sink, (SWA_KV_HEADS, SWA_GROUP, CHUNK, 1)).reshape(SWA_KV_HEADS, SWA_GROUP * CHUNK, 1)
    is_slot = (np.arange(2 * CHUNK) == 0)[None, None, :]
    return cdec, din, qdec, kdec, jnp.where(is_slot, sink, bias), pen, kmask, vmask, hlane, tri


def _router_tables(w_group_router, b_group_router, w_expert_router, b_expert_router):
    w_e = jnp.transpose(w_expert_router, (1, 0, 2)).reshape(D_MODEL, N_EXPERTS)
    w = jnp.concatenate([w_e, w_group_router,
                         jnp.zeros((D_MODEL, LANES - N_EXPERTS - N_GROUPS), F32)], axis=1)
    bias = jnp.concatenate([b_expert_router.reshape(N_EXPERTS), b_group_router,
                            jnp.zeros((LANES - N_EXPERTS - N_GROUPS,), F32)])[None, :]
    w_hi = w.astype(BF16)
    w_lo = (w - w_hi.astype(F32)).astype(BF16)
    return jnp.concatenate([w_hi, w_lo], axis=1), bias


def kernel(x, w_in, ret_gn_g, attn_sinks, w_out, ln1_g, ln1_b, w_group_router, b_group_router,
           w_expert_router, b_expert_router, w_gate, w_up, w_down, ln2_g, ln2_b):
    bsz, s_len, d = x.shape
    assert d == D_MODEL and s_len % SEQ_TILE == 0 and w_in.shape[0] == DEPTH == 1
    assert bsz % TOKEN_SPLITS == 0
    t_len = bsz * s_len
    t_split = t_len // TOKEN_SPLITS
    n_blocks = t_len * TOP_K // MOE_ROWS + N_EXPERTS
    n_rows = n_blocks * MOE_ROWS
    part_rows = n_rows // SLOT_PARTS

    consts = _mixer_constants(attn_sinks[0])
    wr_cat, br = _router_tables(w_group_router[0], b_group_router[0],
                                      w_expert_router[0], b_expert_router[0])
    h, h_packed, ids, wts, cnt = _mixer_call(
        x.reshape(t_len, d), w_in[0].astype(BF16), w_out[0].astype(BF16),
        ret_gn_g[0][None, :], ln1_g[0][None, :], ln1_b[0][None, :], wr_cat, br, consts, bsz, s_len, 0)

    counts = cnt[:, 0].astype(jnp.int32)
    padded = (counts + MOE_ROWS - 1) // MOE_ROWS * MOE_ROWS
    pend = jnp.cumsum(padded)
    pstart = pend - padded
    onehot = ids[0:TOP_K, :, None] == jnp.arange(N_EXPERTS, dtype=jnp.int32)
    dest2 = jnp.sum(jnp.where(onehot, pstart, 0), axis=-1) + ids[TOP_K:2 * TOP_K]
    dest = dest2.reshape(-1)
    nused = (pend[-1:] // MOE_ROWS).astype(jnp.int32)
    blk_start = jnp.minimum(jnp.arange(n_blocks, dtype=jnp.int32), nused[0] - 1) * MOE_ROWS
    bexp = jnp.minimum(jnp.sum(pend[None, :] <= blk_start[:, None], axis=-1), N_EXPERTS - 1).astype(jnp.int32)
    seg_end = jnp.sum(jnp.where(bexp[:, None] == jnp.arange(N_EXPERTS, dtype=jnp.int32), pend // MOE_ROWS, 0),
                      axis=-1).astype(jnp.int32)

    h_tiles = h_packed.reshape(t_len, ROW_SUBLANES, LANES)
    y = None
    for part in range(SLOT_PARTS):
        rows = _sc_dispatch(h_tiles, dest, part * part_rows, part_rows)
        y = _expert_call(bexp, nused, seg_end, rows.reshape(part_rows * ROW_SUBLANES, LANES),
                         w_gate[0], w_up[0], w_down[0], y, part, n_rows)

    y_tiles = y.reshape(n_rows, ROW_SUBLANES, LANES)
    out = None
    for sp in range(TOKEN_SPLITS):
        idx = dest2[:, sp * t_split:(sp + 1) * t_split].reshape(-1)
        yk = _sc_gather_rows(y_tiles, idx, "moe_combine_sc")
        out = _combine_call(h, yk.reshape(TOP_K * t_split * ROW_SUBLANES, LANES), wts,
                            ln2_g[0][None, :], ln2_b[0][None, :], out, sp, t_len)
    return out.reshape(bsz, s_len, d)
```

```python
import functools

import jax
import jax.numpy as jnp
import numpy as np
from jax import lax
from jax.experimental import pallas as pl
from jax.experimental.pallas import tpu as pltpu
from jax.experimental.pallas import tpu_sc as plsc

F32 = jnp.float32
BF16 = jnp.bfloat16
U32 = jnp.uint32

D_MODEL = 1024
RET_HEADS = 4
RET_HEAD_DIM = 128
RET_WIDTH = RET_HEADS * RET_HEAD_DIM
CHUNK = 128
SWA_HEADS = 8
SWA_KV_HEADS = 2
SWA_GROUP = SWA_HEADS // SWA_KV_HEADS
SWA_HEAD_DIM = 64
SWA_WIDTH = SWA_HEADS * SWA_HEAD_DIM
SWA_KV_WIDTH = SWA_KV_HEADS * SWA_HEAD_DIM
IN_WIDTH = 4 * RET_WIDTH + SWA_WIDTH + 2 * SWA_KV_WIDTH
N_GROUPS = 4
EXPERTS_PER_GROUP = 8
N_EXPERTS = N_GROUPS * EXPERTS_PER_GROUP
TOP_K = 2
D_EXPERT = 512
LN_EPS = 1e-5
GN_EPS = 1e-6
DEPTH = 1
DEEPNORM_ALPHA = (2 * DEPTH) ** 0.25
NEG = -1e30

LANES = 128
ROW_SUBLANES = D_MODEL // 2 // LANES
SEQ_TILE = 1024
PROJ_ROWS = 256
RANK_BLOCK = 256
MOE_ROWS = 512
COMBINE_TILE = 512
TOKEN_SPLITS = 2
SLOT_PARTS = 2
SC_CORES = 2
SC_SUBCORES = 16
SC_WORKERS = SC_CORES * SC_SUBCORES
SC_LANES = 16
SC_DEST_PIECES = 8
SC_SCAN_UNROLL = 8
SC_CHUNK_ROWS = 64
VMEM_LIMIT = 56 * 1024 * 1024

_QR, _KR, _VR, _GR = 0, RET_WIDTH, 2 * RET_WIDTH, 3 * RET_WIDTH
_QA = 4 * RET_WIDTH
_KA = _QA + SWA_WIDTH
_VA = _KA + SWA_KV_WIDTH
SWA_VREP = SWA_GROUP * SWA_HEAD_DIM
ROUTE_ROWS = 8
GROUP_LANE0 = N_EXPERTS


def _dot(a, b):
    return jnp.dot(a, b, preferred_element_type=F32)


def _dot_nt(a, b):
    return lax.dot_general(a, b, (((1,), (1,)), ((), ())), preferred_element_type=F32)


def _dot_tn(a, b):
    return lax.dot_general(a, b, (((0,), (0,)), ((), ())), preferred_element_type=F32)


def _layer_norm(z, g, b):
    mu = jnp.mean(z, axis=-1, keepdims=True)
    zc = z - mu
    var = jnp.mean(zc * zc, axis=-1, keepdims=True)
    return zc * lax.rsqrt(var + LN_EPS) * g + b


def _silu(g):
    return g / (1.0 + jnp.exp(-g))


def _tile_heads(v):
    return jnp.concatenate([v[:, j * SWA_HEAD_DIM:(j + 1) * SWA_HEAD_DIM]
                            for j in range(SWA_KV_HEADS) for _ in range(SWA_GROUP)], axis=1)


ROW_BYTES = 4 * ROW_SUBLANES * LANES


def _mixer_cost(t):
    proj = 2 * t * D_MODEL * (IN_WIDTH + D_MODEL + 3 * LANES)
    retention = RET_HEADS * 4 * 2 * t * CHUNK * RET_HEAD_DIM
    swa = SWA_KV_HEADS * 2 * t * SWA_GROUP * 2 * CHUNK * (SWA_HEAD_DIM + 2 * SWA_VREP)
    weights = 2 * D_MODEL * (IN_WIDTH + D_MODEL + 2 * LANES)
    return pl.CostEstimate(flops=proj + retention + swa + 2 * t * RANK_BLOCK * N_EXPERTS,
                           transcendentals=t * (SWA_HEADS * 2 * CHUNK + RET_WIDTH + LANES),
                           bytes_accessed=(2 * 4 * D_MODEL + ROW_BYTES) * t + weights)


def _expert_cost(n_rows):
    return pl.CostEstimate(flops=2 * 3 * n_rows * D_MODEL * D_EXPERT, transcendentals=n_rows * D_EXPERT,
                           bytes_accessed=2 * ROW_BYTES * n_rows + 4 * 3 * N_EXPERTS * D_MODEL * D_EXPERT)


def _row_move_cost(n_rows):
    return pl.CostEstimate(flops=0, transcendentals=0, bytes_accessed=2 * ROW_BYTES * n_rows)


def _pack_rows(ref, val):
    n = val.shape[0]
    half = D_MODEL // 2
    hi = lax.bitcast_convert_type(val[:, :half].astype(BF16).astype(F32), U32)
    lo = lax.bitcast_convert_type(val[:, half:].astype(BF16).astype(F32), U32)
    word = hi | (lo >> 16)
    for j in range(ROW_SUBLANES):
        ref[pl.ds(j, n, stride=ROW_SUBLANES), :] = word[:, j * LANES:(j + 1) * LANES]


def _unpack_rows(ref):
    n = ref.shape[0] // ROW_SUBLANES
    word = jnp.concatenate([ref[pl.ds(j, n, stride=ROW_SUBLANES), :] for j in range(ROW_SUBLANES)], axis=1)
    hi = lax.bitcast_convert_type(word & jnp.uint32(0xFFFF0000), F32)
    lo = lax.bitcast_convert_type(word << 16, F32)
    return jnp.concatenate([hi, lo], axis=1)


def _mixer_kernel(cdec_ref, x_ref, w_in_ref, w_out_ref, gn_ref, ln_g_ref, ln_b_ref,
                  wr_cat_ref, br_ref, din_ref, qdec_ref, kdec_ref,
                  bias_ref, pen_ref, kmask_ref, vmask_ref, hlane_ref, tri_ref,
                  h_ref, hp_ref, ids_ref, wts_ref, cnt_ref,
                  state_scr, kprev_scr, vprev_scr, o_scr, carry_scr):
    b = pl.program_id(0)
    n = pl.program_id(1)
    ts = x_ref.shape[0]

    @pl.when(n == 0)
    def _():
        state_scr[...] = jnp.zeros_like(state_scr)
        kprev_scr[...] = jnp.zeros_like(kprev_scr)
        vprev_scr[...] = jnp.zeros_like(vprev_scr)

    @pl.when((b == 0) & (n == 0))
    def _():
        carry_scr[...] = jnp.zeros_like(carry_scr)

    x = x_ref[...]
    xb = x.astype(BF16)

    chunks_per_part = PROJ_ROWS // CHUNK
    parts = []

    def project_part(part):
        xp = xb[part * PROJ_ROWS:(part + 1) * PROJ_ROWS]
        proj = lambda lo, hi: _dot(xp, w_in_ref[:, lo:hi])
        parts.append(dict(
            q_r=proj(_QR, _KR), k_r=proj(_KR, _VR), v_r=proj(_VR, _GR), g_r=proj(_GR, _QA),
            q_a=proj(_QA, _KA), k_ab=proj(_KA, _VA).astype(BF16),
            v_rep=_tile_heads(proj(_VA, IN_WIDTH).astype(BF16))))

    project_part(0)

    def chunk_of(name, c):
        lo = (c % chunks_per_part) * CHUNK
        return parts[c // chunks_per_part][name][lo:lo + CHUNK]
    first_pen = jnp.where(n == 0, pen_ref[...], 0.0)

    for c in range(ts // CHUNK):
        rs = slice(c * CHUNK, (c + 1) * CHUNK)
        q_r, k_r, v_r, g_r = (chunk_of(name, c) for name in ("q_r", "k_r", "v_r", "g_r"))
        q_a, k_ab, v_rep = (chunk_of(name, c) for name in ("q_a", "k_ab", "v_rep"))
        for hd in range(RET_HEADS):
            cs = slice(hd * RET_HEAD_DIM, (hd + 1) * RET_HEAD_DIM)
            q = q_r[:, cs]
            k = k_r[:, cs]
            v = v_r[:, cs].astype(BF16)
            scores = _dot_nt(q.astype(BF16), k.astype(BF16)) * din_ref[hd]
            st = state_scr[hd]
            o = _dot(jnp.concatenate([scores.astype(BF16), (q * qdec_ref[hd]).astype(BF16)], axis=1),
                     jnp.concatenate([v, st.astype(BF16)], axis=0))
            kv = _dot_tn((k * kdec_ref[hd]).astype(BF16), v)
            state_scr[hd] = st * cdec_ref[hd] + kv
            mu = jnp.mean(o, axis=-1, keepdims=True)
            oc = o - mu
            var = jnp.mean(oc * oc, axis=-1, keepdims=True)
            on = oc * lax.rsqrt(var + GN_EPS) * gn_ref[:, cs] * _silu(g_r[:, cs])
            o_scr[rs, cs] = on.astype(BF16)
        for j in range(SWA_KV_HEADS):
            ks = slice(j * SWA_HEAD_DIM, (j + 1) * SWA_HEAD_DIM)
            vs = slice(j * SWA_VREP, (j + 1) * SWA_VREP)
            if c == 0:
                kp = kprev_scr[:, ks].astype(BF16)
                vp = vprev_scr[:, vs].astype(BF16)
            else:
                kp = chunk_of("k_ab", c - 1)[:, ks]
                vp = chunk_of("v_rep", c - 1)[:, vs]
            kp = kp * kmask_ref[...]
            kcat = jnp.concatenate([kp, k_ab[:, ks]], axis=0)
            vcat = jnp.concatenate([vp, v_rep[:, vs]], axis=0)
            q0 = j * SWA_GROUP * SWA_HEAD_DIM
            qs = jnp.concatenate(
                [q_a[:, q0 + g * SWA_HEAD_DIM:q0 + (g + 1) * SWA_HEAD_DIM] for g in range(SWA_GROUP)],
                axis=0)
            qs = (qs * (SWA_HEAD_DIM ** -0.5)).astype(BF16)
            s = _dot_nt(qs, kcat) + bias_ref[j]
            if c == 0:
                s = s + first_pen
            m = jnp.max(jnp.maximum(s[:, :CHUNK], s[:, CHUNK:]), axis=-1, keepdims=True)
            p32 = jnp.exp(s - m)
            den_col = jnp.sum(p32[:, :CHUNK] + p32[:, CHUNK:], axis=-1, keepdims=True)
            p = p32.astype(BF16)
            p_all = jnp.concatenate([p[g * CHUNK:(g + 1) * CHUNK] for g in range(SWA_GROUP)], axis=1)
            v_blk = jnp.concatenate([vcat * vmask_ref[g] for g in range(SWA_GROUP)], axis=0)
            num = _dot(p_all, v_blk)
            den = den_col[(SWA_GROUP - 1) * CHUNK:]
            for g in range(SWA_GROUP - 2, -1, -1):
                den = jnp.where(hlane_ref[...] == g, den_col[g * CHUNK:(g + 1) * CHUNK], den)
            c0 = RET_WIDTH + j * SWA_GROUP * SWA_HEAD_DIM
            o_scr[rs, c0:c0 + SWA_GROUP * SWA_HEAD_DIM] = (num / den).astype(BF16)
        if c % chunks_per_part == 0 and len(parts) < ts // PROJ_ROWS:
            project_part(len(parts))

    kprev_scr[...] = chunk_of("k_ab", ts // CHUNK - 1).astype(F32)
    vprev_scr[...] = chunk_of("v_rep", ts // CHUNK - 1).astype(F32)

    mix = _dot(o_scr[...], w_out_ref[...])
    h = _layer_norm(DEEPNORM_ALPHA * x + mix, ln_g_ref[...], ln_b_ref[...])
    h_ref[...] = h
    _pack_rows(hp_ref, h)

    h_hi = h.astype(BF16)
    h_lo = (h - h_hi.astype(F32)).astype(BF16)
    hi_terms = _dot(h_hi, wr_cat_ref[...])
    logits = hi_terms[:, :LANES] + hi_terms[:, LANES:] + _dot(h_lo, wr_cat_ref[:, :LANES]) + br_ref[...]
    lt = logits.T
    row = lax.broadcasted_iota(jnp.int32, (EXPERTS_PER_GROUP, ts), 0).astype(F32)
    big = 1e9
    ninf = -jnp.inf
    col_max = lambda v: jnp.max(v, axis=0, keepdims=True)
    first_at = lambda v, m: jnp.min(jnp.where(v == m, row, big), axis=0, keepdims=True)
    gl = jnp.where(row < N_GROUPS, lt[GROUP_LANE0:GROUP_LANE0 + EXPERTS_PER_GROUP], ninf)
    gmax = col_max(gl)
    gidx = first_at(gl, gmax)
    g_w = 1.0 / jnp.sum(jnp.exp(gl - gmax), axis=0, keepdims=True)
    el = lt[(N_GROUPS - 1) * EXPERTS_PER_GROUP:N_GROUPS * EXPERTS_PER_GROUP]
    for g in range(N_GROUPS - 2, -1, -1):
        el = jnp.where(gidx == g, lt[g * EXPERTS_PER_GROUP:(g + 1) * EXPERTS_PER_GROUP], el)
    m1 = col_max(el)
    i1 = first_at(el, m1)
    el2 = jnp.where(row == i1, ninf, el)
    m2 = col_max(el2)
    i2 = first_at(el2, m2)
    t = jnp.exp(m2 - m1)
    w1 = g_w / (1.0 + t)
    w2 = g_w * t / (1.0 + t)
    e1 = gidx * EXPERTS_PER_GROUP + i1
    e2 = gidx * EXPERTS_PER_GROUP + i2
    erow = lax.broadcasted_iota(jnp.int32, (N_EXPERTS, ts), 0).astype(F32)
    hit1 = erow == e1
    hit2 = erow == e2
    onehot = (hit1 | hit2).astype(BF16)
    n_rb = ts // RANK_BLOCK
    blocks = [onehot[:, i * RANK_BLOCK:(i + 1) * RANK_BLOCK] for i in range(n_rb)]
    within = _dot(jnp.concatenate(blocks, axis=0), tri_ref[...])
    carry = carry_scr[:, 0:1]
    pieces = []
    for i in range(n_rb):
        pieces.append(within[i * N_EXPERTS:(i + 1) * N_EXPERTS] + carry)
        carry = carry + jnp.sum(blocks[i].astype(F32), axis=1, keepdims=True)
    prefix = jnp.concatenate(pieces, axis=1)
    r1 = jnp.sum(jnp.where(hit1, prefix, 0.0), axis=0, keepdims=True)
    r2 = jnp.sum(jnp.where(hit2, prefix, 0.0), axis=0, keepdims=True)
    carry_scr[...] = jnp.broadcast_to(carry, carry_scr.shape)
    cnt_ref[...] = jnp.broadcast_to(carry, cnt_ref.shape)
    pick = lambda k, v, rest: jnp.where(row == k, v, rest)
    ids_ref[...] = pick(0, e1, pick(1, e2, pick(2, r1, pick(3, r2, 0.0)))).astype(jnp.int32)
    wts_ref[...] = pick(0, w1, pick(1, w2, 0.0))


def _mixer_call(x2, w_in, w_out, gn, ln_g, ln_b, wr_cat, br, consts, bsz, s_len, first_seq):
    cdec, din, qdec, kdec, bias, pen, kmask, vmask, hlane, tri = consts
    t_len = bsz * s_len
    ns = s_len // SEQ_TILE
    tok = lambda b, n, *_: (b * ns + n, 0)
    tok_in = lambda b, n, *_: ((first_seq + b) * ns + n, 0)
    tok_t = lambda b, n, *_: (0, b * ns + n)
    full2 = lambda b, n, *_: (0, 0)
    full3 = lambda b, n, *_: (0, 0, 0)
    grid_spec = pltpu.PrefetchScalarGridSpec(
        num_scalar_prefetch=1,
        grid=(bsz, ns),
        in_specs=[
            pl.BlockSpec((SEQ_TILE, D_MODEL), tok_in),
            pl.BlockSpec((D_MODEL, IN_WIDTH), full2),
            pl.BlockSpec((D_MODEL, D_MODEL), full2),
            pl.BlockSpec((1, RET_WIDTH), full2),
            pl.BlockSpec((1, D_MODEL), full2),
            pl.BlockSpec((1, D_MODEL), full2),
            pl.BlockSpec((D_MODEL, 2 * LANES), full2),
            pl.BlockSpec((1, LANES), full2),
            pl.BlockSpec((RET_HEADS, CHUNK, CHUNK), full3),
            pl.BlockSpec((RET_HEADS, CHUNK, RET_HEAD_DIM), full3),
            pl.BlockSpec((RET_HEADS, CHUNK, RET_HEAD_DIM), full3),
            pl.BlockSpec((SWA_KV_HEADS, SWA_GROUP * CHUNK, 2 * CHUNK), full3),
            pl.BlockSpec((1, 2 * CHUNK), full2),
            pl.BlockSpec((CHUNK, SWA_HEAD_DIM), full2),
            pl.BlockSpec((SWA_GROUP, 2 * CHUNK, SWA_VREP), full3),
            pl.BlockSpec((1, SWA_VREP), full2),
            pl.BlockSpec((RANK_BLOCK, RANK_BLOCK), full2),
        ],
        out_specs=[
            pl.BlockSpec((SEQ_TILE, D_MODEL), tok),
            pl.BlockSpec((SEQ_TILE * ROW_SUBLANES, LANES), tok),
            pl.BlockSpec((ROUTE_ROWS, SEQ_TILE), tok_t),
            pl.BlockSpec((ROUTE_ROWS, SEQ_TILE), tok_t),
            pl.BlockSpec((N_EXPERTS, LANES), full2),
        ],
        scratch_shapes=[
            pltpu.VMEM((RET_HEADS, RET_HEAD_DIM, RET_HEAD_DIM), F32),
            pltpu.VMEM((CHUNK, SWA_KV_WIDTH), F32),
            pltpu.VMEM((CHUNK, SWA_KV_HEADS * SWA_VREP), F32),
            pltpu.VMEM((SEQ_TILE, D_MODEL), BF16),
            pltpu.VMEM((N_EXPERTS, LANES), F32),
        ],
    )
    return pl.pallas_call(
        _mixer_kernel,
        grid_spec=grid_spec,
        out_shape=[
            jax.ShapeDtypeStruct((t_len, D_MODEL), F32),
            jax.ShapeDtypeStruct((t_len * ROW_SUBLANES, LANES), U32),
            jax.ShapeDtypeStruct((ROUTE_ROWS, t_len), jnp.int32),
            jax.ShapeDtypeStruct((ROUTE_ROWS, t_len), F32),
            jax.ShapeDtypeStruct((N_EXPERTS, LANES), F32),
        ],
        compiler_params=pltpu.CompilerParams(
            dimension_semantics=("arbitrary", "arbitrary"), vmem_limit_bytes=VMEM_LIMIT),
        name="mixer_router",
        cost_estimate=_mixer_cost(t_len),
    )(cdec, x2, w_in, w_out, gn, ln_g, ln_b, wr_cat, br, din, qdec, kdec, bias, pen, kmask, vmask, hlane, tri)


def _sc_mesh():
    return plsc.VectorSubcoreMesh(core_axis_name="core", subcore_axis_name="subcore")


def _sc_stream_rows(table_hbm, idx_v, out_hbm, out_base, n_chunks, bufs, sems):
    assert n_chunks % 2 == 0

    def gather(j, b):
        off = pl.multiple_of(j * SC_CHUNK_ROWS, SC_CHUNK_ROWS)
        return pltpu.make_async_copy(table_hbm.at[idx_v.at[pl.ds(off, SC_CHUNK_ROWS)]], bufs[b], sems[b])

    gather(0, 0).start()

    @pl.loop(0, n_chunks, step=2)
    def _(j0):
        for b in range(2):
            j = j0 + b
            gather(j, b).wait()

            @pl.when(j + 1 < n_chunks)
            def _():
                gather(j + 1, 1 - b).start()

            off = pl.multiple_of(j * SC_CHUNK_ROWS, SC_CHUNK_ROWS)
            pltpu.sync_copy(bufs[b], out_hbm.at[pl.ds(out_base + off, SC_CHUNK_ROWS)])


def _sc_dispatch(h_rows, dest, slot0, n_slots):
    t_len = h_rows.shape[0]
    n_assign = dest.shape[0]
    per_worker = n_slots // SC_WORKERS
    n_chunks = per_worker // SC_CHUNK_ROWS
    assert n_chunks * SC_CHUNK_ROWS * SC_WORKERS == n_slots and slot0 + n_slots <= 3 * t_len
    assert per_worker % SC_LANES == 0 and t_len % (SC_LANES * SC_SCAN_UNROLL) == 0 and n_assign == TOP_K * t_len

    @functools.partial(
        pl.kernel, mesh=_sc_mesh(), name="moe_dispatch_sc", cost_estimate=_row_move_cost(n_slots),
        compiler_params=pltpu.CompilerParams(needs_layout_passes=False),
        out_type=jax.ShapeDtypeStruct((n_slots, ROW_SUBLANES, LANES), U32),
        scratch_types=[pltpu.VMEM((n_assign,), jnp.int32),
                       pltpu.VMEM((per_worker,), jnp.int32),
                       pltpu.VMEM((SC_CHUNK_ROWS, ROW_SUBLANES, LANES), U32),
                       pltpu.VMEM((SC_CHUNK_ROWS, ROW_SUBLANES, LANES), U32),
                       pltpu.SemaphoreType.DMA, pltpu.SemaphoreType.DMA])
    def dispatch(h_hbm, dest_hbm, rows_hbm, dest_v, src_v, buf0, buf1, sem, sem1):
        wid = lax.axis_index("subcore") * SC_CORES + lax.axis_index("core")
        local = wid * per_worker
        base = slot0 + local
        piece = n_assign // SC_DEST_PIECES
        copies = []
        for c in range(SC_DEST_PIECES):
            off = pl.multiple_of(lax.rem(c + wid, SC_DEST_PIECES) * piece, SC_LANES)
            copies.append(pltpu.async_copy(dest_hbm.at[pl.ds(off, piece)], dest_v.at[pl.ds(off, piece)], sem))
        for cp in copies:
            cp.wait()
        lane = lax.iota(jnp.int32, SC_LANES)

        def wrap(a):
            a = jnp.where(a >= t_len, a - t_len, a)
            return jnp.where(a >= t_len, a - t_len, a)

        @pl.loop(0, per_worker // SC_LANES)
        def _(i):
            src_v[pl.ds(i * SC_LANES, SC_LANES)] = wrap(base + i * SC_LANES + lane)

        for k in range(TOP_K):
            @pl.loop(0, t_len // (SC_LANES * SC_SCAN_UNROLL))
            def _(i):
                for u in range(SC_SCAN_UNROLL):
                    tok0 = (i * SC_SCAN_UNROLL + u) * SC_LANES
                    d = dest_v[pl.ds(k * t_len + tok0, SC_LANES)] - base
                    hit = (d >= 0) & (d < per_worker)
                    plsc.store_scatter(src_v, [jnp.where(hit, d, 0)], tok0 + lane, mask=hit)

        _sc_stream_rows(h_hbm, src_v, rows_hbm, local, n_chunks, (buf0, buf1), (sem, sem1))

    return dispatch(h_rows, dest)


def _sc_gather_rows(table, idx, name):
    m = idx.shape[0]
    per_worker = m // (SC_CHUNK_ROWS * SC_WORKERS)
    assert per_worker * SC_CHUNK_ROWS * SC_WORKERS == m

    @functools.partial(
        pl.kernel, mesh=_sc_mesh(), name=name, cost_estimate=_row_move_cost(m),
        out_type=jax.ShapeDtypeStruct((m, ROW_SUBLANES, LANES), U32),
        scratch_types=[pltpu.VMEM((per_worker * SC_CHUNK_ROWS,), jnp.int32),
                       pltpu.VMEM((SC_CHUNK_ROWS, ROW_SUBLANES, LANES), U32),
                       pltpu.VMEM((SC_CHUNK_ROWS, ROW_SUBLANES, LANES), U32),
                       pltpu.SemaphoreType.DMA, pltpu.SemaphoreType.DMA])
    def gather(table_hbm, idx_hbm, out_hbm, idx_v, buf0, buf1, sem0, sem1):
        wid = lax.axis_index("subcore") * SC_CORES + lax.axis_index("core")
        base = pl.multiple_of(wid * (per_worker * SC_CHUNK_ROWS), SC_CHUNK_ROWS)
        pltpu.sync_copy(idx_hbm.at[pl.ds(base, per_worker * SC_CHUNK_ROWS)], idx_v)

        _sc_stream_rows(table_hbm, idx_v, out_hbm, base, per_worker, (buf0, buf1), (sem0, sem1))

    return gather(table, idx)


def _expert_kernel(bexp_ref, nused_ref, nxt_ref, slot_ref, rows_ref, wg_hbm, wu_hbm, wd_hbm, *rest, first_block):
    y_ref, stage_g, stage_u, stage_d, wg_s, wu_s, wd_s, sem = rest[-8:]
    p = pl.program_id(0)
    blk = first_block + p
    expert = bexp_ref[blk]
    slot = slot_ref[blk]

    def weight_copies(e, s):
        return [pltpu.make_async_copy(w.at[e], stage.at[s], sem.at[s, i])
                for i, (w, stage) in enumerate(((wg_hbm, stage_g), (wu_hbm, stage_u), (wd_hbm, stage_d)))]

    @pl.when(p == 0)
    def _():
        for cp in weight_copies(expert, slot):
            cp.start()

    @pl.when((p == 0) | (expert != bexp_ref[jnp.maximum(blk - 1, 0)]))
    def _():
        for cp in weight_copies(expert, slot):
            cp.wait()
        wg_s[...] = stage_g[slot].astype(BF16)
        wu_s[...] = stage_u[slot].astype(BF16)
        wd_s[...] = stage_d[slot].astype(BF16)

        @pl.when(nxt_ref[blk] >= 0)
        def _():
            for cp in weight_copies(nxt_ref[blk], 1 - slot):
                cp.start()

    @pl.when(blk < nused_ref[0])
    def _():
        xb = _unpack_rows(rows_ref).astype(BF16)
        g = _dot(xb, wg_s[...])
        u = _dot(xb, wu_s[...])
        a = (_silu(g) * u).astype(BF16)
        _pack_rows(y_ref, _dot(a, wd_s[...]))


def _expert_call(bexp, nused, seg_end, rows, w_gate, w_up, w_down, y_prev, part, n_rows):
    part_blocks = rows.shape[0] // (ROW_SUBLANES * MOE_ROWS)
    n_blocks = bexp.shape[0]
    first_block = part * part_blocks
    limit = jnp.minimum(nused[0], first_block + part_blocks)
    nxt = jnp.where(seg_end < limit, bexp[jnp.minimum(seg_end, n_blocks - 1)], -1).astype(jnp.int32)
    starts = jnp.concatenate([jnp.ones((1,), jnp.int32), (bexp[1:] != bexp[:-1]).astype(jnp.int32)])
    slot = (jnp.cumsum(starts) % 2).astype(jnp.int32)
    last_used = lambda p, nu: jnp.clip(first_block + p, first_block, jnp.maximum(nu[0] - 1, first_block))
    blk = lambda p, be, nu, nx, sl: (last_used(p, nu) - first_block, 0)
    hbm = pl.BlockSpec(memory_space=pl.ANY)
    in_specs = [pl.BlockSpec((MOE_ROWS * ROW_SUBLANES, LANES), blk), hbm, hbm, hbm]
    args = [rows, w_gate, w_up, w_down]
    aliases = {}
    if y_prev is not None:
        in_specs.append(hbm)
        args.append(y_prev)
        aliases = {4 + len(args) - 1: 0}
    grid_spec = pltpu.PrefetchScalarGridSpec(
        num_scalar_prefetch=4,
        grid=(part_blocks,),
        in_specs=in_specs,
        out_specs=pl.BlockSpec((MOE_ROWS * ROW_SUBLANES, LANES), lambda p, be, nu, nx, sl: (last_used(p, nu), 0)),
        scratch_shapes=[
            pltpu.VMEM((2, D_MODEL, D_EXPERT), F32),
            pltpu.VMEM((2, D_MODEL, D_EXPERT), F32),
            pltpu.VMEM((2, D_EXPERT, D_MODEL), F32),
            pltpu.VMEM((D_MODEL, D_EXPERT), BF16),
            pltpu.VMEM((D_MODEL, D_EXPERT), BF16),
            pltpu.VMEM((D_EXPERT, D_MODEL), BF16),
            pltpu.SemaphoreType.DMA((2, 3)),
        ],
    )
    return pl.pallas_call(
        functools.partial(_expert_kernel, first_block=first_block),
        grid_spec=grid_spec,
        out_shape=jax.ShapeDtypeStruct((n_rows * ROW_SUBLANES, LANES), U32),
        input_output_aliases=aliases,
        compiler_params=pltpu.CompilerParams(
            dimension_semantics=("arbitrary",), vmem_limit_bytes=VMEM_LIMIT),
        name="moe_experts",
        cost_estimate=_expert_cost(part_blocks * MOE_ROWS),
    )(bexp, nused, nxt, slot, *args)


def _combine_kernel(h_ref, y0_ref, y1_ref, wts_ref, ln_g_ref, ln_b_ref, *rest):
    out_ref = rest[-1]
    wts = wts_ref[...].T
    ffn = _unpack_rows(y0_ref) * wts[:, 0:1] + _unpack_rows(y1_ref) * wts[:, 1:2]
    out_ref[...] = _layer_norm(DEEPNORM_ALPHA * h_ref[...] + ffn, ln_g_ref[...], ln_b_ref[...])


def _combine_call(h, yk, wts, ln_g, ln_b, out_prev, split, t_total):
    t_len = t_total // TOKEN_SPLITS
    n_tiles = t_len // COMBINE_TILE
    tok = lambda i: (i, 0)
    tok_split = lambda i: (i + split * n_tiles, 0)
    full2 = lambda i: (0, 0)
    tiles = pl.BlockSpec((COMBINE_TILE * ROW_SUBLANES, LANES), tok)
    tiles_k1 = pl.BlockSpec((COMBINE_TILE * ROW_SUBLANES, LANES), lambda i: (i + n_tiles, 0))
    in_specs = [pl.BlockSpec((COMBINE_TILE, D_MODEL), tok_split), tiles, tiles_k1,
                pl.BlockSpec((ROUTE_ROWS, COMBINE_TILE), lambda i: (0, i + split * n_tiles)),
                pl.BlockSpec((1, D_MODEL), full2),
                pl.BlockSpec((1, D_MODEL), full2)]
    args = [h, yk, yk, wts, ln_g, ln_b]
    aliases = {}
    if out_prev is not None:
        in_specs.append(pl.BlockSpec(memory_space=pl.ANY))
        args.append(out_prev)
        aliases = {len(args) - 1: 0}
    return pl.pallas_call(
        _combine_kernel,
        grid=(n_tiles,),
        in_specs=in_specs,
        out_specs=pl.BlockSpec((COMBINE_TILE, D_MODEL), tok_split),
        out_shape=jax.ShapeDtypeStruct((t_total, D_MODEL), F32),
        input_output_aliases=aliases,
        compiler_params=pltpu.CompilerParams(
            dimension_semantics=("arbitrary",), vmem_limit_bytes=VMEM_LIMIT),
        name="moe_combine_ln",
        cost_estimate=pl.CostEstimate(flops=8 * t_len * D_MODEL, transcendentals=t_len,
                                      bytes_accessed=(2 * 4 * D_MODEL + TOP_K * ROW_BYTES) * t_len),
    )(*args)


def _position_tables():
    c = CHUNK
    f = np.float32
    log_g = np.log1p(-np.exp2(-5.0 - np.arange(RET_HEADS, dtype=f))).astype(f)
    idx = np.arange(c, dtype=f)
    diff = idx[:, None] - idx[None, :]
    scale = f(RET_HEAD_DIM ** -0.5)
    din = np.where(diff >= 0, np.exp(log_g[:, None, None] * np.maximum(diff, 0.0)), 0.0).astype(f) * scale
    qdec = np.broadcast_to(np.exp(log_g[:, None] * (idx + 1.0))[:, :, None], (RET_HEADS, c, RET_HEAD_DIM))
    kdec = np.broadcast_to((np.exp(log_g[:, None] * (c - 1.0 - idx)) * scale)[:, :, None],
                           (RET_HEADS, c, RET_HEAD_DIM))
    cdec = np.exp(log_g * c)
    slopes = np.exp2(-8.0 * (np.arange(SWA_HEADS, dtype=f) + 1.0) / SWA_HEADS).astype(f)
    r = np.arange(c)[:, None]
    col = np.arange(c)[None, :]
    dist_prev = (r - col + c).astype(f)
    dist_cur = (r - col).astype(f)
    bprev = np.where((r < col)[None], -slopes[:, None, None] * dist_prev[None], NEG)
    bcur = np.where((r >= col)[None], -slopes[:, None, None] * dist_cur[None], NEG)
    bias = np.concatenate([bprev, bcur], axis=-1).reshape(SWA_KV_HEADS, SWA_GROUP * c, 2 * c)
    key = np.arange(2 * c)
    pen = np.where((key >= 1) & (key < c), NEG, 0.0)[None, :]
    lane_head = np.arange(SWA_VREP) // SWA_HEAD_DIM
    own = lane_head[None, None, :] == np.arange(SWA_GROUP)[:, None, None]
    kmask = np.broadcast_to(np.arange(c)[:, None] > 0, (c, SWA_HEAD_DIM))
    vmask = own & (key[None, :, None] > 0)
    hlane = lane_head[None, :]
    tr = np.arange(RANK_BLOCK)
    tri = tr[:, None] < tr[None, :]
    as_f32 = lambda v: np.ascontiguousarray(v, dtype=f)
    as_bf16 = lambda v: np.ascontiguousarray(v, dtype=f).astype(BF16)
    return (as_f32(cdec), as_f32(din), as_f32(qdec), as_f32(kdec), as_f32(bias), as_f32(pen),
            as_bf16(kmask), as_bf16(vmask), as_f32(hlane), as_bf16(tri))


def _mixer_constants(attn_sinks):
    cdec, din, qdec, kdec, bias, pen, kmask, vmask, hlane, tri = _position_tables()
    sink = attn_sinks.astype(F32).reshape(SWA_KV_HEADS, SWA_GROUP, 1, 1)
    sink = jnp.broadcast_to(sink, (SWA_KV_HEADS, SWA_GROUP, CHUNK, 1)).reshape(SWA_KV_HEADS, SWA_GROUP * CHUNK, 1)
    is_slot = (np.arange(2 * CHUNK) == 0)[None, None, :]
    return cdec, din, qdec, kdec, jnp.where(is_slot, sink, bias), pen, kmask, vmask, hlane, tri


def _router_tables(w_group_router, b_group_router, w_expert_router, b_expert_router):
    w_e = jnp.transpose(w_expert_router, (1, 0, 2)).reshape(D_MODEL, N_EXPERTS)
    w = jnp.concatenate([w_e, w_group_router,
                         jnp.zeros((D_MODEL, LANES - N_EXPERTS - N_GROUPS), F32)], axis=1)
    bias = jnp.concatenate([b_expert_router.reshape(N_EXPERTS), b_group_router,
                            jnp.zeros((LANES - N_EXPERTS - N_GROUPS,), F32)])[None, :]
    w_hi = w.astype(BF16)
    w_lo = (w - w_hi.astype(F32)).astype(BF16)
    return jnp.concatenate([w_hi, w_lo], axis=1), bias


def kernel(x, w_in, ret_gn_g, attn_sinks, w_out, ln1_g, ln1_b, w_group_router, b_group_router,
           w_expert_router, b_expert_router, w_gate, w_up, w_down, ln2_g, ln2_b):
    bsz, s_len, d = x.shape
    assert d == D_MODEL and s_len % SEQ_TILE == 0 and w_in.shape[0] == DEPTH == 1
    assert bsz % TOKEN_SPLITS == 0
    t_len = bsz * s_len
    t_split = t_len // TOKEN_SPLITS
    n_blocks = t_len * TOP_K // MOE_ROWS + N_EXPERTS
    n_rows = n_blocks * MOE_ROWS
    part_rows = n_rows // SLOT_PARTS

    consts = _mixer_constants(attn_sinks[0])
    wr_cat, br = _router_tables(w_group_router[0], b_group_router[0],
                                      w_expert_router[0], b_expert_router[0])
    h, h_packed, ids, wts, cnt = _mixer_call(
        x.reshape(t_len, d), w_in[0].astype(BF16), w_out[0].astype(BF16),
        ret_gn_g[0][None, :], ln1_g[0][None, :], ln1_b[0][None, :], wr_cat, br, consts, bsz, s_len, 0)

    counts = cnt[:, 0].astype(jnp.int32)
    padded = (counts + MOE_ROWS - 1) // MOE_ROWS * MOE_ROWS
    pend = jnp.cumsum(padded)
    pstart = pend - padded
    onehot = ids[0:TOP_K, :, None] == jnp.arange(N_EXPERTS, dtype=jnp.int32)
    dest2 = jnp.sum(jnp.where(onehot, pstart, 0), axis=-1) + ids[TOP_K:2 * TOP_K]
    dest = dest2.reshape(-1)
    nused = (pend[-1:] // MOE_ROWS).astype(jnp.int32)
    blk_start = jnp.minimum(jnp.arange(n_blocks, dtype=jnp.int32), nused[0] - 1) * MOE_ROWS
    bexp = jnp.minimum(jnp.sum(pend[None, :] <= blk_start[:, None], axis=-1), N_EXPERTS - 1).astype(jnp.int32)
    seg_end = jnp.sum(jnp.where(bexp[:, None] == jnp.arange(N_EXPERTS, dtype=jnp.int32), pend // MOE_ROWS, 0),
                      axis=-1).astype(jnp.int32)

    h_tiles = h_packed.reshape(t_len, ROW_SUBLANES, LANES)
    y = None
    for part in range(SLOT_PARTS):
        rows = _sc_dispatch(h_tiles, dest, part * part_rows, part_rows)
        y = _expert_call(bexp, nused, seg_end, rows.reshape(part_rows * ROW_SUBLANES, LANES),
                         w_gate[0], w_up[0], w_down[0], y, part, n_rows)

    y_tiles = y.reshape(n_rows, ROW_SUBLANES, LANES)
    out = None
    for sp in range(TOKEN_SPLITS):
        idx = dest2[:, sp * t_split:(sp + 1) * t_split].reshape(-1)
        yk = _sc_gather_rows(y_tiles, idx, "moe_combine_sc")
        out = _combine_call(h, yk.reshape(TOP_K * t_split * ROW_SUBLANES, LANES), wts,
                            ln2_g[0][None, :], ln2_b[0][None, :], out, sp, t_len)
    return out.reshape(bsz, s_len, d)
```

```python
import functools

import jax
import jax.numpy as jnp
import numpy as np
from jax import lax
from jax.experimental import pallas as pl
from jax.experimental.pallas import tpu as pltpu
from jax.experimental.pallas import tpu_sc as plsc

F32 = jnp.float32
BF16 = jnp.bfloat16
U32 = jnp.uint32

D_MODEL = 1024
RET_HEADS = 4
RET_HEAD_DIM = 128
RET_WIDTH = RET_HEADS * RET_HEAD_DIM
CHUNK = 128
SWA_HEADS = 8
SWA_KV_HEADS = 2
SWA_GROUP = SWA_HEADS // SWA_KV_HEADS
SWA_HEAD_DIM = 64
SWA_WIDTH = SWA_HEADS * SWA_HEAD_DIM
SWA_KV_WIDTH = SWA_KV_HEADS * SWA_HEAD_DIM
IN_WIDTH = 4 * RET_WIDTH + SWA_WIDTH + 2 * SWA_KV_WIDTH
N_GROUPS = 4
EXPERTS_PER_GROUP = 8
N_EXPERTS = N_GROUPS * EXPERTS_PER_GROUP
TOP_K = 2
D_EXPERT = 512
LN_EPS = 1e-5
GN_EPS = 1e-6
DEPTH = 1
DEEPNORM_ALPHA = (2 * DEPTH) ** 0.25
NEG = -1e30

LANES = 128
ROW_SUBLANES = D_MODEL // 2 // LANES
SEQ_TILE = 1024
PROJ_ROWS = 256
RANK_BLOCK = 256
MOE_ROWS = 512
COMBINE_TILE = 512
TOKEN_SPLITS = 2
SLOT_PARTS = 2
SC_CORES = 2
SC_SUBCORES = 16
SC_WORKERS = SC_CORES * SC_SUBCORES
SC_LANES = 16
SC_DEST_PIECES = 8
SC_SCAN_UNROLL = 8
SC_CHUNK_ROWS = 64
VMEM_LIMIT = 56 * 1024 * 1024

_QR, _KR, _VR, _GR = 0, RET_WIDTH, 2 * RET_WIDTH, 3 * RET_WIDTH
_QA = 4 * RET_WIDTH
_KA = _QA + SWA_WIDTH
_VA = _KA + SWA_KV_WIDTH
SWA_VREP = SWA_GROUP * SWA_HEAD_DIM
ROUTE_ROWS = 8
GROUP_LANE0 = N_EXPERTS


def _dot(a, b):
    return jnp.dot(a, b, preferred_element_type=F32)


def _dot_nt(a, b):
    return lax.dot_general(a, b, (((1,), (1,)), ((), ())), preferred_element_type=F32)


def _dot_tn(a, b):
    return lax.dot_general(a, b, (((0,), (0,)), ((), ())), preferred_element_type=F32)


def _layer_norm(z, g, b):
    mu = jnp.mean(z, axis=-1, keepdims=True)
    zc = z - mu
    var = jnp.mean(zc * zc, axis=-1, keepdims=True)
    return zc * lax.rsqrt(var + LN_EPS) * g + b


def _silu(g):
    return g / (1.0 + jnp.exp(-g))


def _tile_heads(v):
    return jnp.concatenate([v[:, j * SWA_HEAD_DIM:(j + 1) * SWA_HEAD_DIM]
                            for j in range(SWA_KV_HEADS) for _ in range(SWA_GROUP)], axis=1)


ROW_BYTES = 4 * ROW_SUBLANES * LANES


def _mixer_cost(t):
    proj = 2 * t * D_MODEL * (IN_WIDTH + D_MODEL + 3 * LANES)
    retention = RET_HEADS * 4 * 2 * t * CHUNK * RET_HEAD_DIM
    swa = SWA_KV_HEADS * 2 * t * SWA_GROUP * 2 * CHUNK * (SWA_HEAD_DIM + 2 * SWA_VREP)
    weights = 2 * D_MODEL * (IN_WIDTH + D_MODEL + 2 * LANES)
    return pl.CostEstimate(flops=proj + retention + swa + 2 * t * RANK_BLOCK * N_EXPERTS,
                           transcendentals=t * (SWA_HEADS * 2 * CHUNK + RET_WIDTH + LANES),
                           bytes_accessed=(2 * 4 * D_MODEL + ROW_BYTES) * t + weights)


def _expert_cost(n_rows):
    return pl.CostEstimate(flops=2 * 3 * n_rows * D_MODEL * D_EXPERT, transcendentals=n_rows * D_EXPERT,
                           bytes_accessed=2 * ROW_BYTES * n_rows + 4 * 3 * N_EXPERTS * D_MODEL * D_EXPERT)


def _row_move_cost(n_rows):
    return pl.CostEstimate(flops=0, transcendentals=0, bytes_accessed=2 * ROW_BYTES * n_rows)


def _pack_rows(ref, val):
    n = val.shape[0]
    half = D_MODEL // 2
    hi = lax.bitcast_convert_type(val[:, :half].astype(BF16).astype(F32), U32)
    lo = lax.bitcast_convert_type(val[:, half:].astype(BF16).astype(F32), U32)
    word = hi | (lo >> 16)
    for j in range(ROW_SUBLANES):
        ref[pl.ds(j, n, stride=ROW_SUBLANES), :] = word[:, j * LANES:(j + 1) * LANES]


def _unpack_rows(ref):
    n = ref.shape[0] // ROW_SUBLANES
    word = jnp.concatenate([ref[pl.ds(j, n, stride=ROW_SUBLANES), :] for j in range(ROW_SUBLANES)], axis=1)
    hi = lax.bitcast_convert_type(word & jnp.uint32(0xFFFF0000), F32)
    lo = lax.bitcast_convert_type(word << 16, F32)
    return jnp.concatenate([hi, lo], axis=1)


def _mixer_kernel(cdec_ref, x_ref, w_in_ref, w_out_ref, gn_ref, ln_g_ref, ln_b_ref,
                  wr_cat_ref, br_ref, din_ref, qdec_ref, kdec_ref,
                  bias_ref, pen_ref, kmask_ref, vmask_ref, hlane_ref, tri_ref,
                  h_ref, hp_ref, ids_ref, wts_ref, cnt_ref,
                  state_scr, kprev_scr, vprev_scr, o_scr, carry_scr):
    b = pl.program_id(0)
    n = pl.program_id(1)
    ts = x_ref.shape[0]

    @pl.when(n == 0)
    def _():
        state_scr[...] = jnp.zeros_like(state_scr)
        kprev_scr[...] = jnp.zeros_like(kprev_scr)
        vprev_scr[...] = jnp.zeros_like(vprev_scr)

    @pl.when((b == 0) & (n == 0))
    def _():
        carry_scr[...] = jnp.zeros_like(carry_scr)

    x = x_ref[...]
    xb = x.astype(BF16)

    chunks_per_part = PROJ_ROWS // CHUNK
    parts = []

    def project_part(part):
        xp = xb[part * PROJ_ROWS:(part + 1) * PROJ_ROWS]
        proj = lambda lo, hi: _dot(xp, w_in_ref[:, lo:hi])
        parts.append(dict(
            q_r=proj(_QR, _KR), k_r=proj(_KR, _VR), v_r=proj(_VR, _GR), g_r=proj(_GR, _QA),
            q_a=proj(_QA, _KA), k_ab=proj(_KA, _VA).astype(BF16),
            v_rep=_tile_heads(proj(_VA, IN_WIDTH).astype(BF16))))

    project_part(0)

    def chunk_of(name, c):
        lo = (c % chunks_per_part) * CHUNK
        return parts[c // chunks_per_part][name][lo:lo + CHUNK]
    first_pen = jnp.where(n == 0, pen_ref[...], 0.0)

    for c in range(ts // CHUNK):
        rs = slice(c * CHUNK, (c + 1) * CHUNK)
        q_r, k_r, v_r, g_r = (chunk_of(name, c) for name in ("q_r", "k_r", "v_r", "g_r"))
        q_a, k_ab, v_rep = (chunk_of(name, c) for name in ("q_a", "k_ab", "v_rep"))
        for hd in range(RET_HEADS):
            cs = slice(hd * RET_HEAD_DIM, (hd + 1) * RET_HEAD_DIM)
            q = q_r[:, cs]
            k = k_r[:, cs]
            v = v_r[:, cs].astype(BF16)
            scores = _dot_nt(q.astype(BF16), k.astype(BF16)) * din_ref[hd]
            st = state_scr[hd]
            o = _dot(jnp.concatenate([scores.astype(BF16), (q * qdec_ref[hd]).astype(BF16)], axis=1),
                     jnp.concatenate([v, st.astype(BF16)], axis=0))
            kv = _dot_tn((k * kdec_ref[hd]).astype(BF16), v)
            state_scr[hd] = st * cdec_ref[hd] + kv
            mu = jnp.mean(o, axis=-1, keepdims=True)
            oc = o - mu
            var = jnp.mean(oc * oc, axis=-1, keepdims=True)
            on = oc * lax.rsqrt(var + GN_EPS) * gn_ref[:, cs] * _silu(g_r[:, cs])
            o_scr[rs, cs] = on.astype(BF16)
        for j in range(SWA_KV_HEADS):
            ks = slice(j * SWA_HEAD_DIM, (j + 1) * SWA_HEAD_DIM)
            vs = slice(j * SWA_VREP, (j + 1) * SWA_VREP)
            if c == 0:
                kp = kprev_scr[:, ks].astype(BF16)
                vp = vprev_scr[:, vs].astype(BF16)
            else:
                kp = chunk_of("k_ab", c - 1)[:, ks]
                vp = chunk_of("v_rep", c - 1)[:, vs]
            kp = kp * kmask_ref[...]
            kcat = jnp.concatenate([kp, k_ab[:, ks]], axis=0)
            vcat = jnp.concatenate([vp, v_rep[:, vs]], axis=0)
            q0 = j * SWA_GROUP * SWA_HEAD_DIM
            qs = jnp.concatenate(
                [q_a[:, q0 + g * SWA_HEAD_DIM:q0 + (g + 1) * SWA_HEAD_DIM] for g in range(SWA_GROUP)],
                axis=0)
            qs = (qs * (SWA_HEAD_DIM ** -0.5)).astype(BF16)
            s = _dot_nt(qs, kcat) + bias_ref[j]
            if c == 0:
                s = s + first_pen
            m = jnp.max(jnp.maximum(s[:, :CHUNK], s[:, CHUNK:]), axis=-1, keepdims=True)
            p32 = jnp.exp(s - m)
            den_col = jnp.sum(p32[:, :CHUNK] + p32[:, CHUNK:], axis=-1, keepdims=True)
            p = p32.astype(BF16)
            p_all = jnp.concatenate([p[g * CHUNK:(g + 1) * CHUNK] for g in range(SWA_GROUP)], axis=1)
            v_blk = jnp.concatenate([vcat * vmask_ref[g] for g in range(SWA_GROUP)], axis=0)
            num = _dot(p_all, v_blk)
            den = den_col[(SWA_GROUP - 1) * CHUNK:]
            for g in range(SWA_GROUP - 2, -1, -1):
                den = jnp.where(hlane_ref[...] == g, den_col[g * CHUNK:(g + 1) * CHUNK], den)
            c0 = RET_WIDTH + j * SWA_GROUP * SWA_HEAD_DIM
            o_scr[rs, c0:c0 + SWA_GROUP * SWA_HEAD_DIM] = (num / den).astype(BF16)
        if c % chunks_per_part == 0 and len(parts) < ts // PROJ_ROWS:
            project_part(len(parts))

    kprev_scr[...] = chunk_of("k_ab", ts // CHUNK - 1).astype(F32)
    vprev_scr[...] = chunk_of("v_rep", ts // CHUNK - 1).astype(F32)

    mix = _dot(o_scr[...], w_out_ref[...])
    h = _layer_norm(DEEPNORM_ALPHA * x + mix, ln_g_ref[...], ln_b_ref[...])
    h_ref[...] = h
    _pack_rows(hp_ref, h)

    h_hi = h.astype(BF16)
    h_lo = (h - h_hi.astype(F32)).astype(BF16)
    hi_terms = _dot(h_hi, wr_cat_ref[...])
    logits = hi_terms[:, :LANES] + hi_terms[:, LANES:] + _dot(h_lo, wr_cat_ref[:, :LANES]) + br_ref[...]
    lt = logits.T
    row = lax.broadcasted_iota(jnp.int32, (EXPERTS_PER_GROUP, ts), 0).astype(F32)
    big = 1e9
    ninf = -jnp.inf
    col_max = lambda v: jnp.max(v, axis=0, keepdims=True)
    first_at = lambda v, m: jnp.min(jnp.where(v == m, row, big), axis=0, keepdims=True)
    gl = jnp.where(row < N_GROUPS, lt[GROUP_LANE0:GROUP_LANE0 + EXPERTS_PER_GROUP], ninf)
    gmax = col_max(gl)
    gidx = first_at(gl, gmax)
    g_w = 1.0 / jnp.sum(jnp.exp(gl - gmax), axis=0, keepdims=True)
    el = lt[(N_GROUPS - 1) * EXPERTS_PER_GROUP:N_GROUPS * EXPERTS_PER_GROUP]
    for g in range(N_GROUPS - 2, -1, -1):
        el = jnp.where(gidx == g, lt[g * EXPERTS_PER_GROUP:(g + 1) * EXPERTS_PER_GROUP], el)
    m1 = col_max(el)
    i1 = first_at(el, m1)
    el2 = jnp.where(row == i1, ninf, el)
    m2 = col_max(el2)
    i2 = first_at(el2, m2)
    t = jnp.exp(m2 - m1)
    w1 = g_w / (1.0 + t)
    w2 = g_w * t / (1.0 + t)
    e1 = gidx * EXPERTS_PER_GROUP + i1
    e2 = gidx * EXPERTS_PER_GROUP + i2
    erow = lax.broadcasted_iota(jnp.int32, (N_EXPERTS, ts), 0).astype(F32)
    hit1 = erow == e1
    hit2 = erow == e2
    onehot = (hit1 | hit2).astype(BF16)
    n_rb = ts // RANK_BLOCK
    blocks = [onehot[:, i * RANK_BLOCK:(i + 1) * RANK_BLOCK] for i in range(n_rb)]
    within = _dot(jnp.concatenate(blocks, axis=0), tri_ref[...])
    carry = carry_scr[:, 0:1]
    pieces = []
    for i in range(n_rb):
        pieces.append(within[i * N_EXPERTS:(i + 1) * N_EXPERTS] + carry)
        carry = carry + jnp.sum(blocks[i].astype(F32), axis=1, keepdims=True)
    prefix = jnp.concatenate(pieces, axis=1)
    r1 = jnp.sum(jnp.where(hit1, prefix, 0.0), axis=0, keepdims=True)
    r2 = jnp.sum(jnp.where(hit2, prefix, 0.0), axis=0, keepdims=True)
    carry_scr[...] = jnp.broadcast_to(carry, carry_scr.shape)
    cnt_ref[...] = jnp.broadcast_to(carry, cnt_ref.shape)
    pick = lambda k, v, rest: jnp.where(row == k, v, rest)
    ids_ref[...] = pick(0, e1, pick(1, e2, pick(2, r1, pick(3, r2, 0.0)))).astype(jnp.int32)
    wts_ref[...] = pick(0, w1, pick(1, w2, 0.0))


def _mixer_call(x2, w_in, w_out, gn, ln_g, ln_b, wr_cat, br, consts, bsz, s_len, first_seq):
    cdec, din, qdec, kdec, bias, pen, kmask, vmask, hlane, tri = consts
    t_len = bsz * s_len
    ns = s_len // SEQ_TILE
    tok = lambda b, n, *_: (b * ns + n, 0)
    tok_in = lambda b, n, *_: ((first_seq + b) * ns + n, 0)
    tok_t = lambda b, n, *_: (0, b * ns + n)
    full2 = lambda b, n, *_: (0, 0)
    full3 = lambda b, n, *_: (0, 0, 0)
    grid_spec = pltpu.PrefetchScalarGridSpec(
        num_scalar_prefetch=1,
        grid=(bsz, ns),
        in_specs=[
            pl.BlockSpec((SEQ_TILE, D_MODEL), tok_in),
            pl.BlockSpec((D_MODEL, IN_WIDTH), full2),
            pl.BlockSpec((D_MODEL, D_MODEL), full2),
            pl.BlockSpec((1, RET_WIDTH), full2),
            pl.BlockSpec((1, D_MODEL), full2),
            pl.BlockSpec((1, D_MODEL), full2),
            pl.BlockSpec((D_MODEL, 2 * LANES), full2),
            pl.BlockSpec((1, LANES), full2),
            pl.BlockSpec((RET_HEADS, CHUNK, CHUNK), full3),
            pl.BlockSpec((RET_HEADS, CHUNK, RET_HEAD_DIM), full3),
            pl.BlockSpec((RET_HEADS, CHUNK, RET_HEAD_DIM), full3),
            pl.BlockSpec((SWA_KV_HEADS, SWA_GROUP * CHUNK, 2 * CHUNK), full3),
            pl.BlockSpec((1, 2 * CHUNK), full2),
            pl.BlockSpec((CHUNK, SWA_HEAD_DIM), full2),
            pl.BlockSpec((SWA_GROUP, 2 * CHUNK, SWA_VREP), full3),
            pl.BlockSpec((1, SWA_VREP), full2),
            pl.BlockSpec((RANK_BLOCK, RANK_BLOCK), full2),
        ],
        out_specs=[
            pl.BlockSpec((SEQ_TILE, D_MODEL), tok),
            pl.BlockSpec((SEQ_TILE * ROW_SUBLANES, LANES), tok),
            pl.BlockSpec((ROUTE_ROWS, SEQ_TILE), tok_t),
            pl.BlockSpec((ROUTE_ROWS, SEQ_TILE), tok_t),
            pl.BlockSpec((N_EXPERTS, LANES), full2),
        ],
        scratch_shapes=[
            pltpu.VMEM((RET_HEADS, RET_HEAD_DIM, RET_HEAD_DIM), F32),
            pltpu.VMEM((CHUNK, SWA_KV_WIDTH), F32),
            pltpu.VMEM((CHUNK, SWA_KV_HEADS * SWA_VREP), F32),
            pltpu.VMEM((SEQ_TILE, D_MODEL), BF16),
            pltpu.VMEM((N_EXPERTS, LANES), F32),
        ],
    )
    return pl.pallas_call(
        _mixer_kernel,
        grid_spec=grid_spec,
        out_shape=[
            jax.ShapeDtypeStruct((t_len, D_MODEL), F32),
            jax.ShapeDtypeStruct((t_len * ROW_SUBLANES, LANES), U32),
            jax.ShapeDtypeStruct((ROUTE_ROWS, t_len), jnp.int32),
            jax.ShapeDtypeStruct((ROUTE_ROWS, t_len), F32),
            jax.ShapeDtypeStruct((N_EXPERTS, LANES), F32),
        ],
        compiler_params=pltpu.CompilerParams(
            dimension_semantics=("arbitrary", "arbitrary"), vmem_limit_bytes=VMEM_LIMIT),
        name="mixer_router",
        cost_estimate=_mixer_cost(t_len),
    )(cdec, x2, w_in, w_out, gn, ln_g, ln_b, wr_cat, br, din, qdec, kdec, bias, pen, kmask, vmask, hlane, tri)


def _sc_mesh():
    return plsc.VectorSubcoreMesh(core_axis_name="core", subcore_axis_name="subcore")


def _sc_stream_rows(table_hbm, idx_v, out_hbm, out_base, n_chunks, bufs, sems):
    assert n_chunks % 2 == 0

    def gather(j, b):
        off = pl.multiple_of(j * SC_CHUNK_ROWS, SC_CHUNK_ROWS)
        return pltpu.make_async_copy(table_hbm.at[idx_v.at[pl.ds(off, SC_CHUNK_ROWS)]], bufs[b], sems[b])

    gather(0, 0).start()

    @pl.loop(0, n_chunks, step=2)
    def _(j0):
        for b in range(2):
            j = j0 + b
            gather(j, b).wait()

            @pl.when(j + 1 < n_chunks)
            def _():
                gather(j + 1, 1 - b).start()

            off = pl.multiple_of(j * SC_CHUNK_ROWS, SC_CHUNK_ROWS)
            pltpu.sync_copy(bufs[b], out_hbm.at[pl.ds(out_base + off, SC_CHUNK_ROWS)])


def _sc_dispatch(h_rows, dest, slot0, n_slots):
    t_len = h_rows.shape[0]
    n_assign = dest.shape[0]
    per_worker = n_slots // SC_WORKERS
    n_chunks = per_worker // SC_CHUNK_ROWS
    assert n_chunks * SC_CHUNK_ROWS * SC_WORKERS == n_slots and slot0 + n_slots <= 3 * t_len
    assert per_worker % SC_LANES == 0 and t_len % (SC_LANES * SC_SCAN_UNROLL) == 0 and n_assign == TOP_K * t_len

    @functools.partial(
        pl.kernel, mesh=_sc_mesh(), name="moe_dispatch_sc", cost_estimate=_row_move_cost(n_slots),
        compiler_params=pltpu.CompilerParams(needs_layout_passes=False),
        out_type=jax.ShapeDtypeStruct((n_slots, ROW_SUBLANES, LANES), U32),
        scratch_types=[pltpu.VMEM((n_assign,), jnp.int32),
                       pltpu.VMEM((per_worker,), jnp.int32),
                       pltpu.VMEM((SC_CHUNK_ROWS, ROW_SUBLANES, LANES), U32),
                       pltpu.VMEM((SC_CHUNK_ROWS, ROW_SUBLANES, LANES), U32),
                       pltpu.SemaphoreType.DMA, pltpu.SemaphoreType.DMA])
    def dispatch(h_hbm, dest_hbm, rows_hbm, dest_v, src_v, buf0, buf1, sem, sem1):
        wid = lax.axis_index("subcore") * SC_CORES + lax.axis_index("core")
        local = wid * per_worker
        base = slot0 + local
        piece = n_assign // SC_DEST_PIECES
        copies = []
        for c in range(SC_DEST_PIECES):
            off = pl.multiple_of(lax.rem(c + wid, SC_DEST_PIECES) * piece, SC_LANES)
            copies.append(pltpu.async_copy(dest_hbm.at[pl.ds(off, piece)], dest_v.at[pl.ds(off, piece)], sem))
        for cp in copies:
            cp.wait()
        lane = lax.iota(jnp.int32, SC_LANES)

        def wrap(a):
            a = jnp.where(a >= t_len, a - t_len, a)
            return jnp.where(a >= t_len, a - t_len, a)

        @pl.loop(0, per_worker // SC_LANES)
        def _(i):
            src_v[pl.ds(i * SC_LANES, SC_LANES)] = wrap(base + i * SC_LANES + lane)

        for k in range(TOP_K):
            @pl.loop(0, t_len // (SC_LANES * SC_SCAN_UNROLL))
            def _(i):
                for u in range(SC_SCAN_UNROLL):
                    tok0 = (i * SC_SCAN_UNROLL + u) * SC_LANES
                    d = dest_v[pl.ds(k * t_len + tok0, SC_LANES)] - base
                    hit = (d >= 0) & (d < per_worker)
                    plsc.store_scatter(src_v, [jnp.where(hit, d, 0)], tok0 + lane, mask=hit)

        _sc_stream_rows(h_hbm, src_v, rows_hbm, local, n_chunks, (buf0, buf1), (sem, sem1))

    return dispatch(h_rows, dest)


def _sc_gather_rows(table, idx, name):
    m = idx.shape[0]
    per_worker = m // (SC_CHUNK_ROWS * SC_WORKERS)
    assert per_worker * SC_CHUNK_ROWS * SC_WORKERS == m

    @functools.partial(
        pl.kernel, mesh=_sc_mesh(), name=name, cost_estimate=_row_move_cost(m),
        out_type=jax.ShapeDtypeStruct((m, ROW_SUBLANES, LANES), U32),
        scratch_types=[pltpu.VMEM((per_worker * SC_CHUNK_ROWS,), jnp.int32),
                       pltpu.VMEM((SC_CHUNK_ROWS, ROW_SUBLANES, LANES), U32),
                       pltpu.VMEM((SC_CHUNK_ROWS, ROW_SUBLANES, LANES), U32),
                       pltpu.SemaphoreType.DMA, pltpu.SemaphoreType.DMA])
    def gather(table_hbm, idx_hbm, out_hbm, idx_v, buf0, buf1, sem0, sem1):
        wid = lax.axis_index("subcore") * SC_CORES + lax.axis_index("core")
        base = pl.multiple_of(wid * (per_worker * SC_CHUNK_ROWS), SC_CHUNK_ROWS)
        pltpu.sync_copy(idx_hbm.at[pl.ds(base, per_worker * SC_CHUNK_ROWS)], idx_v)

        _sc_stream_rows(table_hbm, idx_v, out_hbm, base, per_worker, (buf0, buf1), (sem0, sem1))

    return gather(table, idx)


def _expert_kernel(bexp_ref, nused_ref, nxt_ref, slot_ref, rows_hbm, wg_hbm, wu_hbm, wd_hbm, *rest, first_block,
                   part_blocks):
    y_hbm, stage_g, stage_u, stage_d, wg_s, wu_s, wd_s, rbuf, ybuf, sem, rsem, ysem = rest[-12:]
    block_rows = MOE_ROWS * ROW_SUBLANES
    n_used = jnp.clip(nused_ref[0] - first_block, 0, part_blocks)

    def weight_copies(e, s):
        return [pltpu.make_async_copy(w.at[e], stage.at[s], sem.at[s, i])
                for i, (w, stage) in enumerate(((wg_hbm, stage_g), (wu_hbm, stage_u), (wd_hbm, stage_d)))]

    def row_copy(p):
        start = pl.multiple_of(p * block_rows, block_rows)
        return pltpu.make_async_copy(rows_hbm.at[pl.ds(start, block_rows)], rbuf.at[p % 2], rsem.at[p % 2])

    def out_copy(p):
        start = pl.multiple_of((first_block + p) * block_rows, block_rows)
        return pltpu.make_async_copy(ybuf.at[p % 2], y_hbm.at[pl.ds(start, block_rows)], ysem.at[p % 2])

    @pl.when(n_used > 0)
    def _():
        row_copy(0).start()
        for cp in weight_copies(bexp_ref[first_block], slot_ref[first_block]):
            cp.start()

    def block(p, carry):
        blk = first_block + p
        expert = bexp_ref[blk]
        slot = slot_ref[blk]
        row_copy(p).wait()

        @pl.when(p + 1 < n_used)
        def _():
            row_copy(p + 1).start()

        @pl.when((p == 0) | (expert != bexp_ref[jnp.maximum(blk - 1, 0)]))
        def _():
            for cp in weight_copies(expert, slot):
                cp.wait()
            wg_s[...] = stage_g[slot].astype(BF16)
            wu_s[...] = stage_u[slot].astype(BF16)
            wd_s[...] = stage_d[slot].astype(BF16)

            @pl.when(nxt_ref[blk] >= 0)
            def _():
                for cp in weight_copies(nxt_ref[blk], 1 - slot):
                    cp.start()

        xb = _unpack_rows(rbuf.at[p % 2]).astype(BF16)
        g = _dot(xb, wg_s[...])
        u = _dot(xb, wu_s[...])
        a = (_silu(g) * u).astype(BF16)
        y = _dot(a, wd_s[...])

        @pl.when(p >= 2)
        def _():
            out_copy(p - 2).wait()

        _pack_rows(ybuf.at[p % 2], y)
        out_copy(p).start()
        return carry

    lax.fori_loop(0, n_used, block, 0)

    for back in (2, 1):
        @pl.when(n_used >= back)
        def _():
            out_copy(n_used - back).wait()


def _expert_call(bexp, nused, seg_end, rows, w_gate, w_up, w_down, y_prev, part, n_rows):
    part_blocks = rows.shape[0] // (ROW_SUBLANES * MOE_ROWS)
    n_blocks = bexp.shape[0]
    first_block = part * part_blocks
    limit = jnp.minimum(nused[0], first_block + part_blocks)
    nxt = jnp.where(seg_end < limit, bexp[jnp.minimum(seg_end, n_blocks - 1)], -1).astype(jnp.int32)
    starts = jnp.concatenate([jnp.ones((1,), jnp.int32), (bexp[1:] != bexp[:-1]).astype(jnp.int32)])
    slot = (jnp.cumsum(starts) % 2).astype(jnp.int32)
    hbm = pl.BlockSpec(memory_space=pl.ANY)
    in_specs = [hbm, hbm, hbm, hbm]
    args = [rows, w_gate, w_up, w_down]
    aliases = {}
    if y_prev is not None:
        in_specs.append(hbm)
        args.append(y_prev)
        aliases = {4 + len(args) - 1: 0}
    grid_spec = pltpu.PrefetchScalarGridSpec(
        num_scalar_prefetch=4,
        grid=(1,),
        in_specs=in_specs,
        out_specs=hbm,
        scratch_shapes=[
            pltpu.VMEM((2, D_MODEL, D_EXPERT), F32),
            pltpu.VMEM((2, D_MODEL, D_EXPERT), F32),
            pltpu.VMEM((2, D_EXPERT, D_MODEL), F32),
            pltpu.VMEM((D_MODEL, D_EXPERT), BF16),
            pltpu.VMEM((D_MODEL, D_EXPERT), BF16),
            pltpu.VMEM((D_EXPERT, D_MODEL), BF16),
            pltpu.VMEM((2, MOE_ROWS * ROW_SUBLANES, LANES), U32),
            pltpu.VMEM((2, MOE_ROWS * ROW_SUBLANES, LANES), U32),
            pltpu.SemaphoreType.DMA((2, 3)),
            pltpu.SemaphoreType.DMA((2,)),
            pltpu.SemaphoreType.DMA((2,)),
        ],
    )
    return pl.pallas_call(
        functools.partial(_expert_kernel, first_block=first_block, part_blocks=part_blocks),
        grid_spec=grid_spec,
        out_shape=jax.ShapeDtypeStruct((n_rows * ROW_SUBLANES, LANES), U32),
        input_output_aliases=aliases,
        compiler_params=pltpu.CompilerParams(
            dimension_semantics=("arbitrary",), vmem_limit_bytes=VMEM_LIMIT),
        name="moe_experts",
        cost_estimate=_expert_cost(part_blocks * MOE_ROWS),
    )(bexp, nused, nxt, slot, *args)


def _combine_kernel(h_ref, y0_ref, y1_ref, wts_ref, ln_g_ref, ln_b_ref, *rest):
    out_ref = rest[-1]
    wts = wts_ref[...].T
    ffn = _unpack_rows(y0_ref) * wts[:, 0:1] + _unpack_rows(y1_ref) * wts[:, 1:2]
    out_ref[...] = _layer_norm(DEEPNORM_ALPHA * h_ref[...] + ffn, ln_g_ref[...], ln_b_ref[...])


def _combine_call(h, yk, wts, ln_g, ln_b, out_prev, split, t_total):
    t_len = t_total // TOKEN_SPLITS
    n_tiles = t_len // COMBINE_TILE
    tok = lambda i: (i, 0)
    tok_split = lambda i: (i + split * n_tiles, 0)
    full2 = lambda i: (0, 0)
    tiles = pl.BlockSpec((COMBINE_TILE * ROW_SUBLANES, LANES), tok)
    tiles_k1 = pl.BlockSpec((COMBINE_TILE * ROW_SUBLANES, LANES), lambda i: (i + n_tiles, 0))
    in_specs = [pl.BlockSpec((COMBINE_TILE, D_MODEL), tok_split), tiles, tiles_k1,
                pl.BlockSpec((ROUTE_ROWS, COMBINE_TILE), lambda i: (0, i + split * n_tiles)),
                pl.BlockSpec((1, D_MODEL), full2),
                pl.BlockSpec((1, D_MODEL), full2)]
    args = [h, yk, yk, wts, ln_g, ln_b]
    aliases = {}
    if out_prev is not None:
        in_specs.append(pl.BlockSpec(memory_space=pl.ANY))
        args.append(out_prev)
        aliases = {len(args) - 1: 0}
    return pl.pallas_call(
        _combine_kernel,
        grid=(n_tiles,),
        in_specs=in_specs,
        out_specs=pl.BlockSpec((COMBINE_TILE, D_MODEL), tok_split),
        out_shape=jax.ShapeDtypeStruct((t_total, D_MODEL), F32),
        input_output_aliases=aliases,
        compiler_params=pltpu.CompilerParams(
            dimension_semantics=("arbitrary",), vmem_limit_bytes=VMEM_LIMIT),
        name="moe_combine_ln",
        cost_estimate=pl.CostEstimate(flops=8 * t_len * D_MODEL, transcendentals=t_len,
                                      bytes_accessed=(2 * 4 * D_MODEL + TOP_K * ROW_BYTES) * t_len),
    )(*args)


def _position_tables():
    c = CHUNK
    f = np.float32
    log_g = np.log1p(-np.exp2(-5.0 - np.arange(RET_HEADS, dtype=f))).astype(f)
    idx = np.arange(c, dtype=f)
    diff = idx[:, None] - idx[None, :]
    scale = f(RET_HEAD_DIM ** -0.5)
    din = np.where(diff >= 0, np.exp(log_g[:, None, None] * np.maximum(diff, 0.0)), 0.0).astype(f) * scale
    qdec = np.broadcast_to(np.exp(log_g[:, None] * (idx + 1.0))[:, :, None], (RET_HEADS, c, RET_HEAD_DIM))
    kdec = np.broadcast_to((np.exp(log_g[:, None] * (c - 1.0 - idx)) * scale)[:, :, None],
                           (RET_HEADS, c, RET_HEAD_DIM))
    cdec = np.exp(log_g * c)
    slopes = np.exp2(-8.0 * (np.arange(SWA_HEADS, dtype=f) + 1.0) / SWA_HEADS).astype(f)
    r = np.arange(c)[:, None]
    col = np.arange(c)[None, :]
    dist_prev = (r - col + c).astype(f)
    dist_cur = (r - col).astype(f)
    bprev = np.where((r < col)[None], -slopes[:, None, None] * dist_prev[None], NEG)
    bcur = np.where((r >= col)[None], -slopes[:, None, None] * dist_cur[None], NEG)
    bias = np.concatenate([bprev, bcur], axis=-1).reshape(SWA_KV_HEADS, SWA_GROUP * c, 2 * c)
    key = np.arange(2 * c)
    pen = np.where((key >= 1) & (key < c), NEG, 0.0)[None, :]
    lane_head = np.arange(SWA_VREP) // SWA_HEAD_DIM
    own = lane_head[None, None, :] == np.arange(SWA_GROUP)[:, None, None]
    kmask = np.broadcast_to(np.arange(c)[:, None] > 0, (c, SWA_HEAD_DIM))
    vmask = own & (key[None, :, None] > 0)
    hlane = lane_head[None, :]
    tr = np.arange(RANK_BLOCK)
    tri = tr[:, None] < tr[None, :]
    as_f32 = lambda v: np.ascontiguousarray(v, dtype=f)
    as_bf16 = lambda v: np.ascontiguousarray(v, dtype=f).astype(BF16)
    return (as_f32(cdec), as_f32(din), as_f32(qdec), as_f32(kdec), as_f32(bias), as_f32(pen),
            as_bf16(kmask), as_bf16(vmask), as_f32(hlane), as_bf16(tri))


def _mixer_constants(attn_sinks):
    cdec, din, qdec, kdec, bias, pen, kmask, vmask, hlane, tri = _position_tables()
    sink = attn_sinks.astype(F32).reshape(SWA_KV_HEADS, SWA_GROUP, 1, 1)
    sink = jnp.broadcast_to(sink, (SWA_KV_HEADS, SWA_GROUP, CHUNK, 1)).reshape(SWA_KV_HEADS, SWA_GROUP * CHUNK, 1)
    is_slot = (np.arange(2 * CHUNK) == 0)[None, None, :]
    return cdec, din, qdec, kdec, jnp.where(is_slot, sink, bias), pen, kmask, vmask, hlane, tri


def _router_tables(w_group_router, b_group_router, w_expert_router, b_expert_router):
    w_e = jnp.transpose(w_expert_router, (1, 0, 2)).reshape(D_MODEL, N_EXPERTS)
    w = jnp.concatenate([w_e, w_group_router,
                         jnp.zeros((D_MODEL, LANES - N_EXPERTS - N_GROUPS), F32)], axis=1)
    bias = jnp.concatenate([b_expert_router.reshape(N_EXPERTS), b_group_router,
                            jnp.zeros((LANES - N_EXPERTS - N_GROUPS,), F32)])[None, :]
    w_hi = w.astype(BF16)
    w_lo = (w - w_hi.astype(F32)).astype(BF16)
    return jnp.concatenate([w_hi, w_lo], axis=1), bias


def kernel(x, w_in, ret_gn_g, attn_sinks, w_out, ln1_g, ln1_b, w_group_router, b_group_router,
           w_expert_router, b_expert_router, w_gate, w_up, w_down, ln2_g, ln2_b):
    bsz, s_len, d = x.shape
    assert d == D_MODEL and s_len % SEQ_TILE == 0 and w_in.shape[0] == DEPTH == 1
    assert bsz % TOKEN_SPLITS == 0
    t_len = bsz * s_len
    t_split = t_len // TOKEN_SPLITS
    n_blocks = t_len * TOP_K // MOE_ROWS + N_EXPERTS
    n_rows = n_blocks * MOE_ROWS
    part_rows = n_rows // SLOT_PARTS

    consts = _mixer_constants(attn_sinks[0])
    wr_cat, br = _router_tables(w_group_router[0], b_group_router[0],
                                      w_expert_router[0], b_expert_router[0])
    h, h_packed, ids, wts, cnt = _mixer_call(
        x.reshape(t_len, d), w_in[0].astype(BF16), w_out[0].astype(BF16),
        ret_gn_g[0][None, :], ln1_g[0][None, :], ln1_b[0][None, :], wr_cat, br, consts, bsz, s_len, 0)

    counts = cnt[:, 0].astype(jnp.int32)
    padded = (counts + MOE_ROWS - 1) // MOE_ROWS * MOE_ROWS
    pend = jnp.cumsum(padded)
    pstart = pend - padded
    onehot = ids[0:TOP_K, :, None] == jnp.arange(N_EXPERTS, dtype=jnp.int32)
    dest2 = jnp.sum(jnp.where(onehot, pstart, 0), axis=-1) + ids[TOP_K:2 * TOP_K]
    dest = dest2.reshape(-1)
    nused = (pend[-1:] // MOE_ROWS).astype(jnp.int32)
    blk_start = jnp.minimum(jnp.arange(n_blocks, dtype=jnp.int32), nused[0] - 1) * MOE_ROWS
    bexp = jnp.minimum(jnp.sum(pend[None, :] <= blk_start[:, None], axis=-1), N_EXPERTS - 1).astype(jnp.int32)
    seg_end = jnp.sum(jnp.where(bexp[:, None] == jnp.arange(N_EXPERTS, dtype=jnp.int32), pend // MOE_ROWS, 0),
                      axis=-1).astype(jnp.int32)

    h_tiles = h_packed.reshape(t_len, ROW_SUBLANES, LANES)
    y = None
    for part in range(SLOT_PARTS):
        rows = _sc_dispatch(h_tiles, dest, part * part_rows, part_rows)
        y = _expert_call(bexp, nused, seg_end, rows.reshape(part_rows * ROW_SUBLANES, LANES),
                         w_gate[0], w_up[0], w_down[0], y, part, n_rows)

    y_tiles = y.reshape(n_rows, ROW_SUBLANES, LANES)
    out = None
    for sp in range(TOKEN_SPLITS):
        idx = dest2[:, sp * t_split:(sp + 1) * t_split].reshape(-1)
        yk = _sc_gather_rows(y_tiles, idx, "moe_combine_sc")
        out = _combine_call(h, yk.reshape(TOP_K * t_split * ROW_SUBLANES, LANES), wts,
                            ln2_g[0][None, :], ln2_b[0][None, :], out, sp, t_len)
    return out.reshape(bsz, s_len, d)
```

```python
import functools

import jax
import jax.numpy as jnp
import numpy as np
from jax import lax
from jax.experimental import pallas as pl
from jax.experimental.pallas import tpu as pltpu
from jax.experimental.pallas import tpu_sc as plsc

F32 = jnp.float32
BF16 = jnp.bfloat16
U32 = jnp.uint32

D_MODEL = 1024
RET_HEADS = 4
RET_HEAD_DIM = 128
RET_WIDTH = RET_HEADS * RET_HEAD_DIM
CHUNK = 128
SWA_HEADS = 8
SWA_KV_HEADS = 2
SWA_GROUP = SWA_HEADS // SWA_KV_HEADS
SWA_HEAD_DIM = 64
SWA_WIDTH = SWA_HEADS * SWA_HEAD_DIM
SWA_KV_WIDTH = SWA_KV_HEADS * SWA_HEAD_DIM
IN_WIDTH = 4 * RET_WIDTH + SWA_WIDTH + 2 * SWA_KV_WIDTH
N_GROUPS = 4
EXPERTS_PER_GROUP = 8
N_EXPERTS = N_GROUPS * EXPERTS_PER_GROUP
TOP_K = 2
D_EXPERT = 512
LN_EPS = 1e-5
GN_EPS = 1e-6
DEPTH = 1
DEEPNORM_ALPHA = (2 * DEPTH) ** 0.25
NEG = -1e30

LANES = 128
ROW_SUBLANES = D_MODEL // 2 // LANES
SEQ_TILE = 1024
PROJ_ROWS = 256
RANK_BLOCK = 256
MOE_ROWS = 512
COMBINE_TILE = 512
TOKEN_SPLITS = 2
SLOT_PARTS = 2
SC_CORES = 2
SC_SUBCORES = 16
SC_WORKERS = SC_CORES * SC_SUBCORES
SC_LANES = 16
SC_DEST_PIECES = 8
SC_SCAN_UNROLL = 8
SC_CHUNK_ROWS = 64
VMEM_LIMIT = 56 * 1024 * 1024

_QR, _KR, _VR, _GR = 0, RET_WIDTH, 2 * RET_WIDTH, 3 * RET_WIDTH
_QA = 4 * RET_WIDTH
_KA = _QA + SWA_WIDTH
_VA = _KA + SWA_KV_WIDTH
SWA_VREP = SWA_GROUP * SWA_HEAD_DIM
ROUTE_ROWS = 8
GROUP_LANE0 = N_EXPERTS


def _dot(a, b):
    return jnp.dot(a, b, preferred_element_type=F32)


def _dot_nt(a, b):
    return lax.dot_general(a, b, (((1,), (1,)), ((), ())), preferred_element_type=F32)


def _dot_tn(a, b):
    return lax.dot_general(a, b, (((0,), (0,)), ((), ())), preferred_element_type=F32)


def _layer_norm(z, g, b):
    mu = jnp.mean(z, axis=-1, keepdims=True)
    zc = z - mu
    var = jnp.mean(zc * zc, axis=-1, keepdims=True)
    return zc * lax.rsqrt(var + LN_EPS) * g + b


def _silu(g):
    return g / (1.0 + jnp.exp(-g))


def _tile_heads(v):
    return jnp.concatenate([v[:, j * SWA_HEAD_DIM:(j + 1) * SWA_HEAD_DIM]
                            for j in range(SWA_KV_HEADS) for _ in range(SWA_GROUP)], axis=1)


ROW_BYTES = 4 * ROW_SUBLANES * LANES


def _mixer_cost(t):
    proj = 2 * t * D_MODEL * (IN_WIDTH + D_MODEL + 3 * LANES)
    retention = RET_HEADS * 4 * 2 * t * CHUNK * RET_HEAD_DIM
    swa = SWA_KV_HEADS * 2 * t * SWA_GROUP * 2 * CHUNK * (SWA_HEAD_DIM + 2 * SWA_VREP)
    weights = 2 * D_MODEL * (IN_WIDTH + D_MODEL + 2 * LANES)
    return pl.CostEstimate(flops=proj + retention + swa + 2 * t * RANK_BLOCK * N_EXPERTS,
                           transcendentals=t * (SWA_HEADS * 2 * CHUNK + RET_WIDTH + LANES),
                           bytes_accessed=(2 * 4 * D_MODEL + ROW_BYTES) * t + weights)


def _expert_cost(n_rows):
    return pl.CostEstimate(flops=2 * 3 * n_rows * D_MODEL * D_EXPERT, transcendentals=n_rows * D_EXPERT,
                           bytes_accessed=2 * ROW_BYTES * n_rows + 4 * 3 * N_EXPERTS * D_MODEL * D_EXPERT)


def _row_move_cost(n_rows):
    return pl.CostEstimate(flops=0, transcendentals=0, bytes_accessed=2 * ROW_BYTES * n_rows)


def _pack_rows(ref, val):
    n = val.shape[0]
    half = D_MODEL // 2
    hi = lax.bitcast_convert_type(val[:, :half].astype(BF16).astype(F32), U32)
    lo = lax.bitcast_convert_type(val[:, half:].astype(BF16).astype(F32), U32)
    word = hi | (lo >> 16)
    for j in range(ROW_SUBLANES):
        ref[pl.ds(j, n, stride=ROW_SUBLANES), :] = word[:, j * LANES:(j + 1) * LANES]


def _unpack_rows(ref):
    n = ref.shape[0] // ROW_SUBLANES
    word = jnp.concatenate([ref[pl.ds(j, n, stride=ROW_SUBLANES), :] for j in range(ROW_SUBLANES)], axis=1)
    hi = lax.bitcast_convert_type(word & jnp.uint32(0xFFFF0000), F32)
    lo = lax.bitcast_convert_type(word << 16, F32)
    return jnp.concatenate([hi, lo], axis=1)


def _mixer_kernel(cdec_ref, x_ref, w_in_ref, w_out_ref, gn_ref, ln_g_ref, ln_b_ref,
                  wr_cat_ref, br_ref, din_ref, qdec_ref, kdec_ref,
                  bias_ref, pen_ref, kmask_ref, vmask_ref, hlane_ref, tri_ref,
                  h_ref, hp_ref, ids_ref, wts_ref, cnt_ref,
                  state_scr, kprev_scr, vprev_scr, o_scr, carry_scr):
    b = pl.program_id(0)
    n = pl.program_id(1)
    ts = x_ref.shape[0]

    @pl.when(n == 0)
    def _():
        state_scr[...] = jnp.zeros_like(state_scr)
        kprev_scr[...] = jnp.zeros_like(kprev_scr)
        vprev_scr[...] = jnp.zeros_like(vprev_scr)

    @pl.when((b == 0) & (n == 0))
    def _():
        carry_scr[...] = jnp.zeros_like(carry_scr)

    x = x_ref[...]
    xb = x.astype(BF16)

    chunks_per_part = PROJ_ROWS // CHUNK
    parts = []

    def project_part(part):
        xp = xb[part * PROJ_ROWS:(part + 1) * PROJ_ROWS]
        proj = lambda lo, hi: _dot(xp, w_in_ref[:, lo:hi])
        parts.append(dict(
            q_r=proj(_QR, _KR), k_r=proj(_KR, _VR), v_r=proj(_VR, _GR), g_r=proj(_GR, _QA),
            q_a=proj(_QA, _KA), k_ab=proj(_KA, _VA).astype(BF16),
            v_rep=_tile_heads(proj(_VA, IN_WIDTH).astype(BF16))))

    project_part(0)

    def chunk_of(name, c):
        lo = (c % chunks_per_part) * CHUNK
        return parts[c // chunks_per_part][name][lo:lo + CHUNK]
    first_pen = jnp.where(n == 0, pen_ref[...], 0.0)

    for c in range(ts // CHUNK):
        rs = slice(c * CHUNK, (c + 1) * CHUNK)
        q_r, k_r, v_r, g_r = (chunk_of(name, c) for name in ("q_r", "k_r", "v_r", "g_r"))
        q_a, k_ab, v_rep = (chunk_of(name, c) for name in ("q_a", "k_ab", "v_rep"))
        for hd in range(RET_HEADS):
            cs = slice(hd * RET_HEAD_DIM, (hd + 1) * RET_HEAD_DIM)
            q = q_r[:, cs]
            k = k_r[:, cs]
            v = v_r[:, cs].astype(BF16)
            scores = _dot_nt(q.astype(BF16), k.astype(BF16)) * din_ref[hd]
            st = state_scr[hd]
            o = _dot(jnp.concatenate([scores.astype(BF16), (q * qdec_ref[hd]).astype(BF16)], axis=1),
                     jnp.concatenate([v, st.astype(BF16)], axis=0))
            kv = _dot_tn((k * kdec_ref[hd]).astype(BF16), v)
            state_scr[hd] = st * cdec_ref[hd] + kv
            mu = jnp.mean(o, axis=-1, keepdims=True)
            oc = o - mu
            var = jnp.mean(oc * oc, axis=-1, keepdims=True)
            on = oc * lax.rsqrt(var + GN_EPS) * gn_ref[:, cs] * _silu(g_r[:, cs])
            o_scr[rs, cs] = on.astype(BF16)
        for j in range(SWA_KV_HEADS):
            ks = slice(j * SWA_HEAD_DIM, (j + 1) * SWA_HEAD_DIM)
            vs = slice(j * SWA_VREP, (j + 1) * SWA_VREP)
            if c == 0:
                kp = kprev_scr[:, ks].astype(BF16)
                vp = vprev_scr[:, vs].astype(BF16)
            else:
                kp = chunk_of("k_ab", c - 1)[:, ks]
                vp = chunk_of("v_rep", c - 1)[:, vs]
            kp = kp * kmask_ref[...]
            kcat = jnp.concatenate([kp, k_ab[:, ks]], axis=0)
            vcat = jnp.concatenate([vp, v_rep[:, vs]], axis=0)
            q0 = j * SWA_GROUP * SWA_HEAD_DIM
            qs = jnp.concatenate(
                [q_a[:, q0 + g * SWA_HEAD_DIM:q0 + (g + 1) * SWA_HEAD_DIM] for g in range(SWA_GROUP)],
                axis=0)
            qs = (qs * (SWA_HEAD_DIM ** -0.5)).astype(BF16)
            s = _dot_nt(qs, kcat) + bias_ref[j]
            if c == 0:
                s = s + first_pen
            m = jnp.max(jnp.maximum(s[:, :CHUNK], s[:, CHUNK:]), axis=-1, keepdims=True)
            p32 = jnp.exp(s - m)
            den_col = jnp.sum(p32[:, :CHUNK] + p32[:, CHUNK:], axis=-1, keepdims=True)
            p = p32.astype(BF16)
            p_all = jnp.concatenate([p[g * CHUNK:(g + 1) * CHUNK] for g in range(SWA_GROUP)], axis=1)
            v_blk = jnp.concatenate([vcat * vmask_ref[g] for g in range(SWA_GROUP)], axis=0)
            num = _dot(p_all, v_blk)
            den = den_col[(SWA_GROUP - 1) * CHUNK:]
            for g in range(SWA_GROUP - 2, -1, -1):
                den = jnp.where(hlane_ref[...] == g, den_col[g * CHUNK:(g + 1) * CHUNK], den)
            c0 = RET_WIDTH + j * SWA_GROUP * SWA_HEAD_DIM
            o_scr[rs, c0:c0 + SWA_GROUP * SWA_HEAD_DIM] = (num / den).astype(BF16)
        if c % chunks_per_part == 0 and len(parts) < ts // PROJ_ROWS:
            project_part(len(parts))

    kprev_scr[...] = chunk_of("k_ab", ts // CHUNK - 1).astype(F32)
    vprev_scr[...] = chunk_of("v_rep", ts // CHUNK - 1).astype(F32)

    mix = _dot(o_scr[...], w_out_ref[...])
    h = _layer_norm(DEEPNORM_ALPHA * x + mix, ln_g_ref[...], ln_b_ref[...])
    h_ref[...] = h
    _pack_rows(hp_ref, h)

    h_hi = h.astype(BF16)
    h_lo = (h - h_hi.astype(F32)).astype(BF16)
    hi_terms = _dot(h_hi, wr_cat_ref[...])
    logits = hi_terms[:, :LANES] + hi_terms[:, LANES:] + _dot(h_lo, wr_cat_ref[:, :LANES]) + br_ref[...]
    lt = logits.T
    row = lax.broadcasted_iota(jnp.int32, (EXPERTS_PER_GROUP, ts), 0).astype(F32)
    big = 1e9
    ninf = -jnp.inf
    col_max = lambda v: jnp.max(v, axis=0, keepdims=True)
    first_at = lambda v, m: jnp.min(jnp.where(v == m, row, big), axis=0, keepdims=True)
    gl = jnp.where(row < N_GROUPS, lt[GROUP_LANE0:GROUP_LANE0 + EXPERTS_PER_GROUP], ninf)
    gmax = col_max(gl)
    gidx = first_at(gl, gmax)
    g_w = 1.0 / jnp.sum(jnp.exp(gl - gmax), axis=0, keepdims=True)
    el = lt[(N_GROUPS - 1) * EXPERTS_PER_GROUP:N_GROUPS * EXPERTS_PER_GROUP]
    for g in range(N_GROUPS - 2, -1, -1):
        el = jnp.where(gidx == g, lt[g * EXPERTS_PER_GROUP:(g + 1) * EXPERTS_PER_GROUP], el)
    m1 = col_max(el)
    i1 = first_at(el, m1)
    el2 = jnp.where(row == i1, ninf, el)
    m2 = col_max(el2)
    i2 = first_at(el2, m2)
    t = jnp.exp(m2 - m1)
    w1 = g_w / (1.0 + t)
    w2 = g_w * t / (1.0 + t)
    e1 = gidx * EXPERTS_PER_GROUP + i1
    e2 = gidx * EXPERTS_PER_GROUP + i2
    erow = lax.broadcasted_iota(jnp.int32, (N_EXPERTS, ts), 0).astype(F32)
    hit1 = erow == e1
    hit2 = erow == e2
    onehot = (hit1 | hit2).astype(BF16)
    n_rb = ts // RANK_BLOCK
    blocks = [onehot[:, i * RANK_BLOCK:(i + 1) * RANK_BLOCK] for i in range(n_rb)]
    within = _dot(jnp.concatenate(blocks, axis=0), tri_ref[...])
    carry = carry_scr[:, 0:1]
    pieces = []
    for i in range(n_rb):
        pieces.append(within[i * N_EXPERTS:(i + 1) * N_EXPERTS] + carry)
        carry = carry + jnp.sum(blocks[i].astype(F32), axis=1, keepdims=True)
    prefix = jnp.concatenate(pieces, axis=1)
    r1 = jnp.sum(jnp.where(hit1, prefix, 0.0), axis=0, keepdims=True)
    r2 = jnp.sum(jnp.where(hit2, prefix, 0.0), axis=0, keepdims=True)
    carry_scr[...] = jnp.broadcast_to(carry, carry_scr.shape)
    cnt_ref[...] = jnp.broadcast_to(carry, cnt_ref.shape)
    pick = lambda k, v, rest: jnp.where(row == k, v, rest)
    ids_ref[...] = pick(0, e1, pick(1, e2, pick(2, r1, pick(3, r2, 0.0)))).astype(jnp.int32)
    wts_ref[...] = pick(0, w1, pick(1, w2, 0.0))


def _mixer_call(x2, w_in, w_out, gn, ln_g, ln_b, wr_cat, br, consts, bsz, s_len, first_seq):
    cdec, din, qdec, kdec, bias, pen, kmask, vmask, hlane, tri = consts
    t_len = bsz * s_len
    ns = s_len // SEQ_TILE
    tok = lambda b, n, *_: (b * ns + n, 0)
    tok_in = lambda b, n, *_: ((first_seq + b) * ns + n, 0)
    tok_t = lambda b, n, *_: (0, b * ns + n)
    full2 = lambda b, n, *_: (0, 0)
    full3 = lambda b, n, *_: (0, 0, 0)
    grid_spec = pltpu.PrefetchScalarGridSpec(
        num_scalar_prefetch=1,
        grid=(bsz, ns),
        in_specs=[
            pl.BlockSpec((SEQ_TILE, D_MODEL), tok_in),
            pl.BlockSpec((D_MODEL, IN_WIDTH), full2),
            pl.BlockSpec((D_MODEL, D_MODEL), full2),
            pl.BlockSpec((1, RET_WIDTH), full2),
            pl.BlockSpec((1, D_MODEL), full2),
            pl.BlockSpec((1, D_MODEL), full2),
            pl.BlockSpec((D_MODEL, 2 * LANES), full2),
            pl.BlockSpec((1, LANES), full2),
            pl.BlockSpec((RET_HEADS, CHUNK, CHUNK), full3),
            pl.BlockSpec((RET_HEADS, CHUNK, RET_HEAD_DIM), full3),
            pl.BlockSpec((RET_HEADS, CHUNK, RET_HEAD_DIM), full3),
            pl.BlockSpec((SWA_KV_HEADS, SWA_GROUP * CHUNK, 2 * CHUNK), full3),
            pl.BlockSpec((1, 2 * CHUNK), full2),
            pl.BlockSpec((CHUNK, SWA_HEAD_DIM), full2),
            pl.BlockSpec((SWA_GROUP, 2 * CHUNK, SWA_VREP), full3),
            pl.BlockSpec((1, SWA_VREP), full2),
            pl.BlockSpec((RANK_BLOCK, RANK_BLOCK), full2),
        ],
        out_specs=[
            pl.BlockSpec((SEQ_TILE, D_MODEL), tok),
            pl.BlockSpec((SEQ_TILE * ROW_SUBLANES, LANES), tok),
            pl.BlockSpec((ROUTE_ROWS, SEQ_TILE), tok_t),
            pl.BlockSpec((ROUTE_ROWS, SEQ_TILE), tok_t),
            pl.BlockSpec((N_EXPERTS, LANES), full2),
        ],
        scratch_shapes=[
            pltpu.VMEM((RET_HEADS, RET_HEAD_DIM, RET_HEAD_DIM), F32),
            pltpu.VMEM((CHUNK, SWA_KV_WIDTH), F32),
            pltpu.VMEM((CHUNK, SWA_KV_HEADS * SWA_VREP), F32),
            pltpu.VMEM((SEQ_TILE, D_MODEL), BF16),
            pltpu.VMEM((N_EXPERTS, LANES), F32),
        ],
    )
    return pl.pallas_call(
        _mixer_kernel,
        grid_spec=grid_spec,
        out_shape=[
            jax.ShapeDtypeStruct((t_len, D_MODEL), F32),
            jax.ShapeDtypeStruct((t_len * ROW_SUBLANES, LANES), U32),
            jax.ShapeDtypeStruct((ROUTE_ROWS, t_len), jnp.int32),
            jax.ShapeDtypeStruct((ROUTE_ROWS, t_len), F32),
            jax.ShapeDtypeStruct((N_EXPERTS, LANES), F32),
        ],
        compiler_params=pltpu.CompilerParams(
            dimension_semantics=("arbitrary", "arbitrary"), vmem_limit_bytes=VMEM_LIMIT),
        name="mixer_router",
        cost_estimate=_mixer_cost(t_len),
    )(cdec, x2, w_in, w_out, gn, ln_g, ln_b, wr_cat, br, din, qdec, kdec, bias, pen, kmask, vmask, hlane, tri)


def _sc_mesh():
    return plsc.VectorSubcoreMesh(core_axis_name="core", subcore_axis_name="subcore")


def _sc_stream_rows(table_hbm, idx_v, out_hbm, out_base, n_chunks, bufs, sems):
    assert n_chunks % 2 == 0

    def gather(j, b):
        off = pl.multiple_of(j * SC_CHUNK_ROWS, SC_CHUNK_ROWS)
        return pltpu.make_async_copy(table_hbm.at[idx_v.at[pl.ds(off, SC_CHUNK_ROWS)]], bufs[b], sems[b])

    gather(0, 0).start()

    @pl.loop(0, n_chunks, step=2)
    def _(j0):
        for b in range(2):
            j = j0 + b
            gather(j, b).wait()

            @pl.when(j + 1 < n_chunks)
            def _():
                gather(j + 1, 1 - b).start()

            off = pl.multiple_of(j * SC_CHUNK_ROWS, SC_CHUNK_ROWS)
            pltpu.sync_copy(bufs[b], out_hbm.at[pl.ds(out_base + off, SC_CHUNK_ROWS)])


def _sc_dispatch(h_rows, dest, slot0, n_slots):
    t_len = h_rows.shape[0]
    n_assign = dest.shape[0]
    per_worker = n_slots // SC_WORKERS
    n_chunks = per_worker // SC_CHUNK_ROWS
    assert n_chunks * SC_CHUNK_ROWS * SC_WORKERS == n_slots and slot0 + n_slots <= 3 * t_len
    assert per_worker % SC_LANES == 0 and t_len % (SC_LANES * SC_SCAN_UNROLL) == 0 and n_assign == TOP_K * t_len

    @functools.partial(
        pl.kernel, mesh=_sc_mesh(), name="moe_dispatch_sc", cost_estimate=_row_move_cost(n_slots),
        compiler_params=pltpu.CompilerParams(needs_layout_passes=False),
        out_type=jax.ShapeDtypeStruct((n_slots, ROW_SUBLANES, LANES), U32),
        scratch_types=[pltpu.VMEM((n_assign,), jnp.int32),
                       pltpu.VMEM((per_worker,), jnp.int32),
                       pltpu.VMEM((SC_CHUNK_ROWS, ROW_SUBLANES, LANES), U32),
                       pltpu.VMEM((SC_CHUNK_ROWS, ROW_SUBLANES, LANES), U32),
                       pltpu.SemaphoreType.DMA, pltpu.SemaphoreType.DMA])
    def dispatch(h_hbm, dest_hbm, rows_hbm, dest_v, src_v, buf0, buf1, sem, sem1):
        wid = lax.axis_index("subcore") * SC_CORES + lax.axis_index("core")
        local = wid * per_worker
        base = slot0 + local
        piece = n_assign // SC_DEST_PIECES
        copies = []
        for c in range(SC_DEST_PIECES):
            off = pl.multiple_of(lax.rem(c + wid, SC_DEST_PIECES) * piece, SC_LANES)
            copies.append(pltpu.async_copy(dest_hbm.at[pl.ds(off, piece)], dest_v.at[pl.ds(off, piece)], sem))
        for cp in copies:
            cp.wait()
        lane = lax.iota(jnp.int32, SC_LANES)

        def wrap(a):
            a = jnp.where(a >= t_len, a - t_len, a)
            return jnp.where(a >= t_len, a - t_len, a)

        @pl.loop(0, per_worker // SC_LANES)
        def _(i):
            src_v[pl.ds(i * SC_LANES, SC_LANES)] = wrap(base + i * SC_LANES + lane)

        for k in range(TOP_K):
            @pl.loop(0, t_len // (SC_LANES * SC_SCAN_UNROLL))
            def _(i):
                for u in range(SC_SCAN_UNROLL):
                    tok0 = (i * SC_SCAN_UNROLL + u) * SC_LANES
                    d = dest_v[pl.ds(k * t_len + tok0, SC_LANES)] - base
                    hit = (d >= 0) & (d < per_worker)
                    plsc.store_scatter(src_v, [jnp.where(hit, d, 0)], tok0 + lane, mask=hit)

        _sc_stream_rows(h_hbm, src_v, rows_hbm, local, n_chunks, (buf0, buf1), (sem, sem1))

    return dispatch(h_rows, dest)


def _sc_gather_rows(table, idx, name):
    m = idx.shape[0]
    per_worker = m // (SC_CHUNK_ROWS * SC_WORKERS)
    assert per_worker * SC_CHUNK_ROWS * SC_WORKERS == m

    @functools.partial(
        pl.kernel, mesh=_sc_mesh(), name=name, cost_estimate=_row_move_cost(m),
        out_type=jax.ShapeDtypeStruct((m, ROW_SUBLANES, LANES), U32),
        scratch_types=[pltpu.VMEM((per_worker * SC_CHUNK_ROWS,), jnp.int32),
                       pltpu.VMEM((SC_CHUNK_ROWS, ROW_SUBLANES, LANES), U32),
                       pltpu.VMEM((SC_CHUNK_ROWS, ROW_SUBLANES, LANES), U32),
                       pltpu.SemaphoreType.DMA, pltpu.SemaphoreType.DMA])
    def gather(table_hbm, idx_hbm, out_hbm, idx_v, buf0, buf1, sem0, sem1):
        wid = lax.axis_index("subcore") * SC_CORES + lax.axis_index("core")
        base = pl.multiple_of(wid * (per_worker * SC_CHUNK_ROWS), SC_CHUNK_ROWS)
        pltpu.sync_copy(idx_hbm.at[pl.ds(base, per_worker * SC_CHUNK_ROWS)], idx_v)

        _sc_stream_rows(table_hbm, idx_v, out_hbm, base, per_worker, (buf0, buf1), (sem0, sem1))

    return gather(table, idx)


def _expert_kernel(bexp_ref, nused_ref, nxt_ref, slot_ref, rows_hbm, wg_hbm, wu_hbm, wd_hbm, *rest, first_block,
                   part_blocks):
    y_hbm, stage_g, stage_u, stage_d, wg_s, wu_s, wd_s, rbuf, ybuf, sem, rsem, ysem = rest[-12:]
    block_rows = MOE_ROWS * ROW_SUBLANES
    n_used = jnp.clip(nused_ref[0] - first_block, 0, part_blocks)

    def weight_copies(e, s):
        return [pltpu.make_async_copy(w.at[e], stage.at[s], sem.at[s, i])
                for i, (w, stage) in enumerate(((wg_hbm, stage_g), (wu_hbm, stage_u), (wd_hbm, stage_d)))]

    pair_rows = 2 * block_rows
    n_pairs = (n_used + 1) // 2

    def row_copy(q):
        start = pl.multiple_of(q * pair_rows, pair_rows)
        return pltpu.make_async_copy(rows_hbm.at[pl.ds(start, pair_rows)], rbuf.at[q % 2], rsem.at[q % 2])

    def out_copy(q):
        start = pl.multiple_of(first_block * block_rows + q * pair_rows, pair_rows)
        return pltpu.make_async_copy(ybuf.at[q % 2], y_hbm.at[pl.ds(start, pair_rows)], ysem.at[q % 2])

    def load_weights(blk):
        slot = slot_ref[blk]
        for cp in weight_copies(bexp_ref[blk], slot):
            cp.wait()
        wg_s[...] = stage_g[slot].astype(BF16)
        wu_s[...] = stage_u[slot].astype(BF16)
        wd_s[...] = stage_d[slot].astype(BF16)

        @pl.when(nxt_ref[blk] >= 0)
        def _():
            for cp in weight_copies(nxt_ref[blk], 1 - slot):
                cp.start()

    def mlp(x_view, y_view):
        xb = _unpack_rows(x_view).astype(BF16)
        g = _dot(xb, wg_s[...])
        u = _dot(xb, wu_s[...])
        a = (_silu(g) * u).astype(BF16)
        _pack_rows(y_view, _dot(a, wd_s[...]))

    @pl.when(n_used > 0)
    def _():
        row_copy(0).start()
        for cp in weight_copies(bexp_ref[first_block], slot_ref[first_block]):
            cp.start()

    def pair(q, carry):
        blk = first_block + 2 * q
        expert = bexp_ref[blk]
        row_copy(q).wait()

        @pl.when(q + 1 < n_pairs)
        def _():
            row_copy(q + 1).start()

        @pl.when((q == 0) | (expert != bexp_ref[jnp.maximum(blk - 1, 0)]))
        def _():
            load_weights(blk)

        @pl.when(q >= 2)
        def _():
            out_copy(q - 2).wait()

        has_second = 2 * q + 1 < n_used
        same = has_second & (bexp_ref[blk + 1] == expert)
        s = q % 2

        @pl.when(same)
        def _():
            mlp(rbuf.at[s], ybuf.at[s])

        @pl.when(jnp.logical_not(same))
        def _():
            mlp(rbuf.at[s, pl.ds(0, block_rows)], ybuf.at[s, pl.ds(0, block_rows)])

            @pl.when(has_second)
            def _():
                load_weights(blk + 1)
                mlp(rbuf.at[s, pl.ds(block_rows, block_rows)], ybuf.at[s, pl.ds(block_rows, block_rows)])

        out_copy(q).start()
        return carry

    lax.fori_loop(0, n_pairs, pair, 0)

    for back in (2, 1):
        @pl.when(n_pairs >= back)
        def _():
            out_copy(n_pairs - back).wait()


def _expert_call(bexp, nused, seg_end, rows, w_gate, w_up, w_down, y_prev, part, n_rows):
    part_blocks = rows.shape[0] // (ROW_SUBLANES * MOE_ROWS)
    n_blocks = bexp.shape[0]
    first_block = part * part_blocks
    assert part_blocks % 2 == 0
    limit = jnp.minimum(nused[0], first_block + part_blocks)
    nxt = jnp.where(seg_end < limit, bexp[jnp.minimum(seg_end, n_blocks - 1)], -1).astype(jnp.int32)
    starts = jnp.concatenate([jnp.ones((1,), jnp.int32), (bexp[1:] != bexp[:-1]).astype(jnp.int32)])
    slot = (jnp.cumsum(starts) % 2).astype(jnp.int32)
    hbm = pl.BlockSpec(memory_space=pl.ANY)
    in_specs = [hbm, hbm, hbm, hbm]
    args = [rows, w_gate, w_up, w_down]
    aliases = {}
    if y_prev is not None:
        in_specs.append(hbm)
        args.append(y_prev)
        aliases = {4 + len(args) - 1: 0}
    grid_spec = pltpu.PrefetchScalarGridSpec(
        num_scalar_prefetch=4,
        grid=(1,),
        in_specs=in_specs,
        out_specs=hbm,
        scratch_shapes=[
            pltpu.VMEM((2, D_MODEL, D_EXPERT), F32),
            pltpu.VMEM((2, D_MODEL, D_EXPERT), F32),
            pltpu.VMEM((2, D_EXPERT, D_MODEL), F32),
            pltpu.VMEM((D_MODEL, D_EXPERT), BF16),
            pltpu.VMEM((D_MODEL, D_EXPERT), BF16),
            pltpu.VMEM((D_EXPERT, D_MODEL), BF16),
            pltpu.VMEM((2, 2 * MOE_ROWS * ROW_SUBLANES, LANES), U32),
            pltpu.VMEM((2, 2 * MOE_ROWS * ROW_SUBLANES, LANES), U32),
            pltpu.SemaphoreType.DMA((2, 3)),
            pltpu.SemaphoreType.DMA((2,)),
            pltpu.SemaphoreType.DMA((2,)),
        ],
    )
    return pl.pallas_call(
        functools.partial(_expert_kernel, first_block=first_block, part_blocks=part_blocks),
        grid_spec=grid_spec,
        out_shape=jax.ShapeDtypeStruct((n_rows * ROW_SUBLANES, LANES), U32),
        input_output_aliases=aliases,
        compiler_params=pltpu.CompilerParams(
            dimension_semantics=("arbitrary",), vmem_limit_bytes=VMEM_LIMIT),
        name="moe_experts",
        cost_estimate=_expert_cost(part_blocks * MOE_ROWS),
    )(bexp, nused, nxt, slot, *args)


def _combine_kernel(h_ref, y0_ref, y1_ref, wts_ref, ln_g_ref, ln_b_ref, *rest):
    out_ref = rest[-1]
    wts = wts_ref[...].T
    ffn = _unpack_rows(y0_ref) * wts[:, 0:1] + _unpack_rows(y1_ref) * wts[:, 1:2]
    out_ref[...] = _layer_norm(DEEPNORM_ALPHA * h_ref[...] + ffn, ln_g_ref[...], ln_b_ref[...])


def _combine_call(h, yk, wts, ln_g, ln_b, out_prev, split, t_total):
    t_len = t_total // TOKEN_SPLITS
    n_tiles = t_len // COMBINE_TILE
    tok = lambda i: (i, 0)
    tok_split = lambda i: (i + split * n_tiles, 0)
    full2 = lambda i: (0, 0)
    tiles = pl.BlockSpec((COMBINE_TILE * ROW_SUBLANES, LANES), tok)
    tiles_k1 = pl.BlockSpec((COMBINE_TILE * ROW_SUBLANES, LANES), lambda i: (i + n_tiles, 0))
    in_specs = [pl.BlockSpec((COMBINE_TILE, D_MODEL), tok_split), tiles, tiles_k1,
                pl.BlockSpec((ROUTE_ROWS, COMBINE_TILE), lambda i: (0, i + split * n_tiles)),
                pl.BlockSpec((1, D_MODEL), full2),
                pl.BlockSpec((1, D_MODEL), full2)]
    args = [h, yk, yk, wts, ln_g, ln_b]
    aliases = {}
    if out_prev is not None:
        in_specs.append(pl.BlockSpec(memory_space=pl.ANY))
        args.append(out_prev)
        aliases = {len(args) - 1: 0}
    return pl.pallas_call(
        _combine_kernel,
        grid=(n_tiles,),
        in_specs=in_specs,
        out_specs=pl.BlockSpec((COMBINE_TILE, D_MODEL), tok_split),
        out_shape=jax.ShapeDtypeStruct((t_total, D_MODEL), F32),
        input_output_aliases=aliases,
        compiler_params=pltpu.CompilerParams(
            dimension_semantics=("arbitrary",), vmem_limit_bytes=VMEM_LIMIT),
        name="moe_combine_ln",
        cost_estimate=pl.CostEstimate(flops=8 * t_len * D_MODEL, transcendentals=t_len,
                                      bytes_accessed=(2 * 4 * D_MODEL + TOP_K * ROW_BYTES) * t_len),
    )(*args)


def _position_tables():
    c = CHUNK
    f = np.float32
    log_g = np.log1p(-np.exp2(-5.0 - np.arange(RET_HEADS, dtype=f))).astype(f)
    idx = np.arange(c, dtype=f)
    diff = idx[:, None] - idx[None, :]
    scale = f(RET_HEAD_DIM ** -0.5)
    din = np.where(diff >= 0, np.exp(log_g[:, None, None] * np.maximum(diff, 0.0)), 0.0).astype(f) * scale
    qdec = np.broadcast_to(np.exp(log_g[:, None] * (idx + 1.0))[:, :, None], (RET_HEADS, c, RET_HEAD_DIM))
    kdec = np.broadcast_to((np.exp(log_g[:, None] * (c - 1.0 - idx)) * scale)[:, :, None],
                           (RET_HEADS, c, RET_HEAD_DIM))
    cdec = np.exp(log_g * c)
    slopes = np.exp2(-8.0 * (np.arange(SWA_HEADS, dtype=f) + 1.0) / SWA_HEADS).astype(f)
    r = np.arange(c)[:, None]
    col = np.arange(c)[None, :]
    dist_prev = (r - col + c).astype(f)
    dist_cur = (r - col).astype(f)
    bprev = np.where((r < col)[None], -slopes[:, None, None] * dist_prev[None], NEG)
    bcur = np.where((r >= col)[None], -slopes[:, None, None] * dist_cur[None], NEG)
    bias = np.concatenate([bprev, bcur], axis=-1).reshape(SWA_KV_HEADS, SWA_GROUP * c, 2 * c)
    key = np.arange(2 * c)
    pen = np.where((key >= 1) & (key < c), NEG, 0.0)[None, :]
    lane_head = np.arange(SWA_VREP) // SWA_HEAD_DIM
    own = lane_head[None, None, :] == np.arange(SWA_GROUP)[:, None, None]
    kmask = np.broadcast_to(np.arange(c)[:, None] > 0, (c, SWA_HEAD_DIM))
    vmask = own & (key[None, :, None] > 0)
    hlane = lane_head[None, :]
    tr = np.arange(RANK_BLOCK)
    tri = tr[:, None] < tr[None, :]
    as_f32 = lambda v: np.ascontiguousarray(v, dtype=f)
    as_bf16 = lambda v: np.ascontiguousarray(v, dtype=f).astype(BF16)
    return (as_f32(cdec), as_f32(din), as_f32(qdec), as_f32(kdec), as_f32(bias), as_f32(pen),
            as_bf16(kmask), as_bf16(vmask), as_f32(hlane), as_bf16(tri))


def _mixer_constants(attn_sinks):
    cdec, din, qdec, kdec, bias, pen, kmask, vmask, hlane, tri = _position_tables()
    sink = attn_sinks.astype(F32).reshape(SWA_KV_HEADS, SWA_GROUP, 1, 1)
    sink = jnp.broadcast_to(sink, (SWA_KV_HEADS, SWA_GROUP, CHUNK, 1)).reshape(SWA_KV_HEADS, SWA_GROUP * CHUNK, 1)
    is_slot = (np.arange(2 * CHUNK) == 0)[None, None, :]
    return cdec, din, qdec, kdec, jnp.where(is_slot, sink, bias), pen, kmask, vmask, hlane, tri


def _router_tables(w_group_router, b_group_router, w_expert_router, b_expert_router):
    w_e = jnp.transpose(w_expert_router, (1, 0, 2)).reshape(D_MODEL, N_EXPERTS)
    w = jnp.concatenate([w_e, w_group_router,
                         jnp.zeros((D_MODEL, LANES - N_EXPERTS - N_GROUPS), F32)], axis=1)
    bias = jnp.concatenate([b_expert_router.reshape(N_EXPERTS), b_group_router,
                            jnp.zeros((LANES - N_EXPERTS - N_GROUPS,), F32)])[None, :]
    w_hi = w.astype(BF16)
    w_lo = (w - w_hi.astype(F32)).astype(BF16)
    return jnp.concatenate([w_hi, w_lo], axis=1), bias


def kernel(x, w_in, ret_gn_g, attn_sinks, w_out, ln1_g, ln1_b, w_group_router, b_group_router,
           w_expert_router, b_expert_router, w_gate, w_up, w_down, ln2_g, ln2_b):
    bsz, s_len, d = x.shape
    assert d == D_MODEL and s_len % SEQ_TILE == 0 and w_in.shape[0] == DEPTH == 1
    assert bsz % TOKEN_SPLITS == 0
    t_len = bsz * s_len
    t_split = t_len // TOKEN_SPLITS
    n_blocks = t_len * TOP_K // MOE_ROWS + N_EXPERTS
    n_rows = n_blocks * MOE_ROWS
    part_rows = n_rows // SLOT_PARTS

    consts = _mixer_constants(attn_sinks[0])
    wr_cat, br = _router_tables(w_group_router[0], b_group_router[0],
                                      w_expert_router[0], b_expert_router[0])
    h, h_packed, ids, wts, cnt = _mixer_call(
        x.reshape(t_len, d), w_in[0].astype(BF16), w_out[0].astype(BF16),
        ret_gn_g[0][None, :], ln1_g[0][None, :], ln1_b[0][None, :], wr_cat, br, consts, bsz, s_len, 0)

    counts = cnt[:, 0].astype(jnp.int32)
    padded = (counts + MOE_ROWS - 1) // MOE_ROWS * MOE_ROWS
    pend = jnp.cumsum(padded)
    pstart = pend - padded
    onehot = ids[0:TOP_K, :, None] == jnp.arange(N_EXPERTS, dtype=jnp.int32)
    dest2 = jnp.sum(jnp.where(onehot, pstart, 0), axis=-1) + ids[TOP_K:2 * TOP_K]
    dest = dest2.reshape(-1)
    nused = (pend[-1:] // MOE_ROWS).astype(jnp.int32)
    blk_start = jnp.minimum(jnp.arange(n_blocks, dtype=jnp.int32), nused[0] - 1) * MOE_ROWS
    bexp = jnp.minimum(jnp.sum(pend[None, :] <= blk_start[:, None], axis=-1), N_EXPERTS - 1).astype(jnp.int32)
    seg_end = jnp.sum(jnp.where(bexp[:, None] == jnp.arange(N_EXPERTS, dtype=jnp.int32), pend // MOE_ROWS, 0),
                      axis=-1).astype(jnp.int32)

    h_tiles = h_packed.reshape(t_len, ROW_SUBLANES, LANES)
    y = None
    for part in range(SLOT_PARTS):
        rows = _sc_dispatch(h_tiles, dest, part * part_rows, part_rows)
        y = _expert_call(bexp, nused, seg_end, rows.reshape(part_rows * ROW_SUBLANES, LANES),
                         w_gate[0], w_up[0], w_down[0], y, part, n_rows)

    y_tiles = y.reshape(n_rows, ROW_SUBLANES, LANES)
    out = None
    for sp in range(TOKEN_SPLITS):
        idx = dest2[:, sp * t_split:(sp + 1) * t_split].reshape(-1)
        yk = _sc_gather_rows(y_tiles, idx, "moe_combine_sc")
        out = _combine_call(h, yk.reshape(TOP_K * t_split * ROW_SUBLANES, LANES), wts,
                            ln2_g[0][None, :], ln2_b[0][None, :], out, sp, t_len)
    return out.reshape(bsz, s_len, d)
```

```python
import functools

import jax
import jax.numpy as jnp
import numpy as np
from jax import lax
from jax.experimental import pallas as pl
from jax.experimental.pallas import tpu as pltpu
from jax.experimental.pallas import tpu_sc as plsc

F32 = jnp.float32
BF16 = jnp.bfloat16
U32 = jnp.uint32

D_MODEL = 1024
RET_HEADS = 4
RET_HEAD_DIM = 128
RET_WIDTH = RET_HEADS * RET_HEAD_DIM
CHUNK = 128
SWA_HEADS = 8
SWA_KV_HEADS = 2
SWA_GROUP = SWA_HEADS // SWA_KV_HEADS
SWA_HEAD_DIM = 64
SWA_WIDTH = SWA_HEADS * SWA_HEAD_DIM
SWA_KV_WIDTH = SWA_KV_HEADS * SWA_HEAD_DIM
IN_WIDTH = 4 * RET_WIDTH + SWA_WIDTH + 2 * SWA_KV_WIDTH
N_GROUPS = 4
EXPERTS_PER_GROUP = 8
N_EXPERTS = N_GROUPS * EXPERTS_PER_GROUP
TOP_K = 2
D_EXPERT = 512
LN_EPS = 1e-5
GN_EPS = 1e-6
DEPTH = 1
DEEPNORM_ALPHA = (2 * DEPTH) ** 0.25
NEG = -1e30

LANES = 128
ROW_SUBLANES = D_MODEL // 2 // LANES
SEQ_TILE = 1024
PROJ_ROWS = 256
RANK_BLOCK = 256
MOE_ROWS = 512
COMBINE_TILE = 512
TOKEN_SPLITS = 2
SLOT_PARTS = 2
SC_CORES = 2
SC_SUBCORES = 16
SC_WORKERS = SC_CORES * SC_SUBCORES
SC_LANES = 16
SC_DEST_PIECES = 8
SC_SCAN_UNROLL = 8
SC_CHUNK_ROWS = 64
VMEM_LIMIT = 56 * 1024 * 1024

_QR, _KR, _VR, _GR = 0, RET_WIDTH, 2 * RET_WIDTH, 3 * RET_WIDTH
_QA = 4 * RET_WIDTH
_KA = _QA + SWA_WIDTH
_VA = _KA + SWA_KV_WIDTH
SWA_VREP = SWA_GROUP * SWA_HEAD_DIM
ROUTE_ROWS = 8
GROUP_LANE0 = N_EXPERTS


def _dot(a, b):
    return jnp.dot(a, b, preferred_element_type=F32)


def _dot_nt(a, b):
    return lax.dot_general(a, b, (((1,), (1,)), ((), ())), preferred_element_type=F32)


def _dot_tn(a, b):
    return lax.dot_general(a, b, (((0,), (0,)), ((), ())), preferred_element_type=F32)


def _layer_norm(z, g, b):
    mu = jnp.mean(z, axis=-1, keepdims=True)
    zc = z - mu
    var = jnp.mean(zc * zc, axis=-1, keepdims=True)
    return zc * lax.rsqrt(var + LN_EPS) * g + b


def _silu(g):
    return g / (1.0 + jnp.exp(-g))


def _tile_heads(v):
    return jnp.concatenate([v[:, j * SWA_HEAD_DIM:(j + 1) * SWA_HEAD_DIM]
                            for j in range(SWA_KV_HEADS) for _ in range(SWA_GROUP)], axis=1)


ROW_BYTES = 4 * ROW_SUBLANES * LANES


def _mixer_cost(t):
    proj = 2 * t * D_MODEL * (IN_WIDTH + D_MODEL + 3 * LANES)
    retention = RET_HEADS * 4 * 2 * t * CHUNK * RET_HEAD_DIM
    swa = SWA_KV_HEADS * 2 * t * SWA_GROUP * 2 * CHUNK * (SWA_HEAD_DIM + 2 * SWA_VREP)
    weights = 2 * D_MODEL * (IN_WIDTH + D_MODEL + 2 * LANES)
    return pl.CostEstimate(flops=proj + retention + swa + 2 * t * RANK_BLOCK * N_EXPERTS,
                           transcendentals=t * (SWA_HEADS * 2 * CHUNK + RET_WIDTH + LANES),
                           bytes_accessed=(2 * 4 * D_MODEL + ROW_BYTES) * t + weights)


def _expert_cost(n_rows):
    return pl.CostEstimate(flops=2 * 3 * n_rows * D_MODEL * D_EXPERT, transcendentals=n_rows * D_EXPERT,
                           bytes_accessed=2 * ROW_BYTES * n_rows + 4 * 3 * N_EXPERTS * D_MODEL * D_EXPERT)


def _row_move_cost(n_rows):
    return pl.CostEstimate(flops=0, transcendentals=0, bytes_accessed=2 * ROW_BYTES * n_rows)


def _pack_rows(ref, val):
    n = val.shape[0]
    half = D_MODEL // 2
    hi = lax.bitcast_convert_type(val[:, :half].astype(BF16).astype(F32), U32)
    lo = lax.bitcast_convert_type(val[:, half:].astype(BF16).astype(F32), U32)
    word = hi | (lo >> 16)
    for j in range(ROW_SUBLANES):
        ref[pl.ds(j, n, stride=ROW_SUBLANES), :] = word[:, j * LANES:(j + 1) * LANES]


def _unpack_rows(ref):
    n = ref.shape[0] // ROW_SUBLANES
    word = jnp.concatenate([ref[pl.ds(j, n, stride=ROW_SUBLANES), :] for j in range(ROW_SUBLANES)], axis=1)
    hi = lax.bitcast_convert_type(word & jnp.uint32(0xFFFF0000), F32)
    lo = lax.bitcast_convert_type(word << 16, F32)
    return jnp.concatenate([hi, lo], axis=1)


def _mixer_kernel(cdec_ref, x_ref, w_in_ref, w_out_ref, gn_ref, ln_g_ref, ln_b_ref,
                  wr_cat_ref, br_ref, din_ref, qdec_ref, kdec_ref,
                  bias_ref, pen_ref, kmask_ref, vmask_ref, hlane_ref, tri_ref,
                  h_ref, hp_ref, ids_ref, wts_ref, cnt_ref,
                  state_scr, kprev_scr, vprev_scr, o_scr, carry_scr):
    b = pl.program_id(0)
    n = pl.program_id(1)
    ts = x_ref.shape[0]

    @pl.when(n == 0)
    def _():
        state_scr[...] = jnp.zeros_like(state_scr)
        kprev_scr[...] = jnp.zeros_like(kprev_scr)
        vprev_scr[...] = jnp.zeros_like(vprev_scr)

    @pl.when((b == 0) & (n == 0))
    def _():
        carry_scr[...] = jnp.zeros_like(carry_scr)

    x = x_ref[...]
    xb = x.astype(BF16)

    chunks_per_part = PROJ_ROWS // CHUNK
    parts = []

    def project_part(part):
        xp = xb[part * PROJ_ROWS:(part + 1) * PROJ_ROWS]
        proj = lambda lo, hi: _dot(xp, w_in_ref[:, lo:hi])
        parts.append(dict(
            q_r=proj(_QR, _KR), k_r=proj(_KR, _VR), v_r=proj(_VR, _GR), g_r=proj(_GR, _QA),
            q_a=proj(_QA, _KA), k_ab=proj(_KA, _VA).astype(BF16),
            v_rep=_tile_heads(proj(_VA, IN_WIDTH).astype(BF16))))

    project_part(0)

    def chunk_of(name, c):
        lo = (c % chunks_per_part) * CHUNK
        return parts[c // chunks_per_part][name][lo:lo + CHUNK]
    first_pen = jnp.where(n == 0, pen_ref[...], 0.0)

    for c in range(ts // CHUNK):
        rs = slice(c * CHUNK, (c + 1) * CHUNK)
        q_r, k_r, v_r, g_r = (chunk_of(name, c) for name in ("q_r", "k_r", "v_r", "g_r"))
        q_a, k_ab, v_rep = (chunk_of(name, c) for name in ("q_a", "k_ab", "v_rep"))
        for hd in range(RET_HEADS):
            cs = slice(hd * RET_HEAD_DIM, (hd + 1) * RET_HEAD_DIM)
            q = q_r[:, cs]
            k = k_r[:, cs]
            v = v_r[:, cs].astype(BF16)
            scores = _dot_nt(q.astype(BF16), k.astype(BF16)) * din_ref[hd]
            st = state_scr[hd]
            o = _dot(jnp.concatenate([scores.astype(BF16), (q * qdec_ref[hd]).astype(BF16)], axis=1),
                     jnp.concatenate([v, st.astype(BF16)], axis=0))
            kv = _dot_tn((k * kdec_ref[hd]).astype(BF16), v)
            state_scr[hd] = st * cdec_ref[hd] + kv
            mu = jnp.mean(o, axis=-1, keepdims=True)
            oc = o - mu
            var = jnp.mean(oc * oc, axis=-1, keepdims=True)
            on = oc * lax.rsqrt(var + GN_EPS) * gn_ref[:, cs] * _silu(g_r[:, cs])
            o_scr[rs, cs] = on.astype(BF16)
        for j in range(SWA_KV_HEADS):
            ks = slice(j * SWA_HEAD_DIM, (j + 1) * SWA_HEAD_DIM)
            vs = slice(j * SWA_VREP, (j + 1) * SWA_VREP)
            if c == 0:
                kp = kprev_scr[:, ks].astype(BF16)
                vp = vprev_scr[:, vs].astype(BF16)
            else:
                kp = chunk_of("k_ab", c - 1)[:, ks]
                vp = chunk_of("v_rep", c - 1)[:, vs]
            kp = kp * kmask_ref[...]
            kcat = jnp.concatenate([kp, k_ab[:, ks]], axis=0)
            vcat = jnp.concatenate([vp, v_rep[:, vs]], axis=0)
            q0 = j * SWA_GROUP * SWA_HEAD_DIM
            qs = jnp.concatenate(
                [q_a[:, q0 + g * SWA_HEAD_DIM:q0 + (g + 1) * SWA_HEAD_DIM] for g in range(SWA_GROUP)],
                axis=0)
            qs = (qs * (SWA_HEAD_DIM ** -0.5)).astype(BF16)
            s = _dot_nt(qs, kcat) + bias_ref[j]
            if c == 0:
                s = s + first_pen
            m = jnp.max(jnp.maximum(s[:, :CHUNK], s[:, CHUNK:]), axis=-1, keepdims=True)
            p32 = jnp.exp(s - m)
            den_col = jnp.sum(p32[:, :CHUNK] + p32[:, CHUNK:], axis=-1, keepdims=True)
            p = p32.astype(BF16)
            p_all = jnp.concatenate([p[g * CHUNK:(g + 1) * CHUNK] for g in range(SWA_GROUP)], axis=1)
            v_blk = jnp.concatenate([vcat * vmask_ref[g] for g in range(SWA_GROUP)], axis=0)
            num = _dot(p_all, v_blk)
            den = den_col[(SWA_GROUP - 1) * CHUNK:]
            for g in range(SWA_GROUP - 2, -1, -1):
                den = jnp.where(hlane_ref[...] == g, den_col[g * CHUNK:(g + 1) * CHUNK], den)
            c0 = RET_WIDTH + j * SWA_GROUP * SWA_HEAD_DIM
            o_scr[rs, c0:c0 + SWA_GROUP * SWA_HEAD_DIM] = (num / den).astype(BF16)
        if c % chunks_per_part == 0 and len(parts) < ts // PROJ_ROWS:
            project_part(len(parts))

    kprev_scr[...] = chunk_of("k_ab", ts // CHUNK - 1).astype(F32)
    vprev_scr[...] = chunk_of("v_rep", ts // CHUNK - 1).astype(F32)

    mix = _dot(o_scr[...], w_out_ref[...])
    h = _layer_norm(DEEPNORM_ALPHA * x + mix, ln_g_ref[...], ln_b_ref[...])
    h_ref[...] = h
    _pack_rows(hp_ref, h)

    h_hi = h.astype(BF16)
    h_lo = (h - h_hi.astype(F32)).astype(BF16)
    hi_terms = _dot(h_hi, wr_cat_ref[...])
    logits = hi_terms[:, :LANES] + hi_terms[:, LANES:] + _dot(h_lo, wr_cat_ref[:, :LANES]) + br_ref[...]
    lt = logits.T
    row = lax.broadcasted_iota(jnp.int32, (EXPERTS_PER_GROUP, ts), 0).astype(F32)
    big = 1e9
    ninf = -jnp.inf
    col_max = lambda v: jnp.max(v, axis=0, keepdims=True)
    first_at = lambda v, m: jnp.min(jnp.where(v == m, row, big), axis=0, keepdims=True)
    gl = jnp.where(row < N_GROUPS, lt[GROUP_LANE0:GROUP_LANE0 + EXPERTS_PER_GROUP], ninf)
    gmax = col_max(gl)
    gidx = first_at(gl, gmax)
    g_w = 1.0 / jnp.sum(jnp.exp(gl - gmax), axis=0, keepdims=True)
    el = lt[(N_GROUPS - 1) * EXPERTS_PER_GROUP:N_GROUPS * EXPERTS_PER_GROUP]
    for g in range(N_GROUPS - 2, -1, -1):
        el = jnp.where(gidx == g, lt[g * EXPERTS_PER_GROUP:(g + 1) * EXPERTS_PER_GROUP], el)
    m1 = col_max(el)
    i1 = first_at(el, m1)
    el2 = jnp.where(row == i1, ninf, el)
    m2 = col_max(el2)
    i2 = first_at(el2, m2)
    t = jnp.exp(m2 - m1)
    w1 = g_w / (1.0 + t)
    w2 = g_w * t / (1.0 + t)
    e1 = gidx * EXPERTS_PER_GROUP + i1
    e2 = gidx * EXPERTS_PER_GROUP + i2
    erow = lax.broadcasted_iota(jnp.int32, (N_EXPERTS, ts), 0).astype(F32)
    hit1 = erow == e1
    hit2 = erow == e2
    onehot = (hit1 | hit2).astype(BF16)
    n_rb = ts // RANK_BLOCK
    blocks = [onehot[:, i * RANK_BLOCK:(i + 1) * RANK_BLOCK] for i in range(n_rb)]
    within = _dot(jnp.concatenate(blocks, axis=0), tri_ref[...])
    carry = carry_scr[:, 0:1]
    pieces = []
    for i in range(n_rb):
        pieces.append(within[i * N_EXPERTS:(i + 1) * N_EXPERTS] + carry)
        carry = carry + jnp.sum(blocks[i].astype(F32), axis=1, keepdims=True)
    prefix = jnp.concatenate(pieces, axis=1)
    r1 = jnp.sum(jnp.where(hit1, prefix, 0.0), axis=0, keepdims=True)
    r2 = jnp.sum(jnp.where(hit2, prefix, 0.0), axis=0, keepdims=True)
    carry_scr[...] = jnp.broadcast_to(carry, carry_scr.shape)
    cnt_ref[...] = jnp.broadcast_to(carry, cnt_ref.shape)
    pick = lambda k, v, rest: jnp.where(row == k, v, rest)
    ids_ref[...] = pick(0, e1, pick(1, e2, pick(2, r1, pick(3, r2, 0.0)))).astype(jnp.int32)
    wts_ref[...] = pick(0, w1, pick(1, w2, 0.0))


def _mixer_call(x2, w_in, w_out, gn, ln_g, ln_b, wr_cat, br, consts, bsz, s_len, first_seq):
    cdec, din, qdec, kdec, bias, pen, kmask, vmask, hlane, tri = consts
    t_len = bsz * s_len
    ns = s_len // SEQ_TILE
    tok = lambda b, n, *_: (b * ns + n, 0)
    tok_in = lambda b, n, *_: ((first_seq + b) * ns + n, 0)
    tok_t = lambda b, n, *_: (0, b * ns + n)
    full2 = lambda b, n, *_: (0, 0)
    full3 = lambda b, n, *_: (0, 0, 0)
    grid_spec = pltpu.PrefetchScalarGridSpec(
        num_scalar_prefetch=1,
        grid=(bsz, ns),
        in_specs=[
            pl.BlockSpec((SEQ_TILE, D_MODEL), tok_in),
            pl.BlockSpec((D_MODEL, IN_WIDTH), full2),
            pl.BlockSpec((D_MODEL, D_MODEL), full2),
            pl.BlockSpec((1, RET_WIDTH), full2),
            pl.BlockSpec((1, D_MODEL), full2),
            pl.BlockSpec((1, D_MODEL), full2),
            pl.BlockSpec((D_MODEL, 2 * LANES), full2),
            pl.BlockSpec((1, LANES), full2),
            pl.BlockSpec((RET_HEADS, CHUNK, CHUNK), full3),
            pl.BlockSpec((RET_HEADS, CHUNK, RET_HEAD_DIM), full3),
            pl.BlockSpec((RET_HEADS, CHUNK, RET_HEAD_DIM), full3),
            pl.BlockSpec((SWA_KV_HEADS, SWA_GROUP * CHUNK, 2 * CHUNK), full3),
            pl.BlockSpec((1, 2 * CHUNK), full2),
            pl.BlockSpec((CHUNK, SWA_HEAD_DIM), full2),
            pl.BlockSpec((SWA_GROUP, 2 * CHUNK, SWA_VREP), full3),
            pl.BlockSpec((1, SWA_VREP), full2),
            pl.BlockSpec((RANK_BLOCK, RANK_BLOCK), full2),
        ],
        out_specs=[
            pl.BlockSpec((SEQ_TILE, D_MODEL), tok),
            pl.BlockSpec((SEQ_TILE * ROW_SUBLANES, LANES), tok),
            pl.BlockSpec((ROUTE_ROWS, SEQ_TILE), tok_t),
            pl.BlockSpec((ROUTE_ROWS, SEQ_TILE), tok_t),
            pl.BlockSpec((N_EXPERTS, LANES), full2),
        ],
        scratch_shapes=[
            pltpu.VMEM((RET_HEADS, RET_HEAD_DIM, RET_HEAD_DIM), F32),
            pltpu.VMEM((CHUNK, SWA_KV_WIDTH), F32),
            pltpu.VMEM((CHUNK, SWA_KV_HEADS * SWA_VREP), F32),
            pltpu.VMEM((SEQ_TILE, D_MODEL), BF16),
            pltpu.VMEM((N_EXPERTS, LANES), F32),
        ],
    )
    return pl.pallas_call(
        _mixer_kernel,
        grid_spec=grid_spec,
        out_shape=[
            jax.ShapeDtypeStruct((t_len, D_MODEL), F32),
            jax.ShapeDtypeStruct((t_len * ROW_SUBLANES, LANES), U32),
            jax.ShapeDtypeStruct((ROUTE_ROWS, t_len), jnp.int32),
            jax.ShapeDtypeStruct((ROUTE_ROWS, t_len), F32),
            jax.ShapeDtypeStruct((N_EXPERTS, LANES), F32),
        ],
        compiler_params=pltpu.CompilerParams(
            dimension_semantics=("arbitrary", "arbitrary"), vmem_limit_bytes=VMEM_LIMIT),
        name="mixer_router",
        cost_estimate=_mixer_cost(t_len),
    )(cdec, x2, w_in, w_out, gn, ln_g, ln_b, wr_cat, br, din, qdec, kdec, bias, pen, kmask, vmask, hlane, tri)


def _sc_mesh():
    return plsc.VectorSubcoreMesh(core_axis_name="core", subcore_axis_name="subcore")


def _sc_stream_rows(table_hbm, idx_v, out_hbm, out_base, n_chunks, bufs, sems):
    assert n_chunks % 2 == 0

    def gather(j, b):
        off = pl.multiple_of(j * SC_CHUNK_ROWS, SC_CHUNK_ROWS)
        return pltpu.make_async_copy(table_hbm.at[idx_v.at[pl.ds(off, SC_CHUNK_ROWS)]], bufs[b], sems[b])

    gather(0, 0).start()

    @pl.loop(0, n_chunks, step=2)
    def _(j0):
        for b in range(2):
            j = j0 + b
            gather(j, b).wait()

            @pl.when(j + 1 < n_chunks)
            def _():
                gather(j + 1, 1 - b).start()

            off = pl.multiple_of(j * SC_CHUNK_ROWS, SC_CHUNK_ROWS)
            pltpu.sync_copy(bufs[b], out_hbm.at[pl.ds(out_base + off, SC_CHUNK_ROWS)])


def _sc_dispatch(h_rows, dest, slot0, n_slots):
    t_len = h_rows.shape[0]
    n_assign = dest.shape[0]
    per_worker = n_slots // SC_WORKERS
    n_chunks = per_worker // SC_CHUNK_ROWS
    assert n_chunks * SC_CHUNK_ROWS * SC_WORKERS == n_slots and slot0 + n_slots <= 3 * t_len
    assert per_worker % SC_LANES == 0 and t_len % (SC_LANES * SC_SCAN_UNROLL) == 0 and n_assign == TOP_K * t_len

    @functools.partial(
        pl.kernel, mesh=_sc_mesh(), name="moe_dispatch_sc", cost_estimate=_row_move_cost(n_slots),
        compiler_params=pltpu.CompilerParams(needs_layout_passes=False),
        out_type=jax.ShapeDtypeStruct((n_slots, ROW_SUBLANES, LANES), U32),
        scratch_types=[pltpu.VMEM((n_assign,), jnp.int32),
                       pltpu.VMEM((per_worker,), jnp.int32),
                       pltpu.VMEM((SC_CHUNK_ROWS, ROW_SUBLANES, LANES), U32),
                       pltpu.VMEM((SC_CHUNK_ROWS, ROW_SUBLANES, LANES), U32),
                       pltpu.SemaphoreType.DMA, pltpu.SemaphoreType.DMA])
    def dispatch(h_hbm, dest_hbm, rows_hbm, dest_v, src_v, buf0, buf1, sem, sem1):
        wid = lax.axis_index("subcore") * SC_CORES + lax.axis_index("core")
        local = wid * per_worker
        base = slot0 + local
        piece = n_assign // SC_DEST_PIECES
        copies = []
        for c in range(SC_DEST_PIECES):
            off = pl.multiple_of(lax.rem(c + wid, SC_DEST_PIECES) * piece, SC_LANES)
            copies.append(pltpu.async_copy(dest_hbm.at[pl.ds(off, piece)], dest_v.at[pl.ds(off, piece)], sem))
        for cp in copies:
            cp.wait()
        lane = lax.iota(jnp.int32, SC_LANES)

        def wrap(a):
            a = jnp.where(a >= t_len, a - t_len, a)
            return jnp.where(a >= t_len, a - t_len, a)

        @pl.loop(0, per_worker // SC_LANES)
        def _(i):
            src_v[pl.ds(i * SC_LANES, SC_LANES)] = wrap(base + i * SC_LANES + lane)

        for k in range(TOP_K):
            @pl.loop(0, t_len // (SC_LANES * SC_SCAN_UNROLL))
            def _(i):
                for u in range(SC_SCAN_UNROLL):
                    tok0 = (i * SC_SCAN_UNROLL + u) * SC_LANES
                    d = dest_v[pl.ds(k * t_len + tok0, SC_LANES)] - base
                    hit = (d >= 0) & (d < per_worker)
                    plsc.store_scatter(src_v, [jnp.where(hit, d, 0)], tok0 + lane, mask=hit)

        _sc_stream_rows(h_hbm, src_v, rows_hbm, local, n_chunks, (buf0, buf1), (sem, sem1))

    return dispatch(h_rows, dest)


def _sc_gather_rows(table, idx, name):
    m = idx.shape[0]
    per_worker = m // (SC_CHUNK_ROWS * SC_WORKERS)
    assert per_worker * SC_CHUNK_ROWS * SC_WORKERS == m

    @functools.partial(
        pl.kernel, mesh=_sc_mesh(), name=name, cost_estimate=_row_move_cost(m),
        out_type=jax.ShapeDtypeStruct((m, ROW_SUBLANES, LANES), U32),
        scratch_types=[pltpu.VMEM((per_worker * SC_CHUNK_ROWS,), jnp.int32),
                       pltpu.VMEM((SC_CHUNK_ROWS, ROW_SUBLANES, LANES), U32),
                       pltpu.VMEM((SC_CHUNK_ROWS, ROW_SUBLANES, LANES), U32),
                       pltpu.SemaphoreType.DMA, pltpu.SemaphoreType.DMA])
    def gather(table_hbm, idx_hbm, out_hbm, idx_v, buf0, buf1, sem0, sem1):
        wid = lax.axis_index("subcore") * SC_CORES + lax.axis_index("core")
        base = pl.multiple_of(wid * (per_worker * SC_CHUNK_ROWS), SC_CHUNK_ROWS)
        pltpu.sync_copy(idx_hbm.at[pl.ds(base, per_worker * SC_CHUNK_ROWS)], idx_v)

        _sc_stream_rows(table_hbm, idx_v, out_hbm, base, per_worker, (buf0, buf1), (sem0, sem1))

    return gather(table, idx)


def _expert_kernel(bexp_ref, nused_ref, nxt_ref, slot_ref, rows_hbm, wg_hbm, wu_hbm, wd_hbm, *rest, first_block,
                   part_blocks):
    y_hbm, stage_g, stage_u, stage_d, wg_s, wu_s, wd_s, rbuf, ybuf, sem, rsem, ysem = rest[-12:]
    block_rows = MOE_ROWS * ROW_SUBLANES
    n_used = jnp.clip(nused_ref[0] - first_block, 0, part_blocks)

    def weight_copies(e, s):
        return [pltpu.make_async_copy(w.at[e], stage.at[s], sem.at[s, i])
                for i, (w, stage) in enumerate(((wg_hbm, stage_g), (wu_hbm, stage_u), (wd_hbm, stage_d)))]

    pair_rows = 2 * block_rows
    n_blocks = bexp_ref.shape[0]

    def row_copy(p, s):
        start = pl.multiple_of(jnp.minimum(p, part_blocks - 2) * block_rows, block_rows)
        return pltpu.make_async_copy(rows_hbm.at[pl.ds(start, pair_rows)], rbuf.at[s], rsem.at[s])

    def out_copy(p, s, paired):
        n = pair_rows if paired else block_rows
        start = pl.multiple_of((first_block + p) * block_rows, block_rows)
        return pltpu.make_async_copy(ybuf.at[s, pl.ds(0, n)], y_hbm.at[pl.ds(start, n)], ysem.at[s])

    def out_wait(s, was_pair):
        @pl.when(was_pair == 1)
        def _():
            out_copy(0, s, True).wait()

        @pl.when(was_pair == 0)
        def _():
            out_copy(0, s, False).wait()

    def load_weights(blk):
        slot = slot_ref[blk]
        for cp in weight_copies(bexp_ref[blk], slot):
            cp.wait()
        wg_s[...] = stage_g[slot].astype(BF16)
        wu_s[...] = stage_u[slot].astype(BF16)
        wd_s[...] = stage_d[slot].astype(BF16)

        @pl.when(nxt_ref[blk] >= 0)
        def _():
            for cp in weight_copies(nxt_ref[blk], 1 - slot):
                cp.start()

    def mlp(x_view, y_view):
        xb = _unpack_rows(x_view).astype(BF16)
        g = _dot(xb, wg_s[...])
        u = _dot(xb, wu_s[...])
        a = (_silu(g) * u).astype(BF16)
        _pack_rows(y_view, _dot(a, wd_s[...]))

    @pl.when(n_used > 0)
    def _():
        row_copy(0, 0).start()
        for cp in weight_copies(bexp_ref[first_block], slot_ref[first_block]):
            cp.start()

    def step(carry):
        p, q, pair1, pair2 = carry
        blk = first_block + p
        expert = bexp_ref[blk]
        s = q % 2
        same = (p + 1 < n_used) & (bexp_ref[jnp.minimum(blk + 1, n_blocks - 1)] == expert)
        p_next = p + 1 + same.astype(jnp.int32)
        row_copy(p, s).wait()

        @pl.when(p_next < n_used)
        def _():
            row_copy(p_next, 1 - s).start()

        @pl.when((q == 0) | (expert != bexp_ref[jnp.maximum(blk - 1, 0)]))
        def _():
            load_weights(blk)

        @pl.when(q >= 2)
        def _():
            out_wait(s, pair2)

        @pl.when(same)
        def _():
            mlp(rbuf.at[s], ybuf.at[s])
            out_copy(p, s, True).start()

        @pl.when(jnp.logical_not(same))
        def _():
            off = pl.multiple_of((p - jnp.minimum(p, part_blocks - 2)) * block_rows, block_rows)
            mlp(rbuf.at[s, pl.ds(off, block_rows)], ybuf.at[s, pl.ds(0, block_rows)])
            out_copy(p, s, False).start()

        return p_next, q + 1, same.astype(jnp.int32), pair1

    zero = jnp.int32(0)
    _, n_steps, pair1, pair2 = lax.while_loop(lambda c: c[0] < n_used, step, (zero, zero, zero, zero))

    @pl.when(n_steps >= 2)
    def _():
        out_wait(n_steps % 2, pair2)

    @pl.when(n_steps >= 1)
    def _():
        out_wait((n_steps + 1) % 2, pair1)


def _expert_call(bexp, nused, seg_end, rows, w_gate, w_up, w_down, y_prev, part, n_rows):
    part_blocks = rows.shape[0] // (ROW_SUBLANES * MOE_ROWS)
    n_blocks = bexp.shape[0]
    first_block = part * part_blocks
    assert part_blocks % 2 == 0
    limit = jnp.minimum(nused[0], first_block + part_blocks)
    nxt = jnp.where(seg_end < limit, bexp[jnp.minimum(seg_end, n_blocks - 1)], -1).astype(jnp.int32)
    starts = jnp.concatenate([jnp.ones((1,), jnp.int32), (bexp[1:] != bexp[:-1]).astype(jnp.int32)])
    slot = (jnp.cumsum(starts) % 2).astype(jnp.int32)
    hbm = pl.BlockSpec(memory_space=pl.ANY)
    in_specs = [hbm, hbm, hbm, hbm]
    args = [rows, w_gate, w_up, w_down]
    aliases = {}
    if y_prev is not None:
        in_specs.append(hbm)
        args.append(y_prev)
        aliases = {4 + len(args) - 1: 0}
    grid_spec = pltpu.PrefetchScalarGridSpec(
        num_scalar_prefetch=4,
        grid=(1,),
        in_specs=in_specs,
        out_specs=hbm,
        scratch_shapes=[
            pltpu.VMEM((2, D_MODEL, D_EXPERT), F32),
            pltpu.VMEM((2, D_MODEL, D_EXPERT), F32),
            pltpu.VMEM((2, D_EXPERT, D_MODEL), F32),
            pltpu.VMEM((D_MODEL, D_EXPERT), BF16),
            pltpu.VMEM((D_MODEL, D_EXPERT), BF16),
            pltpu.VMEM((D_EXPERT, D_MODEL), BF16),
            pltpu.VMEM((2, 2 * MOE_ROWS * ROW_SUBLANES, LANES), U32),
            pltpu.VMEM((2, 2 * MOE_ROWS * ROW_SUBLANES, LANES), U32),
            pltpu.SemaphoreType.DMA((2, 3)),
            pltpu.SemaphoreType.DMA((2,)),
            pltpu.SemaphoreType.DMA((2,)),
        ],
    )
    return pl.pallas_call(
        functools.partial(_expert_kernel, first_block=first_block, part_blocks=part_blocks),
        grid_spec=grid_spec,
        out_shape=jax.ShapeDtypeStruct((n_rows * ROW_SUBLANES, LANES), U32),
        input_output_aliases=aliases,
        compiler_params=pltpu.CompilerParams(
            dimension_semantics=("arbitrary",), vmem_limit_bytes=VMEM_LIMIT),
        name="moe_experts",
        cost_estimate=_expert_cost(part_blocks * MOE_ROWS),
    )(bexp, nused, nxt, slot, *args)


def _combine_kernel(h_ref, y0_ref, y1_ref, wts_ref, ln_g_ref, ln_b_ref, *rest):
    out_ref = rest[-1]
    wts = wts_ref[...].T
    ffn = _unpack_rows(y0_ref) * wts[:, 0:1] + _unpack_rows(y1_ref) * wts[:, 1:2]
    out_ref[...] = _layer_norm(DEEPNORM_ALPHA * h_ref[...] + ffn, ln_g_ref[...], ln_b_ref[...])


def _combine_call(h, yk, wts, ln_g, ln_b, out_prev, split, t_total):
    t_len = t_total // TOKEN_SPLITS
    n_tiles = t_len // COMBINE_TILE
    tok = lambda i: (i, 0)
    tok_split = lambda i: (i + split * n_tiles, 0)
    full2 = lambda i: (0, 0)
    tiles = pl.BlockSpec((COMBINE_TILE * ROW_SUBLANES, LANES), tok)
    tiles_k1 = pl.BlockSpec((COMBINE_TILE * ROW_SUBLANES, LANES), lambda i: (i + n_tiles, 0))
    in_specs = [pl.BlockSpec((COMBINE_TILE, D_MODEL), tok_split), tiles, tiles_k1,
                pl.BlockSpec((ROUTE_ROWS, COMBINE_TILE), lambda i: (0, i + split * n_tiles)),
                pl.BlockSpec((1, D_MODEL), full2),
                pl.BlockSpec((1, D_MODEL), full2)]
    args = [h, yk, yk, wts, ln_g, ln_b]
    aliases = {}
    if out_prev is not None:
        in_specs.append(pl.BlockSpec(memory_space=pl.ANY))
        args.append(out_prev)
        aliases = {len(args) - 1: 0}
    return pl.pallas_call(
        _combine_kernel,
        grid=(n_tiles,),
        in_specs=in_specs,
        out_specs=pl.BlockSpec((COMBINE_TILE, D_MODEL), tok_split),
        out_shape=jax.ShapeDtypeStruct((t_total, D_MODEL), F32),
        input_output_aliases=aliases,
        compiler_params=pltpu.CompilerParams(
            dimension_semantics=("arbitrary",), vmem_limit_bytes=VMEM_LIMIT),
        name="moe_combine_ln",
        cost_estimate=pl.CostEstimate(flops=8 * t_len * D_MODEL, transcendentals=t_len,
                                      bytes_accessed=(2 * 4 * D_MODEL + TOP_K * ROW_BYTES) * t_len),
    )(*args)


def _position_tables():
    c = CHUNK
    f = np.float32
    log_g = np.log1p(-np.exp2(-5.0 - np.arange(RET_HEADS, dtype=f))).astype(f)
    idx = np.arange(c, dtype=f)
    diff = idx[:, None] - idx[None, :]
    scale = f(RET_HEAD_DIM ** -0.5)
    din = np.where(diff >= 0, np.exp(log_g[:, None, None] * np.maximum(diff, 0.0)), 0.0).astype(f) * scale
    qdec = np.broadcast_to(np.exp(log_g[:, None] * (idx + 1.0))[:, :, None], (RET_HEADS, c, RET_HEAD_DIM))
    kdec = np.broadcast_to((np.exp(log_g[:, None] * (c - 1.0 - idx)) * scale)[:, :, None],
                           (RET_HEADS, c, RET_HEAD_DIM))
    cdec = np.exp(log_g * c)
    slopes = np.exp2(-8.0 * (np.arange(SWA_HEADS, dtype=f) + 1.0) / SWA_HEADS).astype(f)
    r = np.arange(c)[:, None]
    col = np.arange(c)[None, :]
    dist_prev = (r - col + c).astype(f)
    dist_cur = (r - col).astype(f)
    bprev = np.where((r < col)[None], -slopes[:, None, None] * dist_prev[None], NEG)
    bcur = np.where((r >= col)[None], -slopes[:, None, None] * dist_cur[None], NEG)
    bias = np.concatenate([bprev, bcur], axis=-1).reshape(SWA_KV_HEADS, SWA_GROUP * c, 2 * c)
    key = np.arange(2 * c)
    pen = np.where((key >= 1) & (key < c), NEG, 0.0)[None, :]
    lane_head = np.arange(SWA_VREP) // SWA_HEAD_DIM
    own = lane_head[None, None, :] == np.arange(SWA_GROUP)[:, None, None]
    kmask = np.broadcast_to(np.arange(c)[:, None] > 0, (c, SWA_HEAD_DIM))
    vmask = own & (key[None, :, None] > 0)
    hlane = lane_head[None, :]
    tr = np.arange(RANK_BLOCK)
    tri = tr[:, None] < tr[None, :]
    as_f32 = lambda v: np.ascontiguousarray(v, dtype=f)
    as_bf16 = lambda v: np.ascontiguousarray(v, dtype=f).astype(BF16)
    return (as_f32(cdec), as_f32(din), as_f32(qdec), as_f32(kdec), as_f32(bias), as_f32(pen),
            as_bf16(kmask), as_bf16(vmask), as_f32(hlane), as_bf16(tri))


def _mixer_constants(attn_sinks):
    cdec, din, qdec, kdec, bias, pen, kmask, vmask, hlane, tri = _position_tables()
    sink = attn_sinks.astype(F32).reshape(SWA_KV_HEADS, SWA_GROUP, 1, 1)
    sink = jnp.broadcast_to(sink, (SWA_KV_HEADS, SWA_GROUP, CHUNK, 1)).reshape(SWA_KV_HEADS, SWA_GROUP * CHUNK, 1)
    is_slot = (np.arange(2 * CHUNK) == 0)[None, None, :]
    return cdec, din, qdec, kdec, jnp.where(is_slot, sink, bias), pen, kmask, vmask, hlane, tri


def _router_tables(w_group_router, b_group_router, w_expert_router, b_expert_router):
    w_e = jnp.transpose(w_expert_router, (1, 0, 2)).reshape(D_MODEL, N_EXPERTS)
    w = jnp.concatenate([w_e, w_group_router,
                         jnp.zeros((D_MODEL, LANES - N_EXPERTS - N_GROUPS), F32)], axis=1)
    bias = jnp.concatenate([b_expert_router.reshape(N_EXPERTS), b_group_router,
                            jnp.zeros((LANES - N_EXPERTS - N_GROUPS,), F32)])[None, :]
    w_hi = w.astype(BF16)
    w_lo = (w - w_hi.astype(F32)).astype(BF16)
    return jnp.concatenate([w_hi, w_lo], axis=1), bias


def kernel(x, w_in, ret_gn_g, attn_sinks, w_out, ln1_g, ln1_b, w_group_router, b_group_router,
           w_expert_router, b_expert_router, w_gate, w_up, w_down, ln2_g, ln2_b):
    bsz, s_len, d = x.shape
    assert d == D_MODEL and s_len % SEQ_TILE == 0 and w_in.shape[0] == DEPTH == 1
    assert bsz % TOKEN_SPLITS == 0
    t_len = bsz * s_len
    t_split = t_len // TOKEN_SPLITS
    n_blocks = t_len * TOP_K // MOE_ROWS + N_EXPERTS
    n_rows = n_blocks * MOE_ROWS
    part_rows = n_rows // SLOT_PARTS

    consts = _mixer_constants(attn_sinks[0])
    wr_cat, br = _router_tables(w_group_router[0], b_group_router[0],
                                      w_expert_router[0], b_expert_router[0])
    h, h_packed, ids, wts, cnt = _mixer_call(
        x.reshape(t_len, d), w_in[0].astype(BF16), w_out[0].astype(BF16),
        ret_gn_g[0][None, :], ln1_g[0][None, :], ln1_b[0][None, :], wr_cat, br, consts, bsz, s_len, 0)

    counts = cnt[:, 0].astype(jnp.int32)
    padded = (counts + MOE_ROWS - 1) // MOE_ROWS * MOE_ROWS
    pend = jnp.cumsum(padded)
    pstart = pend - padded
    onehot = ids[0:TOP_K, :, None] == jnp.arange(N_EXPERTS, dtype=jnp.int32)
    dest2 = jnp.sum(jnp.where(onehot, pstart, 0), axis=-1) + ids[TOP_K:2 * TOP_K]
    dest = dest2.reshape(-1)
    nused = (pend[-1:] // MOE_ROWS).astype(jnp.int32)
    blk_start = jnp.minimum(jnp.arange(n_blocks, dtype=jnp.int32), nused[0] - 1) * MOE_ROWS
    bexp = jnp.minimum(jnp.sum(pend[None, :] <= blk_start[:, None], axis=-1), N_EXPERTS - 1).astype(jnp.int32)
    seg_end = jnp.sum(jnp.where(bexp[:, None] == jnp.arange(N_EXPERTS, dtype=jnp.int32), pend // MOE_ROWS, 0),
                      axis=-1).astype(jnp.int32)

    h_tiles = h_packed.reshape(t_len, ROW_SUBLANES, LANES)
    y = None
    for part in range(SLOT_PARTS):
        rows = _sc_dispatch(h_tiles, dest, part * part_rows, part_rows)
        y = _expert_call(bexp, nused, seg_end, rows.reshape(part_rows * ROW_SUBLANES, LANES),
                         w_gate[0], w_up[0], w_down[0], y, part, n_rows)

    y_tiles = y.reshape(n_rows, ROW_SUBLANES, LANES)
    out = None
    for sp in range(TOKEN_SPLITS):
        idx = dest2[:, sp * t_split:(sp + 1) * t_split].reshape(-1)
        yk = _sc_gather_rows(y_tiles, idx, "moe_combine_sc")
        out = _combine_call(h, yk.reshape(TOP_K * t_split * ROW_SUBLANES, LANES), wts,
                            ln2_g[0][None, :], ln2_b[0][None, :], out, sp, t_len)
    return out.reshape(bsz, s_len, d)
```

```python
import functools

import jax
import jax.numpy as jnp
import numpy as np
from jax import lax
from jax.experimental import pallas as pl
from jax.experimental.pallas import tpu as pltpu
from jax.experimental.pallas import tpu_sc as plsc

F32 = jnp.float32
BF16 = jnp.bfloat16
U32 = jnp.uint32

D_MODEL = 1024
RET_HEADS = 4
RET_HEAD_DIM = 128
RET_WIDTH = RET_HEADS * RET_HEAD_DIM
CHUNK = 128
SWA_HEADS = 8
SWA_KV_HEADS = 2
SWA_GROUP = SWA_HEADS // SWA_KV_HEADS
SWA_HEAD_DIM = 64
SWA_WIDTH = SWA_HEADS * SWA_HEAD_DIM
SWA_KV_WIDTH = SWA_KV_HEADS * SWA_HEAD_DIM
IN_WIDTH = 4 * RET_WIDTH + SWA_WIDTH + 2 * SWA_KV_WIDTH
N_GROUPS = 4
EXPERTS_PER_GROUP = 8
N_EXPERTS = N_GROUPS * EXPERTS_PER_GROUP
TOP_K = 2
D_EXPERT = 512
LN_EPS = 1e-5
GN_EPS = 1e-6
DEPTH = 1
DEEPNORM_ALPHA = (2 * DEPTH) ** 0.25
NEG = -1e30

LANES = 128
ROW_SUBLANES = D_MODEL // 2 // LANES
SEQ_TILE = 1024
PROJ_ROWS = 512
RANK_BLOCK = 256
MOE_ROWS = 512
COMBINE_TILE = 512
TOKEN_SPLITS = 2
SLOT_PARTS = 2
SC_CORES = 2
SC_SUBCORES = 16
SC_WORKERS = SC_CORES * SC_SUBCORES
SC_LANES = 16
SC_DEST_PIECES = 8
SC_SCAN_UNROLL = 8
SC_CHUNK_ROWS = 64
VMEM_LIMIT = 56 * 1024 * 1024

_QR, _KR, _VR, _GR = 0, RET_WIDTH, 2 * RET_WIDTH, 3 * RET_WIDTH
_QA = 4 * RET_WIDTH
_KA = _QA + SWA_WIDTH
_VA = _KA + SWA_KV_WIDTH
SWA_VREP = SWA_GROUP * SWA_HEAD_DIM
ROUTE_ROWS = 8
GROUP_LANE0 = N_EXPERTS


def _dot(a, b):
    return jnp.dot(a, b, preferred_element_type=F32)


def _dot_nt(a, b):
    return lax.dot_general(a, b, (((1,), (1,)), ((), ())), preferred_element_type=F32)


def _dot_tn(a, b):
    return lax.dot_general(a, b, (((0,), (0,)), ((), ())), preferred_element_type=F32)


def _layer_norm(z, g, b):
    mu = jnp.mean(z, axis=-1, keepdims=True)
    zc = z - mu
    var = jnp.mean(zc * zc, axis=-1, keepdims=True)
    return zc * lax.rsqrt(var + LN_EPS) * g + b


def _silu(g):
    return g / (1.0 + jnp.exp(-g))


def _tile_heads(v):
    return jnp.concatenate([v[:, j * SWA_HEAD_DIM:(j + 1) * SWA_HEAD_DIM]
                            for j in range(SWA_KV_HEADS) for _ in range(SWA_GROUP)], axis=1)


ROW_BYTES = 4 * ROW_SUBLANES * LANES


def _mixer_cost(t):
    proj = 2 * t * D_MODEL * (IN_WIDTH + D_MODEL + 3 * LANES)
    retention = RET_HEADS * 4 * 2 * t * CHUNK * RET_HEAD_DIM
    swa = SWA_KV_HEADS * 2 * t * SWA_GROUP * 2 * CHUNK * (SWA_HEAD_DIM + 2 * SWA_VREP)
    weights = 2 * D_MODEL * (IN_WIDTH + D_MODEL + 2 * LANES)
    return pl.CostEstimate(flops=proj + retention + swa + 2 * t * RANK_BLOCK * N_EXPERTS,
                           transcendentals=t * (SWA_HEADS * 2 * CHUNK + RET_WIDTH + LANES),
                           bytes_accessed=(2 * 4 * D_MODEL + ROW_BYTES) * t + weights)


def _expert_cost(n_rows):
    return pl.CostEstimate(flops=2 * 3 * n_rows * D_MODEL * D_EXPERT, transcendentals=n_rows * D_EXPERT,
                           bytes_accessed=2 * ROW_BYTES * n_rows + 4 * 3 * N_EXPERTS * D_MODEL * D_EXPERT)


def _row_move_cost(n_rows):
    return pl.CostEstimate(flops=0, transcendentals=0, bytes_accessed=2 * ROW_BYTES * n_rows)


def _pack_rows(ref, val):
    n = val.shape[0]
    half = D_MODEL // 2
    hi = lax.bitcast_convert_type(val[:, :half].astype(BF16).astype(F32), U32)
    lo = lax.bitcast_convert_type(val[:, half:].astype(BF16).astype(F32), U32)
    word = hi | (lo >> 16)
    for j in range(ROW_SUBLANES):
        ref[pl.ds(j, n, stride=ROW_SUBLANES), :] = word[:, j * LANES:(j + 1) * LANES]


def _unpack_rows(ref):
    n = ref.shape[0] // ROW_SUBLANES
    word = jnp.concatenate([ref[pl.ds(j, n, stride=ROW_SUBLANES), :] for j in range(ROW_SUBLANES)], axis=1)
    hi = lax.bitcast_convert_type(word & jnp.uint32(0xFFFF0000), F32)
    lo = lax.bitcast_convert_type(word << 16, F32)
    return jnp.concatenate([hi, lo], axis=1)


def _mixer_kernel(cdec_ref, x_ref, w_in_ref, w_out_ref, gn_ref, ln_g_ref, ln_b_ref,
                  wr_cat_ref, br_ref, din_ref, qdec_ref, kdec_ref,
                  bias_ref, pen_ref, kmask_ref, vmask_ref, hlane_ref, tri_ref,
                  h_ref, hp_ref, ids_ref, wts_ref, cnt_ref,
                  state_scr, kprev_scr, vprev_scr, o_scr, carry_scr):
    b = pl.program_id(0)
    n = pl.program_id(1)
    ts = x_ref.shape[0]

    @pl.when(n == 0)
    def _():
        state_scr[...] = jnp.zeros_like(state_scr)
        kprev_scr[...] = jnp.zeros_like(kprev_scr)
        vprev_scr[...] = jnp.zeros_like(vprev_scr)

    @pl.when((b == 0) & (n == 0))
    def _():
        carry_scr[...] = jnp.zeros_like(carry_scr)

    x = x_ref[...]
    xb = x.astype(BF16)

    chunks_per_part = PROJ_ROWS // CHUNK
    parts = []

    def project_part(part):
        xp = xb[part * PROJ_ROWS:(part + 1) * PROJ_ROWS]
        proj = lambda lo, hi: _dot(xp, w_in_ref[:, lo:hi])
        parts.append(dict(
            q_r=proj(_QR, _KR), k_r=proj(_KR, _VR), v_r=proj(_VR, _GR), g_r=proj(_GR, _QA),
            q_a=proj(_QA, _KA), k_ab=proj(_KA, _VA).astype(BF16),
            v_rep=_tile_heads(proj(_VA, IN_WIDTH).astype(BF16))))

    project_part(0)

    def chunk_of(name, c):
        lo = (c % chunks_per_part) * CHUNK
        return parts[c // chunks_per_part][name][lo:lo + CHUNK]
    first_pen = jnp.where(n == 0, pen_ref[...], 0.0)

    for c in range(ts // CHUNK):
        rs = slice(c * CHUNK, (c + 1) * CHUNK)
        q_r, k_r, v_r, g_r = (chunk_of(name, c) for name in ("q_r", "k_r", "v_r", "g_r"))
        q_a, k_ab, v_rep = (chunk_of(name, c) for name in ("q_a", "k_ab", "v_rep"))
        for hd in range(RET_HEADS):
            cs = slice(hd * RET_HEAD_DIM, (hd + 1) * RET_HEAD_DIM)
            q = q_r[:, cs]
            k = k_r[:, cs]
            v = v_r[:, cs].astype(BF16)
            scores = _dot_nt(q.astype(BF16), k.astype(BF16)) * din_ref[hd]
            st = state_scr[hd]
            o = _dot(jnp.concatenate([scores.astype(BF16), (q * qdec_ref[hd]).astype(BF16)], axis=1),
                     jnp.concatenate([v, st.astype(BF16)], axis=0))
            kv = _dot_tn((k * kdec_ref[hd]).astype(BF16), v)
            state_scr[hd] = st * cdec_ref[hd] + kv
            mu = jnp.mean(o, axis=-1, keepdims=True)
            oc = o - mu
            var = jnp.mean(oc * oc, axis=-1, keepdims=True)
            on = oc * lax.rsqrt(var + GN_EPS) * gn_ref[:, cs] * _silu(g_r[:, cs])
            o_scr[rs, cs] = on.astype(BF16)
        for j in range(SWA_KV_HEADS):
            ks = slice(j * SWA_HEAD_DIM, (j + 1) * SWA_HEAD_DIM)
            vs = slice(j * SWA_VREP, (j + 1) * SWA_VREP)
            if c == 0:
                kp = kprev_scr[:, ks].astype(BF16)
                vp = vprev_scr[:, vs].astype(BF16)
            else:
                kp = chunk_of("k_ab", c - 1)[:, ks]
                vp = chunk_of("v_rep", c - 1)[:, vs]
            kp = kp * kmask_ref[...]
            kcat = jnp.concatenate([kp, k_ab[:, ks]], axis=0)
            vcat = jnp.concatenate([vp, v_rep[:, vs]], axis=0)
            q0 = j * SWA_GROUP * SWA_HEAD_DIM
            qs = jnp.concatenate(
                [q_a[:, q0 + g * SWA_HEAD_DIM:q0 + (g + 1) * SWA_HEAD_DIM] for g in range(SWA_GROUP)],
                axis=0)
            qs = (qs * (SWA_HEAD_DIM ** -0.5)).astype(BF16)
            s = _dot_nt(qs, kcat) + bias_ref[j]
            if c == 0:
                s = s + first_pen
            m = jnp.max(jnp.maximum(s[:, :CHUNK], s[:, CHUNK:]), axis=-1, keepdims=True)
            p32 = jnp.exp(s - m)
            den_col = jnp.sum(p32[:, :CHUNK] + p32[:, CHUNK:], axis=-1, keepdims=True)
            p = p32.astype(BF16)
            p_all = jnp.concatenate([p[g * CHUNK:(g + 1) * CHUNK] for g in range(SWA_GROUP)], axis=1)
            v_blk = jnp.concatenate([vcat * vmask_ref[g] for g in range(SWA_GROUP)], axis=0)
            num = _dot(p_all, v_blk)
            den = den_col[(SWA_GROUP - 1) * CHUNK:]
            for g in range(SWA_GROUP - 2, -1, -1):
                den = jnp.where(hlane_ref[...] == g, den_col[g * CHUNK:(g + 1) * CHUNK], den)
            c0 = RET_WIDTH + j * SWA_GROUP * SWA_HEAD_DIM
            o_scr[rs, c0:c0 + SWA_GROUP * SWA_HEAD_DIM] = (num / den).astype(BF16)
        if c % chunks_per_part == 0 and len(parts) < ts // PROJ_ROWS:
            project_part(len(parts))

    kprev_scr[...] = chunk_of("k_ab", ts // CHUNK - 1).astype(F32)
    vprev_scr[...] = chunk_of("v_rep", ts // CHUNK - 1).astype(F32)

    mix = _dot(o_scr[...], w_out_ref[...])
    h = _layer_norm(DEEPNORM_ALPHA * x + mix, ln_g_ref[...], ln_b_ref[...])
    h_ref[...] = h
    _pack_rows(hp_ref, h)

    h_hi = h.astype(BF16)
    h_lo = (h - h_hi.astype(F32)).astype(BF16)
    hi_terms = _dot(h_hi, wr_cat_ref[...])
    logits = hi_terms[:, :LANES] + hi_terms[:, LANES:] + _dot(h_lo, wr_cat_ref[:, :LANES]) + br_ref[...]
    lt = logits.T
    row = lax.broadcasted_iota(jnp.int32, (EXPERTS_PER_GROUP, ts), 0).astype(F32)
    big = 1e9
    ninf = -jnp.inf
    col_max = lambda v: jnp.max(v, axis=0, keepdims=True)
    first_at = lambda v, m: jnp.min(jnp.where(v == m, row, big), axis=0, keepdims=True)
    gl = jnp.where(row < N_GROUPS, lt[GROUP_LANE0:GROUP_LANE0 + EXPERTS_PER_GROUP], ninf)
    gmax = col_max(gl)
    gidx = first_at(gl, gmax)
    g_w = 1.0 / jnp.sum(jnp.exp(gl - gmax), axis=0, keepdims=True)
    el = lt[(N_GROUPS - 1) * EXPERTS_PER_GROUP:N_GROUPS * EXPERTS_PER_GROUP]
    for g in range(N_GROUPS - 2, -1, -1):
        el = jnp.where(gidx == g, lt[g * EXPERTS_PER_GROUP:(g + 1) * EXPERTS_PER_GROUP], el)
    m1 = col_max(el)
    i1 = first_at(el, m1)
    el2 = jnp.where(row == i1, ninf, el)
    m2 = col_max(el2)
    i2 = first_at(el2, m2)
    t = jnp.exp(m2 - m1)
    w1 = g_w / (1.0 + t)
    w2 = g_w * t / (1.0 + t)
    e1 = gidx * EXPERTS_PER_GROUP + i1
    e2 = gidx * EXPERTS_PER_GROUP + i2
    erow = lax.broadcasted_iota(jnp.int32, (N_EXPERTS, ts), 0).astype(F32)
    hit1 = erow == e1
    hit2 = erow == e2
    onehot = (hit1 | hit2).astype(BF16)
    n_rb = ts // RANK_BLOCK
    blocks = [onehot[:, i * RANK_BLOCK:(i + 1) * RANK_BLOCK] for i in range(n_rb)]
    within = _dot(jnp.concatenate(blocks, axis=0), tri_ref[...])
    carry = carry_scr[:, 0:1]
    pieces = []
    for i in range(n_rb):
        pieces.append(within[i * N_EXPERTS:(i + 1) * N_EXPERTS] + carry)
        carry = carry + jnp.sum(blocks[i].astype(F32), axis=1, keepdims=True)
    prefix = jnp.concatenate(pieces, axis=1)
    r1 = jnp.sum(jnp.where(hit1, prefix, 0.0), axis=0, keepdims=True)
    r2 = jnp.sum(jnp.where(hit2, prefix, 0.0), axis=0, keepdims=True)
    carry_scr[...] = jnp.broadcast_to(carry, carry_scr.shape)
    cnt_ref[...] = jnp.broadcast_to(carry, cnt_ref.shape)
    pick = lambda k, v, rest: jnp.where(row == k, v, rest)
    ids_ref[...] = pick(0, e1, pick(1, e2, pick(2, r1, pick(3, r2, 0.0)))).astype(jnp.int32)
    wts_ref[...] = pick(0, w1, pick(1, w2, 0.0))


def _mixer_call(x2, w_in, w_out, gn, ln_g, ln_b, wr_cat, br, consts, bsz, s_len, first_seq):
    cdec, din, qdec, kdec, bias, pen, kmask, vmask, hlane, tri = consts
    t_len = bsz * s_len
    ns = s_len // SEQ_TILE
    tok = lambda b, n, *_: (b * ns + n, 0)
    tok_in = lambda b, n, *_: ((first_seq + b) * ns + n, 0)
    tok_t = lambda b, n, *_: (0, b * ns + n)
    full2 = lambda b, n, *_: (0, 0)
    full3 = lambda b, n, *_: (0, 0, 0)
    grid_spec = pltpu.PrefetchScalarGridSpec(
        num_scalar_prefetch=1,
        grid=(bsz, ns),
        in_specs=[
            pl.BlockSpec((SEQ_TILE, D_MODEL), tok_in),
            pl.BlockSpec((D_MODEL, IN_WIDTH), full2),
            pl.BlockSpec((D_MODEL, D_MODEL), full2),
            pl.BlockSpec((1, RET_WIDTH), full2),
            pl.BlockSpec((1, D_MODEL), full2),
            pl.BlockSpec((1, D_MODEL), full2),
            pl.BlockSpec((D_MODEL, 2 * LANES), full2),
            pl.BlockSpec((1, LANES), full2),
            pl.BlockSpec((RET_HEADS, CHUNK, CHUNK), full3),
            pl.BlockSpec((RET_HEADS, CHUNK, RET_HEAD_DIM), full3),
            pl.BlockSpec((RET_HEADS, CHUNK, RET_HEAD_DIM), full3),
            pl.BlockSpec((SWA_KV_HEADS, SWA_GROUP * CHUNK, 2 * CHUNK), full3),
            pl.BlockSpec((1, 2 * CHUNK), full2),
            pl.BlockSpec((CHUNK, SWA_HEAD_DIM), full2),
            pl.BlockSpec((SWA_GROUP, 2 * CHUNK, SWA_VREP), full3),
            pl.BlockSpec((1, SWA_VREP), full2),
            pl.BlockSpec((RANK_BLOCK, RANK_BLOCK), full2),
        ],
        out_specs=[
            pl.BlockSpec((SEQ_TILE, D_MODEL), tok),
            pl.BlockSpec((SEQ_TILE * ROW_SUBLANES, LANES), tok),
            pl.BlockSpec((ROUTE_ROWS, SEQ_TILE), tok_t),
            pl.BlockSpec((ROUTE_ROWS, SEQ_TILE), tok_t),
            pl.BlockSpec((N_EXPERTS, LANES), full2),
        ],
        scratch_shapes=[
            pltpu.VMEM((RET_HEADS, RET_HEAD_DIM, RET_HEAD_DIM), F32),
            pltpu.VMEM((CHUNK, SWA_KV_WIDTH), F32),
            pltpu.VMEM((CHUNK, SWA_KV_HEADS * SWA_VREP), F32),
            pltpu.VMEM((SEQ_TILE, D_MODEL), BF16),
            pltpu.VMEM((N_EXPERTS, LANES), F32),
        ],
    )
    return pl.pallas_call(
        _mixer_kernel,
        grid_spec=grid_spec,
        out_shape=[
            jax.ShapeDtypeStruct((t_len, D_MODEL), F32),
            jax.ShapeDtypeStruct((t_len * ROW_SUBLANES, LANES), U32),
            jax.ShapeDtypeStruct((ROUTE_ROWS, t_len), jnp.int32),
            jax.ShapeDtypeStruct((ROUTE_ROWS, t_len), F32),
            jax.ShapeDtypeStruct((N_EXPERTS, LANES), F32),
        ],
        compiler_params=pltpu.CompilerParams(
            dimension_semantics=("arbitrary", "arbitrary"), vmem_limit_bytes=VMEM_LIMIT),
        name="mixer_router",
        cost_estimate=_mixer_cost(t_len),
    )(cdec, x2, w_in, w_out, gn, ln_g, ln_b, wr_cat, br, din, qdec, kdec, bias, pen, kmask, vmask, hlane, tri)


def _sc_mesh():
    return plsc.VectorSubcoreMesh(core_axis_name="core", subcore_axis_name="subcore")


def _sc_stream_rows(table_hbm, idx_v, out_hbm, out_base, n_chunks, bufs, sems):
    assert n_chunks % 2 == 0

    def gather(j, b):
        off = pl.multiple_of(j * SC_CHUNK_ROWS, SC_CHUNK_ROWS)
        return pltpu.make_async_copy(table_hbm.at[idx_v.at[pl.ds(off, SC_CHUNK_ROWS)]], bufs[b], sems[b])

    gather(0, 0).start()

    @pl.loop(0, n_chunks, step=2)
    def _(j0):
        for b in range(2):
            j = j0 + b
            gather(j, b).wait()

            @pl.when(j + 1 < n_chunks)
            def _():
                gather(j + 1, 1 - b).start()

            off = pl.multiple_of(j * SC_CHUNK_ROWS, SC_CHUNK_ROWS)
            pltpu.sync_copy(bufs[b], out_hbm.at[pl.ds(out_base + off, SC_CHUNK_ROWS)])


def _sc_dispatch(h_rows, dest, slot0, n_slots):
    t_len = h_rows.shape[0]
    n_assign = dest.shape[0]
    per_worker = n_slots // SC_WORKERS
    n_chunks = per_worker // SC_CHUNK_ROWS
    assert n_chunks * SC_CHUNK_ROWS * SC_WORKERS == n_slots and slot0 + n_slots <= 3 * t_len
    assert per_worker % SC_LANES == 0 and t_len % (SC_LANES * SC_SCAN_UNROLL) == 0 and n_assign == TOP_K * t_len

    @functools.partial(
        pl.kernel, mesh=_sc_mesh(), name="moe_dispatch_sc", cost_estimate=_row_move_cost(n_slots),
        compiler_params=pltpu.CompilerParams(needs_layout_passes=False),
        out_type=jax.ShapeDtypeStruct((n_slots, ROW_SUBLANES, LANES), U32),
        scratch_types=[pltpu.VMEM((n_assign,), jnp.int32),
                       pltpu.VMEM((per_worker,), jnp.int32),
                       pltpu.VMEM((SC_CHUNK_ROWS, ROW_SUBLANES, LANES), U32),
                       pltpu.VMEM((SC_CHUNK_ROWS, ROW_SUBLANES, LANES), U32),
                       pltpu.SemaphoreType.DMA, pltpu.SemaphoreType.DMA])
    def dispatch(h_hbm, dest_hbm, rows_hbm, dest_v, src_v, buf0, buf1, sem, sem1):
        wid = lax.axis_index("subcore") * SC_CORES + lax.axis_index("core")
        local = wid * per_worker
        base = slot0 + local
        piece = n_assign // SC_DEST_PIECES
        copies = []
        for c in range(SC_DEST_PIECES):
            off = pl.multiple_of(lax.rem(c + wid, SC_DEST_PIECES) * piece, SC_LANES)
            copies.append(pltpu.async_copy(dest_hbm.at[pl.ds(off, piece)], dest_v.at[pl.ds(off, piece)], sem))
        for cp in copies:
            cp.wait()
        lane = lax.iota(jnp.int32, SC_LANES)

        def wrap(a):
            a = jnp.where(a >= t_len, a - t_len, a)
            return jnp.where(a >= t_len, a - t_len, a)

        @pl.loop(0, per_worker // SC_LANES)
        def _(i):
            src_v[pl.ds(i * SC_LANES, SC_LANES)] = wrap(base + i * SC_LANES + lane)

        for k in range(TOP_K):
            @pl.loop(0, t_len // (SC_LANES * SC_SCAN_UNROLL))
            def _(i):
                for u in range(SC_SCAN_UNROLL):
                    tok0 = (i * SC_SCAN_UNROLL + u) * SC_LANES
                    d = dest_v[pl.ds(k * t_len + tok0, SC_LANES)] - base
                    hit = (d >= 0) & (d < per_worker)
                    plsc.store_scatter(src_v, [jnp.where(hit, d, 0)], tok0 + lane, mask=hit)

        _sc_stream_rows(h_hbm, src_v, rows_hbm, local, n_chunks, (buf0, buf1), (sem, sem1))

    return dispatch(h_rows, dest)


def _sc_gather_rows(table, idx, name):
    m = idx.shape[0]
    per_worker = m // (SC_CHUNK_ROWS * SC_WORKERS)
    assert per_worker * SC_CHUNK_ROWS * SC_WORKERS == m

    @functools.partial(
        pl.kernel, mesh=_sc_mesh(), name=name, cost_estimate=_row_move_cost(m),
        out_type=jax.ShapeDtypeStruct((m, ROW_SUBLANES, LANES), U32),
        scratch_types=[pltpu.VMEM((per_worker * SC_CHUNK_ROWS,), jnp.int32),
                       pltpu.VMEM((SC_CHUNK_ROWS, ROW_SUBLANES, LANES), U32),
                       pltpu.VMEM((SC_CHUNK_ROWS, ROW_SUBLANES, LANES), U32),
                       pltpu.SemaphoreType.DMA, pltpu.SemaphoreType.DMA])
    def gather(table_hbm, idx_hbm, out_hbm, idx_v, buf0, buf1, sem0, sem1):
        wid = lax.axis_index("subcore") * SC_CORES + lax.axis_index("core")
        base = pl.multiple_of(wid * (per_worker * SC_CHUNK_ROWS), SC_CHUNK_ROWS)
        pltpu.sync_copy(idx_hbm.at[pl.ds(base, per_worker * SC_CHUNK_ROWS)], idx_v)

        _sc_stream_rows(table_hbm, idx_v, out_hbm, base, per_worker, (buf0, buf1), (sem0, sem1))

    return gather(table, idx)


def _expert_kernel(bexp_ref, nused_ref, nxt_ref, slot_ref, rows_hbm, wg_hbm, wu_hbm, wd_hbm, *rest, first_block,
                   part_blocks):
    y_hbm, stage_g, stage_u, stage_d, wg_s, wu_s, wd_s, rbuf, ybuf, sem, rsem, ysem = rest[-12:]
    block_rows = MOE_ROWS * ROW_SUBLANES
    n_used = jnp.clip(nused_ref[0] - first_block, 0, part_blocks)

    def weight_copies(e, s):
        return [pltpu.make_async_copy(w.at[e], stage.at[s], sem.at[s, i])
                for i, (w, stage) in enumerate(((wg_hbm, stage_g), (wu_hbm, stage_u), (wd_hbm, stage_d)))]

    pair_rows = 2 * block_rows
    n_pairs = (n_used + 1) // 2

    def row_copy(q):
        start = pl.multiple_of(q * pair_rows, pair_rows)
        return pltpu.make_async_copy(rows_hbm.at[pl.ds(start, pair_rows)], rbuf.at[q % 2], rsem.at[q % 2])

    def out_copy(q):
        start = pl.multiple_of(first_block * block_rows + q * pair_rows, pair_rows)
        return pltpu.make_async_copy(ybuf.at[q % 2], y_hbm.at[pl.ds(start, pair_rows)], ysem.at[q % 2])

    def load_weights(blk):
        slot = slot_ref[blk]
        for cp in weight_copies(bexp_ref[blk], slot):
            cp.wait()
        wg_s[...] = stage_g[slot].astype(BF16)
        wu_s[...] = stage_u[slot].astype(BF16)
        wd_s[...] = stage_d[slot].astype(BF16)

        @pl.when(nxt_ref[blk] >= 0)
        def _():
            for cp in weight_copies(nxt_ref[blk], 1 - slot):
                cp.start()

    def mlp(x_view, y_view):
        xb = _unpack_rows(x_view).astype(BF16)
        g = _dot(xb, wg_s[...])
        u = _dot(xb, wu_s[...])
        a = (_silu(g) * u).astype(BF16)
        _pack_rows(y_view, _dot(a, wd_s[...]))

    @pl.when(n_used > 0)
    def _():
        row_copy(0).start()
        for cp in weight_copies(bexp_ref[first_block], slot_ref[first_block]):
            cp.start()

    def pair(q, carry):
        blk = first_block + 2 * q
        expert = bexp_ref[blk]
        row_copy(q).wait()

        @pl.when(q + 1 < n_pairs)
        def _():
            row_copy(q + 1).start()

        @pl.when((q == 0) | (expert != bexp_ref[jnp.maximum(blk - 1, 0)]))
        def _():
            load_weights(blk)

        @pl.when(q >= 2)
        def _():
            out_copy(q - 2).wait()

        has_second = 2 * q + 1 < n_used
        same = has_second & (bexp_ref[blk + 1] == expert)
        s = q % 2

        @pl.when(same)
        def _():
            mlp(rbuf.at[s], ybuf.at[s])

        @pl.when(jnp.logical_not(same))
        def _():
            mlp(rbuf.at[s, pl.ds(0, block_rows)], ybuf.at[s, pl.ds(0, block_rows)])

            @pl.when(has_second)
            def _():
                load_weights(blk + 1)
                mlp(rbuf.at[s, pl.ds(block_rows, block_rows)], ybuf.at[s, pl.ds(block_rows, block_rows)])

        out_copy(q).start()
        return carry

    lax.fori_loop(0, n_pairs, pair, 0)

    for back in (2, 1):
        @pl.when(n_pairs >= back)
        def _():
            out_copy(n_pairs - back).wait()


def _expert_call(bexp, nused, seg_end, rows, w_gate, w_up, w_down, y_prev, part, n_rows):
    part_blocks = rows.shape[0] // (ROW_SUBLANES * MOE_ROWS)
    n_blocks = bexp.shape[0]
    first_block = part * part_blocks
    assert part_blocks % 2 == 0
    limit = jnp.minimum(nused[0], first_block + part_blocks)
    nxt = jnp.where(seg_end < limit, bexp[jnp.minimum(seg_end, n_blocks - 1)], -1).astype(jnp.int32)
    starts = jnp.concatenate([jnp.ones((1,), jnp.int32), (bexp[1:] != bexp[:-1]).astype(jnp.int32)])
    slot = (jnp.cumsum(starts) % 2).astype(jnp.int32)
    hbm = pl.BlockSpec(memory_space=pl.ANY)
    in_specs = [hbm, hbm, hbm, hbm]
    args = [rows, w_gate, w_up, w_down]
    aliases = {}
    if y_prev is not None:
        in_specs.append(hbm)
        args.append(y_prev)
        aliases = {4 + len(args) - 1: 0}
    grid_spec = pltpu.PrefetchScalarGridSpec(
        num_scalar_prefetch=4,
        grid=(1,),
        in_specs=in_specs,
        out_specs=hbm,
        scratch_shapes=[
            pltpu.VMEM((2, D_MODEL, D_EXPERT), F32),
            pltpu.VMEM((2, D_MODEL, D_EXPERT), F32),
            pltpu.VMEM((2, D_EXPERT, D_MODEL), F32),
            pltpu.VMEM((D_MODEL, D_EXPERT), BF16),
            pltpu.VMEM((D_MODEL, D_EXPERT), BF16),
            pltpu.VMEM((D_EXPERT, D_MODEL), BF16),
            pltpu.VMEM((2, 2 * MOE_ROWS * ROW_SUBLANES, LANES), U32),
            pltpu.VMEM((2, 2 * MOE_ROWS * ROW_SUBLANES, LANES), U32),
            pltpu.SemaphoreType.DMA((2, 3)),
            pltpu.SemaphoreType.DMA((2,)),
            pltpu.SemaphoreType.DMA((2,)),
        ],
    )
    return pl.pallas_call(
        functools.partial(_expert_kernel, first_block=first_block, part_blocks=part_blocks),
        grid_spec=grid_spec,
        out_shape=jax.ShapeDtypeStruct((n_rows * ROW_SUBLANES, LANES), U32),
        input_output_aliases=aliases,
        compiler_params=pltpu.CompilerParams(
            dimension_semantics=("arbitrary",), vmem_limit_bytes=VMEM_LIMIT),
        name="moe_experts",
        cost_estimate=_expert_cost(part_blocks * MOE_ROWS),
    )(bexp, nused, nxt, slot, *args)


def _combine_kernel(h_ref, y0_ref, y1_ref, wts_ref, ln_g_ref, ln_b_ref, *rest):
    out_ref = rest[-1]
    wts = wts_ref[...].T
    ffn = _unpack_rows(y0_ref) * wts[:, 0:1] + _unpack_rows(y1_ref) * wts[:, 1:2]
    out_ref[...] = _layer_norm(DEEPNORM_ALPHA * h_ref[...] + ffn, ln_g_ref[...], ln_b_ref[...])


def _combine_call(h, yk, wts, ln_g, ln_b, out_prev, split, t_total):
    t_len = t_total // TOKEN_SPLITS
    n_tiles = t_len // COMBINE_TILE
    tok = lambda i: (i, 0)
    tok_split = lambda i: (i + split * n_tiles, 0)
    full2 = lambda i: (0, 0)
    tiles = pl.BlockSpec((COMBINE_TILE * ROW_SUBLANES, LANES), tok)
    tiles_k1 = pl.BlockSpec((COMBINE_TILE * ROW_SUBLANES, LANES), lambda i: (i + n_tiles, 0))
    in_specs = [pl.BlockSpec((COMBINE_TILE, D_MODEL), tok_split), tiles, tiles_k1,
                pl.BlockSpec((ROUTE_ROWS, COMBINE_TILE), lambda i: (0, i + split * n_tiles)),
                pl.BlockSpec((1, D_MODEL), full2),
                pl.BlockSpec((1, D_MODEL), full2)]
    args = [h, yk, yk, wts, ln_g, ln_b]
    aliases = {}
    if out_prev is not None:
        in_specs.append(pl.BlockSpec(memory_space=pl.ANY))
        args.append(out_prev)
        aliases = {len(args) - 1: 0}
    return pl.pallas_call(
        _combine_kernel,
        grid=(n_tiles,),
        in_specs=in_specs,
        out_specs=pl.BlockSpec((COMBINE_TILE, D_MODEL), tok_split),
        out_shape=jax.ShapeDtypeStruct((t_total, D_MODEL), F32),
        input_output_aliases=aliases,
        compiler_params=pltpu.CompilerParams(
            dimension_semantics=("arbitrary",), vmem_limit_bytes=VMEM_LIMIT),
        name="moe_combine_ln",
        cost_estimate=pl.CostEstimate(flops=8 * t_len * D_MODEL, transcendentals=t_len,
                                      bytes_accessed=(2 * 4 * D_MODEL + TOP_K * ROW_BYTES) * t_len),
    )(*args)


def _position_tables():
    c = CHUNK
    f = np.float32
    log_g = np.log1p(-np.exp2(-5.0 - np.arange(RET_HEADS, dtype=f))).astype(f)
    idx = np.arange(c, dtype=f)
    diff = idx[:, None] - idx[None, :]
    scale = f(RET_HEAD_DIM ** -0.5)
    din = np.where(diff >= 0, np.exp(log_g[:, None, None] * np.maximum(diff, 0.0)), 0.0).astype(f) * scale
    qdec = np.broadcast_to(np.exp(log_g[:, None] * (idx + 1.0))[:, :, None], (RET_HEADS, c, RET_HEAD_DIM))
    kdec = np.broadcast_to((np.exp(log_g[:, None] * (c - 1.0 - idx)) * scale)[:, :, None],
                           (RET_HEADS, c, RET_HEAD_DIM))
    cdec = np.exp(log_g * c)
    slopes = np.exp2(-8.0 * (np.arange(SWA_HEADS, dtype=f) + 1.0) / SWA_HEADS).astype(f)
    r = np.arange(c)[:, None]
    col = np.arange(c)[None, :]
    dist_prev = (r - col + c).astype(f)
    dist_cur = (r - col).astype(f)
    bprev = np.where((r < col)[None], -slopes[:, None, None] * dist_prev[None], NEG)
    bcur = np.where((r >= col)[None], -slopes[:, None, None] * dist_cur[None], NEG)
    bias = np.concatenate([bprev, bcur], axis=-1).reshape(SWA_KV_HEADS, SWA_GROUP * c, 2 * c)
    key = np.arange(2 * c)
    pen = np.where((key >= 1) & (key < c), NEG, 0.0)[None, :]
    lane_head = np.arange(SWA_VREP) // SWA_HEAD_DIM
    own = lane_head[None, None, :] == np.arange(SWA_GROUP)[:, None, None]
    kmask = np.broadcast_to(np.arange(c)[:, None] > 0, (c, SWA_HEAD_DIM))
    vmask = own & (key[None, :, None] > 0)
    hlane = lane_head[None, :]
    tr = np.arange(RANK_BLOCK)
    tri = tr[:, None] < tr[None, :]
    as_f32 = lambda v: np.ascontiguousarray(v, dtype=f)
    as_bf16 = lambda v: np.ascontiguousarray(v, dtype=f).astype(BF16)
    return (as_f32(cdec), as_f32(din), as_f32(qdec), as_f32(kdec), as_f32(bias), as_f32(pen),
            as_bf16(kmask), as_bf16(vmask), as_f32(hlane), as_bf16(tri))


def _mixer_constants(attn_sinks):
    cdec, din, qdec, kdec, bias, pen, kmask, vmask, hlane, tri = _position_tables()
    sink = attn_sinks.astype(F32).reshape(SWA_KV_HEADS, SWA_GROUP, 1, 1)
    sink = jnp.broadcast_to(sink, (SWA_KV_HEADS, SWA_GROUP, CHUNK, 1)).reshape(SWA_KV_HEADS, SWA_GROUP * CHUNK, 1)
    is_slot = (np.arange(2 * CHUNK) == 0)[None, None, :]
    return cdec, din, qdec, kdec, jnp.where(is_slot, sink, bias), pen, kmask, vmask, hlane, tri


def _router_tables(w_group_router, b_group_router, w_expert_router, b_expert_router):
    w_e = jnp.transpose(w_expert_router, (1, 0, 2)).reshape(D_MODEL, N_EXPERTS)
    w = jnp.concatenate([w_e, w_group_router,
                         jnp.zeros((D_MODEL, LANES - N_EXPERTS - N_GROUPS), F32)], axis=1)
    bias = jnp.concatenate([b_expert_router.reshape(N_EXPERTS), b_group_router,
                            jnp.zeros((LANES - N_EXPERTS - N_GROUPS,), F32)])[None, :]
    w_hi = w.astype(BF16)
    w_lo = (w - w_hi.astype(F32)).astype(BF16)
    return jnp.concatenate([w_hi, w_lo], axis=1), bias


def kernel(x, w_in, ret_gn_g, attn_sinks, w_out, ln1_g, ln1_b, w_group_router, b_group_router,
           w_expert_router, b_expert_router, w_gate, w_up, w_down, ln2_g, ln2_b):
    bsz, s_len, d = x.shape
    assert d == D_MODEL and s_len % SEQ_TILE == 0 and w_in.shape[0] == DEPTH == 1
    assert bsz % TOKEN_SPLITS == 0
    t_len = bsz * s_len
    t_split = t_len // TOKEN_SPLITS
    n_blocks = t_len * TOP_K // MOE_ROWS + N_EXPERTS
    n_rows = n_blocks * MOE_ROWS
    part_rows = n_rows // SLOT_PARTS

    consts = _mixer_constants(attn_sinks[0])
    wr_cat, br = _router_tables(w_group_router[0], b_group_router[0],
                                      w_expert_router[0], b_expert_router[0])
    h, h_packed, ids, wts, cnt = _mixer_call(
        x.reshape(t_len, d), w_in[0].astype(BF16), w_out[0].astype(BF16),
        ret_gn_g[0][None, :], ln1_g[0][None, :], ln1_b[0][None, :], wr_cat, br, consts, bsz, s_len, 0)

    counts = cnt[:, 0].astype(jnp.int32)
    padded = (counts + MOE_ROWS - 1) // MOE_ROWS * MOE_ROWS
    pend = jnp.cumsum(padded)
    pstart = pend - padded
    onehot = ids[0:TOP_K, :, None] == jnp.arange(N_EXPERTS, dtype=jnp.int32)
    dest2 = jnp.sum(jnp.where(onehot, pstart, 0), axis=-1) + ids[TOP_K:2 * TOP_K]
    dest = dest2.reshape(-1)
    nused = (pend[-1:] // MOE_ROWS).astype(jnp.int32)
    blk_start = jnp.minimum(jnp.arange(n_blocks, dtype=jnp.int32), nused[0] - 1) * MOE_ROWS
    bexp = jnp.minimum(jnp.sum(pend[None, :] <= blk_start[:, None], axis=-1), N_EXPERTS - 1).astype(jnp.int32)
    seg_end = jnp.sum(jnp.where(bexp[:, None] == jnp.arange(N_EXPERTS, dtype=jnp.int32), pend // MOE_ROWS, 0),
                      axis=-1).astype(jnp.int32)

    h_tiles = h_packed.reshape(t_len, ROW_SUBLANES, LANES)
    y = None
    for part in range(SLOT_PARTS):
        rows = _sc_dispatch(h_tiles, dest, part * part_rows, part_rows)
        y = _expert_call(bexp, nused, seg_end, rows.reshape(part_rows * ROW_SUBLANES, LANES),
                         w_gate[0], w_up[0], w_down[0], y, part, n_rows)

    y_tiles = y.reshape(n_rows, ROW_SUBLANES, LANES)
    out = None
    for sp in range(TOKEN_SPLITS):
        idx = dest2[:, sp * t_split:(sp + 1) * t_split].reshape(-1)
        yk = _sc_gather_rows(y_tiles, idx, "moe_combine_sc")
        out = _combine_call(h, yk.reshape(TOP_K * t_split * ROW_SUBLANES, LANES), wts,
                            ln2_g[0][None, :], ln2_b[0][None, :], out, sp, t_len)
    return out.reshape(bsz, s_len, d)
```
